```python
import math
import jax, jax.numpy as jnp
from jax import lax
import numpy as np

D_MODEL = 1024
BATCH = 4
SEQ = 8192
DEPTH = 1

MEM_LEN = 256
XA_HEADS = 4
XA_HEAD_DIM = 128
DIFF_HEADS = 4
DIFF_QK_DIM = 64
DIFF_V_DIM = 2 * DIFF_QK_DIM
DSA_HEADS = 8
DSA_HEAD_DIM = 64
IDX_HEADS = 8
IDX_DIM = 64
TOPK_MAX = 256
ROPE_THETA = 500000.0
ROPE_FRACTION = 4
Q_BLOCK = 128
N_BRANCH = 3
N_GROUPS = 4
EXPERTS_PER_GROUP = 4
N_EXPERTS = N_GROUPS * EXPERTS_PER_GROUP
TOP_K_EXPERTS = 2
D_EXPERT = 512
EPS = 1e-6

W_DIFF_QK = DIFF_HEADS * 2 * DIFF_QK_DIM
W_DIFF_V = DIFF_HEADS * DIFF_V_DIM
W_DSA = DSA_HEADS * DSA_HEAD_DIM
W_IDX_Q = IDX_HEADS * IDX_DIM
W_IDX_K = IDX_DIM
W_IDX_W = IDX_HEADS
W_MEM_Q = XA_HEADS * XA_HEAD_DIM
W_GATES = N_BRANCH * D_MODEL
WIDTHS = (W_DIFF_QK, W_DIFF_QK, W_DIFF_V, W_DSA, W_DSA, W_DSA,
          W_IDX_Q, W_IDX_K, W_IDX_W, W_MEM_Q, W_GATES)
D_IN_PROJ = sum(WIDTHS)
SPLIT_POINTS = tuple(sum(WIDTHS[:i + 1]) for i in range(len(WIDTHS) - 1))

kernel_name = "hybrid_gated_diffattn_dsa_memxattn_hiermoe"


def rmsnorm(x, g):
    xf = x.astype(jnp.float32)
    y = xf * lax.rsqrt(jnp.mean(xf * xf, axis=-1, keepdims=True) + EPS)
    return (y * g.astype(jnp.float32)).astype(x.dtype)


def partial_rope(x, pos):
    d = x.shape[-1]
    rot = d // ROPE_FRACTION
    half = rot // 2
    inv_freq = ROPE_THETA ** (-jnp.arange(0, rot, 2, dtype=jnp.float32) / rot)
    ang = pos.astype(jnp.float32)[..., None] * inv_freq
    shape = ang.shape[:2] + (1,) * (x.ndim - 3) + (half,)
    cos = jnp.cos(ang).reshape(shape)
    sin = jnp.sin(ang).reshape(shape)
    xf = x.astype(jnp.float32)
    x1, x2, rest = xf[..., :half], xf[..., half:rot], xf[..., rot:]
    out = jnp.concatenate([x1 * cos - x2 * sin, x2 * cos + x1 * sin, rest], axis=-1)
    return out.astype(x.dtype)


def to_blocks(a):
    b, s = a.shape[:2]
    return a.reshape((b, s // Q_BLOCK, Q_BLOCK) + a.shape[2:]).swapaxes(0, 1)


def from_blocks(a):
    a = a.swapaxes(0, 1)
    return a.reshape((a.shape[0], a.shape[1] * a.shape[2]) + a.shape[3:])


def diff_attention(q1, q2, k1, k2, v, lam, g_sub, lam_init):
    b, s, h, dq = q1.shape
    scale = dq ** -0.5
    key_pos = jnp.arange(s)

    def block(args):
        i, q1b, q2b = args
        q_pos = i * Q_BLOCK + jnp.arange(Q_BLOCK)
        causal = (key_pos[None, :] <= q_pos[:, None])[None, None]
        s1 = jnp.einsum('bqhd,bkhd->bhqk', q1b, k1).astype(jnp.float32) * scale
        s2 = jnp.einsum('bqhd,bkhd->bhqk', q2b, k2).astype(jnp.float32) * scale
        p1 = jax.nn.softmax(jnp.where(causal, s1, -jnp.inf), axis=-1)
        p2 = jax.nn.softmax(jnp.where(causal, s2, -jnp.inf), axis=-1)
        a = p1 - lam * p2
        return jnp.einsum('bhqk,bkhd->bqhd', a.astype(v.dtype), v)

    nb = s // Q_BLOCK
    out = from_blocks(lax.map(block, (jnp.arange(nb), to_blocks(q1), to_blocks(q2))))
    out = rmsnorm(out, g_sub) * (1.0 - lam_init)
    return out.reshape(b, s, h * v.shape[-1])


def dsa_attention(q, k, v, q_idx, k_idx, w_idx, top_k):
    b, s, h, dh = q.shape
    key_pos = jnp.arange(s)
    idx_scale = IDX_DIM ** -0.5
    head_scale = IDX_HEADS ** -0.5
    att_scale = dh ** -0.5
    gather_rows = jax.vmap(lambda arr, sel: arr[sel])

    def block(args):
        i, qb, qib, wib = args
        q_pos = i * Q_BLOCK + jnp.arange(Q_BLOCK)
        causal = key_pos[None, :] <= q_pos[:, None]
        logits = jnp.einsum('bqhd,bkd->bqhk', qib, k_idx).astype(jnp.float32) * idx_scale
        score = jnp.einsum('bqh,bqhk->bqk', wib.astype(jnp.float32) * head_scale,
                           jax.nn.relu(logits))
        score = jnp.where(causal[None], score, -jnp.inf)
        _, sel = lax.top_k(score, top_k)
        valid = sel <= q_pos[None, :, None]
        ks = gather_rows(k, sel)
        vs = gather_rows(v, sel)
        sc = jnp.einsum('bqhd,bqkhd->bhqk', qb, ks).astype(jnp.float32) * att_scale
        sc = jnp.where(valid[:, None], sc, -jnp.inf)
        p = jax.nn.softmax(sc, axis=-1)
        return jnp.einsum('bhqk,bqkhd->bqhd', p.astype(v.dtype), vs)

    nb = s // Q_BLOCK
    out = from_blocks(lax.map(block, (jnp.arange(nb), to_blocks(q), to_blocks(q_idx), to_blocks(w_idx))))
    return out.reshape(b, s, h * dh)


def memory_attention(q, mk, mv):
    b, s, h, d = q.shape
    sc = jnp.einsum('bqhd,bmhd->bhqm', q, mk).astype(jnp.float32) * (d ** -0.5)
    p = jax.nn.softmax(sc, axis=-1)
    out = jnp.einsum('bhqm,bmhd->bqhd', p.astype(mv.dtype), mv)
    return out.reshape(b, s, h * d)


def hier_moe(xn, w_rg, b_rg, w_re, b_re, w_e_in, w_e_out):
    b, s, d = xn.shape
    t = xn.reshape(b * s, d)
    n_tok = t.shape[0]
    g_logits = (t @ w_rg).astype(jnp.float32) + b_rg.astype(jnp.float32)
    g_prob = jax.nn.softmax(g_logits, axis=-1)
    grp = jnp.argmax(g_logits, axis=-1)
    p_grp = jnp.take_along_axis(g_prob, grp[:, None], axis=-1)
    e_logits = ((t @ w_re).astype(jnp.float32) + b_re.astype(jnp.float32)).reshape(
        n_tok, N_GROUPS, EXPERTS_PER_GROUP)
    e_logits = jnp.take_along_axis(e_logits, grp[:, None, None], axis=1)[:, 0]
    e_prob = jax.nn.softmax(e_logits, axis=-1)
    top_w, top_i = lax.top_k(e_prob, TOP_K_EXPERTS)
    top_w = top_w / jnp.sum(top_w, axis=-1, keepdims=True)
    expert_id = grp[:, None] * EXPERTS_PER_GROUP + top_i
    combine = p_grp * top_w
    cw = jnp.sum(jax.nn.one_hot(expert_id, N_EXPERTS, dtype=jnp.float32) * combine[..., None], axis=1)
    y = jnp.zeros((n_tok, d), jnp.float32)
    for e in range(N_EXPERTS):
        gu = t @ w_e_in[e]
        hid = jax.nn.silu(gu[:, :D_EXPERT]) * gu[:, D_EXPERT:]
        y = y + cw[:, e:e + 1] * (hid @ w_e_out[e]).astype(jnp.float32)
    return y.astype(xn.dtype).reshape(b, s, d)


def setup_inputs(seed: int = 0) -> dict:
    key = jax.random.key(seed)
    ks = jax.random.split(key, 32)
    f32 = jnp.float32
    L, D = DEPTH, D_MODEL

    def nrm(k, shape, scale):
        return jax.random.normal(k, shape, f32) * scale

    x = jax.random.normal(ks[0], (BATCH, SEQ, D), f32)
    offsets = jax.random.randint(ks[1], (BATCH, 1), 0, 4096, dtype=jnp.int32)
    positions = offsets + jnp.arange(SEQ, dtype=jnp.int32)[None, :]
    mem = jax.random.normal(ks[2], (BATCH, MEM_LEN, D), f32)
    return {
        "x": x,
        "positions": positions,
        "mem": mem,
        "g_mix": 1.0 + nrm(ks[3], (L, D), 0.02),
        "w_in": nrm(ks[4], (L, D, D_IN_PROJ), D ** -0.5),
        "b_gate": nrm(ks[5], (L, W_GATES), 0.1),
        "lambda_q1": nrm(ks[6], (L, DIFF_QK_DIM), 0.1),
        "lambda_k1": nrm(ks[7], (L, DIFF_QK_DIM), 0.1),
        "lambda_q2": nrm(ks[8], (L, DIFF_QK_DIM), 0.1),
        "lambda_k2": nrm(ks[9], (L, DIFF_QK_DIM), 0.1),
        "g_diff_sub": 1.0 + nrm(ks[10], (L, DIFF_V_DIM), 0.02),
        "g_mem": 1.0 + nrm(ks[11], (L, D), 0.02),
        "w_mem_kv": nrm(ks[12], (L, D, 2 * W_MEM_Q), D ** -0.5),
        "w_br_diff": nrm(ks[13], (L, W_DIFF_V, D), W_DIFF_V ** -0.5),
        "w_br_dsa": nrm(ks[14], (L, W_DSA, D), W_DSA ** -0.5),
        "w_br_mem": nrm(ks[15], (L, W_MEM_Q, D), W_MEM_Q ** -0.5),
        "w_out": nrm(ks[16], (L, D, D), D ** -0.5),
        "g_ffn": 1.0 + nrm(ks[17], (L, D), 0.02),
        "w_route_group": nrm(ks[18], (L, D, N_GROUPS), D ** -0.5),
        "b_route_group": nrm(ks[19], (L, N_GROUPS), 0.01),
        "w_route_expert": nrm(ks[20], (L, D, N_EXPERTS), D ** -0.5),
        "b_route_expert": nrm(ks[21], (L, N_EXPERTS), 0.01),
        "w_exp_in": nrm(ks[22], (L, N_EXPERTS, D, 2 * D_EXPERT), D ** -0.5),
        "w_exp_out": nrm(ks[23], (L, N_EXPERTS, D_EXPERT, D), D_EXPERT ** -0.5),
        "g_final": 1.0 + nrm(ks[24], (D,), 0.02),
    }


def reference(x, positions, mem, g_mix, w_in, b_gate, lambda_q1, lambda_k1, lambda_q2, lambda_k2,
              g_diff_sub, g_mem, w_mem_kv, w_br_diff, w_br_dsa, w_br_mem, w_out, g_ffn,
              w_route_group, b_route_group, w_route_expert, b_route_expert, w_exp_in, w_exp_out,
              g_final):
    b, s, d = x.shape
    m = mem.shape[1]
    top_k = min(TOPK_MAX, s // 4)
    h = x
    for l in range(DEPTH):
        lam_init = 0.8 - 0.6 * math.exp(-0.3 * l)
        n = rmsnorm(h, g_mix[l])
        proj = n @ w_in[l]
        dq, dk, dv, sq, sk, sv, iq, ik, iw, mq, gl = jnp.split(proj, SPLIT_POINTS, axis=-1)

        dq = partial_rope(dq.reshape(b, s, DIFF_HEADS, 2, DIFF_QK_DIM), positions)
        dk = partial_rope(dk.reshape(b, s, DIFF_HEADS, 2, DIFF_QK_DIM), positions)
        dv = dv.reshape(b, s, DIFF_HEADS, DIFF_V_DIM)
        lam = (jnp.exp(jnp.sum(lambda_q1[l].astype(jnp.float32) * lambda_k1[l].astype(jnp.float32)))
               - jnp.exp(jnp.sum(lambda_q2[l].astype(jnp.float32) * lambda_k2[l].astype(jnp.float32)))
               + lam_init)
        y_diff = diff_attention(dq[..., 0, :], dq[..., 1, :], dk[..., 0, :], dk[..., 1, :], dv,
                                lam, g_diff_sub[l], lam_init)

        sq = partial_rope(sq.reshape(b, s, DSA_HEADS, DSA_HEAD_DIM), positions)
        sk = partial_rope(sk.reshape(b, s, DSA_HEADS, DSA_HEAD_DIM), positions)
        sv = sv.reshape(b, s, DSA_HEADS, DSA_HEAD_DIM)
        iq = partial_rope(iq.reshape(b, s, IDX_HEADS, IDX_DIM), positions)
        ik = partial_rope(ik, positions)
        y_dsa = dsa_attention(sq, sk, sv, iq, ik, iw, top_k)

        mkv = rmsnorm(mem, g_mem[l]) @ w_mem_kv[l]
        mk = mkv[..., :W_MEM_Q].reshape(b, m, XA_HEADS, XA_HEAD_DIM)
        mv = mkv[..., W_MEM_Q:].reshape(b, m, XA_HEADS, XA_HEAD_DIM)
        y_mem = memory_attention(mq.reshape(b, s, XA_HEADS, XA_HEAD_DIM), mk, mv)

        gates = jax.nn.sigmoid((gl + b_gate[l]).astype(jnp.float32)).astype(h.dtype).reshape(b, s, N_BRANCH, d)
        merged = (gates[:, :, 0] * (y_diff @ w_br_diff[l])
                  + gates[:, :, 1] * (y_dsa @ w_br_dsa[l])
                  + gates[:, :, 2] * (y_mem @ w_br_mem[l]))
        h = h + merged @ w_out[l]

        h = h + hier_moe(rmsnorm(h, g_ffn[l]), w_route_group[l], b_route_group[l],
                         w_route_expert[l], b_route_expert[l], w_exp_in[l], w_exp_out[l])
    return rmsnorm(h, g_final)
```

```python
import functools
import math

import jax
import jax.numpy as jnp
from jax import lax
from jax.experimental import pallas as pl
from jax.experimental.pallas import tpu as pltpu

D_MODEL = 1024
MEM_LEN = 256
XA_HEADS = 4
XA_HEAD_DIM = 128
DIFF_HEADS = 4
DIFF_QK_DIM = 64
DIFF_V_DIM = 128
DSA_HEADS = 8
DSA_HEAD_DIM = 64
IDX_HEADS = 8
IDX_DIM = 64
TOPK_MAX = 256
ROPE_THETA = 500000.0
ROPE_FRACTION = 4
N_GROUPS = 4
EXPERTS_PER_GROUP = 4
N_EXPERTS = 16
D_EXPERT = 512
EPS = 1e-6

LANES = 128
VMEM_LIMIT = 56 * 1024 * 1024
NEG_BIG = -1e30

PROJ_TM = 512
PROJ_TN = 512
DIFF_T = 512
DSA_TQ = 256
MEM_TQ = 512
MERGE_TM = 512
MOE_TM = 512

_NT = (((1,), (1,)), ((), ()))


def _dot(a, b):
    return jnp.dot(a, b, preferred_element_type=jnp.float32)


def _dot_nt(a, b):
    return lax.dot_general(a, b, _NT, preferred_element_type=jnp.float32)


def _rms(xf, g):
    return xf * lax.rsqrt(jnp.mean(xf * xf, axis=-1, keepdims=True) + EPS) * g


def _proj_kernel(x_ref, pos_ref, g_ref, inv_ref, wa_ref, ws_ref,
                 p_ref, ikk_ref, iw_ref, n_scr, cos_scr, sa_scr, sb_scr):
    j = pl.program_id(1)

    def rope(v, reps):
        n = v.shape[1]
        c = jnp.concatenate([cos_scr[...]] * reps, axis=1) if reps > 1 else cos_scr[...]
        a = jnp.concatenate([sa_scr[...]] * reps, axis=1) if reps > 1 else sa_scr[...]
        b = jnp.concatenate([sb_scr[...]] * reps, axis=1) if reps > 1 else sb_scr[...]
        up = pltpu.roll(v, n - 8, 1)
        dn = pltpu.roll(v, 8, 1)
        return v * c + up * a + dn * b

    @pl.when(j == 0)
    def _():
        xf = x_ref[...]
        n_scr[...] = _rms(xf, g_ref[...]).astype(jnp.bfloat16)
        ang = pos_ref[...].astype(jnp.float32) * inv_ref[...]
        lane = lax.broadcasted_iota(jnp.int32, ang.shape, 1) % 64
        cs = jnp.cos(ang)
        sn = jnp.sin(ang)
        cos_scr[...] = jnp.where(lane < 16, cs, 1.0)
        sa_scr[...] = jnp.where(lane < 8, -sn, 0.0)
        sb_scr[...] = jnp.where((lane >= 8) & (lane < 16), sn, 0.0)
        small = _dot(n_scr[...], ws_ref[...])
        ikk_ref[...] = rope(small[:, :LANES], 1).astype(jnp.bfloat16)
        iw_ref[...] = small[:, LANES:] * (IDX_HEADS ** -0.5)

    acc = _dot(n_scr[...], wa_ref[...])
    is_rope = (j == 0) | (j == 1) | (j == 3) | (j == 4) | (j == 6)

    @pl.when(is_rope)
    def _():
        p_ref[...] = rope(acc, PROJ_TN // LANES).astype(jnp.bfloat16)

    @pl.when(jnp.logical_not(is_rope))
    def _():
        p_ref[...] = acc.astype(jnp.bfloat16)


def _proj(x2, pos2, g_mix, inv_lane, w_a, w_s):
    t = x2.shape[0]
    ncol = w_a.shape[1] // PROJ_TN
    return pl.pallas_call(
        _proj_kernel,
        grid=(t // PROJ_TM, ncol),
        in_specs=[
            pl.BlockSpec((PROJ_TM, D_MODEL), lambda i, j: (i, 0)),
            pl.BlockSpec((PROJ_TM, 1), lambda i, j: (i, 0)),
            pl.BlockSpec((1, D_MODEL), lambda i, j: (0, 0)),
            pl.BlockSpec((1, LANES), lambda i, j: (0, 0)),
            pl.BlockSpec((D_MODEL, PROJ_TN), lambda i, j: (0, j)),
            pl.BlockSpec((D_MODEL, 2 * LANES), lambda i, j: (0, 0)),
        ],
        out_specs=[
            pl.BlockSpec((PROJ_TM, PROJ_TN), lambda i, j: (i, j)),
            pl.BlockSpec((PROJ_TM, LANES), lambda i, j: (i, 0)),
            pl.BlockSpec((PROJ_TM, LANES), lambda i, j: (i, 0)),
        ],
        out_shape=[
            jax.ShapeDtypeStruct((t, w_a.shape[1]), jnp.bfloat16),
            jax.ShapeDtypeStruct((t, LANES), jnp.bfloat16),
            jax.ShapeDtypeStruct((t, LANES), jnp.float32),
        ],
        scratch_shapes=[
            pltpu.VMEM((PROJ_TM, D_MODEL), jnp.bfloat16),
            pltpu.VMEM((PROJ_TM, LANES), jnp.float32),
            pltpu.VMEM((PROJ_TM, LANES), jnp.float32),
            pltpu.VMEM((PROJ_TM, LANES), jnp.float32),
        ],
        compiler_params=pltpu.CompilerParams(
            dimension_semantics=("parallel", "arbitrary"), vmem_limit_bytes=VMEM_LIMIT),
        name="proj",
    )(x2, pos2, g_mix, inv_lane, w_a, w_s)


def _diff_kernel(qi_tab, ki_tab, q_ref, k_ref, v_ref, lq1_ref, lk1_ref, lq2_ref, lk2_ref, gs_ref, o_ref,
                 qlo, qhi, m1, l1, m2, l2, acc1, acc2, *, lam_init):
    qi = qi_tab[pl.program_id(2)]
    ki = ki_tab[pl.program_id(2)]

    @pl.when(ki == 0)
    def _():
        q = q_ref[...]
        lane = lax.broadcasted_iota(jnp.int32, q.shape, 1)
        qlo[...] = jnp.where(lane < DIFF_QK_DIM, q, jnp.zeros_like(q))
        qhi[...] = jnp.where(lane >= DIFF_QK_DIM, q, jnp.zeros_like(q))
        m1[...] = jnp.full(m1.shape, NEG_BIG, jnp.float32)
        m2[...] = jnp.full(m2.shape, NEG_BIG, jnp.float32)
        l1[...] = jnp.zeros(l1.shape, jnp.float32)
        l2[...] = jnp.zeros(l2.shape, jnp.float32)
        acc1[...] = jnp.zeros(acc1.shape, jnp.float32)
        acc2[...] = jnp.zeros(acc2.shape, jnp.float32)

    def step(diagonal):
        k = k_ref[...]
        v = v_ref[...]
        if diagonal:
            row = lax.broadcasted_iota(jnp.int32, (DIFF_T, DIFF_T), 0)
            col = lax.broadcasted_iota(jnp.int32, (DIFF_T, DIFF_T), 1)
            keep = col <= row
        for qm, m_r, l_r, a_r in ((qlo, m1, l1, acc1), (qhi, m2, l2, acc2)):
            s = _dot_nt(qm[...], k)
            if diagonal:
                s = jnp.where(keep, s, NEG_BIG)
            m_old = m_r[...]
            m_new = jnp.maximum(m_old, jnp.max(s, axis=1, keepdims=True))
            alpha = jnp.exp(m_old - m_new)
            p = jnp.exp(s - m_new)
            l_r[...] = alpha * l_r[...] + jnp.sum(p, axis=1, keepdims=True)
            a_r[...] = alpha * a_r[...] + _dot(p.astype(jnp.bfloat16), v)
            m_r[...] = m_new

    @pl.when(ki < qi)
    def _():
        step(False)

    @pl.when(ki == qi)
    def _():
        step(True)
        lam = (jnp.exp(jnp.sum(lq1_ref[...] * lk1_ref[...], axis=1, keepdims=True))
               - jnp.exp(jnp.sum(lq2_ref[...] * lk2_ref[...], axis=1, keepdims=True))
               + lam_init)
        o = acc1[...] / l1[...] - lam * (acc2[...] / l2[...])
        y = _rms(o, gs_ref[...]) * (1.0 - lam_init)
        o_ref[...] = y.astype(o_ref.dtype)


def _diff_attention(p, lq1, lk1, lq2, lk2, g_sub, batch, seq, lam_init):
    nb = seq // DIFF_T
    pairs = [(qi, ki) for qi in range(nb) for ki in range(qi + 1)]
    qi_tab = jnp.asarray([a for a, _ in pairs], jnp.int32)
    ki_tab = jnp.asarray([c for _, c in pairs], jnp.int32)
    vec = pl.BlockSpec((1, DIFF_QK_DIM), lambda b, h, s, qt, kt: (0, 0))
    grid_spec = pltpu.PrefetchScalarGridSpec(
        num_scalar_prefetch=2,
        grid=(batch, DIFF_HEADS, len(pairs)),
        in_specs=[
            pl.BlockSpec((DIFF_T, LANES), lambda b, h, s, qt, kt: (b * nb + qt[s], h)),
            pl.BlockSpec((DIFF_T, LANES), lambda b, h, s, qt, kt: (b * nb + kt[s], DIFF_HEADS + h)),
            pl.BlockSpec((DIFF_T, LANES), lambda b, h, s, qt, kt: (b * nb + kt[s], 2 * DIFF_HEADS + h)),
            vec, vec, vec, vec,
            pl.BlockSpec((1, DIFF_V_DIM), lambda b, h, s, qt, kt: (0, 0)),
        ],
        out_specs=pl.BlockSpec((DIFF_T, LANES), lambda b, h, s, qt, kt: (b * nb + qt[s], h)),
        scratch_shapes=[
            pltpu.VMEM((DIFF_T, LANES), jnp.bfloat16),
            pltpu.VMEM((DIFF_T, LANES), jnp.bfloat16),
            pltpu.VMEM((DIFF_T, 1), jnp.float32),
            pltpu.VMEM((DIFF_T, 1), jnp.float32),
            pltpu.VMEM((DIFF_T, 1), jnp.float32),
            pltpu.VMEM((DIFF_T, 1), jnp.float32),
            pltpu.VMEM((DIFF_T, LANES), jnp.float32),
            pltpu.VMEM((DIFF_T, LANES), jnp.float32),
        ],
    )
    return pl.pallas_call(
        functools.partial(_diff_kernel, lam_init=lam_init),
        grid_spec=grid_spec,
        out_shape=jax.ShapeDtypeStruct((batch * seq, DIFF_HEADS * DIFF_V_DIM), jnp.bfloat16),
        compiler_params=pltpu.CompilerParams(
            dimension_semantics=("parallel", "parallel", "arbitrary"),
            vmem_limit_bytes=VMEM_LIMIT),
        name="diffattn",
    )(qi_tab, ki_tab, p, p, p, lq1, lk1, lq2, lk2, g_sub)


def _key_to_float(key):
    bits = jnp.where(key >= 0, key, key ^ jnp.int32(0x7FFFFFFF))
    return lax.bitcast_convert_type(bits, jnp.float32)


def _dsa_kernel(sq_ref, iq_ref, iw_ref, ikk_ref, sk_ref, sv_ref, o_ref,
                score, iqm, sqm, wb, acc, m_s, l_s, thr, jcut, *, seq, top_k):
    qi = pl.program_id(1)
    tq = DSA_TQ
    nkc = qi + 1
    lane = lax.broadcasted_iota(jnp.int32, (tq, LANES), 1)
    half = lane // DSA_HEAD_DIM

    for h in range(DSA_HEADS):
        pr = h // 2
        iqp = iq_ref[:, pr * LANES:(pr + 1) * LANES]
        sqp = sq_ref[:, pr * LANES:(pr + 1) * LANES]
        iqm[h] = jnp.where(half == h % 2, iqp, jnp.zeros_like(iqp))
        sqm[h] = jnp.where(half == h % 2, sqp, jnp.zeros_like(sqp))
        wb[h] = jnp.broadcast_to(iw_ref[:, h:h + 1], (tq, tq))

    row_l = lax.broadcasted_iota(jnp.int32, (tq, tq), 0)
    col_l = lax.broadcasted_iota(jnp.int32, (tq, tq), 1)

    def score_chunk(c, carry):
        kk = ikk_ref[pl.ds(pl.multiple_of(c * tq, tq), tq), :]
        sc = jnp.zeros((tq, tq), jnp.float32)
        for h in range(IDX_HEADS):
            sc = sc + wb[h] * jnp.maximum(_dot_nt(iqm[h], kk), 0.0)
        sc = jnp.where(col_l <= row_l + (qi - c) * tq, sc, -jnp.inf)
        score[c] = sc
        return carry

    lax.fori_loop(0, nkc, score_chunk, 0)

    def count_ge(cand_f):
        def body(c, cnt):
            blk = score[c]
            hit = jnp.where(blk >= cand_f, 1.0, 0.0)
            return cnt + hit[:, :LANES] + hit[:, LANES:]
        cnt = lax.fori_loop(0, nkc, body, jnp.zeros((tq, LANES), jnp.float32))
        return jnp.sum(cnt, axis=1, keepdims=True)

    def bit_step(i, key):
        cand = key + lax.shift_left(jnp.int32(1), jnp.int32(31) - i)
        ok = count_ge(_key_to_float(cand)) >= float(top_k)
        return jnp.where(ok, cand, key)

    int_min = jnp.full((tq, 1), -2 ** 31, jnp.int32)
    key = lax.fori_loop(0, 32, bit_step, int_min)
    t_f = _key_to_float(key)

    def count_gt_ge(c, carry):
        gt, ge = carry
        blk = score[c]
        a = jnp.where(blk > t_f, 1.0, 0.0)
        b = jnp.where(blk >= t_f, 1.0, 0.0)
        return gt + a[:, :LANES] + a[:, LANES:], ge + b[:, :LANES] + b[:, LANES:]

    z = jnp.zeros((tq, LANES), jnp.float32)
    gt, ge = lax.fori_loop(0, nkc, count_gt_ge, (z, z))
    n_gt = jnp.sum(gt, axis=1, keepdims=True)
    n_ge = jnp.sum(ge, axis=1, keepdims=True)
    row_pos = qi * tq + lax.broadcasted_iota(jnp.int32, (tq, 1), 0)
    few = row_pos < top_k - 1
    thr[...] = jnp.where(few, -jnp.inf, t_f)
    jcut[...] = jnp.where(few, -1, seq)
    need = float(top_k) - n_gt
    split = jnp.logical_and(jnp.logical_not(few), n_ge > float(top_k))

    @pl.when(jnp.max(jnp.where(split, 1.0, 0.0)) > 0.0)
    def _():
        def count_eq_below(jc):
            def body(c, cnt):
                blk = score[c]
                hit = jnp.where((blk == t_f) & (c * tq + col_l < jc), 1.0, 0.0)
                return cnt + hit[:, :LANES] + hit[:, LANES:]
            cnt = lax.fori_loop(0, nkc, body, jnp.zeros((tq, LANES), jnp.float32))
            return jnp.sum(cnt, axis=1, keepdims=True)

        nbits = (seq - 1).bit_length()

        def jbit(i, jv):
            cand = jv + lax.shift_left(jnp.int32(1), jnp.int32(nbits - 1) - i)
            return jnp.where(count_eq_below(cand) < need, cand, jv)

        jv = lax.fori_loop(0, nbits, jbit, jnp.zeros((tq, 1), jnp.int32))
        jcut[...] = jnp.where(split, jv, jcut[...])

    m_s[...] = jnp.full(m_s.shape, NEG_BIG, jnp.float32)
    l_s[...] = jnp.zeros(l_s.shape, jnp.float32)
    acc[...] = jnp.zeros(acc.shape, jnp.float32)

    def attend(c, carry):
        off = pl.multiple_of(c * tq, tq)
        blk = score[c]
        tt = thr[...]
        keep = (blk > tt) | ((blk == tt) & (c * tq + col_l <= jcut[...]))
        bias = jnp.where(keep, 0.0, NEG_BIG)
        for pr in range(DSA_HEADS // 2):
            kp = sk_ref[pl.ds(off, tq), pr * LANES:(pr + 1) * LANES]
            vp = sv_ref[pl.ds(off, tq), pr * LANES:(pr + 1) * LANES]
            upd = []
            for hh in range(2):
                h = 2 * pr + hh
                s = _dot_nt(sqm[h], kp) + bias
                m_old = m_s[h]
                m_new = jnp.maximum(m_old, jnp.max(s, axis=1, keepdims=True))
                alpha = jnp.exp(m_old - m_new)
                p = jnp.exp(s - m_new)
                l_s[h] = alpha * l_s[h] + jnp.sum(p, axis=1, keepdims=True)
                m_s[h] = m_new
                upd.append((alpha, _dot(p.astype(jnp.bfloat16), vp)))
            a_sel = jnp.where(half == 0, upd[0][0], upd[1][0])
            pv_sel = jnp.where(half == 0, upd[0][1], upd[1][1])
            acc[pr] = a_sel * acc[pr] + pv_sel
        return carry

    lax.fori_loop(0, nkc, attend, 0)

    outs = []
    for pr in range(DSA_HEADS // 2):
        l_sel = jnp.where(half == 0, l_s[2 * pr], l_s[2 * pr + 1])
        outs.append(acc[pr] / l_sel)
    o_ref[...] = jnp.concatenate(outs, axis=1).astype(o_ref.dtype)


def _dsa_attention(p, ikk, iw, batch, seq, top_k):
    nq = seq // DSA_TQ
    w = DSA_HEADS * DSA_HEAD_DIM
    once = pl.Buffered(1)
    return pl.pallas_call(
        functools.partial(_dsa_kernel, seq=seq, top_k=top_k),
        grid=(batch, nq),
        in_specs=[
            pl.BlockSpec((DSA_TQ, w), lambda b, qi: (b * nq + qi, 3)),
            pl.BlockSpec((DSA_TQ, w), lambda b, qi: (b * nq + qi, 6)),
            pl.BlockSpec((DSA_TQ, LANES), lambda b, qi: (b * nq + qi, 0)),
            pl.BlockSpec((seq, LANES), lambda b, qi: (b, 0), pipeline_mode=once),
            pl.BlockSpec((seq, w), lambda b, qi: (b, 4), pipeline_mode=once),
            pl.BlockSpec((seq, w), lambda b, qi: (b, 5), pipeline_mode=once),
        ],
        out_specs=pl.BlockSpec((DSA_TQ, w), lambda b, qi: (b * nq + qi, 0)),
        out_shape=jax.ShapeDtypeStruct((batch * seq, w), jnp.bfloat16),
        scratch_shapes=[
            pltpu.VMEM((nq, DSA_TQ, DSA_TQ), jnp.float32),
            pltpu.VMEM((IDX_HEADS, DSA_TQ, LANES), jnp.bfloat16),
            pltpu.VMEM((DSA_HEADS, DSA_TQ, LANES), jnp.bfloat16),
            pltpu.VMEM((IDX_HEADS, DSA_TQ, DSA_TQ), jnp.float32),
            pltpu.VMEM((DSA_HEADS // 2, DSA_TQ, LANES), jnp.float32),
            pltpu.VMEM((DSA_HEADS, DSA_TQ, 1), jnp.float32),
            pltpu.VMEM((DSA_HEADS, DSA_TQ, 1), jnp.float32),
            pltpu.VMEM((DSA_TQ, 1), jnp.float32),
            pltpu.VMEM((DSA_TQ, 1), jnp.int32),
        ],
        compiler_params=pltpu.CompilerParams(
            dimension_semantics=("parallel", "arbitrary"), vmem_limit_bytes=VMEM_LIMIT),
        name="dsa",
    )(p, p, iw, ikk, p, p)


def _memkv_kernel(mem_ref, g_ref, w_ref, o_ref):
    n = _rms(mem_ref[...], g_ref[...]).astype(jnp.bfloat16)
    o_ref[...] = _dot(n, w_ref[...]).astype(o_ref.dtype)


def _memkv(mem2, g_mem, w_kv, batch):
    return pl.pallas_call(
        _memkv_kernel,
        grid=(batch,),
        in_specs=[
            pl.BlockSpec((MEM_LEN, D_MODEL), lambda b: (b, 0)),
            pl.BlockSpec((1, D_MODEL), lambda b: (0, 0)),
            pl.BlockSpec(w_kv.shape, lambda b: (0, 0)),
        ],
        out_specs=pl.BlockSpec((MEM_LEN, w_kv.shape[1]), lambda b: (b, 0)),
        out_shape=jax.ShapeDtypeStruct((batch * MEM_LEN, w_kv.shape[1]), jnp.bfloat16),
        compiler_params=pltpu.CompilerParams(
            dimension_semantics=("parallel",), vmem_limit_bytes=VMEM_LIMIT),
        name="memkv",
    )(mem2, g_mem, w_kv)


def _memattn_kernel(q_ref, kv_ref, o_ref):
    scale = XA_HEAD_DIM ** -0.5
    outs = []
    for h in range(XA_HEADS):
        q = q_ref[:, h * LANES:(h + 1) * LANES]
        k = kv_ref[:, h * LANES:(h + 1) * LANES]
        v = kv_ref[:, (XA_HEADS + h) * LANES:(XA_HEADS + h + 1) * LANES]
        s = _dot_nt(q, k) * scale
        m = jnp.max(s, axis=1, keepdims=True)
        p = jnp.exp(s - m)
        l = jnp.sum(p, axis=1, keepdims=True)
        outs.append(_dot((p / l).astype(jnp.bfloat16), v))
    o_ref[...] = jnp.concatenate(outs, axis=1).astype(o_ref.dtype)


def _mem_attention(p, mkv, batch, seq):
    nq = seq // MEM_TQ
    w = XA_HEADS * XA_HEAD_DIM
    return pl.pallas_call(
        _memattn_kernel,
        grid=(batch, nq),
        in_specs=[
            pl.BlockSpec((MEM_TQ, w), lambda b, qi: (b * nq + qi, 7)),
            pl.BlockSpec((MEM_LEN, 2 * w), lambda b, qi: (b, 0)),
        ],
        out_specs=pl.BlockSpec((MEM_TQ, w), lambda b, qi: (b * nq + qi, 0)),
        out_shape=jax.ShapeDtypeStruct((batch * seq, w), jnp.bfloat16),
        compiler_params=pltpu.CompilerParams(
            dimension_semantics=("parallel", "parallel"), vmem_limit_bytes=VMEM_LIMIT),
        name="memattn",
    )(p, mkv)


def _merge_kernel(x_ref, yd_ref, ys_ref, ym_ref, gmix_ref, wg_ref, bg_ref, wbr_ref, wout_ref,
                  gffn_ref, wr_ref, br_ref, h_ref, hn_ref, lg_ref):
    xf = x_ref[...]
    n = _rms(xf, gmix_ref[...]).astype(jnp.bfloat16)
    merged = jnp.zeros(xf.shape, jnp.float32)
    for i, y_ref in enumerate((yd_ref, ys_ref, ym_ref)):
        gate = jax.nn.sigmoid(_dot(n, wg_ref[i]) + bg_ref[i])
        merged = merged + gate * _dot(y_ref[...], wbr_ref[i])
    h = xf + _dot(merged.astype(jnp.bfloat16), wout_ref[...])
    h_ref[...] = h
    hn = _rms(h, gffn_ref[...]).astype(jnp.bfloat16)
    hn_ref[...] = hn
    lg_ref[...] = _dot(hn, wr_ref[...]) + br_ref[...]


def _merge(x2, y_diff, y_dsa, y_mem, g_mix, w_g, b_g, w_br, w_out, g_ffn, w_r, b_r):
    t = x2.shape[0]
    row = lambda w: pl.BlockSpec((MERGE_TM, w), lambda i: (i, 0))
    full = lambda a: pl.BlockSpec(a.shape, lambda i: (0,) * a.ndim, pipeline_mode=pl.Buffered(1))
    return pl.pallas_call(
        _merge_kernel,
        grid=(t // MERGE_TM,),
        in_specs=[row(D_MODEL), row(512), row(512), row(512), full(g_mix), full(w_g), full(b_g),
                  full(w_br), full(w_out), full(g_ffn), full(w_r), full(b_r)],
        out_specs=[row(D_MODEL), row(D_MODEL), row(LANES)],
        out_shape=[
            jax.ShapeDtypeStruct((t, D_MODEL), jnp.float32),
            jax.ShapeDtypeStruct((t, D_MODEL), jnp.bfloat16),
            jax.ShapeDtypeStruct((t, LANES), jnp.float32),
        ],
        compiler_params=pltpu.CompilerParams(
            dimension_semantics=("parallel",), vmem_limit_bytes=VMEM_LIMIT),
        name="merge",
    )(x2, y_diff, y_dsa, y_mem, g_mix, w_g, b_g, w_br, w_out, g_ffn, w_r, b_r)


def _route_weights(lg):
    lane = lax.broadcasted_iota(jnp.int32, lg.shape, 1).astype(jnp.float32)
    big = float(LANES)
    gl = jnp.where(lane < N_GROUPS, lg, -jnp.inf)
    gmax = jnp.max(gl, axis=1, keepdims=True)
    grp = jnp.min(jnp.where(gl == gmax, lane, big), axis=1, keepdims=True)
    gsum = jnp.sum(jnp.where(lane < N_GROUPS, jnp.exp(gl - gmax), 0.0), axis=1, keepdims=True)
    p_grp = 1.0 / gsum
    lo = N_GROUPS + grp * EXPERTS_PER_GROUP
    el = jnp.where((lane >= lo) & (lane < lo + EXPERTS_PER_GROUP), lg, -jnp.inf)
    e1 = jnp.max(el, axis=1, keepdims=True)
    i1 = jnp.min(jnp.where(el == e1, lane, big), axis=1, keepdims=True)
    el2 = jnp.where(lane == i1, -jnp.inf, el)
    e2 = jnp.max(el2, axis=1, keepdims=True)
    i2 = jnp.min(jnp.where(el2 == e2, lane, big), axis=1, keepdims=True)
    r = jnp.exp(e2 - e1)
    w1 = p_grp / (1.0 + r)
    w2 = p_grp * r / (1.0 + r)
    return jnp.where(lane == i1, w1, 0.0) + jnp.where(lane == i2, w2, 0.0)


def _moe_kernel(hn_ref, lg_ref, h_ref, win_ref, wout_ref, gfin_ref, o_ref, acc, cw):
    e = pl.program_id(1)

    @pl.when(e == 0)
    def _():
        acc[...] = jnp.zeros(acc.shape, jnp.float32)
        cw[...] = _route_weights(lg_ref[...])

    lane = lax.broadcasted_iota(jnp.int32, cw.shape, 1)
    cw_e = jnp.sum(jnp.where(lane == N_GROUPS + e, cw[...], 0.0), axis=1, keepdims=True)
    gu = _dot(hn_ref[...], win_ref[0])
    g = gu[:, :D_EXPERT]
    hid = (g * jax.nn.sigmoid(g)) * gu[:, D_EXPERT:]
    acc[...] += cw_e * _dot(hid.astype(jnp.bfloat16), wout_ref[0])

    @pl.when(e == N_EXPERTS - 1)
    def _():
        o_ref[...] = _rms(h_ref[...] + acc[...], gfin_ref[...])


def _moe(hn, lg, h, w_e_in, w_e_out, g_final):
    t = hn.shape[0]
    return pl.pallas_call(
        _moe_kernel,
        grid=(t // MOE_TM, N_EXPERTS),
        in_specs=[
            pl.BlockSpec((MOE_TM, D_MODEL), lambda i, e: (i, 0)),
            pl.BlockSpec((MOE_TM, LANES), lambda i, e: (i, 0)),
            pl.BlockSpec((MOE_TM, D_MODEL), lambda i, e: (i, 0)),
            pl.BlockSpec((1, D_MODEL, 2 * D_EXPERT), lambda i, e: (e, 0, 0)),
            pl.BlockSpec((1, D_EXPERT, D_MODEL), lambda i, e: (e, 0, 0)),
            pl.BlockSpec((1, D_MODEL), lambda i, e: (0, 0)),
        ],
        out_specs=pl.BlockSpec((MOE_TM, D_MODEL), lambda i, e: (i, 0)),
        out_shape=jax.ShapeDtypeStruct((t, D_MODEL), jnp.float32),
        scratch_shapes=[
            pltpu.VMEM((MOE_TM, D_MODEL), jnp.float32),
            pltpu.VMEM((MOE_TM, LANES), jnp.float32),
        ],
        compiler_params=pltpu.CompilerParams(
            dimension_semantics=("parallel", "arbitrary"), vmem_limit_bytes=VMEM_LIMIT),
        name="moe",
    )(hn, lg, h, w_e_in, w_e_out, g_final)


def kernel(x, positions, mem, g_mix, w_in, b_gate, lambda_q1, lambda_k1, lambda_q2, lambda_k2,
           g_diff_sub, g_mem, w_mem_kv, w_br_diff, w_br_dsa, w_br_mem, w_out, g_ffn,
           w_route_group, b_route_group, w_route_expert, b_route_expert, w_exp_in, w_exp_out,
           g_final):
    b, s, d = x.shape
    t = b * s
    bf = jnp.bfloat16
    top_k = min(TOPK_MAX, s // 4)
    assert d == D_MODEL and s % DIFF_T == 0 and s % DSA_TQ == 0 and top_k <= DSA_TQ
    assert g_mix.shape[0] == 1, "single layer"
    lam_init = 0.8 - 0.6 * math.exp(-0.3 * 0)

    wi = w_in[0]
    c = 512
    seg = lambda k: wi[:, k * c:(k + 1) * c]
    o_ik = 7 * c
    w_ik = wi[:, o_ik:o_ik + IDX_DIM]
    w_iw = wi[:, o_ik + IDX_DIM:o_ik + IDX_DIM + IDX_HEADS]
    o_mq = o_ik + IDX_DIM + IDX_HEADS
    w_mq = wi[:, o_mq:o_mq + c]
    w_gl = wi[:, o_mq + c:]
    qs = DIFF_QK_DIM ** -0.5
    w_a = jnp.concatenate([seg(0) * qs, seg(1), seg(2), seg(3) * qs, seg(4), seg(5), seg(6) * qs, w_mq],
                          axis=1).astype(bf)
    w_s = jnp.concatenate([w_ik, w_ik, w_iw, jnp.zeros((d, LANES - IDX_HEADS), wi.dtype)], axis=1).astype(bf)
    w_g = w_gl.reshape(d, 3, d).transpose(1, 0, 2).astype(bf)
    b_g = b_gate[0].reshape(3, 1, d)
    w_br = jnp.stack([w_br_diff[0], w_br_dsa[0], w_br_mem[0]]).astype(bf)
    w_r = jnp.concatenate([w_route_group[0], w_route_expert[0],
                           jnp.zeros((d, LANES - N_GROUPS - N_EXPERTS), wi.dtype)], axis=1).astype(bf)
    b_r = jnp.concatenate([b_route_group[0], b_route_expert[0],
                           jnp.zeros((LANES - N_GROUPS - N_EXPERTS,), jnp.float32)]).reshape(1, LANES)

    rot = IDX_DIM // ROPE_FRACTION
    inv_freq = ROPE_THETA ** (-jnp.arange(0, rot, 2, dtype=jnp.float32) / rot)
    inv64 = jnp.concatenate([inv_freq, inv_freq, jnp.zeros((IDX_DIM - rot,), jnp.float32)])
    inv_lane = jnp.concatenate([inv64, inv64]).reshape(1, LANES)

    x2 = x.reshape(t, d)
    pos2 = positions.reshape(t, 1)
    p, ikk, iw = _proj(x2, pos2, g_mix, inv_lane, w_a, w_s)

    y_diff = _diff_attention(p, lambda_q1, lambda_k1, lambda_q2, lambda_k2, g_diff_sub, b, s, lam_init)
    y_dsa = _dsa_attention(p, ikk, iw, b, s, top_k)
    mkv = _memkv(mem.reshape(b * MEM_LEN, d), g_mem, w_mem_kv[0].astype(bf), b)
    y_mem = _mem_attention(p, mkv, b, s)

    h, hn, lg = _merge(x2, y_diff, y_dsa, y_mem, g_mix, w_g, b_g, w_br, w_out[0].astype(bf), g_ffn, w_r, b_r)
    out = _moe(hn, lg, h, w_exp_in[0].astype(bf), w_exp_out[0].astype(bf), g_final.reshape(1, d))
    return out.reshape(b, s, d)
```

```python
import functools
import math

import jax
import jax.numpy as jnp
from jax import lax
from jax.experimental import pallas as pl
from jax.experimental.pallas import tpu as pltpu

D_MODEL = 1024
MEM_LEN = 256
XA_HEADS = 4
XA_HEAD_DIM = 128
DIFF_HEADS = 4
DIFF_QK_DIM = 64
DIFF_V_DIM = 128
DSA_HEADS = 8
DSA_HEAD_DIM = 64
IDX_HEADS = 8
IDX_DIM = 64
TOPK_MAX = 256
ROPE_THETA = 500000.0
ROPE_FRACTION = 4
N_GROUPS = 4
EXPERTS_PER_GROUP = 4
N_EXPERTS = 16
D_EXPERT = 512
EPS = 1e-6

LANES = 128
VMEM_LIMIT = 56 * 1024 * 1024
NEG_BIG = -1e30

PROJ_TM = 512
PROJ_TN = 512
DIFF_T = 512
DSA_TQ = 256
MEM_TQ = 512
MERGE_TM = 512
MOE_TM = 512

_NT = (((1,), (1,)), ((), ()))


def _dot(a, b):
    return jnp.dot(a, b, preferred_element_type=jnp.float32)


def _dot_nt(a, b):
    return lax.dot_general(a, b, _NT, preferred_element_type=jnp.float32)


def _rms(xf, g):
    return xf * lax.rsqrt(jnp.mean(xf * xf, axis=-1, keepdims=True) + EPS) * g


def _proj_kernel(x_ref, pos_ref, g_ref, inv_ref, wa_ref, ws_ref,
                 p_ref, ikk_ref, iw_ref, dvt_ref, svt_ref, iwt_ref, n_scr, cos_scr, sa_scr, sb_scr):
    j = pl.program_id(1)

    def rope(v, reps):
        n = v.shape[1]
        c = jnp.concatenate([cos_scr[...]] * reps, axis=1) if reps > 1 else cos_scr[...]
        a = jnp.concatenate([sa_scr[...]] * reps, axis=1) if reps > 1 else sa_scr[...]
        b = jnp.concatenate([sb_scr[...]] * reps, axis=1) if reps > 1 else sb_scr[...]
        up = pltpu.roll(v, n - 8, 1)
        dn = pltpu.roll(v, 8, 1)
        return v * c + up * a + dn * b

    @pl.when(j == 0)
    def _():
        xf = x_ref[...]
        n_scr[...] = _rms(xf, g_ref[...]).astype(jnp.bfloat16)
        ang = pos_ref[...].astype(jnp.float32) * inv_ref[...]
        lane = lax.broadcasted_iota(jnp.int32, ang.shape, 1) % 64
        cs = jnp.cos(ang)
        sn = jnp.sin(ang)
        cos_scr[...] = jnp.where(lane < 16, cs, 1.0)
        sa_scr[...] = jnp.where(lane < 8, -sn, 0.0)
        sb_scr[...] = jnp.where((lane >= 8) & (lane < 16), sn, 0.0)
        small = _dot(n_scr[...], ws_ref[...])
        ikk_ref[...] = rope(small[:, :LANES], 1).astype(jnp.bfloat16)
        w = small[:, LANES:] * (IDX_HEADS ** -0.5)
        iw_ref[...] = w
        iwt_ref[...] = w.T[:IDX_HEADS, :]

    acc = _dot(n_scr[...], wa_ref[...])
    is_rope = (j == 0) | (j == 1) | (j == 3) | (j == 4) | (j == 6)

    @pl.when(is_rope)
    def _():
        p_ref[...] = rope(acc, PROJ_TN // LANES).astype(jnp.bfloat16)

    @pl.when(jnp.logical_not(is_rope))
    def _():
        p_ref[...] = acc.astype(jnp.bfloat16)

    @pl.when(j == 2)
    def _():
        dvt_ref[...] = acc.T.astype(jnp.bfloat16)

    @pl.when(j == 5)
    def _():
        svt_ref[...] = acc.T.astype(jnp.bfloat16)


def _proj(x2, pos2, g_mix, inv_lane, w_a, w_s):
    t = x2.shape[0]
    ncol = w_a.shape[1] // PROJ_TN
    return pl.pallas_call(
        _proj_kernel,
        grid=(t // PROJ_TM, ncol),
        in_specs=[
            pl.BlockSpec((PROJ_TM, D_MODEL), lambda i, j: (i, 0)),
            pl.BlockSpec((PROJ_TM, 1), lambda i, j: (i, 0)),
            pl.BlockSpec((1, D_MODEL), lambda i, j: (0, 0)),
            pl.BlockSpec((1, LANES), lambda i, j: (0, 0)),
            pl.BlockSpec((D_MODEL, PROJ_TN), lambda i, j: (0, j)),
            pl.BlockSpec((D_MODEL, 2 * LANES), lambda i, j: (0, 0)),
        ],
        out_specs=[
            pl.BlockSpec((PROJ_TM, PROJ_TN), lambda i, j: (i, j)),
            pl.BlockSpec((PROJ_TM, LANES), lambda i, j: (i, 0)),
            pl.BlockSpec((PROJ_TM, LANES), lambda i, j: (i, 0)),
            pl.BlockSpec((PROJ_TN, PROJ_TM), lambda i, j: (0, i)),
            pl.BlockSpec((PROJ_TN, PROJ_TM), lambda i, j: (0, i)),
            pl.BlockSpec((IDX_HEADS, PROJ_TM), lambda i, j: (0, i)),
        ],
        out_shape=[
            jax.ShapeDtypeStruct((t, w_a.shape[1]), jnp.bfloat16),
            jax.ShapeDtypeStruct((t, LANES), jnp.bfloat16),
            jax.ShapeDtypeStruct((t, LANES), jnp.float32),
            jax.ShapeDtypeStruct((PROJ_TN, t), jnp.bfloat16),
            jax.ShapeDtypeStruct((PROJ_TN, t), jnp.bfloat16),
            jax.ShapeDtypeStruct((IDX_HEADS, t), jnp.float32),
        ],
        scratch_shapes=[
            pltpu.VMEM((PROJ_TM, D_MODEL), jnp.bfloat16),
            pltpu.VMEM((PROJ_TM, LANES), jnp.float32),
            pltpu.VMEM((PROJ_TM, LANES), jnp.float32),
            pltpu.VMEM((PROJ_TM, LANES), jnp.float32),
        ],
        compiler_params=pltpu.CompilerParams(
            dimension_semantics=("parallel", "arbitrary"), vmem_limit_bytes=VMEM_LIMIT),
        name="proj",
    )(x2, pos2, g_mix, inv_lane, w_a, w_s)


def _diff_kernel(qi_tab, ki_tab, q_ref, k_ref, vt_ref, lq1_ref, lk1_ref, lq2_ref, lk2_ref, gs_ref, o_ref,
                 qlo, qhi, m1, l1, m2, l2, acc1, acc2, *, lam_init):
    qi = qi_tab[pl.program_id(2)]
    ki = ki_tab[pl.program_id(2)]

    @pl.when(ki == 0)
    def _():
        q = q_ref[...]
        lane = lax.broadcasted_iota(jnp.int32, q.shape, 1)
        qlo[...] = jnp.where(lane < DIFF_QK_DIM, q, jnp.zeros_like(q))
        qhi[...] = jnp.where(lane >= DIFF_QK_DIM, q, jnp.zeros_like(q))
        m1[...] = jnp.full(m1.shape, NEG_BIG, jnp.float32)
        m2[...] = jnp.full(m2.shape, NEG_BIG, jnp.float32)
        l1[...] = jnp.zeros(l1.shape, jnp.float32)
        l2[...] = jnp.zeros(l2.shape, jnp.float32)
        acc1[...] = jnp.zeros(acc1.shape, jnp.float32)
        acc2[...] = jnp.zeros(acc2.shape, jnp.float32)

    def step(diagonal):
        k = k_ref[...]
        vt = vt_ref[...]
        if diagonal:
            krow = lax.broadcasted_iota(jnp.int32, (DIFF_T, DIFF_T), 0)
            qcol = lax.broadcasted_iota(jnp.int32, (DIFF_T, DIFF_T), 1)
            keep = krow <= qcol
        for qm, m_r, l_r, a_r in ((qlo, m1, l1, acc1), (qhi, m2, l2, acc2)):
            st = _dot_nt(k, qm[...])
            if diagonal:
                st = jnp.where(keep, st, NEG_BIG)
            m_old = m_r[...]
            m_new = jnp.maximum(m_old, jnp.max(st, axis=0, keepdims=True))
            alpha = jnp.exp(m_old - m_new)
            p = jnp.exp(st - m_new)
            l_r[...] = alpha * l_r[...] + jnp.sum(p, axis=0, keepdims=True)
            a_r[...] = alpha * a_r[...] + _dot(vt, p.astype(jnp.bfloat16))
            m_r[...] = m_new

    @pl.when(ki < qi)
    def _():
        step(False)

    @pl.when(ki == qi)
    def _():
        step(True)
        lam = (jnp.exp(jnp.sum(lq1_ref[...] * lk1_ref[...], axis=1, keepdims=True))
               - jnp.exp(jnp.sum(lq2_ref[...] * lk2_ref[...], axis=1, keepdims=True))
               + lam_init)
        ot = acc1[...] / l1[...] - lam * (acc2[...] / l2[...])
        yt = ot * lax.rsqrt(jnp.mean(ot * ot, axis=0, keepdims=True) + EPS) * gs_ref[...]
        o_ref[...] = (yt * (1.0 - lam_init)).T.astype(o_ref.dtype)


def _diff_attention(p, dvt, lq1, lk1, lq2, lk2, g_sub_col, batch, seq, lam_init):
    nb = seq // DIFF_T
    pairs = [(qi, ki) for qi in range(nb) for ki in range(qi + 1)]
    qi_tab = jnp.asarray([a for a, _ in pairs], jnp.int32)
    ki_tab = jnp.asarray([c for _, c in pairs], jnp.int32)
    vec = pl.BlockSpec((1, DIFF_QK_DIM), lambda b, h, s, qt, kt: (0, 0))
    grid_spec = pltpu.PrefetchScalarGridSpec(
        num_scalar_prefetch=2,
        grid=(batch, DIFF_HEADS, len(pairs)),
        in_specs=[
            pl.BlockSpec((DIFF_T, LANES), lambda b, h, s, qt, kt: (b * nb + qt[s], h)),
            pl.BlockSpec((DIFF_T, LANES), lambda b, h, s, qt, kt: (b * nb + kt[s], DIFF_HEADS + h)),
            pl.BlockSpec((DIFF_V_DIM, DIFF_T), lambda b, h, s, qt, kt: (h, b * nb + kt[s])),
            vec, vec, vec, vec,
            pl.BlockSpec((DIFF_V_DIM, 1), lambda b, h, s, qt, kt: (0, 0)),
        ],
        out_specs=pl.BlockSpec((DIFF_T, LANES), lambda b, h, s, qt, kt: (b * nb + qt[s], h)),
        scratch_shapes=[
            pltpu.VMEM((DIFF_T, LANES), jnp.bfloat16),
            pltpu.VMEM((DIFF_T, LANES), jnp.bfloat16),
            pltpu.VMEM((1, DIFF_T), jnp.float32),
            pltpu.VMEM((1, DIFF_T), jnp.float32),
            pltpu.VMEM((1, DIFF_T), jnp.float32),
            pltpu.VMEM((1, DIFF_T), jnp.float32),
            pltpu.VMEM((DIFF_V_DIM, DIFF_T), jnp.float32),
            pltpu.VMEM((DIFF_V_DIM, DIFF_T), jnp.float32),
        ],
    )
    return pl.pallas_call(
        functools.partial(_diff_kernel, lam_init=lam_init),
        grid_spec=grid_spec,
        out_shape=jax.ShapeDtypeStruct((batch * seq, DIFF_HEADS * DIFF_V_DIM), jnp.bfloat16),
        compiler_params=pltpu.CompilerParams(
            dimension_semantics=("parallel", "parallel", "arbitrary"),
            vmem_limit_bytes=VMEM_LIMIT),
        name="diffattn",
    )(qi_tab, ki_tab, p, p, dvt, lq1, lk1, lq2, lk2, g_sub_col)


def _key_to_float(key):
    bits = jnp.where(key >= 0, key, key ^ jnp.int32(0x7FFFFFFF))
    return lax.bitcast_convert_type(bits, jnp.float32)


def _dsa_kernel(sq_ref, iq_ref, iw_ref, ikk_ref, sk_ref, sv_ref, o_ref,
                score, iqm, sqm, wb, acc, m_s, l_s, thr, jcut, *, seq, top_k):
    qi = pl.program_id(1)
    tq = DSA_TQ
    nkc = qi + 1
    lane = lax.broadcasted_iota(jnp.int32, (tq, LANES), 1)
    half = lane // DSA_HEAD_DIM

    for h in range(DSA_HEADS):
        pr = h // 2
        iqp = iq_ref[:, pr * LANES:(pr + 1) * LANES]
        sqp = sq_ref[:, pr * LANES:(pr + 1) * LANES]
        iqm[h] = jnp.where(half == h % 2, iqp, jnp.zeros_like(iqp))
        sqm[h] = jnp.where(half == h % 2, sqp, jnp.zeros_like(sqp))
        wb[h] = jnp.broadcast_to(iw_ref[:, h:h + 1], (tq, tq))

    row_l = lax.broadcasted_iota(jnp.int32, (tq, tq), 0)
    col_l = lax.broadcasted_iota(jnp.int32, (tq, tq), 1)

    def score_chunk(c, carry):
        kk = ikk_ref[pl.ds(pl.multiple_of(c * tq, tq), tq), :]
        sc = jnp.zeros((tq, tq), jnp.float32)
        for h in range(IDX_HEADS):
            sc = sc + wb[h] * jnp.maximum(_dot_nt(iqm[h], kk), 0.0)
        sc = jnp.where(col_l <= row_l + (qi - c) * tq, sc, -jnp.inf)
        score[c] = sc
        return carry

    lax.fori_loop(0, nkc, score_chunk, 0)

    def count_ge(cand_f):
        def body(c, cnt):
            blk = score[c]
            hit = jnp.where(blk >= cand_f, 1.0, 0.0)
            return cnt + hit[:, :LANES] + hit[:, LANES:]
        cnt = lax.fori_loop(0, nkc, body, jnp.zeros((tq, LANES), jnp.float32))
        return jnp.sum(cnt, axis=1, keepdims=True)

    def bit_step(i, key):
        cand = key + lax.shift_left(jnp.int32(1), jnp.int32(31) - i)
        ok = count_ge(_key_to_float(cand)) >= float(top_k)
        return jnp.where(ok, cand, key)

    int_min = jnp.full((tq, 1), -2 ** 31, jnp.int32)
    key = lax.fori_loop(0, 32, bit_step, int_min)
    t_f = _key_to_float(key)

    def count_gt_ge(c, carry):
        gt, ge = carry
        blk = score[c]
        a = jnp.where(blk > t_f, 1.0, 0.0)
        b = jnp.where(blk >= t_f, 1.0, 0.0)
        return gt + a[:, :LANES] + a[:, LANES:], ge + b[:, :LANES] + b[:, LANES:]

    z = jnp.zeros((tq, LANES), jnp.float32)
    gt, ge = lax.fori_loop(0, nkc, count_gt_ge, (z, z))
    n_gt = jnp.sum(gt, axis=1, keepdims=True)
    n_ge = jnp.sum(ge, axis=1, keepdims=True)
    row_pos = qi * tq + lax.broadcasted_iota(jnp.int32, (tq, 1), 0)
    few = row_pos < top_k - 1
    thr[...] = jnp.where(few, -jnp.inf, t_f)
    jcut[...] = jnp.where(few, -1, seq)
    need = float(top_k) - n_gt
    split = jnp.logical_and(jnp.logical_not(few), n_ge > float(top_k))

    @pl.when(jnp.max(jnp.where(split, 1.0, 0.0)) > 0.0)
    def _():
        def count_eq_below(jc):
            def body(c, cnt):
                blk = score[c]
                hit = jnp.where((blk == t_f) & (c * tq + col_l < jc), 1.0, 0.0)
                return cnt + hit[:, :LANES] + hit[:, LANES:]
            cnt = lax.fori_loop(0, nkc, body, jnp.zeros((tq, LANES), jnp.float32))
            return jnp.sum(cnt, axis=1, keepdims=True)

        nbits = (seq - 1).bit_length()

        def jbit(i, jv):
            cand = jv + lax.shift_left(jnp.int32(1), jnp.int32(nbits - 1) - i)
            return jnp.where(count_eq_below(cand) < need, cand, jv)

        jv = lax.fori_loop(0, nbits, jbit, jnp.zeros((tq, 1), jnp.int32))
        jcut[...] = jnp.where(split, jv, jcut[...])

    m_s[...] = jnp.full(m_s.shape, NEG_BIG, jnp.float32)
    l_s[...] = jnp.zeros(l_s.shape, jnp.float32)
    acc[...] = jnp.zeros(acc.shape, jnp.float32)

    def attend(c, carry):
        off = pl.multiple_of(c * tq, tq)
        blk = score[c]
        tt = thr[...]
        keep = (blk > tt) | ((blk == tt) & (c * tq + col_l <= jcut[...]))
        bias = jnp.where(keep, 0.0, NEG_BIG)
        for pr in range(DSA_HEADS // 2):
            kp = sk_ref[pl.ds(off, tq), pr * LANES:(pr + 1) * LANES]
            vp = sv_ref[pl.ds(off, tq), pr * LANES:(pr + 1) * LANES]
            upd = []
            for hh in range(2):
                h = 2 * pr + hh
                s = _dot_nt(sqm[h], kp) + bias
                m_old = m_s[h]
                m_new = jnp.maximum(m_old, jnp.max(s, axis=1, keepdims=True))
                alpha = jnp.exp(m_old - m_new)
                p = jnp.exp(s - m_new)
                l_s[h] = alpha * l_s[h] + jnp.sum(p, axis=1, keepdims=True)
                m_s[h] = m_new
                upd.append((alpha, _dot(p.astype(jnp.bfloat16), vp)))
            a_sel = jnp.where(half == 0, upd[0][0], upd[1][0])
            pv_sel = jnp.where(half == 0, upd[0][1], upd[1][1])
            acc[pr] = a_sel * acc[pr] + pv_sel
        return carry

    lax.fori_loop(0, nkc, attend, 0)

    outs = []
    for pr in range(DSA_HEADS // 2):
        l_sel = jnp.where(half == 0, l_s[2 * pr], l_s[2 * pr + 1])
        outs.append(acc[pr] / l_sel)
    o_ref[...] = jnp.concatenate(outs, axis=1).astype(o_ref.dtype)


def _dsa_attention(p, ikk, iw, batch, seq, top_k):
    nq = seq // DSA_TQ
    w = DSA_HEADS * DSA_HEAD_DIM
    once = pl.Buffered(1)
    return pl.pallas_call(
        functools.partial(_dsa_kernel, seq=seq, top_k=top_k),
        grid=(batch, nq),
        in_specs=[
            pl.BlockSpec((DSA_TQ, w), lambda b, qi: (b * nq + qi, 3)),
            pl.BlockSpec((DSA_TQ, w), lambda b, qi: (b * nq + qi, 6)),
            pl.BlockSpec((DSA_TQ, LANES), lambda b, qi: (b * nq + qi, 0)),
            pl.BlockSpec((seq, LANES), lambda b, qi: (b, 0), pipeline_mode=once),
            pl.BlockSpec((seq, w), lambda b, qi: (b, 4), pipeline_mode=once),
            pl.BlockSpec((seq, w), lambda b, qi: (b, 5), pipeline_mode=once),
        ],
        out_specs=pl.BlockSpec((DSA_TQ, w), lambda b, qi: (b * nq + qi, 0)),
        out_shape=jax.ShapeDtypeStruct((batch * seq, w), jnp.bfloat16),
        scratch_shapes=[
            pltpu.VMEM((nq, DSA_TQ, DSA_TQ), jnp.float32),
            pltpu.VMEM((IDX_HEADS, DSA_TQ, LANES), jnp.bfloat16),
            pltpu.VMEM((DSA_HEADS, DSA_TQ, LANES), jnp.bfloat16),
            pltpu.VMEM((IDX_HEADS, DSA_TQ, DSA_TQ), jnp.float32),
            pltpu.VMEM((DSA_HEADS // 2, DSA_TQ, LANES), jnp.float32),
            pltpu.VMEM((DSA_HEADS, DSA_TQ, 1), jnp.float32),
            pltpu.VMEM((DSA_HEADS, DSA_TQ, 1), jnp.float32),
            pltpu.VMEM((DSA_TQ, 1), jnp.float32),
            pltpu.VMEM((DSA_TQ, 1), jnp.int32),
        ],
        compiler_params=pltpu.CompilerParams(
            dimension_semantics=("parallel", "arbitrary"), vmem_limit_bytes=VMEM_LIMIT),
        name="dsa",
    )(p, p, iw, ikk, p, p)


def _memkv_kernel(mem_ref, g_ref, w_ref, o_ref):
    n = _rms(mem_ref[...], g_ref[...]).astype(jnp.bfloat16)
    o_ref[...] = _dot(n, w_ref[...]).astype(o_ref.dtype)


def _memkv(mem2, g_mem, w_kv, batch):
    return pl.pallas_call(
        _memkv_kernel,
        grid=(batch,),
        in_specs=[
            pl.BlockSpec((MEM_LEN, D_MODEL), lambda b: (b, 0)),
            pl.BlockSpec((1, D_MODEL), lambda b: (0, 0)),
            pl.BlockSpec(w_kv.shape, lambda b: (0, 0)),
        ],
        out_specs=pl.BlockSpec((MEM_LEN, w_kv.shape[1]), lambda b: (b, 0)),
        out_shape=jax.ShapeDtypeStruct((batch * MEM_LEN, w_kv.shape[1]), jnp.bfloat16),
        compiler_params=pltpu.CompilerParams(
            dimension_semantics=("parallel",), vmem_limit_bytes=VMEM_LIMIT),
        name="memkv",
    )(mem2, g_mem, w_kv)


def _memattn_kernel(q_ref, kv_ref, o_ref):
    scale = XA_HEAD_DIM ** -0.5
    outs = []
    for h in range(XA_HEADS):
        q = q_ref[:, h * LANES:(h + 1) * LANES]
        k = kv_ref[:, h * LANES:(h + 1) * LANES]
        v = kv_ref[:, (XA_HEADS + h) * LANES:(XA_HEADS + h + 1) * LANES]
        s = _dot_nt(q, k) * scale
        m = jnp.max(s, axis=1, keepdims=True)
        p = jnp.exp(s - m)
        l = jnp.sum(p, axis=1, keepdims=True)
        outs.append(_dot((p / l).astype(jnp.bfloat16), v))
    o_ref[...] = jnp.concatenate(outs, axis=1).astype(o_ref.dtype)


def _mem_attention(p, mkv, batch, seq):
    nq = seq // MEM_TQ
    w = XA_HEADS * XA_HEAD_DIM
    return pl.pallas_call(
        _memattn_kernel,
        grid=(batch, nq),
        in_specs=[
            pl.BlockSpec((MEM_TQ, w), lambda b, qi: (b * nq + qi, 7)),
            pl.BlockSpec((MEM_LEN, 2 * w), lambda b, qi: (b, 0)),
        ],
        out_specs=pl.BlockSpec((MEM_TQ, w), lambda b, qi: (b * nq + qi, 0)),
        out_shape=jax.ShapeDtypeStruct((batch * seq, w), jnp.bfloat16),
        compiler_params=pltpu.CompilerParams(
            dimension_semantics=("parallel", "parallel"), vmem_limit_bytes=VMEM_LIMIT),
        name="memattn",
    )(p, mkv)


def _merge_kernel(x_ref, yd_ref, ys_ref, ym_ref, gmix_ref, wg_ref, bg_ref, wbr_ref, wout_ref,
                  gffn_ref, wr_ref, br_ref, h_ref, hn_ref, lg_ref):
    xf = x_ref[...]
    n = _rms(xf, gmix_ref[...]).astype(jnp.bfloat16)
    merged = jnp.zeros(xf.shape, jnp.float32)
    for i, y_ref in enumerate((yd_ref, ys_ref, ym_ref)):
        gate = jax.nn.sigmoid(_dot(n, wg_ref[i]) + bg_ref[i])
        merged = merged + gate * _dot(y_ref[...], wbr_ref[i])
    h = xf + _dot(merged.astype(jnp.bfloat16), wout_ref[...])
    h_ref[...] = h
    hn = _rms(h, gffn_ref[...]).astype(jnp.bfloat16)
    hn_ref[...] = hn
    lg_ref[...] = _dot(hn, wr_ref[...]) + br_ref[...]


def _merge(x2, y_diff, y_dsa, y_mem, g_mix, w_g, b_g, w_br, w_out, g_ffn, w_r, b_r):
    t = x2.shape[0]
    row = lambda w: pl.BlockSpec((MERGE_TM, w), lambda i: (i, 0))
    full = lambda a: pl.BlockSpec(a.shape, lambda i: (0,) * a.ndim, pipeline_mode=pl.Buffered(1))
    return pl.pallas_call(
        _merge_kernel,
        grid=(t // MERGE_TM,),
        in_specs=[row(D_MODEL), row(512), row(512), row(512), full(g_mix), full(w_g), full(b_g),
                  full(w_br), full(w_out), full(g_ffn), full(w_r), full(b_r)],
        out_specs=[row(D_MODEL), row(D_MODEL), row(LANES)],
        out_shape=[
            jax.ShapeDtypeStruct((t, D_MODEL), jnp.float32),
            jax.ShapeDtypeStruct((t, D_MODEL), jnp.bfloat16),
            jax.ShapeDtypeStruct((t, LANES), jnp.float32),
        ],
        compiler_params=pltpu.CompilerParams(
            dimension_semantics=("parallel",), vmem_limit_bytes=VMEM_LIMIT),
        name="merge",
    )(x2, y_diff, y_dsa, y_mem, g_mix, w_g, b_g, w_br, w_out, g_ffn, w_r, b_r)


def _route_weights(lg):
    lane = lax.broadcasted_iota(jnp.int32, lg.shape, 1).astype(jnp.float32)
    big = float(LANES)
    gl = jnp.where(lane < N_GROUPS, lg, -jnp.inf)
    gmax = jnp.max(gl, axis=1, keepdims=True)
    grp = jnp.min(jnp.where(gl == gmax, lane, big), axis=1, keepdims=True)
    gsum = jnp.sum(jnp.where(lane < N_GROUPS, jnp.exp(gl - gmax), 0.0), axis=1, keepdims=True)
    p_grp = 1.0 / gsum
    lo = N_GROUPS + grp * EXPERTS_PER_GROUP
    el = jnp.where((lane >= lo) & (lane < lo + EXPERTS_PER_GROUP), lg, -jnp.inf)
    e1 = jnp.max(el, axis=1, keepdims=True)
    i1 = jnp.min(jnp.where(el == e1, lane, big), axis=1, keepdims=True)
    el2 = jnp.where(lane == i1, -jnp.inf, el)
    e2 = jnp.max(el2, axis=1, keepdims=True)
    i2 = jnp.min(jnp.where(el2 == e2, lane, big), axis=1, keepdims=True)
    r = jnp.exp(e2 - e1)
    w1 = p_grp / (1.0 + r)
    w2 = p_grp * r / (1.0 + r)
    return jnp.where(lane == i1, w1, 0.0) + jnp.where(lane == i2, w2, 0.0)


def _moe_kernel(hn_ref, lg_ref, h_ref, win_ref, wout_ref, gfin_ref, o_ref, acc, cw):
    e = pl.program_id(1)

    @pl.when(e == 0)
    def _():
        acc[...] = jnp.zeros(acc.shape, jnp.float32)
        cw[...] = _route_weights(lg_ref[...])

    lane = lax.broadcasted_iota(jnp.int32, cw.shape, 1)
    cw_e = jnp.sum(jnp.where(lane == N_GROUPS + e, cw[...], 0.0), axis=1, keepdims=True)
    gu = _dot(hn_ref[...], win_ref[0])
    g = gu[:, :D_EXPERT]
    hid = (g * jax.nn.sigmoid(g)) * gu[:, D_EXPERT:]
    acc[...] += cw_e * _dot(hid.astype(jnp.bfloat16), wout_ref[0])

    @pl.when(e == N_EXPERTS - 1)
    def _():
        o_ref[...] = _rms(h_ref[...] + acc[...], gfin_ref[...])


def _moe(hn, lg, h, w_e_in, w_e_out, g_final):
    t = hn.shape[0]
    return pl.pallas_call(
        _moe_kernel,
        grid=(t // MOE_TM, N_EXPERTS),
        in_specs=[
            pl.BlockSpec((MOE_TM, D_MODEL), lambda i, e: (i, 0)),
            pl.BlockSpec((MOE_TM, LANES), lambda i, e: (i, 0)),
            pl.BlockSpec((MOE_TM, D_MODEL), lambda i, e: (i, 0)),
            pl.BlockSpec((1, D_MODEL, 2 * D_EXPERT), lambda i, e: (e, 0, 0)),
            pl.BlockSpec((1, D_EXPERT, D_MODEL), lambda i, e: (e, 0, 0)),
            pl.BlockSpec((1, D_MODEL), lambda i, e: (0, 0)),
        ],
        out_specs=pl.BlockSpec((MOE_TM, D_MODEL), lambda i, e: (i, 0)),
        out_shape=jax.ShapeDtypeStruct((t, D_MODEL), jnp.float32),
        scratch_shapes=[
            pltpu.VMEM((MOE_TM, D_MODEL), jnp.float32),
            pltpu.VMEM((MOE_TM, LANES), jnp.float32),
        ],
        compiler_params=pltpu.CompilerParams(
            dimension_semantics=("parallel", "arbitrary"), vmem_limit_bytes=VMEM_LIMIT),
        name="moe",
    )(hn, lg, h, w_e_in, w_e_out, g_final)


def kernel(x, positions, mem, g_mix, w_in, b_gate, lambda_q1, lambda_k1, lambda_q2, lambda_k2,
           g_diff_sub, g_mem, w_mem_kv, w_br_diff, w_br_dsa, w_br_mem, w_out, g_ffn,
           w_route_group, b_route_group, w_route_expert, b_route_expert, w_exp_in, w_exp_out,
           g_final):
    b, s, d = x.shape
    t = b * s
    bf = jnp.bfloat16
    top_k = min(TOPK_MAX, s // 4)
    assert d == D_MODEL and s % DIFF_T == 0 and s % DSA_TQ == 0 and top_k <= DSA_TQ
    assert g_mix.shape[0] == 1, "single layer"
    lam_init = 0.8 - 0.6 * math.exp(-0.3 * 0)

    wi = w_in[0]
    c = 512
    seg = lambda k: wi[:, k * c:(k + 1) * c]
    o_ik = 7 * c
    w_ik = wi[:, o_ik:o_ik + IDX_DIM]
    w_iw = wi[:, o_ik + IDX_DIM:o_ik + IDX_DIM + IDX_HEADS]
    o_mq = o_ik + IDX_DIM + IDX_HEADS
    w_mq = wi[:, o_mq:o_mq + c]
    w_gl = wi[:, o_mq + c:]
    qs = DIFF_QK_DIM ** -0.5
    w_a = jnp.concatenate([seg(0) * qs, seg(1), seg(2), seg(3) * qs, seg(4), seg(5), seg(6) * qs, w_mq],
                          axis=1).astype(bf)
    w_s = jnp.concatenate([w_ik, w_ik, w_iw, jnp.zeros((d, LANES - IDX_HEADS), wi.dtype)], axis=1).astype(bf)
    w_g = w_gl.reshape(d, 3, d).transpose(1, 0, 2).astype(bf)
    b_g = b_gate[0].reshape(3, 1, d)
    w_br = jnp.stack([w_br_diff[0], w_br_dsa[0], w_br_mem[0]]).astype(bf)
    w_r = jnp.concatenate([w_route_group[0], w_route_expert[0],
                           jnp.zeros((d, LANES - N_GROUPS - N_EXPERTS), wi.dtype)], axis=1).astype(bf)
    b_r = jnp.concatenate([b_route_group[0], b_route_expert[0],
                           jnp.zeros((LANES - N_GROUPS - N_EXPERTS,), jnp.float32)]).reshape(1, LANES)

    rot = IDX_DIM // ROPE_FRACTION
    inv_freq = ROPE_THETA ** (-jnp.arange(0, rot, 2, dtype=jnp.float32) / rot)
    inv64 = jnp.concatenate([inv_freq, inv_freq, jnp.zeros((IDX_DIM - rot,), jnp.float32)])
    inv_lane = jnp.concatenate([inv64, inv64]).reshape(1, LANES)

    x2 = x.reshape(t, d)
    pos2 = positions.reshape(t, 1)
    p, ikk, iw, dvt, svt, iwt = _proj(x2, pos2, g_mix, inv_lane, w_a, w_s)

    y_diff = _diff_attention(p, dvt, lambda_q1, lambda_k1, lambda_q2, lambda_k2,
                             g_diff_sub.reshape(DIFF_V_DIM, 1), b, s, lam_init)
    y_dsa = _dsa_attention(p, ikk, iw, b, s, top_k)
    mkv = _memkv(mem.reshape(b * MEM_LEN, d), g_mem, w_mem_kv[0].astype(bf), b)
    y_mem = _mem_attention(p, mkv, b, s)

    h, hn, lg = _merge(x2, y_diff, y_dsa, y_mem, g_mix, w_g, b_g, w_br, w_out[0].astype(bf), g_ffn, w_r, b_r)
    out = _moe(hn, lg, h, w_exp_in[0].astype(bf), w_exp_out[0].astype(bf), g_final.reshape(1, d))
    return out.reshape(b, s, d)
```

```python
import functools
import math

import jax
import jax.numpy as jnp
from jax import lax
from jax.experimental import pallas as pl
from jax.experimental.pallas import tpu as pltpu

D_MODEL = 1024
MEM_LEN = 256
XA_HEADS = 4
XA_HEAD_DIM = 128
DIFF_HEADS = 4
DIFF_QK_DIM = 64
DIFF_V_DIM = 128
DSA_HEADS = 8
DSA_HEAD_DIM = 64
IDX_HEADS = 8
IDX_DIM = 64
TOPK_MAX = 256
ROPE_THETA = 500000.0
ROPE_FRACTION = 4
N_GROUPS = 4
EXPERTS_PER_GROUP = 4
N_EXPERTS = 16
D_EXPERT = 512
EPS = 1e-6

LANES = 128
SUBLANES = 8
VMEM_LIMIT = 56 * 1024 * 1024
NEG_BIG = -1e30

PROJ_TM = 512
PROJ_TN = 512
DIFF_T = 512
DSA_TQ = 256
MEM_TQ = 512
MERGE_TM = 512
MOE_TM = 512

_NT = (((1,), (1,)), ((), ()))


def _dot(a, b):
    return jnp.dot(a, b, preferred_element_type=jnp.float32)


def _dot_nt(a, b):
    return lax.dot_general(a, b, _NT, preferred_element_type=jnp.float32)


def _rms(xf, g):
    return xf * lax.rsqrt(jnp.mean(xf * xf, axis=-1, keepdims=True) + EPS) * g


def _proj_kernel(x_ref, pos_ref, g_ref, inv_ref, wa_ref, ws_ref,
                 p_ref, ikk_ref, dvt_ref, svt_ref, iwt_ref, n_scr, cos_scr, sa_scr, sb_scr):
    j = pl.program_id(1)

    def rope(v, reps):
        n = v.shape[1]
        c = jnp.concatenate([cos_scr[...]] * reps, axis=1) if reps > 1 else cos_scr[...]
        a = jnp.concatenate([sa_scr[...]] * reps, axis=1) if reps > 1 else sa_scr[...]
        b = jnp.concatenate([sb_scr[...]] * reps, axis=1) if reps > 1 else sb_scr[...]
        up = pltpu.roll(v, n - 8, 1)
        dn = pltpu.roll(v, 8, 1)
        return v * c + up * a + dn * b

    @pl.when(j == 0)
    def _():
        xf = x_ref[...]
        n_scr[...] = _rms(xf, g_ref[...]).astype(jnp.bfloat16)
        ang = pos_ref[...].astype(jnp.float32) * inv_ref[...]
        lane = lax.broadcasted_iota(jnp.int32, ang.shape, 1) % 64
        cs = jnp.cos(ang)
        sn = jnp.sin(ang)
        cos_scr[...] = jnp.where(lane < 16, cs, 1.0)
        sa_scr[...] = jnp.where(lane < 8, -sn, 0.0)
        sb_scr[...] = jnp.where((lane >= 8) & (lane < 16), sn, 0.0)
        small = _dot(n_scr[...], ws_ref[...])
        ikk_ref[...] = rope(small[:, :LANES], 1).astype(jnp.bfloat16)
        w = small[:, LANES:] * (IDX_HEADS ** -0.5)
        iwt_ref[...] = w.T[:IDX_HEADS, :]

    acc = _dot(n_scr[...], wa_ref[...])
    is_rope = (j == 0) | (j == 1) | (j == 3) | (j == 4) | (j == 6)

    @pl.when(is_rope)
    def _():
        p_ref[...] = rope(acc, PROJ_TN // LANES).astype(jnp.bfloat16)

    @pl.when(jnp.logical_not(is_rope))
    def _():
        p_ref[...] = acc.astype(jnp.bfloat16)

    @pl.when(j == 2)
    def _():
        dvt_ref[...] = acc.T.astype(jnp.bfloat16)

    @pl.when(j == 5)
    def _():
        at = acc.T.astype(jnp.bfloat16)
        for r in range(PROJ_TM // DSA_TQ):
            svt_ref[r] = at[:, r * DSA_TQ:(r + 1) * DSA_TQ]


def _proj(x2, pos2, g_mix, inv_lane, w_a, w_s):
    t = x2.shape[0]
    ncol = w_a.shape[1] // PROJ_TN
    cpt = PROJ_TM // DSA_TQ
    return pl.pallas_call(
        _proj_kernel,
        grid=(t // PROJ_TM, ncol),
        in_specs=[
            pl.BlockSpec((PROJ_TM, D_MODEL), lambda i, j: (i, 0)),
            pl.BlockSpec((PROJ_TM, 1), lambda i, j: (i, 0)),
            pl.BlockSpec((1, D_MODEL), lambda i, j: (0, 0)),
            pl.BlockSpec((1, LANES), lambda i, j: (0, 0)),
            pl.BlockSpec((D_MODEL, PROJ_TN), lambda i, j: (0, j)),
            pl.BlockSpec((D_MODEL, 2 * LANES), lambda i, j: (0, 0)),
        ],
        out_specs=[
            pl.BlockSpec((PROJ_TM, PROJ_TN), lambda i, j: (i, j)),
            pl.BlockSpec((PROJ_TM, LANES), lambda i, j: (i, 0)),
            pl.BlockSpec((PROJ_TN, PROJ_TM), lambda i, j: (0, i)),
            pl.BlockSpec((cpt, PROJ_TN, DSA_TQ), lambda i, j: (i, 0, 0)),
            pl.BlockSpec((IDX_HEADS, PROJ_TM), lambda i, j: (0, i)),
        ],
        out_shape=[
            jax.ShapeDtypeStruct((t, w_a.shape[1]), jnp.bfloat16),
            jax.ShapeDtypeStruct((t, LANES), jnp.bfloat16),
            jax.ShapeDtypeStruct((PROJ_TN, t), jnp.bfloat16),
            jax.ShapeDtypeStruct((t // DSA_TQ, PROJ_TN, DSA_TQ), jnp.bfloat16),
            jax.ShapeDtypeStruct((IDX_HEADS, t), jnp.float32),
        ],
        scratch_shapes=[
            pltpu.VMEM((PROJ_TM, D_MODEL), jnp.bfloat16),
            pltpu.VMEM((PROJ_TM, LANES), jnp.float32),
            pltpu.VMEM((PROJ_TM, LANES), jnp.float32),
            pltpu.VMEM((PROJ_TM, LANES), jnp.float32),
        ],
        compiler_params=pltpu.CompilerParams(
            dimension_semantics=("parallel", "arbitrary"), vmem_limit_bytes=VMEM_LIMIT),
        name="proj",
    )(x2, pos2, g_mix, inv_lane, w_a, w_s)


def _diff_kernel(qi_tab, ki_tab, q_ref, k_ref, vt_ref, lq1_ref, lk1_ref, lq2_ref, lk2_ref, gs_ref, o_ref,
                 qlo, qhi, m1, l1, m2, l2, acc1, acc2, *, lam_init):
    qi = qi_tab[pl.program_id(2)]
    ki = ki_tab[pl.program_id(2)]

    @pl.when(ki == 0)
    def _():
        q = q_ref[...]
        lane = lax.broadcasted_iota(jnp.int32, q.shape, 1)
        qlo[...] = jnp.where(lane < DIFF_QK_DIM, q, jnp.zeros_like(q))
        qhi[...] = jnp.where(lane >= DIFF_QK_DIM, q, jnp.zeros_like(q))
        m1[...] = jnp.full(m1.shape, NEG_BIG, jnp.float32)
        m2[...] = jnp.full(m2.shape, NEG_BIG, jnp.float32)
        l1[...] = jnp.zeros(l1.shape, jnp.float32)
        l2[...] = jnp.zeros(l2.shape, jnp.float32)
        acc1[...] = jnp.zeros(acc1.shape, jnp.float32)
        acc2[...] = jnp.zeros(acc2.shape, jnp.float32)

    def step(diagonal):
        k = k_ref[...]
        vt = vt_ref[...]
        if diagonal:
            krow = lax.broadcasted_iota(jnp.int32, (DIFF_T, DIFF_T), 0)
            qcol = lax.broadcasted_iota(jnp.int32, (DIFF_T, DIFF_T), 1)
            keep = krow <= qcol
        for qm, m_r, l_r, a_r in ((qlo, m1, l1, acc1), (qhi, m2, l2, acc2)):
            st = _dot_nt(k, qm[...])
            if diagonal:
                st = jnp.where(keep, st, NEG_BIG)
            m_old = m_r[...]
            m_new = jnp.maximum(m_old, jnp.max(st, axis=0, keepdims=True))
            alpha = jnp.exp(m_old - m_new)
            p = jnp.exp(st - m_new)
            l_r[...] = alpha * l_r[...] + jnp.sum(p, axis=0, keepdims=True)
            a_r[...] = alpha * a_r[...] + _dot(vt, p.astype(jnp.bfloat16))
            m_r[...] = m_new

    @pl.when(ki < qi)
    def _():
        step(False)

    @pl.when(ki == qi)
    def _():
        step(True)
        lam = (jnp.exp(jnp.sum(lq1_ref[...] * lk1_ref[...], axis=1, keepdims=True))
               - jnp.exp(jnp.sum(lq2_ref[...] * lk2_ref[...], axis=1, keepdims=True))
               + lam_init)
        ot = acc1[...] / l1[...] - lam * (acc2[...] / l2[...])
        yt = ot * lax.rsqrt(jnp.mean(ot * ot, axis=0, keepdims=True) + EPS) * gs_ref[...]
        o_ref[...] = (yt * (1.0 - lam_init)).T.astype(o_ref.dtype)


def _diff_attention(p, dvt, lq1, lk1, lq2, lk2, g_sub_col, batch, seq, lam_init):
    nb = seq // DIFF_T
    pairs = [(qi, ki) for qi in range(nb) for ki in range(qi + 1)]
    qi_tab = jnp.asarray([a for a, _ in pairs], jnp.int32)
    ki_tab = jnp.asarray([c for _, c in pairs], jnp.int32)
    vec = pl.BlockSpec((1, DIFF_QK_DIM), lambda b, h, s, qt, kt: (0, 0))
    grid_spec = pltpu.PrefetchScalarGridSpec(
        num_scalar_prefetch=2,
        grid=(batch, DIFF_HEADS, len(pairs)),
        in_specs=[
            pl.BlockSpec((DIFF_T, LANES), lambda b, h, s, qt, kt: (b * nb + qt[s], h)),
            pl.BlockSpec((DIFF_T, LANES), lambda b, h, s, qt, kt: (b * nb + kt[s], DIFF_HEADS + h)),
            pl.BlockSpec((DIFF_V_DIM, DIFF_T), lambda b, h, s, qt, kt: (h, b * nb + kt[s])),
            vec, vec, vec, vec,
            pl.BlockSpec((DIFF_V_DIM, 1), lambda b, h, s, qt, kt: (0, 0)),
        ],
        out_specs=pl.BlockSpec((DIFF_T, LANES), lambda b, h, s, qt, kt: (b * nb + qt[s], h)),
        scratch_shapes=[
            pltpu.VMEM((DIFF_T, LANES), jnp.bfloat16),
            pltpu.VMEM((DIFF_T, LANES), jnp.bfloat16),
            pltpu.VMEM((1, DIFF_T), jnp.float32),
            pltpu.VMEM((1, DIFF_T), jnp.float32),
            pltpu.VMEM((1, DIFF_T), jnp.float32),
            pltpu.VMEM((1, DIFF_T), jnp.float32),
            pltpu.VMEM((DIFF_V_DIM, DIFF_T), jnp.float32),
            pltpu.VMEM((DIFF_V_DIM, DIFF_T), jnp.float32),
        ],
    )
    return pl.pallas_call(
        functools.partial(_diff_kernel, lam_init=lam_init),
        grid_spec=grid_spec,
        out_shape=jax.ShapeDtypeStruct((batch * seq, DIFF_HEADS * DIFF_V_DIM), jnp.bfloat16),
        compiler_params=pltpu.CompilerParams(
            dimension_semantics=("parallel", "parallel", "arbitrary"),
            vmem_limit_bytes=VMEM_LIMIT),
        name="diffattn",
    )(qi_tab, ki_tab, p, p, dvt, lq1, lk1, lq2, lk2, g_sub_col)


def _key_to_float(key):
    bits = jnp.where(key >= 0, key, key ^ jnp.int32(0x7FFFFFFF))
    return lax.bitcast_convert_type(bits, jnp.float32)


def _count_rows(hit):
    tk, tq = hit.shape
    return jnp.sum(hit.reshape(tk // (4 * SUBLANES), 4 * SUBLANES, tq), axis=0)


def _dsa_kernel(sq_ref, iq_ref, iwt_ref, ikk_ref, sk_ref, svt_ref, o_ref,
                score, iqm, sqm, acc, m_s, l_s, thr, jcut, st_scr, p_scr, *, seq, top_k):
    qi = pl.program_id(1)
    tq = DSA_TQ
    nkc = qi + 1
    lane = lax.broadcasted_iota(jnp.int32, (tq, LANES), 1)
    half = lane // DSA_HEAD_DIM

    for h in range(DSA_HEADS):
        pr = h // 2
        iqp = iq_ref[:, pr * LANES:(pr + 1) * LANES]
        sqp = sq_ref[:, pr * LANES:(pr + 1) * LANES]
        iqm[h] = jnp.where(half == h % 2, iqp, jnp.zeros_like(iqp))
        sqm[h] = jnp.where(half == h % 2, sqp, jnp.zeros_like(sqp))

    krow = lax.broadcasted_iota(jnp.int32, (tq, tq), 0)
    qcol = lax.broadcasted_iota(jnp.int32, (tq, tq), 1)

    def score_chunk(c, carry):
        kk = ikk_ref[pl.ds(pl.multiple_of(c * tq, tq), tq), :]
        sc = jnp.zeros((tq, tq), jnp.float32)
        for h in range(IDX_HEADS):
            sc = sc + iwt_ref[h:h + 1, :] * jnp.maximum(_dot_nt(kk, iqm[h]), 0.0)
        sc = jnp.where(krow <= qcol + (qi - c) * tq, sc, -jnp.inf)
        score[c] = sc
        return carry

    lax.fori_loop(0, nkc, score_chunk, 0)

    zero_cnt = jnp.zeros((4 * SUBLANES, tq), jnp.float32)

    def count_ge(cand_f):
        def body(c, cnt):
            return cnt + _count_rows(jnp.where(score[c] >= cand_f, 1.0, 0.0))
        return jnp.sum(lax.fori_loop(0, nkc, body, zero_cnt), axis=0, keepdims=True)

    def bit_step(i, key):
        cand = key + lax.shift_left(jnp.int32(1), jnp.int32(31) - i)
        ok = count_ge(_key_to_float(cand)) >= float(top_k)
        return jnp.where(ok, cand, key)

    int_min = jnp.full((1, tq), -2 ** 31, jnp.int32)
    key = lax.fori_loop(0, 32, bit_step, int_min)
    t_f = _key_to_float(key)

    def count_gt_ge(c, carry):
        gt, ge = carry
        blk = score[c]
        return (gt + _count_rows(jnp.where(blk > t_f, 1.0, 0.0)),
                ge + _count_rows(jnp.where(blk >= t_f, 1.0, 0.0)))

    gt, ge = lax.fori_loop(0, nkc, count_gt_ge, (zero_cnt, zero_cnt))
    n_gt = jnp.sum(gt, axis=0, keepdims=True)
    n_ge = jnp.sum(ge, axis=0, keepdims=True)
    q_pos = qi * tq + lax.broadcasted_iota(jnp.int32, (1, tq), 1)
    few = q_pos < top_k - 1
    thr[...] = jnp.where(few, -jnp.inf, t_f)
    jcut[...] = jnp.where(few, -1, seq)
    need = float(top_k) - n_gt
    split = jnp.logical_and(jnp.logical_not(few), n_ge > float(top_k))

    @pl.when(jnp.max(jnp.where(split, 1.0, 0.0)) > 0.0)
    def _():
        def count_eq_below(jc):
            def body(c, cnt):
                hit = jnp.where((score[c] == t_f) & (c * tq + krow < jc), 1.0, 0.0)
                return cnt + _count_rows(hit)
            return jnp.sum(lax.fori_loop(0, nkc, body, zero_cnt), axis=0, keepdims=True)

        nbits = (seq - 1).bit_length()

        def jbit(i, jv):
            cand = jv + lax.shift_left(jnp.int32(1), jnp.int32(nbits - 1) - i)
            return jnp.where(count_eq_below(cand) < need, cand, jv)

        jv = lax.fori_loop(0, nbits, jbit, jnp.zeros((1, tq), jnp.int32))
        jcut[...] = jnp.where(split, jv, jcut[...])

    m_s[...] = jnp.full(m_s.shape, NEG_BIG, jnp.float32)
    l_s[...] = jnp.zeros(l_s.shape, jnp.float32)
    acc[...] = jnp.zeros(acc.shape, jnp.float32)

    def attend(c, carry):
        off = pl.multiple_of(c * tq, tq)
        blk = score[c]
        tt = thr[...]
        keep = (blk > tt) | ((blk == tt) & (c * tq + krow <= jcut[...]))
        bias = jnp.where(keep, 0.0, NEG_BIG)
        for h in range(DSA_HEADS):
            kp = sk_ref[pl.ds(off, tq), (h // 2) * LANES:(h // 2 + 1) * LANES]
            st_scr[h] = _dot_nt(kp, sqm[h]) + bias
        alphas = []
        for h in range(DSA_HEADS):
            st = st_scr[h]
            m_old = m_s[h]
            m_new = jnp.maximum(m_old, jnp.max(st, axis=0, keepdims=True))
            alpha = jnp.exp(m_old - m_new)
            p = jnp.exp(st - m_new)
            l_s[h] = alpha * l_s[h] + jnp.sum(p, axis=0, keepdims=True)
            m_s[h] = m_new
            p_scr[h] = p.astype(jnp.bfloat16)
            alphas.append(alpha)
        for h in range(DSA_HEADS):
            vt = svt_ref[c, h * DSA_HEAD_DIM:(h + 1) * DSA_HEAD_DIM, :]
            acc[h] = alphas[h] * acc[h] + _dot(vt, p_scr[h])
        return carry

    lax.fori_loop(0, nkc, attend, 0)

    outs = [acc[h] / l_s[h] for h in range(DSA_HEADS)]
    o_ref[...] = jnp.concatenate(outs, axis=0).T.astype(o_ref.dtype)


def _dsa_attention(p, ikk, iwt, svt, batch, seq, top_k):
    nq = seq // DSA_TQ
    w = DSA_HEADS * DSA_HEAD_DIM
    once = pl.Buffered(1)
    return pl.pallas_call(
        functools.partial(_dsa_kernel, seq=seq, top_k=top_k),
        grid=(batch, nq),
        in_specs=[
            pl.BlockSpec((DSA_TQ, w), lambda b, qi: (b * nq + qi, 3)),
            pl.BlockSpec((DSA_TQ, w), lambda b, qi: (b * nq + qi, 6)),
            pl.BlockSpec((IDX_HEADS, DSA_TQ), lambda b, qi: (0, b * nq + qi)),
            pl.BlockSpec((seq, LANES), lambda b, qi: (b, 0), pipeline_mode=once),
            pl.BlockSpec((seq, w), lambda b, qi: (b, 4), pipeline_mode=once),
            pl.BlockSpec((nq, w, DSA_TQ), lambda b, qi: (b, 0, 0), pipeline_mode=once),
        ],
        out_specs=pl.BlockSpec((DSA_TQ, w), lambda b, qi: (b * nq + qi, 0)),
        out_shape=jax.ShapeDtypeStruct((batch * seq, w), jnp.bfloat16),
        scratch_shapes=[
            pltpu.VMEM((nq, DSA_TQ, DSA_TQ), jnp.float32),
            pltpu.VMEM((IDX_HEADS, DSA_TQ, LANES), jnp.bfloat16),
            pltpu.VMEM((DSA_HEADS, DSA_TQ, LANES), jnp.bfloat16),
            pltpu.VMEM((DSA_HEADS, DSA_HEAD_DIM, DSA_TQ), jnp.float32),
            pltpu.VMEM((DSA_HEADS, 1, DSA_TQ), jnp.float32),
            pltpu.VMEM((DSA_HEADS, 1, DSA_TQ), jnp.float32),
            pltpu.VMEM((1, DSA_TQ), jnp.float32),
            pltpu.VMEM((1, DSA_TQ), jnp.int32),
            pltpu.VMEM((DSA_HEADS, DSA_TQ, DSA_TQ), jnp.float32),
            pltpu.VMEM((DSA_HEADS, DSA_TQ, DSA_TQ), jnp.bfloat16),
        ],
        compiler_params=pltpu.CompilerParams(
            dimension_semantics=("parallel", "arbitrary"), vmem_limit_bytes=VMEM_LIMIT),
        name="dsa",
    )(p, p, iwt, ikk, p, svt)


def _memkv_kernel(mem_ref, g_ref, w_ref, o_ref):
    n = _rms(mem_ref[...], g_ref[...]).astype(jnp.bfloat16)
    o_ref[...] = _dot(n, w_ref[...]).astype(o_ref.dtype)


def _memkv(mem2, g_mem, w_kv, batch):
    return pl.pallas_call(
        _memkv_kernel,
        grid=(batch,),
        in_specs=[
            pl.BlockSpec((MEM_LEN, D_MODEL), lambda b: (b, 0)),
            pl.BlockSpec((1, D_MODEL), lambda b: (0, 0)),
            pl.BlockSpec(w_kv.shape, lambda b: (0, 0)),
        ],
        out_specs=pl.BlockSpec((MEM_LEN, w_kv.shape[1]), lambda b: (b, 0)),
        out_shape=jax.ShapeDtypeStruct((batch * MEM_LEN, w_kv.shape[1]), jnp.bfloat16),
        compiler_params=pltpu.CompilerParams(
            dimension_semantics=("parallel",), vmem_limit_bytes=VMEM_LIMIT),
        name="memkv",
    )(mem2, g_mem, w_kv)


def _memattn_kernel(q_ref, kv_ref, o_ref):
    scale = XA_HEAD_DIM ** -0.5
    outs = []
    for h in range(XA_HEADS):
        q = q_ref[:, h * LANES:(h + 1) * LANES]
        k = kv_ref[:, h * LANES:(h + 1) * LANES]
        v = kv_ref[:, (XA_HEADS + h) * LANES:(XA_HEADS + h + 1) * LANES]
        s = _dot_nt(q, k) * scale
        m = jnp.max(s, axis=1, keepdims=True)
        p = jnp.exp(s - m)
        l = jnp.sum(p, axis=1, keepdims=True)
        outs.append(_dot((p / l).astype(jnp.bfloat16), v))
    o_ref[...] = jnp.concatenate(outs, axis=1).astype(o_ref.dtype)


def _mem_attention(p, mkv, batch, seq):
    nq = seq // MEM_TQ
    w = XA_HEADS * XA_HEAD_DIM
    return pl.pallas_call(
        _memattn_kernel,
        grid=(batch, nq),
        in_specs=[
            pl.BlockSpec((MEM_TQ, w), lambda b, qi: (b * nq + qi, 7)),
            pl.BlockSpec((MEM_LEN, 2 * w), lambda b, qi: (b, 0)),
        ],
        out_specs=pl.BlockSpec((MEM_TQ, w), lambda b, qi: (b * nq + qi, 0)),
        out_shape=jax.ShapeDtypeStruct((batch * seq, w), jnp.bfloat16),
        compiler_params=pltpu.CompilerParams(
            dimension_semantics=("parallel", "parallel"), vmem_limit_bytes=VMEM_LIMIT),
        name="memattn",
    )(p, mkv)


def _merge_kernel(x_ref, yd_ref, ys_ref, ym_ref, gmix_ref, wg_ref, bg_ref, wbr_ref, wout_ref,
                  gffn_ref, wr_ref, br_ref, h_ref, hn_ref, lg_ref):
    xf = x_ref[...]
    n = _rms(xf, gmix_ref[...]).astype(jnp.bfloat16)
    merged = jnp.zeros(xf.shape, jnp.float32)
    for i, y_ref in enumerate((yd_ref, ys_ref, ym_ref)):
        gate = jax.nn.sigmoid(_dot(n, wg_ref[i]) + bg_ref[i])
        merged = merged + gate * _dot(y_ref[...], wbr_ref[i])
    h = xf + _dot(merged.astype(jnp.bfloat16), wout_ref[...])
    h_ref[...] = h
    hn = _rms(h, gffn_ref[...]).astype(jnp.bfloat16)
    hn_ref[...] = hn
    lg_ref[...] = _dot(hn, wr_ref[...]) + br_ref[...]


def _merge(x2, y_diff, y_dsa, y_mem, g_mix, w_g, b_g, w_br, w_out, g_ffn, w_r, b_r):
    t = x2.shape[0]
    row = lambda w: pl.BlockSpec((MERGE_TM, w), lambda i: (i, 0))
    full = lambda a: pl.BlockSpec(a.shape, lambda i: (0,) * a.ndim, pipeline_mode=pl.Buffered(1))
    return pl.pallas_call(
        _merge_kernel,
        grid=(t // MERGE_TM,),
        in_specs=[row(D_MODEL), row(512), row(512), row(512), full(g_mix), full(w_g), full(b_g),
                  full(w_br), full(w_out), full(g_ffn), full(w_r), full(b_r)],
        out_specs=[row(D_MODEL), row(D_MODEL), row(LANES)],
        out_shape=[
            jax.ShapeDtypeStruct((t, D_MODEL), jnp.float32),
            jax.ShapeDtypeStruct((t, D_MODEL), jnp.bfloat16),
            jax.ShapeDtypeStruct((t, LANES), jnp.float32),
        ],
        compiler_params=pltpu.CompilerParams(
            dimension_semantics=("parallel",), vmem_limit_bytes=VMEM_LIMIT),
        name="merge",
    )(x2, y_diff, y_dsa, y_mem, g_mix, w_g, b_g, w_br, w_out, g_ffn, w_r, b_r)


def _route_weights(lg):
    lane = lax.broadcasted_iota(jnp.int32, lg.shape, 1).astype(jnp.float32)
    big = float(LANES)
    gl = jnp.where(lane < N_GROUPS, lg, -jnp.inf)
    gmax = jnp.max(gl, axis=1, keepdims=True)
    grp = jnp.min(jnp.where(gl == gmax, lane, big), axis=1, keepdims=True)
    gsum = jnp.sum(jnp.where(lane < N_GROUPS, jnp.exp(gl - gmax), 0.0), axis=1, keepdims=True)
    p_grp = 1.0 / gsum
    lo = N_GROUPS + grp * EXPERTS_PER_GROUP
    el = jnp.where((lane >= lo) & (lane < lo + EXPERTS_PER_GROUP), lg, -jnp.inf)
    e1 = jnp.max(el, axis=1, keepdims=True)
    i1 = jnp.min(jnp.where(el == e1, lane, big), axis=1, keepdims=True)
    el2 = jnp.where(lane == i1, -jnp.inf, el)
    e2 = jnp.max(el2, axis=1, keepdims=True)
    i2 = jnp.min(jnp.where(el2 == e2, lane, big), axis=1, keepdims=True)
    r = jnp.exp(e2 - e1)
    w1 = p_grp / (1.0 + r)
    w2 = p_grp * r / (1.0 + r)
    return jnp.where(lane == i1, w1, 0.0) + jnp.where(lane == i2, w2, 0.0)


def _moe_kernel(hn_ref, lg_ref, h_ref, win_ref, wout_ref, gfin_ref, o_ref, acc, cw):
    e = pl.program_id(1)

    @pl.when(e == 0)
    def _():
        acc[...] = jnp.zeros(acc.shape, jnp.float32)
        cw[...] = _route_weights(lg_ref[...])

    lane = lax.broadcasted_iota(jnp.int32, cw.shape, 1)
    cw_e = jnp.sum(jnp.where(lane == N_GROUPS + e, cw[...], 0.0), axis=1, keepdims=True)
    gu = _dot(hn_ref[...], win_ref[0])
    g = gu[:, :D_EXPERT]
    hid = (g * jax.nn.sigmoid(g)) * gu[:, D_EXPERT:]
    acc[...] += cw_e * _dot(hid.astype(jnp.bfloat16), wout_ref[0])

    @pl.when(e == N_EXPERTS - 1)
    def _():
        o_ref[...] = _rms(h_ref[...] + acc[...], gfin_ref[...])


def _moe(hn, lg, h, w_e_in, w_e_out, g_final):
    t = hn.shape[0]
    return pl.pallas_call(
        _moe_kernel,
        grid=(t // MOE_TM, N_EXPERTS),
        in_specs=[
            pl.BlockSpec((MOE_TM, D_MODEL), lambda i, e: (i, 0)),
            pl.BlockSpec((MOE_TM, LANES), lambda i, e: (i, 0)),
            pl.BlockSpec((MOE_TM, D_MODEL), lambda i, e: (i, 0)),
            pl.BlockSpec((1, D_MODEL, 2 * D_EXPERT), lambda i, e: (e, 0, 0)),
            pl.BlockSpec((1, D_EXPERT, D_MODEL), lambda i, e: (e, 0, 0)),
            pl.BlockSpec((1, D_MODEL), lambda i, e: (0, 0)),
        ],
        out_specs=pl.BlockSpec((MOE_TM, D_MODEL), lambda i, e: (i, 0)),
        out_shape=jax.ShapeDtypeStruct((t, D_MODEL), jnp.float32),
        scratch_shapes=[
            pltpu.VMEM((MOE_TM, D_MODEL), jnp.float32),
            pltpu.VMEM((MOE_TM, LANES), jnp.float32),
        ],
        compiler_params=pltpu.CompilerParams(
            dimension_semantics=("parallel", "arbitrary"), vmem_limit_bytes=VMEM_LIMIT),
        name="moe",
    )(hn, lg, h, w_e_in, w_e_out, g_final)


def kernel(x, positions, mem, g_mix, w_in, b_gate, lambda_q1, lambda_k1, lambda_q2, lambda_k2,
           g_diff_sub, g_mem, w_mem_kv, w_br_diff, w_br_dsa, w_br_mem, w_out, g_ffn,
           w_route_group, b_route_group, w_route_expert, b_route_expert, w_exp_in, w_exp_out,
           g_final):
    b, s, d = x.shape
    t = b * s
    bf = jnp.bfloat16
    top_k = min(TOPK_MAX, s // 4)
    assert d == D_MODEL and s % DIFF_T == 0 and s % DSA_TQ == 0 and top_k <= DSA_TQ
    assert g_mix.shape[0] == 1, "single layer"
    lam_init = 0.8 - 0.6 * math.exp(-0.3 * 0)

    wi = w_in[0]
    c = 512
    seg = lambda k: wi[:, k * c:(k + 1) * c]
    o_ik = 7 * c
    w_ik = wi[:, o_ik:o_ik + IDX_DIM]
    w_iw = wi[:, o_ik + IDX_DIM:o_ik + IDX_DIM + IDX_HEADS]
    o_mq = o_ik + IDX_DIM + IDX_HEADS
    w_mq = wi[:, o_mq:o_mq + c]
    w_gl = wi[:, o_mq + c:]
    qs = DIFF_QK_DIM ** -0.5
    w_a = jnp.concatenate([seg(0) * qs, seg(1), seg(2), seg(3) * qs, seg(4), seg(5), seg(6) * qs, w_mq],
                          axis=1).astype(bf)
    w_s = jnp.concatenate([w_ik, w_ik, w_iw, jnp.zeros((d, LANES - IDX_HEADS), wi.dtype)], axis=1).astype(bf)
    w_g = w_gl.reshape(d, 3, d).transpose(1, 0, 2).astype(bf)
    b_g = b_gate[0].reshape(3, 1, d)
    w_br = jnp.stack([w_br_diff[0], w_br_dsa[0], w_br_mem[0]]).astype(bf)
    w_r = jnp.concatenate([w_route_group[0], w_route_expert[0],
                           jnp.zeros((d, LANES - N_GROUPS - N_EXPERTS), wi.dtype)], axis=1).astype(bf)
    b_r = jnp.concatenate([b_route_group[0], b_route_expert[0],
                           jnp.zeros((LANES - N_GROUPS - N_EXPERTS,), jnp.float32)]).reshape(1, LANES)

    rot = IDX_DIM // ROPE_FRACTION
    inv_freq = ROPE_THETA ** (-jnp.arange(0, rot, 2, dtype=jnp.float32) / rot)
    inv64 = jnp.concatenate([inv_freq, inv_freq, jnp.zeros((IDX_DIM - rot,), jnp.float32)])
    inv_lane = jnp.concatenate([inv64, inv64]).reshape(1, LANES)

    x2 = x.reshape(t, d)
    pos2 = positions.reshape(t, 1)
    p, ikk, dvt, svt, iwt = _proj(x2, pos2, g_mix, inv_lane, w_a, w_s)

    y_diff = _diff_attention(p, dvt, lambda_q1, lambda_k1, lambda_q2, lambda_k2,
                             g_diff_sub.reshape(DIFF_V_DIM, 1), b, s, lam_init)
    y_dsa = _dsa_attention(p, ikk, iwt, svt, b, s, top_k)
    mkv = _memkv(mem.reshape(b * MEM_LEN, d), g_mem, w_mem_kv[0].astype(bf), b)
    y_mem = _mem_attention(p, mkv, b, s)

    h, hn, lg = _merge(x2, y_diff, y_dsa, y_mem, g_mix, w_g, b_g, w_br, w_out[0].astype(bf), g_ffn, w_r, b_r)
    out = _moe(hn, lg, h, w_exp_in[0].astype(bf), w_exp_out[0].astype(bf), g_final.reshape(1, d))
    return out.reshape(b, s, d)
```

```python
import functools
import math

import jax
import jax.numpy as jnp
from jax import lax
from jax.experimental import pallas as pl
from jax.experimental.pallas import tpu as pltpu

D_MODEL = 1024
MEM_LEN = 256
XA_HEADS = 4
XA_HEAD_DIM = 128
DIFF_HEADS = 4
DIFF_QK_DIM = 64
DIFF_V_DIM = 128
DSA_HEADS = 8
DSA_HEAD_DIM = 64
IDX_HEADS = 8
IDX_DIM = 64
TOPK_MAX = 256
ROPE_THETA = 500000.0
ROPE_FRACTION = 4
N_GROUPS = 4
EXPERTS_PER_GROUP = 4
N_EXPERTS = 16
D_EXPERT = 512
EPS = 1e-6

LANES = 128
SUBLANES = 8
VMEM_LIMIT = 56 * 1024 * 1024
NEG_BIG = -1e30

PROJ_TM = 512
PROJ_TN = 512
DIFF_T = 512
DSA_TQ = 256
MEM_TQ = 512
MERGE_TM = 512
MOE_TM = 512

_NT = (((1,), (1,)), ((), ()))


def _dot(a, b):
    return jnp.dot(a, b, preferred_element_type=jnp.float32)


def _dot_nt(a, b):
    return lax.dot_general(a, b, _NT, preferred_element_type=jnp.float32)


def _rms(xf, g):
    return xf * lax.rsqrt(jnp.mean(xf * xf, axis=-1, keepdims=True) + EPS) * g


def _proj_kernel(x_ref, pos_ref, g_ref, inv_ref, wa_ref, ws_ref,
                 p_ref, ikk_ref, dvt_ref, svt_ref, iwt_ref, n_scr, cos_scr, sa_scr, sb_scr):
    j = pl.program_id(1)

    def rope(v, reps):
        n = v.shape[1]
        c = jnp.concatenate([cos_scr[...]] * reps, axis=1) if reps > 1 else cos_scr[...]
        a = jnp.concatenate([sa_scr[...]] * reps, axis=1) if reps > 1 else sa_scr[...]
        b = jnp.concatenate([sb_scr[...]] * reps, axis=1) if reps > 1 else sb_scr[...]
        up = pltpu.roll(v, n - 8, 1)
        dn = pltpu.roll(v, 8, 1)
        return v * c + up * a + dn * b

    @pl.when(j == 0)
    def _():
        xf = x_ref[...]
        n_scr[...] = _rms(xf, g_ref[...]).astype(jnp.bfloat16)
        ang = pos_ref[...].astype(jnp.float32) * inv_ref[...]
        lane = lax.broadcasted_iota(jnp.int32, ang.shape, 1) % 64
        cs = jnp.cos(ang)
        sn = jnp.sin(ang)
        cos_scr[...] = jnp.where(lane < 16, cs, 1.0)
        sa_scr[...] = jnp.where(lane < 8, -sn, 0.0)
        sb_scr[...] = jnp.where((lane >= 8) & (lane < 16), sn, 0.0)
        small = _dot(n_scr[...], ws_ref[...])
        ikk_ref[...] = rope(small[:, :LANES], 1).astype(jnp.bfloat16)
        w = small[:, LANES:] * (IDX_HEADS ** -0.5)
        iwt_ref[...] = w.T[:IDX_HEADS, :]

    acc = _dot(n_scr[...], wa_ref[...])
    is_rope = (j == 0) | (j == 1) | (j == 3) | (j == 4) | (j == 6)

    @pl.when(is_rope)
    def _():
        p_ref[...] = rope(acc, PROJ_TN // LANES).astype(jnp.bfloat16)

    @pl.when(jnp.logical_not(is_rope))
    def _():
        p_ref[...] = acc.astype(jnp.bfloat16)

    @pl.when(j == 2)
    def _():
        dvt_ref[0] = acc.T.astype(jnp.bfloat16)

    @pl.when(j == 5)
    def _():
        at = acc.T.astype(jnp.bfloat16)
        for r in range(PROJ_TM // DSA_TQ):
            svt_ref[r] = at[:, r * DSA_TQ:(r + 1) * DSA_TQ]


def _proj(x2, pos2, g_mix, inv_lane, w_a, w_s):
    t = x2.shape[0]
    ncol = w_a.shape[1] // PROJ_TN
    cpt = PROJ_TM // DSA_TQ
    return pl.pallas_call(
        _proj_kernel,
        grid=(t // PROJ_TM, ncol),
        in_specs=[
            pl.BlockSpec((PROJ_TM, D_MODEL), lambda i, j: (i, 0)),
            pl.BlockSpec((PROJ_TM, 1), lambda i, j: (i, 0)),
            pl.BlockSpec((1, D_MODEL), lambda i, j: (0, 0)),
            pl.BlockSpec((1, LANES), lambda i, j: (0, 0)),
            pl.BlockSpec((D_MODEL, PROJ_TN), lambda i, j: (0, j)),
            pl.BlockSpec((D_MODEL, 2 * LANES), lambda i, j: (0, 0)),
        ],
        out_specs=[
            pl.BlockSpec((PROJ_TM, PROJ_TN), lambda i, j: (i, j)),
            pl.BlockSpec((PROJ_TM, LANES), lambda i, j: (i, 0)),
            pl.BlockSpec((1, PROJ_TN, PROJ_TM), lambda i, j: (i, 0, 0)),
            pl.BlockSpec((cpt, PROJ_TN, DSA_TQ), lambda i, j: (i, 0, 0)),
            pl.BlockSpec((IDX_HEADS, PROJ_TM), lambda i, j: (0, i)),
        ],
        out_shape=[
            jax.ShapeDtypeStruct((t, w_a.shape[1]), jnp.bfloat16),
            jax.ShapeDtypeStruct((t, LANES), jnp.bfloat16),
            jax.ShapeDtypeStruct((t // PROJ_TM, PROJ_TN, PROJ_TM), jnp.bfloat16),
            jax.ShapeDtypeStruct((t // DSA_TQ, PROJ_TN, DSA_TQ), jnp.bfloat16),
            jax.ShapeDtypeStruct((IDX_HEADS, t), jnp.float32),
        ],
        scratch_shapes=[
            pltpu.VMEM((PROJ_TM, D_MODEL), jnp.bfloat16),
            pltpu.VMEM((PROJ_TM, LANES), jnp.float32),
            pltpu.VMEM((PROJ_TM, LANES), jnp.float32),
            pltpu.VMEM((PROJ_TM, LANES), jnp.float32),
        ],
        compiler_params=pltpu.CompilerParams(
            dimension_semantics=("parallel", "arbitrary"), vmem_limit_bytes=VMEM_LIMIT),
        name="proj",
    )(x2, pos2, g_mix, inv_lane, w_a, w_s)


def _diff_kernel(q_ref, k_ref, vt_ref, lq1_ref, lk1_ref, lq2_ref, lk2_ref, gs_ref, o_ref,
                 qm, m_s, l_s, acc, st_a, st_b, p_scr, *, lam_init):
    qi = pl.program_id(2)
    t = DIFF_T
    q = q_ref[...]
    lane = lax.broadcasted_iota(jnp.int32, q.shape, 1)
    qm[0] = jnp.where(lane < DIFF_QK_DIM, q, jnp.zeros_like(q))
    qm[1] = jnp.where(lane >= DIFF_QK_DIM, q, jnp.zeros_like(q))
    m_s[...] = jnp.full(m_s.shape, NEG_BIG, jnp.float32)
    l_s[...] = jnp.zeros(l_s.shape, jnp.float32)
    acc[...] = jnp.zeros(acc.shape, jnp.float32)

    def qk(j, st_ref):
        k = k_ref[pl.ds(pl.multiple_of(j * t, t), t), :]
        for i in range(2):
            st_ref[i] = _dot_nt(k, qm[i])

    def softmax_pv(j, st_ref, diagonal):
        vt = vt_ref[j]
        if diagonal:
            krow = lax.broadcasted_iota(jnp.int32, (t, t), 0)
            qcol = lax.broadcasted_iota(jnp.int32, (t, t), 1)
            keep = krow <= qcol
        alphas = []
        for i in range(2):
            st = st_ref[i]
            if diagonal:
                st = jnp.where(keep, st, NEG_BIG)
            m_old = m_s[i]
            m_new = jnp.maximum(m_old, jnp.max(st, axis=0, keepdims=True))
            alpha = jnp.exp(m_old - m_new)
            p = jnp.exp(st - m_new)
            l_s[i] = alpha * l_s[i] + jnp.sum(p, axis=0, keepdims=True)
            m_s[i] = m_new
            p_scr[i] = p.astype(jnp.bfloat16)
            alphas.append(alpha)
        for i in range(2):
            acc[i] = alphas[i] * acc[i] + _dot(vt, p_scr[i])

    qk(0, st_a)

    def pair(tt, carry):
        j = 2 * tt
        qk(j + 1, st_b)
        softmax_pv(j, st_a, False)
        qk(j + 2, st_a)
        softmax_pv(j + 1, st_b, False)
        return carry

    lax.fori_loop(0, qi // 2, pair, 0)

    @pl.when(qi % 2 == 0)
    def _():
        softmax_pv(qi, st_a, True)

    @pl.when(qi % 2 == 1)
    def _():
        qk(qi, st_b)
        softmax_pv(qi - 1, st_a, False)
        softmax_pv(qi, st_b, True)

    lam = (jnp.exp(jnp.sum(lq1_ref[...] * lk1_ref[...], axis=1, keepdims=True))
           - jnp.exp(jnp.sum(lq2_ref[...] * lk2_ref[...], axis=1, keepdims=True))
           + lam_init)
    ot = acc[0] / l_s[0] - lam * (acc[1] / l_s[1])
    yt = ot * lax.rsqrt(jnp.mean(ot * ot, axis=0, keepdims=True) + EPS) * gs_ref[...]
    o_ref[...] = (yt * (1.0 - lam_init)).T.astype(o_ref.dtype)


def _diff_attention(p, dvt, lq1, lk1, lq2, lk2, g_sub_col, batch, seq, lam_init):
    nb = seq // DIFF_T
    vec = pl.BlockSpec((1, DIFF_QK_DIM), lambda b, h, qi: (0, 0))
    return pl.pallas_call(
        functools.partial(_diff_kernel, lam_init=lam_init),
        grid=(batch, DIFF_HEADS, nb),
        in_specs=[
            pl.BlockSpec((DIFF_T, LANES), lambda b, h, qi: (b * nb + qi, h)),
            pl.BlockSpec((seq, LANES), lambda b, h, qi: (b, DIFF_HEADS + h)),
            pl.BlockSpec((nb, DIFF_V_DIM, DIFF_T), lambda b, h, qi: (b, h, 0)),
            vec, vec, vec, vec,
            pl.BlockSpec((DIFF_V_DIM, 1), lambda b, h, qi: (0, 0)),
        ],
        out_specs=pl.BlockSpec((DIFF_T, LANES), lambda b, h, qi: (b * nb + qi, h)),
        out_shape=jax.ShapeDtypeStruct((batch * seq, DIFF_HEADS * DIFF_V_DIM), jnp.bfloat16),
        scratch_shapes=[
            pltpu.VMEM((2, DIFF_T, LANES), jnp.bfloat16),
            pltpu.VMEM((2, 1, DIFF_T), jnp.float32),
            pltpu.VMEM((2, 1, DIFF_T), jnp.float32),
            pltpu.VMEM((2, DIFF_V_DIM, DIFF_T), jnp.float32),
            pltpu.VMEM((2, DIFF_T, DIFF_T), jnp.float32),
            pltpu.VMEM((2, DIFF_T, DIFF_T), jnp.float32),
            pltpu.VMEM((2, DIFF_T, DIFF_T), jnp.bfloat16),
        ],
        compiler_params=pltpu.CompilerParams(
            dimension_semantics=("parallel", "parallel", "arbitrary"),
            vmem_limit_bytes=VMEM_LIMIT),
        name="diffattn",
    )(p, p, dvt, lq1, lk1, lq2, lk2, g_sub_col)


def _key_to_float(key):
    bits = jnp.where(key >= 0, key, key ^ jnp.int32(0x7FFFFFFF))
    return lax.bitcast_convert_type(bits, jnp.float32)


def _count_rows(hit):
    tk, tq = hit.shape
    return jnp.sum(hit.reshape(tk // (4 * SUBLANES), 4 * SUBLANES, tq), axis=0)


def _dsa_kernel(sq_ref, iq_ref, iwt_ref, ikk_ref, sk_ref, svt_ref, o_ref,
                score, iqm, sqm, acc, m_s, l_s, thr, jcut, st_scr, p_scr, *, seq, top_k):
    qi = pl.program_id(1)
    tq = DSA_TQ
    nkc = qi + 1
    lane = lax.broadcasted_iota(jnp.int32, (tq, LANES), 1)
    half = lane // DSA_HEAD_DIM

    for h in range(DSA_HEADS):
        pr = h // 2
        iqp = iq_ref[:, pr * LANES:(pr + 1) * LANES]
        sqp = sq_ref[:, pr * LANES:(pr + 1) * LANES]
        iqm[h] = jnp.where(half == h % 2, iqp, jnp.zeros_like(iqp))
        sqm[h] = jnp.where(half == h % 2, sqp, jnp.zeros_like(sqp))

    krow = lax.broadcasted_iota(jnp.int32, (tq, tq), 0)
    qcol = lax.broadcasted_iota(jnp.int32, (tq, tq), 1)

    def score_chunk(c, carry):
        kk = ikk_ref[pl.ds(pl.multiple_of(c * tq, tq), tq), :]
        sc = jnp.zeros((tq, tq), jnp.float32)
        for h in range(IDX_HEADS):
            sc = sc + iwt_ref[h:h + 1, :] * jnp.maximum(_dot_nt(kk, iqm[h]), 0.0)
        sc = jnp.where(krow <= qcol + (qi - c) * tq, sc, -jnp.inf)
        score[c] = sc
        return carry

    lax.fori_loop(0, nkc, score_chunk, 0)

    zero_cnt = jnp.zeros((4 * SUBLANES, tq), jnp.float32)

    def count_ge(cand_f):
        def body(c, cnt):
            return cnt + _count_rows(jnp.where(score[c] >= cand_f, 1.0, 0.0))
        return jnp.sum(lax.fori_loop(0, nkc, body, zero_cnt), axis=0, keepdims=True)

    def bit_step(i, key):
        cand = key + lax.shift_left(jnp.int32(1), jnp.int32(31) - i)
        ok = count_ge(_key_to_float(cand)) >= float(top_k)
        return jnp.where(ok, cand, key)

    int_min = jnp.full((1, tq), -2 ** 31, jnp.int32)
    key = lax.fori_loop(0, 32, bit_step, int_min)
    t_f = _key_to_float(key)

    def count_gt_ge(c, carry):
        gt, ge = carry
        blk = score[c]
        return (gt + _count_rows(jnp.where(blk > t_f, 1.0, 0.0)),
                ge + _count_rows(jnp.where(blk >= t_f, 1.0, 0.0)))

    gt, ge = lax.fori_loop(0, nkc, count_gt_ge, (zero_cnt, zero_cnt))
    n_gt = jnp.sum(gt, axis=0, keepdims=True)
    n_ge = jnp.sum(ge, axis=0, keepdims=True)
    q_pos = qi * tq + lax.broadcasted_iota(jnp.int32, (1, tq), 1)
    few = q_pos < top_k - 1
    thr[...] = jnp.where(few, -jnp.inf, t_f)
    jcut[...] = jnp.where(few, -1, seq)
    need = float(top_k) - n_gt
    split = jnp.logical_and(jnp.logical_not(few), n_ge > float(top_k))

    @pl.when(jnp.max(jnp.where(split, 1.0, 0.0)) > 0.0)
    def _():
        def count_eq_below(jc):
            def body(c, cnt):
                hit = jnp.where((score[c] == t_f) & (c * tq + krow < jc), 1.0, 0.0)
                return cnt + _count_rows(hit)
            return jnp.sum(lax.fori_loop(0, nkc, body, zero_cnt), axis=0, keepdims=True)

        nbits = (seq - 1).bit_length()

        def jbit(i, jv):
            cand = jv + lax.shift_left(jnp.int32(1), jnp.int32(nbits - 1) - i)
            return jnp.where(count_eq_below(cand) < need, cand, jv)

        jv = lax.fori_loop(0, nbits, jbit, jnp.zeros((1, tq), jnp.int32))
        jcut[...] = jnp.where(split, jv, jcut[...])

    m_s[...] = jnp.full(m_s.shape, NEG_BIG, jnp.float32)
    l_s[...] = jnp.zeros(l_s.shape, jnp.float32)
    acc[...] = jnp.zeros(acc.shape, jnp.float32)

    def attend(c, carry):
        off = pl.multiple_of(c * tq, tq)
        blk = score[c]
        tt = thr[...]
        keep = (blk > tt) | ((blk == tt) & (c * tq + krow <= jcut[...]))
        bias = jnp.where(keep, 0.0, NEG_BIG)
        for h in range(DSA_HEADS):
            kp = sk_ref[pl.ds(off, tq), (h // 2) * LANES:(h // 2 + 1) * LANES]
            st_scr[h] = _dot_nt(kp, sqm[h]) + bias
        alphas = []
        for h in range(DSA_HEADS):
            st = st_scr[h]
            m_old = m_s[h]
            m_new = jnp.maximum(m_old, jnp.max(st, axis=0, keepdims=True))
            alpha = jnp.exp(m_old - m_new)
            p = jnp.exp(st - m_new)
            l_s[h] = alpha * l_s[h] + jnp.sum(p, axis=0, keepdims=True)
            m_s[h] = m_new
            p_scr[h] = p.astype(jnp.bfloat16)
            alphas.append(alpha)
        for h in range(DSA_HEADS):
            vt = svt_ref[c, h * DSA_HEAD_DIM:(h + 1) * DSA_HEAD_DIM, :]
            acc[h] = alphas[h] * acc[h] + _dot(vt, p_scr[h])
        return carry

    lax.fori_loop(0, nkc, attend, 0)

    outs = [acc[h] / l_s[h] for h in range(DSA_HEADS)]
    o_ref[...] = jnp.concatenate(outs, axis=0).T.astype(o_ref.dtype)


def _dsa_attention(p, ikk, iwt, svt, batch, seq, top_k):
    nq = seq // DSA_TQ
    w = DSA_HEADS * DSA_HEAD_DIM
    once = pl.Buffered(1)
    return pl.pallas_call(
        functools.partial(_dsa_kernel, seq=seq, top_k=top_k),
        grid=(batch, nq),
        in_specs=[
            pl.BlockSpec((DSA_TQ, w), lambda b, qi: (b * nq + qi, 3)),
            pl.BlockSpec((DSA_TQ, w), lambda b, qi: (b * nq + qi, 6)),
            pl.BlockSpec((IDX_HEADS, DSA_TQ), lambda b, qi: (0, b * nq + qi)),
            pl.BlockSpec((seq, LANES), lambda b, qi: (b, 0), pipeline_mode=once),
            pl.BlockSpec((seq, w), lambda b, qi: (b, 4), pipeline_mode=once),
            pl.BlockSpec((nq, w, DSA_TQ), lambda b, qi: (b, 0, 0), pipeline_mode=once),
        ],
        out_specs=pl.BlockSpec((DSA_TQ, w), lambda b, qi: (b * nq + qi, 0)),
        out_shape=jax.ShapeDtypeStruct((batch * seq, w), jnp.bfloat16),
        scratch_shapes=[
            pltpu.VMEM((nq, DSA_TQ, DSA_TQ), jnp.float32),
            pltpu.VMEM((IDX_HEADS, DSA_TQ, LANES), jnp.bfloat16),
            pltpu.VMEM((DSA_HEADS, DSA_TQ, LANES), jnp.bfloat16),
            pltpu.VMEM((DSA_HEADS, DSA_HEAD_DIM, DSA_TQ), jnp.float32),
            pltpu.VMEM((DSA_HEADS, 1, DSA_TQ), jnp.float32),
            pltpu.VMEM((DSA_HEADS, 1, DSA_TQ), jnp.float32),
            pltpu.VMEM((1, DSA_TQ), jnp.float32),
            pltpu.VMEM((1, DSA_TQ), jnp.int32),
            pltpu.VMEM((DSA_HEADS, DSA_TQ, DSA_TQ), jnp.float32),
            pltpu.VMEM((DSA_HEADS, DSA_TQ, DSA_TQ), jnp.bfloat16),
        ],
        compiler_params=pltpu.CompilerParams(
            dimension_semantics=("parallel", "arbitrary"), vmem_limit_bytes=VMEM_LIMIT),
        name="dsa",
    )(p, p, iwt, ikk, p, svt)


def _memkv_kernel(mem_ref, g_ref, w_ref, o_ref):
    n = _rms(mem_ref[...], g_ref[...]).astype(jnp.bfloat16)
    o_ref[...] = _dot(n, w_ref[...]).astype(o_ref.dtype)


def _memkv(mem2, g_mem, w_kv, batch):
    return pl.pallas_call(
        _memkv_kernel,
        grid=(batch,),
        in_specs=[
            pl.BlockSpec((MEM_LEN, D_MODEL), lambda b: (b, 0)),
            pl.BlockSpec((1, D_MODEL), lambda b: (0, 0)),
            pl.BlockSpec(w_kv.shape, lambda b: (0, 0)),
        ],
        out_specs=pl.BlockSpec((MEM_LEN, w_kv.shape[1]), lambda b: (b, 0)),
        out_shape=jax.ShapeDtypeStruct((batch * MEM_LEN, w_kv.shape[1]), jnp.bfloat16),
        compiler_params=pltpu.CompilerParams(
            dimension_semantics=("parallel",), vmem_limit_bytes=VMEM_LIMIT),
        name="memkv",
    )(mem2, g_mem, w_kv)


def _memattn_kernel(q_ref, kv_ref, o_ref):
    scale = XA_HEAD_DIM ** -0.5
    outs = []
    for h in range(XA_HEADS):
        q = q_ref[:, h * LANES:(h + 1) * LANES]
        k = kv_ref[:, h * LANES:(h + 1) * LANES]
        v = kv_ref[:, (XA_HEADS + h) * LANES:(XA_HEADS + h + 1) * LANES]
        s = _dot_nt(q, k) * scale
        m = jnp.max(s, axis=1, keepdims=True)
        p = jnp.exp(s - m)
        l = jnp.sum(p, axis=1, keepdims=True)
        outs.append(_dot((p / l).astype(jnp.bfloat16), v))
    o_ref[...] = jnp.concatenate(outs, axis=1).astype(o_ref.dtype)


def _mem_attention(p, mkv, batch, seq):
    nq = seq // MEM_TQ
    w = XA_HEADS * XA_HEAD_DIM
    return pl.pallas_call(
        _memattn_kernel,
        grid=(batch, nq),
        in_specs=[
            pl.BlockSpec((MEM_TQ, w), lambda b, qi: (b * nq + qi, 7)),
            pl.BlockSpec((MEM_LEN, 2 * w), lambda b, qi: (b, 0)),
        ],
        out_specs=pl.BlockSpec((MEM_TQ, w), lambda b, qi: (b * nq + qi, 0)),
        out_shape=jax.ShapeDtypeStruct((batch * seq, w), jnp.bfloat16),
        compiler_params=pltpu.CompilerParams(
            dimension_semantics=("parallel", "parallel"), vmem_limit_bytes=VMEM_LIMIT),
        name="memattn",
    )(p, mkv)


def _merge_kernel(x_ref, yd_ref, ys_ref, ym_ref, gmix_ref, wg_ref, bg_ref, wbr_ref, wout_ref,
                  gffn_ref, wr_ref, br_ref, h_ref, hn_ref, lg_ref):
    xf = x_ref[...]
    n = _rms(xf, gmix_ref[...]).astype(jnp.bfloat16)
    merged = jnp.zeros(xf.shape, jnp.float32)
    for i, y_ref in enumerate((yd_ref, ys_ref, ym_ref)):
        gate = jax.nn.sigmoid(_dot(n, wg_ref[i]) + bg_ref[i])
        merged = merged + gate * _dot(y_ref[...], wbr_ref[i])
    h = xf + _dot(merged.astype(jnp.bfloat16), wout_ref[...])
    h_ref[...] = h
    hn = _rms(h, gffn_ref[...]).astype(jnp.bfloat16)
    hn_ref[...] = hn
    lg_ref[...] = _dot(hn, wr_ref[...]) + br_ref[...]


def _merge(x2, y_diff, y_dsa, y_mem, g_mix, w_g, b_g, w_br, w_out, g_ffn, w_r, b_r):
    t = x2.shape[0]
    row = lambda w: pl.BlockSpec((MERGE_TM, w), lambda i: (i, 0))
    full = lambda a: pl.BlockSpec(a.shape, lambda i: (0,) * a.ndim, pipeline_mode=pl.Buffered(1))
    return pl.pallas_call(
        _merge_kernel,
        grid=(t // MERGE_TM,),
        in_specs=[row(D_MODEL), row(512), row(512), row(512), full(g_mix), full(w_g), full(b_g),
                  full(w_br), full(w_out), full(g_ffn), full(w_r), full(b_r)],
        out_specs=[row(D_MODEL), row(D_MODEL), row(LANES)],
        out_shape=[
            jax.ShapeDtypeStruct((t, D_MODEL), jnp.float32),
            jax.ShapeDtypeStruct((t, D_MODEL), jnp.bfloat16),
            jax.ShapeDtypeStruct((t, LANES), jnp.float32),
        ],
        compiler_params=pltpu.CompilerParams(
            dimension_semantics=("parallel",), vmem_limit_bytes=VMEM_LIMIT),
        name="merge",
    )(x2, y_diff, y_dsa, y_mem, g_mix, w_g, b_g, w_br, w_out, g_ffn, w_r, b_r)


def _route_weights(lg):
    lane = lax.broadcasted_iota(jnp.int32, lg.shape, 1).astype(jnp.float32)
    big = float(LANES)
    gl = jnp.where(lane < N_GROUPS, lg, -jnp.inf)
    gmax = jnp.max(gl, axis=1, keepdims=True)
    grp = jnp.min(jnp.where(gl == gmax, lane, big), axis=1, keepdims=True)
    gsum = jnp.sum(jnp.where(lane < N_GROUPS, jnp.exp(gl - gmax), 0.0), axis=1, keepdims=True)
    p_grp = 1.0 / gsum
    lo = N_GROUPS + grp * EXPERTS_PER_GROUP
    el = jnp.where((lane >= lo) & (lane < lo + EXPERTS_PER_GROUP), lg, -jnp.inf)
    e1 = jnp.max(el, axis=1, keepdims=True)
    i1 = jnp.min(jnp.where(el == e1, lane, big), axis=1, keepdims=True)
    el2 = jnp.where(lane == i1, -jnp.inf, el)
    e2 = jnp.max(el2, axis=1, keepdims=True)
    i2 = jnp.min(jnp.where(el2 == e2, lane, big), axis=1, keepdims=True)
    r = jnp.exp(e2 - e1)
    w1 = p_grp / (1.0 + r)
    w2 = p_grp * r / (1.0 + r)
    return jnp.where(lane == i1, w1, 0.0) + jnp.where(lane == i2, w2, 0.0)


def _moe_kernel(hn_ref, lg_ref, h_ref, win_ref, wout_ref, gfin_ref, o_ref, acc, cw):
    e = pl.program_id(1)

    @pl.when(e == 0)
    def _():
        acc[...] = jnp.zeros(acc.shape, jnp.float32)
        cw[...] = _route_weights(lg_ref[...])

    lane = lax.broadcasted_iota(jnp.int32, cw.shape, 1)
    cw_e = jnp.sum(jnp.where(lane == N_GROUPS + e, cw[...], 0.0), axis=1, keepdims=True)
    gu = _dot(hn_ref[...], win_ref[0])
    g = gu[:, :D_EXPERT]
    hid = (g * jax.nn.sigmoid(g)) * gu[:, D_EXPERT:]
    acc[...] += cw_e * _dot(hid.astype(jnp.bfloat16), wout_ref[0])

    @pl.when(e == N_EXPERTS - 1)
    def _():
        o_ref[...] = _rms(h_ref[...] + acc[...], gfin_ref[...])


def _moe(hn, lg, h, w_e_in, w_e_out, g_final):
    t = hn.shape[0]
    return pl.pallas_call(
        _moe_kernel,
        grid=(t // MOE_TM, N_EXPERTS),
        in_specs=[
            pl.BlockSpec((MOE_TM, D_MODEL), lambda i, e: (i, 0)),
            pl.BlockSpec((MOE_TM, LANES), lambda i, e: (i, 0)),
            pl.BlockSpec((MOE_TM, D_MODEL), lambda i, e: (i, 0)),
            pl.BlockSpec((1, D_MODEL, 2 * D_EXPERT), lambda i, e: (e, 0, 0)),
            pl.BlockSpec((1, D_EXPERT, D_MODEL), lambda i, e: (e, 0, 0)),
            pl.BlockSpec((1, D_MODEL), lambda i, e: (0, 0)),
        ],
        out_specs=pl.BlockSpec((MOE_TM, D_MODEL), lambda i, e: (i, 0)),
        out_shape=jax.ShapeDtypeStruct((t, D_MODEL), jnp.float32),
        scratch_shapes=[
            pltpu.VMEM((MOE_TM, D_MODEL), jnp.float32),
            pltpu.VMEM((MOE_TM, LANES), jnp.float32),
        ],
        compiler_params=pltpu.CompilerParams(
            dimension_semantics=("parallel", "arbitrary"), vmem_limit_bytes=VMEM_LIMIT),
        name="moe",
    )(hn, lg, h, w_e_in, w_e_out, g_final)


def kernel(x, positions, mem, g_mix, w_in, b_gate, lambda_q1, lambda_k1, lambda_q2, lambda_k2,
           g_diff_sub, g_mem, w_mem_kv, w_br_diff, w_br_dsa, w_br_mem, w_out, g_ffn,
           w_route_group, b_route_group, w_route_expert, b_route_expert, w_exp_in, w_exp_out,
           g_final):
    b, s, d = x.shape
    t = b * s
    bf = jnp.bfloat16
    top_k = min(TOPK_MAX, s // 4)
    assert d == D_MODEL and s % DIFF_T == 0 and s % DSA_TQ == 0 and top_k <= DSA_TQ
    assert PROJ_TM == DIFF_T and PROJ_TM % DSA_TQ == 0, "proj writes V^T in the attention kernels' key tiles"
    assert g_mix.shape[0] == 1, "single layer"
    lam_init = 0.8 - 0.6 * math.exp(-0.3 * 0)

    wi = w_in[0]
    c = 512
    seg = lambda k: wi[:, k * c:(k + 1) * c]
    o_ik = 7 * c
    w_ik = wi[:, o_ik:o_ik + IDX_DIM]
    w_iw = wi[:, o_ik + IDX_DIM:o_ik + IDX_DIM + IDX_HEADS]
    o_mq = o_ik + IDX_DIM + IDX_HEADS
    w_mq = wi[:, o_mq:o_mq + c]
    w_gl = wi[:, o_mq + c:]
    qs = DIFF_QK_DIM ** -0.5
    w_a = jnp.concatenate([seg(0) * qs, seg(1), seg(2), seg(3) * qs, seg(4), seg(5), seg(6) * qs, w_mq],
                          axis=1).astype(bf)
    w_s = jnp.concatenate([w_ik, w_ik, w_iw, jnp.zeros((d, LANES - IDX_HEADS), wi.dtype)], axis=1).astype(bf)
    w_g = w_gl.reshape(d, 3, d).transpose(1, 0, 2).astype(bf)
    b_g = b_gate[0].reshape(3, 1, d)
    w_br = jnp.stack([w_br_diff[0], w_br_dsa[0], w_br_mem[0]]).astype(bf)
    w_r = jnp.concatenate([w_route_group[0], w_route_expert[0],
                           jnp.zeros((d, LANES - N_GROUPS - N_EXPERTS), wi.dtype)], axis=1).astype(bf)
    b_r = jnp.concatenate([b_route_group[0], b_route_expert[0],
                           jnp.zeros((LANES - N_GROUPS - N_EXPERTS,), jnp.float32)]).reshape(1, LANES)

    rot = IDX_DIM // ROPE_FRACTION
    inv_freq = ROPE_THETA ** (-jnp.arange(0, rot, 2, dtype=jnp.float32) / rot)
    inv64 = jnp.concatenate([inv_freq, inv_freq, jnp.zeros((IDX_DIM - rot,), jnp.float32)])
    inv_lane = jnp.concatenate([inv64, inv64]).reshape(1, LANES)

    x2 = x.reshape(t, d)
    pos2 = positions.reshape(t, 1)
    p, ikk, dvt, svt, iwt = _proj(x2, pos2, g_mix, inv_lane, w_a, w_s)

    y_diff = _diff_attention(p, dvt, lambda_q1, lambda_k1, lambda_q2, lambda_k2,
                             g_diff_sub.reshape(DIFF_V_DIM, 1), b, s, lam_init)
    y_dsa = _dsa_attention(p, ikk, iwt, svt, b, s, top_k)
    mkv = _memkv(mem.reshape(b * MEM_LEN, d), g_mem, w_mem_kv[0].astype(bf), b)
    y_mem = _mem_attention(p, mkv, b, s)

    h, hn, lg = _merge(x2, y_diff, y_dsa, y_mem, g_mix, w_g, b_g, w_br, w_out[0].astype(bf), g_ffn, w_r, b_r)
    out = _moe(hn, lg, h, w_exp_in[0].astype(bf), w_exp_out[0].astype(bf), g_final.reshape(1, d))
    return out.reshape(b, s, d)
```

```python
import functools
import math

import jax
import jax.numpy as jnp
from jax import lax
from jax.experimental import pallas as pl
from jax.experimental.pallas import tpu as pltpu

D_MODEL = 1024
MEM_LEN = 256
XA_HEADS = 4
XA_HEAD_DIM = 128
DIFF_HEADS = 4
DIFF_QK_DIM = 64
DIFF_V_DIM = 128
DSA_HEADS = 8
DSA_HEAD_DIM = 64
IDX_HEADS = 8
IDX_DIM = 64
TOPK_MAX = 256
ROPE_THETA = 500000.0
ROPE_FRACTION = 4
N_GROUPS = 4
EXPERTS_PER_GROUP = 4
N_EXPERTS = 16
D_EXPERT = 512
EPS = 1e-6

LANES = 128
SUBLANES = 8
VMEM_LIMIT = 56 * 1024 * 1024
NEG_BIG = -1e30

PROJ_TM = 512
PROJ_TN = 512
DIFF_T = 512
DSA_TQ = 256
MEM_TQ = 512
MERGE_TM = 512
MOE_TM = 512

_NT = (((1,), (1,)), ((), ()))


def _dot(a, b):
    return jnp.dot(a, b, preferred_element_type=jnp.float32)


def _dot_nt(a, b):
    return lax.dot_general(a, b, _NT, preferred_element_type=jnp.float32)


def _rms(xf, g):
    return xf * lax.rsqrt(jnp.mean(xf * xf, axis=-1, keepdims=True) + EPS) * g


def _proj_kernel(x_ref, pos_ref, g_ref, inv_ref, wa_ref, ws_ref,
                 p_ref, ikk_ref, dvt_ref, svt_ref, iwt_ref, n_scr, cos_scr, sa_scr, sb_scr):
    j = pl.program_id(1)

    def rope(v, reps):
        n = v.shape[1]
        c = jnp.concatenate([cos_scr[...]] * reps, axis=1) if reps > 1 else cos_scr[...]
        a = jnp.concatenate([sa_scr[...]] * reps, axis=1) if reps > 1 else sa_scr[...]
        b = jnp.concatenate([sb_scr[...]] * reps, axis=1) if reps > 1 else sb_scr[...]
        up = pltpu.roll(v, n - 8, 1)
        dn = pltpu.roll(v, 8, 1)
        return v * c + up * a + dn * b

    @pl.when(j == 0)
    def _():
        xf = x_ref[...]
        n_scr[...] = _rms(xf, g_ref[...]).astype(jnp.bfloat16)
        ang = pos_ref[...].astype(jnp.float32) * inv_ref[...]
        lane = lax.broadcasted_iota(jnp.int32, ang.shape, 1) % 64
        cs = jnp.cos(ang)
        sn = jnp.sin(ang)
        cos_scr[...] = jnp.where(lane < 16, cs, 1.0)
        sa_scr[...] = jnp.where(lane < 8, -sn, 0.0)
        sb_scr[...] = jnp.where((lane >= 8) & (lane < 16), sn, 0.0)
        small = _dot(n_scr[...], ws_ref[...])
        ikk_ref[...] = rope(small[:, :LANES], 1).astype(jnp.bfloat16)
        w = small[:, LANES:] * (IDX_HEADS ** -0.5)
        iwt_ref[...] = w.T[:IDX_HEADS, :]

    acc = _dot(n_scr[...], wa_ref[...])
    is_rope = (j == 0) | (j == 1) | (j == 3) | (j == 4) | (j == 6)

    @pl.when(is_rope)
    def _():
        p_ref[...] = rope(acc, PROJ_TN // LANES).astype(jnp.bfloat16)

    @pl.when(jnp.logical_not(is_rope))
    def _():
        p_ref[...] = acc.astype(jnp.bfloat16)

    @pl.when(j == 2)
    def _():
        dvt_ref[0] = acc.T.astype(jnp.bfloat16)

    @pl.when(j == 5)
    def _():
        at = acc.T.astype(jnp.bfloat16)
        for r in range(PROJ_TM // DSA_TQ):
            svt_ref[r] = at[:, r * DSA_TQ:(r + 1) * DSA_TQ]


def _proj(x2, pos2, g_mix, inv_lane, w_a, w_s):
    t = x2.shape[0]
    ncol = w_a.shape[1] // PROJ_TN
    cpt = PROJ_TM // DSA_TQ
    return pl.pallas_call(
        _proj_kernel,
        grid=(t // PROJ_TM, ncol),
        in_specs=[
            pl.BlockSpec((PROJ_TM, D_MODEL), lambda i, j: (i, 0)),
            pl.BlockSpec((PROJ_TM, 1), lambda i, j: (i, 0)),
            pl.BlockSpec((1, D_MODEL), lambda i, j: (0, 0)),
            pl.BlockSpec((1, LANES), lambda i, j: (0, 0)),
            pl.BlockSpec((D_MODEL, PROJ_TN), lambda i, j: (0, j)),
            pl.BlockSpec((D_MODEL, 2 * LANES), lambda i, j: (0, 0)),
        ],
        out_specs=[
            pl.BlockSpec((PROJ_TM, PROJ_TN), lambda i, j: (i, j)),
            pl.BlockSpec((PROJ_TM, LANES), lambda i, j: (i, 0)),
            pl.BlockSpec((1, PROJ_TN, PROJ_TM), lambda i, j: (i, 0, 0)),
            pl.BlockSpec((cpt, PROJ_TN, DSA_TQ), lambda i, j: (i, 0, 0)),
            pl.BlockSpec((IDX_HEADS, PROJ_TM), lambda i, j: (0, i)),
        ],
        out_shape=[
            jax.ShapeDtypeStruct((t, w_a.shape[1]), jnp.bfloat16),
            jax.ShapeDtypeStruct((t, LANES), jnp.bfloat16),
            jax.ShapeDtypeStruct((t // PROJ_TM, PROJ_TN, PROJ_TM), jnp.bfloat16),
            jax.ShapeDtypeStruct((t // DSA_TQ, PROJ_TN, DSA_TQ), jnp.bfloat16),
            jax.ShapeDtypeStruct((IDX_HEADS, t), jnp.float32),
        ],
        scratch_shapes=[
            pltpu.VMEM((PROJ_TM, D_MODEL), jnp.bfloat16),
            pltpu.VMEM((PROJ_TM, LANES), jnp.float32),
            pltpu.VMEM((PROJ_TM, LANES), jnp.float32),
            pltpu.VMEM((PROJ_TM, LANES), jnp.float32),
        ],
        compiler_params=pltpu.CompilerParams(
            dimension_semantics=("parallel", "arbitrary"), vmem_limit_bytes=VMEM_LIMIT),
        name="proj",
    )(x2, pos2, g_mix, inv_lane, w_a, w_s)


def _diff_kernel(q_ref, k_ref, vt_ref, lq1_ref, lk1_ref, lq2_ref, lk2_ref, gs_ref, o_ref,
                 qm, m_s, l_s, acc, st_a, st_b, p_scr, *, lam_init):
    qi = pl.program_id(2)
    t = DIFF_T
    q = q_ref[...]
    lane = lax.broadcasted_iota(jnp.int32, q.shape, 1)
    qm[0] = jnp.where(lane < DIFF_QK_DIM, q, jnp.zeros_like(q))
    qm[1] = jnp.where(lane >= DIFF_QK_DIM, q, jnp.zeros_like(q))
    m_s[...] = jnp.full(m_s.shape, NEG_BIG, jnp.float32)
    l_s[...] = jnp.zeros(l_s.shape, jnp.float32)
    acc[...] = jnp.zeros(acc.shape, jnp.float32)

    def qk(j, st_ref):
        k = k_ref[pl.ds(pl.multiple_of(j * t, t), t), :]
        for i in range(2):
            st_ref[i] = _dot_nt(k, qm[i])

    def softmax_pv(j, st_ref, diagonal):
        vt = vt_ref[j]
        if diagonal:
            krow = lax.broadcasted_iota(jnp.int32, (t, t), 0)
            qcol = lax.broadcasted_iota(jnp.int32, (t, t), 1)
            keep = krow <= qcol
        alphas = []
        for i in range(2):
            st = st_ref[i]
            if diagonal:
                st = jnp.where(keep, st, NEG_BIG)
            m_old = m_s[i]
            m_new = jnp.maximum(m_old, jnp.max(st, axis=0, keepdims=True))
            alpha = jnp.exp(m_old - m_new)
            p = jnp.exp(st - m_new)
            l_s[i] = alpha * l_s[i] + jnp.sum(p, axis=0, keepdims=True)
            m_s[i] = m_new
            p_scr[i] = p.astype(jnp.bfloat16)
            alphas.append(alpha)
        for i in range(2):
            acc[i] = alphas[i] * acc[i] + _dot(vt, p_scr[i])

    qk(0, st_a)

    def pair(tt, carry):
        j = 2 * tt
        qk(j + 1, st_b)
        softmax_pv(j, st_a, False)
        qk(j + 2, st_a)
        softmax_pv(j + 1, st_b, False)
        return carry

    lax.fori_loop(0, qi // 2, pair, 0)

    @pl.when(qi % 2 == 0)
    def _():
        softmax_pv(qi, st_a, True)

    @pl.when(qi % 2 == 1)
    def _():
        qk(qi, st_b)
        softmax_pv(qi - 1, st_a, False)
        softmax_pv(qi, st_b, True)

    lam = (jnp.exp(jnp.sum(lq1_ref[...] * lk1_ref[...], axis=1, keepdims=True))
           - jnp.exp(jnp.sum(lq2_ref[...] * lk2_ref[...], axis=1, keepdims=True))
           + lam_init)
    ot = acc[0] / l_s[0] - lam * (acc[1] / l_s[1])
    yt = ot * lax.rsqrt(jnp.mean(ot * ot, axis=0, keepdims=True) + EPS) * gs_ref[...]
    o_ref[...] = (yt * (1.0 - lam_init)).T.astype(o_ref.dtype)


def _diff_attention(p, dvt, lq1, lk1, lq2, lk2, g_sub_col, batch, seq, lam_init):
    nb = seq // DIFF_T
    vec = pl.BlockSpec((1, DIFF_QK_DIM), lambda b, h, qi: (0, 0))
    return pl.pallas_call(
        functools.partial(_diff_kernel, lam_init=lam_init),
        grid=(batch, DIFF_HEADS, nb),
        in_specs=[
            pl.BlockSpec((DIFF_T, LANES), lambda b, h, qi: (b * nb + qi, h)),
            pl.BlockSpec((seq, LANES), lambda b, h, qi: (b, DIFF_HEADS + h)),
            pl.BlockSpec((nb, DIFF_V_DIM, DIFF_T), lambda b, h, qi: (b, h, 0)),
            vec, vec, vec, vec,
            pl.BlockSpec((DIFF_V_DIM, 1), lambda b, h, qi: (0, 0)),
        ],
        out_specs=pl.BlockSpec((DIFF_T, LANES), lambda b, h, qi: (b * nb + qi, h)),
        out_shape=jax.ShapeDtypeStruct((batch * seq, DIFF_HEADS * DIFF_V_DIM), jnp.bfloat16),
        scratch_shapes=[
            pltpu.VMEM((2, DIFF_T, LANES), jnp.bfloat16),
            pltpu.VMEM((2, 1, DIFF_T), jnp.float32),
            pltpu.VMEM((2, 1, DIFF_T), jnp.float32),
            pltpu.VMEM((2, DIFF_V_DIM, DIFF_T), jnp.float32),
            pltpu.VMEM((2, DIFF_T, DIFF_T), jnp.float32),
            pltpu.VMEM((2, DIFF_T, DIFF_T), jnp.float32),
            pltpu.VMEM((2, DIFF_T, DIFF_T), jnp.bfloat16),
        ],
        compiler_params=pltpu.CompilerParams(
            dimension_semantics=("parallel", "parallel", "arbitrary"),
            vmem_limit_bytes=VMEM_LIMIT),
        name="diffattn",
    )(p, p, dvt, lq1, lk1, lq2, lk2, g_sub_col)


def _key_to_float(key):
    bits = jnp.where(key >= 0, key, key ^ jnp.int32(0x7FFFFFFF))
    return lax.bitcast_convert_type(bits, jnp.float32)


def _count_rows(hit):
    tk, tq = hit.shape
    return jnp.sum(hit.reshape(tk // (4 * SUBLANES), 4 * SUBLANES, tq), axis=0)


def _dsa_kernel(sq_ref, iq_ref, iwt_ref, ikk_ref, sk_ref, svt_ref, o_ref,
                score, hi16, lo16, iqm, sqm, acc, m_s, l_s, thr, jcut, st_a, st_b, p_scr, *, seq, top_k):
    qi = pl.program_id(1)
    tq = DSA_TQ
    nkc = qi + 1
    lane = lax.broadcasted_iota(jnp.int32, (tq, LANES), 1)
    half = lane // DSA_HEAD_DIM

    for h in range(DSA_HEADS):
        pr = h // 2
        iqp = iq_ref[:, pr * LANES:(pr + 1) * LANES]
        sqp = sq_ref[:, pr * LANES:(pr + 1) * LANES]
        iqm[h] = jnp.where(half == h % 2, iqp, jnp.zeros_like(iqp))
        sqm[h] = jnp.where(half == h % 2, sqp, jnp.zeros_like(sqp))

    krow = lax.broadcasted_iota(jnp.int32, (tq, tq), 0)
    qcol = lax.broadcasted_iota(jnp.int32, (tq, tq), 1)

    def score_chunk(c, carry):
        kk = ikk_ref[pl.ds(pl.multiple_of(c * tq, tq), tq), :]
        sc = jnp.zeros((tq, tq), jnp.float32)
        for h in range(IDX_HEADS):
            sc = sc + iwt_ref[h:h + 1, :] * jnp.maximum(_dot_nt(kk, iqm[h]), 0.0)
        sc = jnp.where(krow <= qcol + (qi - c) * tq, sc, -jnp.inf)
        score[c] = sc
        bits = lax.bitcast_convert_type(sc, jnp.int32)
        okey = jnp.where(bits >= 0, bits, bits ^ jnp.int32(0x7FFFFFFF))
        hi16[c] = lax.shift_right_arithmetic(okey, 16).astype(jnp.int16)
        lo16[c] = ((okey & 0xFFFF) - 2 ** 15).astype(jnp.int16)
        return carry

    lax.fori_loop(0, nkc, score_chunk, 0)

    zero_cnt = jnp.zeros((4 * SUBLANES, tq), jnp.float32)
    i16_min = -2 ** 15

    def count16(buf, pred):
        def body(c, cnt):
            hit = jnp.where(pred(buf[c]), jnp.int16(1), jnp.int16(0))
            h3 = hit.reshape(tq // (4 * SUBLANES), 4 * SUBLANES, tq)
            part = h3[0]
            for r in range(1, h3.shape[0]):
                part = part + h3[r]
            return cnt + part
        cnt = lax.fori_loop(0, nkc, body, jnp.zeros((4 * SUBLANES, tq), jnp.int16))
        return jnp.sum(cnt.astype(jnp.int32), axis=0, keepdims=True)

    def bisect16(buf, want):
        def bit_step(i, cur):
            cand = cur + lax.shift_left(jnp.int32(1), jnp.int32(15) - i)
            c16 = cand.astype(jnp.int16)
            return jnp.where(count16(buf, lambda blk: blk >= c16) >= want, cand, cur)
        return lax.fori_loop(0, 16, bit_step, jnp.full((1, tq), i16_min, jnp.int32))

    t_hi = bisect16(hi16, top_k)
    t_hi16 = t_hi.astype(jnp.int16)
    rest = top_k - count16(hi16, lambda blk: blk > t_hi16)

    def bucket_only(c, carry):
        lo16[c] = jnp.where(hi16[c] == t_hi16, lo16[c], jnp.int16(i16_min))
        return carry

    lax.fori_loop(0, nkc, bucket_only, 0)
    t_lo = bisect16(lo16, rest)
    key = lax.shift_left(t_hi, 16) + (t_lo - i16_min)
    t_f = _key_to_float(key)

    def count_gt_ge(c, carry):
        gt, ge = carry
        blk = score[c]
        return (gt + _count_rows(jnp.where(blk > t_f, 1.0, 0.0)),
                ge + _count_rows(jnp.where(blk >= t_f, 1.0, 0.0)))

    gt, ge = lax.fori_loop(0, nkc, count_gt_ge, (zero_cnt, zero_cnt))
    n_gt = jnp.sum(gt, axis=0, keepdims=True)
    n_ge = jnp.sum(ge, axis=0, keepdims=True)
    q_pos = qi * tq + lax.broadcasted_iota(jnp.int32, (1, tq), 1)
    few = q_pos < top_k - 1
    thr[...] = jnp.where(few, -jnp.inf, t_f)
    jcut[...] = jnp.where(few, -1, seq)
    need = float(top_k) - n_gt
    split = jnp.logical_and(jnp.logical_not(few), n_ge > float(top_k))

    @pl.when(jnp.max(jnp.where(split, 1.0, 0.0)) > 0.0)
    def _():
        def count_eq_below(jc):
            def body(c, cnt):
                hit = jnp.where((score[c] == t_f) & (c * tq + krow < jc), 1.0, 0.0)
                return cnt + _count_rows(hit)
            return jnp.sum(lax.fori_loop(0, nkc, body, zero_cnt), axis=0, keepdims=True)

        nbits = (seq - 1).bit_length()

        def jbit(i, jv):
            cand = jv + lax.shift_left(jnp.int32(1), jnp.int32(nbits - 1) - i)
            return jnp.where(count_eq_below(cand) < need, cand, jv)

        jv = lax.fori_loop(0, nbits, jbit, jnp.zeros((1, tq), jnp.int32))
        jcut[...] = jnp.where(split, jv, jcut[...])

    m_s[...] = jnp.full(m_s.shape, NEG_BIG, jnp.float32)
    l_s[...] = jnp.zeros(l_s.shape, jnp.float32)
    acc[...] = jnp.zeros(acc.shape, jnp.float32)

    def qk(c, st_ref):
        off = pl.multiple_of(c * tq, tq)
        for h in range(DSA_HEADS):
            kp = sk_ref[pl.ds(off, tq), (h // 2) * LANES:(h // 2 + 1) * LANES]
            st_ref[h] = _dot_nt(kp, sqm[h])

    def softmax_pv(c, st_ref):
        blk = score[c]
        tt = thr[...]
        keep = (blk > tt) | ((blk == tt) & (c * tq + krow <= jcut[...]))
        bias = jnp.where(keep, 0.0, NEG_BIG)
        alphas = []
        for h in range(DSA_HEADS):
            st = st_ref[h] + bias
            m_old = m_s[h]
            m_new = jnp.maximum(m_old, jnp.max(st, axis=0, keepdims=True))
            alpha = jnp.exp(m_old - m_new)
            p = jnp.exp(st - m_new)
            l_s[h] = alpha * l_s[h] + jnp.sum(p, axis=0, keepdims=True)
            m_s[h] = m_new
            p_scr[h] = p.astype(jnp.bfloat16)
            alphas.append(alpha)
        for h in range(DSA_HEADS):
            vt = svt_ref[c, h * DSA_HEAD_DIM:(h + 1) * DSA_HEAD_DIM, :]
            acc[h] = alphas[h] * acc[h] + _dot(vt, p_scr[h])

    qk(0, st_a)

    def pair(tt, carry):
        c = 2 * tt
        qk(c + 1, st_b)
        softmax_pv(c, st_a)
        qk(c + 2, st_a)
        softmax_pv(c + 1, st_b)
        return carry

    lax.fori_loop(0, qi // 2, pair, 0)

    @pl.when(qi % 2 == 0)
    def _():
        softmax_pv(qi, st_a)

    @pl.when(qi % 2 == 1)
    def _():
        qk(qi, st_b)
        softmax_pv(qi - 1, st_a)
        softmax_pv(qi, st_b)

    outs = [acc[h] / l_s[h] for h in range(DSA_HEADS)]
    o_ref[...] = jnp.concatenate(outs, axis=0).T.astype(o_ref.dtype)


def _dsa_attention(p, ikk, iwt, svt, batch, seq, top_k):
    nq = seq // DSA_TQ
    w = DSA_HEADS * DSA_HEAD_DIM
    once = pl.Buffered(1)
    return pl.pallas_call(
        functools.partial(_dsa_kernel, seq=seq, top_k=top_k),
        grid=(batch, nq),
        in_specs=[
            pl.BlockSpec((DSA_TQ, w), lambda b, qi: (b * nq + qi, 3)),
            pl.BlockSpec((DSA_TQ, w), lambda b, qi: (b * nq + qi, 6)),
            pl.BlockSpec((IDX_HEADS, DSA_TQ), lambda b, qi: (0, b * nq + qi)),
            pl.BlockSpec((seq, LANES), lambda b, qi: (b, 0), pipeline_mode=once),
            pl.BlockSpec((seq, w), lambda b, qi: (b, 4), pipeline_mode=once),
            pl.BlockSpec((nq, w, DSA_TQ), lambda b, qi: (b, 0, 0), pipeline_mode=once),
        ],
        out_specs=pl.BlockSpec((DSA_TQ, w), lambda b, qi: (b * nq + qi, 0)),
        out_shape=jax.ShapeDtypeStruct((batch * seq, w), jnp.bfloat16),
        scratch_shapes=[
            pltpu.VMEM((nq, DSA_TQ, DSA_TQ), jnp.float32),
            pltpu.VMEM((nq, DSA_TQ, DSA_TQ), jnp.int16),
            pltpu.VMEM((nq, DSA_TQ, DSA_TQ), jnp.int16),
            pltpu.VMEM((IDX_HEADS, DSA_TQ, LANES), jnp.bfloat16),
            pltpu.VMEM((DSA_HEADS, DSA_TQ, LANES), jnp.bfloat16),
            pltpu.VMEM((DSA_HEADS, DSA_HEAD_DIM, DSA_TQ), jnp.float32),
            pltpu.VMEM((DSA_HEADS, 1, DSA_TQ), jnp.float32),
            pltpu.VMEM((DSA_HEADS, 1, DSA_TQ), jnp.float32),
            pltpu.VMEM((1, DSA_TQ), jnp.float32),
            pltpu.VMEM((1, DSA_TQ), jnp.int32),
            pltpu.VMEM((DSA_HEADS, DSA_TQ, DSA_TQ), jnp.float32),
            pltpu.VMEM((DSA_HEADS, DSA_TQ, DSA_TQ), jnp.float32),
            pltpu.VMEM((DSA_HEADS, DSA_TQ, DSA_TQ), jnp.bfloat16),
        ],
        compiler_params=pltpu.CompilerParams(
            dimension_semantics=("parallel", "arbitrary"), vmem_limit_bytes=VMEM_LIMIT),
        name="dsa",
    )(p, p, iwt, ikk, p, svt)


def _memkv_kernel(mem_ref, g_ref, w_ref, o_ref):
    n = _rms(mem_ref[...], g_ref[...]).astype(jnp.bfloat16)
    o_ref[...] = _dot(n, w_ref[...]).astype(o_ref.dtype)


def _memkv(mem2, g_mem, w_kv, batch):
    return pl.pallas_call(
        _memkv_kernel,
        grid=(batch,),
        in_specs=[
            pl.BlockSpec((MEM_LEN, D_MODEL), lambda b: (b, 0)),
            pl.BlockSpec((1, D_MODEL), lambda b: (0, 0)),
            pl.BlockSpec(w_kv.shape, lambda b: (0, 0)),
        ],
        out_specs=pl.BlockSpec((MEM_LEN, w_kv.shape[1]), lambda b: (b, 0)),
        out_shape=jax.ShapeDtypeStruct((batch * MEM_LEN, w_kv.shape[1]), jnp.bfloat16),
        compiler_params=pltpu.CompilerParams(
            dimension_semantics=("parallel",), vmem_limit_bytes=VMEM_LIMIT),
        name="memkv",
    )(mem2, g_mem, w_kv)


def _memattn_kernel(q_ref, kv_ref, o_ref):
    scale = XA_HEAD_DIM ** -0.5
    outs = []
    for h in range(XA_HEADS):
        q = q_ref[:, h * LANES:(h + 1) * LANES]
        k = kv_ref[:, h * LANES:(h + 1) * LANES]
        v = kv_ref[:, (XA_HEADS + h) * LANES:(XA_HEADS + h + 1) * LANES]
        s = _dot_nt(q, k) * scale
        m = jnp.max(s, axis=1, keepdims=True)
        p = jnp.exp(s - m)
        l = jnp.sum(p, axis=1, keepdims=True)
        outs.append(_dot((p / l).astype(jnp.bfloat16), v))
    o_ref[...] = jnp.concatenate(outs, axis=1).astype(o_ref.dtype)


def _mem_attention(p, mkv, batch, seq):
    nq = seq // MEM_TQ
    w = XA_HEADS * XA_HEAD_DIM
    return pl.pallas_call(
        _memattn_kernel,
        grid=(batch, nq),
        in_specs=[
            pl.BlockSpec((MEM_TQ, w), lambda b, qi: (b * nq + qi, 7)),
            pl.BlockSpec((MEM_LEN, 2 * w), lambda b, qi: (b, 0)),
        ],
        out_specs=pl.BlockSpec((MEM_TQ, w), lambda b, qi: (b * nq + qi, 0)),
        out_shape=jax.ShapeDtypeStruct((batch * seq, w), jnp.bfloat16),
        compiler_params=pltpu.CompilerParams(
            dimension_semantics=("parallel", "parallel"), vmem_limit_bytes=VMEM_LIMIT),
        name="memattn",
    )(p, mkv)


def _merge_kernel(x_ref, yd_ref, ys_ref, ym_ref, gmix_ref, wg_ref, bg_ref, wbr_ref, wout_ref,
                  gffn_ref, wr_ref, br_ref, h_ref, hn_ref, lg_ref):
    xf = x_ref[...]
    n = _rms(xf, gmix_ref[...]).astype(jnp.bfloat16)
    merged = jnp.zeros(xf.shape, jnp.float32)
    for i, y_ref in enumerate((yd_ref, ys_ref, ym_ref)):
        gate = jax.nn.sigmoid(_dot(n, wg_ref[i]) + bg_ref[i])
        merged = merged + gate * _dot(y_ref[...], wbr_ref[i])
    h = xf + _dot(merged.astype(jnp.bfloat16), wout_ref[...])
    h_ref[...] = h
    hn = _rms(h, gffn_ref[...]).astype(jnp.bfloat16)
    hn_ref[...] = hn
    lg_ref[...] = _dot(hn, wr_ref[...]) + br_ref[...]


def _merge(x2, y_diff, y_dsa, y_mem, g_mix, w_g, b_g, w_br, w_out, g_ffn, w_r, b_r):
    t = x2.shape[0]
    row = lambda w: pl.BlockSpec((MERGE_TM, w), lambda i: (i, 0))
    full = lambda a: pl.BlockSpec(a.shape, lambda i: (0,) * a.ndim, pipeline_mode=pl.Buffered(1))
    return pl.pallas_call(
        _merge_kernel,
        grid=(t // MERGE_TM,),
        in_specs=[row(D_MODEL), row(512), row(512), row(512), full(g_mix), full(w_g), full(b_g),
                  full(w_br), full(w_out), full(g_ffn), full(w_r), full(b_r)],
        out_specs=[row(D_MODEL), row(D_MODEL), row(LANES)],
        out_shape=[
            jax.ShapeDtypeStruct((t, D_MODEL), jnp.float32),
            jax.ShapeDtypeStruct((t, D_MODEL), jnp.bfloat16),
            jax.ShapeDtypeStruct((t, LANES), jnp.float32),
        ],
        compiler_params=pltpu.CompilerParams(
            dimension_semantics=("parallel",), vmem_limit_bytes=VMEM_LIMIT),
        name="merge",
    )(x2, y_diff, y_dsa, y_mem, g_mix, w_g, b_g, w_br, w_out, g_ffn, w_r, b_r)


def _route_weights(lg):
    lane = lax.broadcasted_iota(jnp.int32, lg.shape, 1).astype(jnp.float32)
    big = float(LANES)
    gl = jnp.where(lane < N_GROUPS, lg, -jnp.inf)
    gmax = jnp.max(gl, axis=1, keepdims=True)
    grp = jnp.min(jnp.where(gl == gmax, lane, big), axis=1, keepdims=True)
    gsum = jnp.sum(jnp.where(lane < N_GROUPS, jnp.exp(gl - gmax), 0.0), axis=1, keepdims=True)
    p_grp = 1.0 / gsum
    lo = N_GROUPS + grp * EXPERTS_PER_GROUP
    el = jnp.where((lane >= lo) & (lane < lo + EXPERTS_PER_GROUP), lg, -jnp.inf)
    e1 = jnp.max(el, axis=1, keepdims=True)
    i1 = jnp.min(jnp.where(el == e1, lane, big), axis=1, keepdims=True)
    el2 = jnp.where(lane == i1, -jnp.inf, el)
    e2 = jnp.max(el2, axis=1, keepdims=True)
    i2 = jnp.min(jnp.where(el2 == e2, lane, big), axis=1, keepdims=True)
    r = jnp.exp(e2 - e1)
    w1 = p_grp / (1.0 + r)
    w2 = p_grp * r / (1.0 + r)
    return jnp.where(lane == i1, w1, 0.0) + jnp.where(lane == i2, w2, 0.0)


def _moe_kernel(hn_ref, lg_ref, h_ref, win_ref, wout_ref, gfin_ref, o_ref, acc, cw):
    e = pl.program_id(1)

    @pl.when(e == 0)
    def _():
        acc[...] = jnp.zeros(acc.shape, jnp.float32)
        cw[...] = _route_weights(lg_ref[...])

    lane = lax.broadcasted_iota(jnp.int32, cw.shape, 1)
    cw_e = jnp.sum(jnp.where(lane == N_GROUPS + e, cw[...], 0.0), axis=1, keepdims=True)
    gu = _dot(hn_ref[...], win_ref[0])
    g = gu[:, :D_EXPERT]
    hid = (g * jax.nn.sigmoid(g)) * gu[:, D_EXPERT:]
    acc[...] += cw_e * _dot(hid.astype(jnp.bfloat16), wout_ref[0])

    @pl.when(e == N_EXPERTS - 1)
    def _():
        o_ref[...] = _rms(h_ref[...] + acc[...], gfin_ref[...])


def _moe(hn, lg, h, w_e_in, w_e_out, g_final):
    t = hn.shape[0]
    return pl.pallas_call(
        _moe_kernel,
        grid=(t // MOE_TM, N_EXPERTS),
        in_specs=[
            pl.BlockSpec((MOE_TM, D_MODEL), lambda i, e: (i, 0)),
            pl.BlockSpec((MOE_TM, LANES), lambda i, e: (i, 0)),
            pl.BlockSpec((MOE_TM, D_MODEL), lambda i, e: (i, 0)),
            pl.BlockSpec((1, D_MODEL, 2 * D_EXPERT), lambda i, e: (e, 0, 0)),
            pl.BlockSpec((1, D_EXPERT, D_MODEL), lambda i, e: (e, 0, 0)),
            pl.BlockSpec((1, D_MODEL), lambda i, e: (0, 0)),
        ],
        out_specs=pl.BlockSpec((MOE_TM, D_MODEL), lambda i, e: (i, 0)),
        out_shape=jax.ShapeDtypeStruct((t, D_MODEL), jnp.float32),
        scratch_shapes=[
            pltpu.VMEM((MOE_TM, D_MODEL), jnp.float32),
            pltpu.VMEM((MOE_TM, LANES), jnp.float32),
        ],
        compiler_params=pltpu.CompilerParams(
            dimension_semantics=("parallel", "arbitrary"), vmem_limit_bytes=VMEM_LIMIT),
        name="moe",
    )(hn, lg, h, w_e_in, w_e_out, g_final)


def kernel(x, positions, mem, g_mix, w_in, b_gate, lambda_q1, lambda_k1, lambda_q2, lambda_k2,
           g_diff_sub, g_mem, w_mem_kv, w_br_diff, w_br_dsa, w_br_mem, w_out, g_ffn,
           w_route_group, b_route_group, w_route_expert, b_route_expert, w_exp_in, w_exp_out,
           g_final):
    b, s, d = x.shape
    t = b * s
    bf = jnp.bfloat16
    top_k = min(TOPK_MAX, s // 4)
    assert d == D_MODEL and s % DIFF_T == 0 and s % DSA_TQ == 0 and top_k <= DSA_TQ
    assert PROJ_TM == DIFF_T and PROJ_TM % DSA_TQ == 0, "proj writes V^T in the attention kernels' key tiles"
    assert g_mix.shape[0] == 1, "single layer"
    lam_init = 0.8 - 0.6 * math.exp(-0.3 * 0)

    wi = w_in[0]
    c = 512
    seg = lambda k: wi[:, k * c:(k + 1) * c]
    o_ik = 7 * c
    w_ik = wi[:, o_ik:o_ik + IDX_DIM]
    w_iw = wi[:, o_ik + IDX_DIM:o_ik + IDX_DIM + IDX_HEADS]
    o_mq = o_ik + IDX_DIM + IDX_HEADS
    w_mq = wi[:, o_mq:o_mq + c]
    w_gl = wi[:, o_mq + c:]
    qs = DIFF_QK_DIM ** -0.5
    w_a = jnp.concatenate([seg(0) * qs, seg(1), seg(2), seg(3) * qs, seg(4), seg(5), seg(6) * qs, w_mq],
                          axis=1).astype(bf)
    w_s = jnp.concatenate([w_ik, w_ik, w_iw, jnp.zeros((d, LANES - IDX_HEADS), wi.dtype)], axis=1).astype(bf)
    w_g = w_gl.reshape(d, 3, d).transpose(1, 0, 2).astype(bf)
    b_g = b_gate[0].reshape(3, 1, d)
    w_br = jnp.stack([w_br_diff[0], w_br_dsa[0], w_br_mem[0]]).astype(bf)
    w_r = jnp.concatenate([w_route_group[0], w_route_expert[0],
                           jnp.zeros((d, LANES - N_GROUPS - N_EXPERTS), wi.dtype)], axis=1).astype(bf)
    b_r = jnp.concatenate([b_route_group[0], b_route_expert[0],
                           jnp.zeros((LANES - N_GROUPS - N_EXPERTS,), jnp.float32)]).reshape(1, LANES)

    rot = IDX_DIM // ROPE_FRACTION
    inv_freq = ROPE_THETA ** (-jnp.arange(0, rot, 2, dtype=jnp.float32) / rot)
    inv64 = jnp.concatenate([inv_freq, inv_freq, jnp.zeros((IDX_DIM - rot,), jnp.float32)])
    inv_lane = jnp.concatenate([inv64, inv64]).reshape(1, LANES)

    x2 = x.reshape(t, d)
    pos2 = positions.reshape(t, 1)
    p, ikk, dvt, svt, iwt = _proj(x2, pos2, g_mix, inv_lane, w_a, w_s)

    y_diff = _diff_attention(p, dvt, lambda_q1, lambda_k1, lambda_q2, lambda_k2,
                             g_diff_sub.reshape(DIFF_V_DIM, 1), b, s, lam_init)
    y_dsa = _dsa_attention(p, ikk, iwt, svt, b, s, top_k)
    mkv = _memkv(mem.reshape(b * MEM_LEN, d), g_mem, w_mem_kv[0].astype(bf), b)
    y_mem = _mem_attention(p, mkv, b, s)

    h, hn, lg = _merge(x2, y_diff, y_dsa, y_mem, g_mix, w_g, b_g, w_br, w_out[0].astype(bf), g_ffn, w_r, b_r)
    out = _moe(hn, lg, h, w_exp_in[0].astype(bf), w_exp_out[0].astype(bf), g_final.reshape(1, d))
    return out.reshape(b, s, d)
```

```python
import functools
import math

import jax
import jax.numpy as jnp
from jax import lax
from jax.experimental import pallas as pl
from jax.experimental.pallas import tpu as pltpu

D_MODEL = 1024
MEM_LEN = 256
XA_HEADS = 4
XA_HEAD_DIM = 128
DIFF_HEADS = 4
DIFF_QK_DIM = 64
DIFF_V_DIM = 128
DSA_HEADS = 8
DSA_HEAD_DIM = 64
IDX_HEADS = 8
IDX_DIM = 64
TOPK_MAX = 256
ROPE_THETA = 500000.0
ROPE_FRACTION = 4
N_GROUPS = 4
EXPERTS_PER_GROUP = 4
N_EXPERTS = 16
D_EXPERT = 512
EPS = 1e-6

LANES = 128
SUBLANES = 8
VMEM_LIMIT = 56 * 1024 * 1024
NEG_BIG = -1e30
ONES_ROWS = 16
LOG2E = math.log2(math.e)

PROJ_TM = 512
PROJ_TN = 512
DIFF_T = 512
DSA_TQ = 256
MEM_TQ = 512
MERGE_TM = 512
MOE_TM = 512

_NT = (((1,), (1,)), ((), ()))


def _dot(a, b):
    return jnp.dot(a, b, preferred_element_type=jnp.float32)


def _dot_nt(a, b):
    return lax.dot_general(a, b, _NT, preferred_element_type=jnp.float32)


def _rms(xf, g):
    return xf * lax.rsqrt(jnp.mean(xf * xf, axis=-1, keepdims=True) + EPS) * g


def _proj_kernel(x_ref, pos_ref, g_ref, inv_ref, wa_ref, ws_ref,
                 p_ref, ikk_ref, dvt_ref, svt_ref, iwt_ref, n_scr, cos_scr, sa_scr, sb_scr):
    j = pl.program_id(1)

    def rope(v, reps):
        n = v.shape[1]
        c = jnp.concatenate([cos_scr[...]] * reps, axis=1) if reps > 1 else cos_scr[...]
        a = jnp.concatenate([sa_scr[...]] * reps, axis=1) if reps > 1 else sa_scr[...]
        b = jnp.concatenate([sb_scr[...]] * reps, axis=1) if reps > 1 else sb_scr[...]
        up = pltpu.roll(v, n - 8, 1)
        dn = pltpu.roll(v, 8, 1)
        return v * c + up * a + dn * b

    @pl.when(j == 0)
    def _():
        xf = x_ref[...]
        n_scr[...] = _rms(xf, g_ref[...]).astype(jnp.bfloat16)
        ang = pos_ref[...].astype(jnp.float32) * inv_ref[...]
        lane = lax.broadcasted_iota(jnp.int32, ang.shape, 1) % 64
        cs = jnp.cos(ang)
        sn = jnp.sin(ang)
        cos_scr[...] = jnp.where(lane < 16, cs, 1.0)
        sa_scr[...] = jnp.where(lane < 8, -sn, 0.0)
        sb_scr[...] = jnp.where((lane >= 8) & (lane < 16), sn, 0.0)
        small = _dot(n_scr[...], ws_ref[...])
        ikk_ref[...] = rope(small[:, :LANES], 1).astype(jnp.bfloat16)
        w = small[:, LANES:] * (IDX_HEADS ** -0.5)
        iwt_ref[...] = w.T[:IDX_HEADS, :]

    acc = _dot(n_scr[...], wa_ref[...])
    is_rope = (j == 0) | (j == 1) | (j == 3) | (j == 4) | (j == 6)

    @pl.when(is_rope)
    def _():
        p_ref[...] = rope(acc, PROJ_TN // LANES).astype(jnp.bfloat16)

    @pl.when(jnp.logical_not(is_rope))
    def _():
        p_ref[...] = acc.astype(jnp.bfloat16)

    @pl.when(j == 2)
    def _():
        dvt_ref[0] = acc.T.astype(jnp.bfloat16)

    @pl.when(j == 5)
    def _():
        at = acc.T.astype(jnp.bfloat16)
        for r in range(PROJ_TM // DSA_TQ):
            svt_ref[r] = at[:, r * DSA_TQ:(r + 1) * DSA_TQ]


def _proj(x2, pos2, g_mix, inv_lane, w_a, w_s):
    t = x2.shape[0]
    ncol = w_a.shape[1] // PROJ_TN
    cpt = PROJ_TM // DSA_TQ
    return pl.pallas_call(
        _proj_kernel,
        grid=(t // PROJ_TM, ncol),
        in_specs=[
            pl.BlockSpec((PROJ_TM, D_MODEL), lambda i, j: (i, 0)),
            pl.BlockSpec((PROJ_TM, 1), lambda i, j: (i, 0)),
            pl.BlockSpec((1, D_MODEL), lambda i, j: (0, 0)),
            pl.BlockSpec((1, LANES), lambda i, j: (0, 0)),
            pl.BlockSpec((D_MODEL, PROJ_TN), lambda i, j: (0, j)),
            pl.BlockSpec((D_MODEL, 2 * LANES), lambda i, j: (0, 0)),
        ],
        out_specs=[
            pl.BlockSpec((PROJ_TM, PROJ_TN), lambda i, j: (i, j)),
            pl.BlockSpec((PROJ_TM, LANES), lambda i, j: (i, 0)),
            pl.BlockSpec((1, PROJ_TN, PROJ_TM), lambda i, j: (i, 0, 0)),
            pl.BlockSpec((cpt, PROJ_TN, DSA_TQ), lambda i, j: (i, 0, 0)),
            pl.BlockSpec((IDX_HEADS, PROJ_TM), lambda i, j: (0, i)),
        ],
        out_shape=[
            jax.ShapeDtypeStruct((t, w_a.shape[1]), jnp.bfloat16),
            jax.ShapeDtypeStruct((t, LANES), jnp.bfloat16),
            jax.ShapeDtypeStruct((t // PROJ_TM, PROJ_TN, PROJ_TM), jnp.bfloat16),
            jax.ShapeDtypeStruct((t // DSA_TQ, PROJ_TN, DSA_TQ), jnp.bfloat16),
            jax.ShapeDtypeStruct((IDX_HEADS, t), jnp.float32),
        ],
        scratch_shapes=[
            pltpu.VMEM((PROJ_TM, D_MODEL), jnp.bfloat16),
            pltpu.VMEM((PROJ_TM, LANES), jnp.float32),
            pltpu.VMEM((PROJ_TM, LANES), jnp.float32),
            pltpu.VMEM((PROJ_TM, LANES), jnp.float32),
        ],
        compiler_params=pltpu.CompilerParams(
            dimension_semantics=("parallel", "arbitrary"), vmem_limit_bytes=VMEM_LIMIT),
        name="proj",
    )(x2, pos2, g_mix, inv_lane, w_a, w_s)


def _diff_kernel(q_ref, k_ref, vt_ref, lq1_ref, lk1_ref, lq2_ref, lk2_ref, gs_ref, o_ref,
                 qm, m_s, acc, st_a, st_b, p_scr, *, lam_init):
    qi = pl.program_id(2)
    t = DIFF_T
    q = q_ref[...]
    lane = lax.broadcasted_iota(jnp.int32, q.shape, 1)
    qm[0] = jnp.where(lane < DIFF_QK_DIM, q, jnp.zeros_like(q))
    qm[1] = jnp.where(lane >= DIFF_QK_DIM, q, jnp.zeros_like(q))
    m_s[...] = jnp.full(m_s.shape, NEG_BIG, jnp.float32)
    acc[...] = jnp.zeros(acc.shape, jnp.float32)

    def qk(j, st_ref):
        k = k_ref[pl.ds(pl.multiple_of(j * t, t), t), :]
        for i in range(2):
            st_ref[i] = _dot_nt(k, qm[i])

    def softmax_pv(j, st_ref, diagonal):
        vt = jnp.concatenate([vt_ref[j], jnp.ones((ONES_ROWS, t), jnp.bfloat16)], axis=0)
        if diagonal:
            krow = lax.broadcasted_iota(jnp.int32, (t, t), 0)
            qcol = lax.broadcasted_iota(jnp.int32, (t, t), 1)
            keep = krow <= qcol
        alphas = []
        for i in range(2):
            st = st_ref[i]
            if diagonal:
                st = jnp.where(keep, st, NEG_BIG)
            m_old = m_s[i]
            m_new = jnp.maximum(m_old, jnp.max(st, axis=0, keepdims=True))
            alphas.append(jnp.exp2(m_old - m_new))
            m_s[i] = m_new
            p_scr[i] = jnp.exp2(st - m_new).astype(jnp.bfloat16)
        for i in range(2):
            acc[i] = alphas[i] * acc[i] + _dot(vt, p_scr[i])

    qk(0, st_a)

    def pair(tt, carry):
        j = 2 * tt
        qk(j + 1, st_b)
        softmax_pv(j, st_a, False)
        qk(j + 2, st_a)
        softmax_pv(j + 1, st_b, False)
        return carry

    lax.fori_loop(0, qi // 2, pair, 0)

    @pl.when(qi % 2 == 0)
    def _():
        softmax_pv(qi, st_a, True)

    @pl.when(qi % 2 == 1)
    def _():
        qk(qi, st_b)
        softmax_pv(qi - 1, st_a, False)
        softmax_pv(qi, st_b, True)

    lam = (jnp.exp(jnp.sum(lq1_ref[...] * lk1_ref[...], axis=1, keepdims=True))
           - jnp.exp(jnp.sum(lq2_ref[...] * lk2_ref[...], axis=1, keepdims=True))
           + lam_init)
    dv = DIFF_V_DIM
    ot = (acc[0, :dv, :] / acc[0, dv:dv + 1, :]
          - lam * (acc[1, :dv, :] / acc[1, dv:dv + 1, :]))
    yt = ot * lax.rsqrt(jnp.mean(ot * ot, axis=0, keepdims=True) + EPS) * gs_ref[...]
    o_ref[...] = (yt * (1.0 - lam_init)).T.astype(o_ref.dtype)


def _diff_attention(p, dvt, lq1, lk1, lq2, lk2, g_sub_col, batch, seq, lam_init):
    nb = seq // DIFF_T
    vec = pl.BlockSpec((1, DIFF_QK_DIM), lambda b, h, qi: (0, 0))
    return pl.pallas_call(
        functools.partial(_diff_kernel, lam_init=lam_init),
        grid=(batch, DIFF_HEADS, nb),
        in_specs=[
            pl.BlockSpec((DIFF_T, LANES), lambda b, h, qi: (b * nb + qi, h)),
            pl.BlockSpec((seq, LANES), lambda b, h, qi: (b, DIFF_HEADS + h)),
            pl.BlockSpec((nb, DIFF_V_DIM, DIFF_T), lambda b, h, qi: (b, h, 0)),
            vec, vec, vec, vec,
            pl.BlockSpec((DIFF_V_DIM, 1), lambda b, h, qi: (0, 0)),
        ],
        out_specs=pl.BlockSpec((DIFF_T, LANES), lambda b, h, qi: (b * nb + qi, h)),
        out_shape=jax.ShapeDtypeStruct((batch * seq, DIFF_HEADS * DIFF_V_DIM), jnp.bfloat16),
        scratch_shapes=[
            pltpu.VMEM((2, DIFF_T, LANES), jnp.bfloat16),
            pltpu.VMEM((2, 1, DIFF_T), jnp.float32),
            pltpu.VMEM((2, DIFF_V_DIM + ONES_ROWS, DIFF_T), jnp.float32),
            pltpu.VMEM((2, DIFF_T, DIFF_T), jnp.float32),
            pltpu.VMEM((2, DIFF_T, DIFF_T), jnp.float32),
            pltpu.VMEM((2, DIFF_T, DIFF_T), jnp.bfloat16),
        ],
        compiler_params=pltpu.CompilerParams(
            dimension_semantics=("parallel", "parallel", "arbitrary"),
            vmem_limit_bytes=VMEM_LIMIT),
        name="diffattn",
    )(p, p, dvt, lq1, lk1, lq2, lk2, g_sub_col)


def _key_to_float(key):
    bits = jnp.where(key >= 0, key, key ^ jnp.int32(0x7FFFFFFF))
    return lax.bitcast_convert_type(bits, jnp.float32)


def _count_rows(hit):
    tk, tq = hit.shape
    return jnp.sum(hit.reshape(tk // (4 * SUBLANES), 4 * SUBLANES, tq), axis=0)


def _dsa_kernel(sq_ref, iq_ref, iwt_ref, ikk_ref, sk_ref, svt_ref, o_ref,
                score, hi16, lo16, iqm, sqm, acc, m_s, thr, jcut, st_a, st_b, p_scr, *, seq, top_k):
    qi = pl.program_id(1)
    tq = DSA_TQ
    nkc = qi + 1
    lane = lax.broadcasted_iota(jnp.int32, (tq, LANES), 1)
    half = lane // DSA_HEAD_DIM

    for h in range(DSA_HEADS):
        pr = h // 2
        iqp = iq_ref[:, pr * LANES:(pr + 1) * LANES]
        sqp = sq_ref[:, pr * LANES:(pr + 1) * LANES]
        iqm[h] = jnp.where(half == h % 2, iqp, jnp.zeros_like(iqp))
        sqm[h] = jnp.where(half == h % 2, sqp, jnp.zeros_like(sqp))

    krow = lax.broadcasted_iota(jnp.int32, (tq, tq), 0)
    qcol = lax.broadcasted_iota(jnp.int32, (tq, tq), 1)

    def score_chunk(c, carry):
        kk = ikk_ref[pl.ds(pl.multiple_of(c * tq, tq), tq), :]
        sc = jnp.zeros((tq, tq), jnp.float32)
        for h in range(IDX_HEADS):
            sc = sc + iwt_ref[h:h + 1, :] * jnp.maximum(_dot_nt(kk, iqm[h]), 0.0)
        sc = jnp.where(krow <= qcol + (qi - c) * tq, sc, -jnp.inf)
        score[c] = sc
        bits = lax.bitcast_convert_type(sc, jnp.int32)
        okey = jnp.where(bits >= 0, bits, bits ^ jnp.int32(0x7FFFFFFF))
        hi16[c] = lax.shift_right_arithmetic(okey, 16).astype(jnp.int16)
        lo16[c] = ((okey & 0xFFFF) - 2 ** 15).astype(jnp.int16)
        return carry

    lax.fori_loop(0, nkc, score_chunk, 0)

    zero_cnt = jnp.zeros((4 * SUBLANES, tq), jnp.float32)
    i16_min = -2 ** 15

    def count16(buf, pred):
        def body(c, cnt):
            hit = jnp.where(pred(buf[c]), jnp.int16(1), jnp.int16(0))
            h3 = hit.reshape(tq // (4 * SUBLANES), 4 * SUBLANES, tq)
            part = h3[0]
            for r in range(1, h3.shape[0]):
                part = part + h3[r]
            return cnt + part
        cnt = lax.fori_loop(0, nkc, body, jnp.zeros((4 * SUBLANES, tq), jnp.int16))
        return jnp.sum(cnt.astype(jnp.int32), axis=0, keepdims=True)

    def bisect16(buf, want):
        def bit_step(i, cur):
            cand = cur + lax.shift_left(jnp.int32(1), jnp.int32(15) - i)
            c16 = cand.astype(jnp.int16)
            return jnp.where(count16(buf, lambda blk: blk >= c16) >= want, cand, cur)
        return lax.fori_loop(0, 16, bit_step, jnp.full((1, tq), i16_min, jnp.int32))

    t_hi = bisect16(hi16, top_k)
    t_hi16 = t_hi.astype(jnp.int16)
    rest = top_k - count16(hi16, lambda blk: blk > t_hi16)

    def bucket_only(c, carry):
        lo16[c] = jnp.where(hi16[c] == t_hi16, lo16[c], jnp.int16(i16_min))
        return carry

    lax.fori_loop(0, nkc, bucket_only, 0)
    t_lo = bisect16(lo16, rest)
    key = lax.shift_left(t_hi, 16) + (t_lo - i16_min)
    t_f = _key_to_float(key)

    def count_gt_ge(c, carry):
        gt, ge = carry
        blk = score[c]
        return (gt + _count_rows(jnp.where(blk > t_f, 1.0, 0.0)),
                ge + _count_rows(jnp.where(blk >= t_f, 1.0, 0.0)))

    gt, ge = lax.fori_loop(0, nkc, count_gt_ge, (zero_cnt, zero_cnt))
    n_gt = jnp.sum(gt, axis=0, keepdims=True)
    n_ge = jnp.sum(ge, axis=0, keepdims=True)
    q_pos = qi * tq + lax.broadcasted_iota(jnp.int32, (1, tq), 1)
    few = q_pos < top_k - 1
    thr[...] = jnp.where(few, -jnp.inf, t_f)
    jcut[...] = jnp.where(few, -1, seq)
    need = float(top_k) - n_gt
    split = jnp.logical_and(jnp.logical_not(few), n_ge > float(top_k))

    @pl.when(jnp.max(jnp.where(split, 1.0, 0.0)) > 0.0)
    def _():
        def count_eq_below(jc):
            def body(c, cnt):
                hit = jnp.where((score[c] == t_f) & (c * tq + krow < jc), 1.0, 0.0)
                return cnt + _count_rows(hit)
            return jnp.sum(lax.fori_loop(0, nkc, body, zero_cnt), axis=0, keepdims=True)

        nbits = (seq - 1).bit_length()

        def jbit(i, jv):
            cand = jv + lax.shift_left(jnp.int32(1), jnp.int32(nbits - 1) - i)
            return jnp.where(count_eq_below(cand) < need, cand, jv)

        jv = lax.fori_loop(0, nbits, jbit, jnp.zeros((1, tq), jnp.int32))
        jcut[...] = jnp.where(split, jv, jcut[...])

    m_s[...] = jnp.full(m_s.shape, NEG_BIG, jnp.float32)
    acc[...] = jnp.zeros(acc.shape, jnp.float32)

    def qk(c, st_ref):
        off = pl.multiple_of(c * tq, tq)
        for h in range(DSA_HEADS):
            kp = sk_ref[pl.ds(off, tq), (h // 2) * LANES:(h // 2 + 1) * LANES]
            st_ref[h] = _dot_nt(kp, sqm[h])

    def softmax_pv(c, st_ref):
        blk = score[c]
        tt = thr[...]
        keep = (blk > tt) | ((blk == tt) & (c * tq + krow <= jcut[...]))
        bias = jnp.where(keep, 0.0, NEG_BIG)
        alphas = []
        for h in range(DSA_HEADS):
            st = st_ref[h] + bias
            m_old = m_s[h]
            m_new = jnp.maximum(m_old, jnp.max(st, axis=0, keepdims=True))
            alphas.append(jnp.exp2(m_old - m_new))
            m_s[h] = m_new
            p_scr[h] = jnp.exp2(st - m_new).astype(jnp.bfloat16)
        ones = jnp.ones((ONES_ROWS, tq), jnp.bfloat16)
        for h in range(DSA_HEADS):
            vt = jnp.concatenate([svt_ref[c, h * DSA_HEAD_DIM:(h + 1) * DSA_HEAD_DIM, :], ones], axis=0)
            acc[h] = alphas[h] * acc[h] + _dot(vt, p_scr[h])

    qk(0, st_a)

    def pair(tt, carry):
        c = 2 * tt
        qk(c + 1, st_b)
        softmax_pv(c, st_a)
        qk(c + 2, st_a)
        softmax_pv(c + 1, st_b)
        return carry

    lax.fori_loop(0, qi // 2, pair, 0)

    @pl.when(qi % 2 == 0)
    def _():
        softmax_pv(qi, st_a)

    @pl.when(qi % 2 == 1)
    def _():
        qk(qi, st_b)
        softmax_pv(qi - 1, st_a)
        softmax_pv(qi, st_b)

    dh = DSA_HEAD_DIM
    outs = [acc[h, :dh, :] / acc[h, dh:dh + 1, :] for h in range(DSA_HEADS)]
    o_ref[...] = jnp.concatenate(outs, axis=0).T.astype(o_ref.dtype)


def _dsa_attention(p, ikk, iwt, svt, batch, seq, top_k):
    nq = seq // DSA_TQ
    w = DSA_HEADS * DSA_HEAD_DIM
    once = pl.Buffered(1)
    return pl.pallas_call(
        functools.partial(_dsa_kernel, seq=seq, top_k=top_k),
        grid=(batch, nq),
        in_specs=[
            pl.BlockSpec((DSA_TQ, w), lambda b, qi: (b * nq + qi, 3)),
            pl.BlockSpec((DSA_TQ, w), lambda b, qi: (b * nq + qi, 6)),
            pl.BlockSpec((IDX_HEADS, DSA_TQ), lambda b, qi: (0, b * nq + qi)),
            pl.BlockSpec((seq, LANES), lambda b, qi: (b, 0), pipeline_mode=once),
            pl.BlockSpec((seq, w), lambda b, qi: (b, 4), pipeline_mode=once),
            pl.BlockSpec((nq, w, DSA_TQ), lambda b, qi: (b, 0, 0), pipeline_mode=once),
        ],
        out_specs=pl.BlockSpec((DSA_TQ, w), lambda b, qi: (b * nq + qi, 0)),
        out_shape=jax.ShapeDtypeStruct((batch * seq, w), jnp.bfloat16),
        scratch_shapes=[
            pltpu.VMEM((nq, DSA_TQ, DSA_TQ), jnp.float32),
            pltpu.VMEM((nq, DSA_TQ, DSA_TQ), jnp.int16),
            pltpu.VMEM((nq, DSA_TQ, DSA_TQ), jnp.int16),
            pltpu.VMEM((IDX_HEADS, DSA_TQ, LANES), jnp.bfloat16),
            pltpu.VMEM((DSA_HEADS, DSA_TQ, LANES), jnp.bfloat16),
            pltpu.VMEM((DSA_HEADS, DSA_HEAD_DIM + ONES_ROWS, DSA_TQ), jnp.float32),
            pltpu.VMEM((DSA_HEADS, 1, DSA_TQ), jnp.float32),
            pltpu.VMEM((1, DSA_TQ), jnp.float32),
            pltpu.VMEM((1, DSA_TQ), jnp.int32),
            pltpu.VMEM((DSA_HEADS, DSA_TQ, DSA_TQ), jnp.float32),
            pltpu.VMEM((DSA_HEADS, DSA_TQ, DSA_TQ), jnp.float32),
            pltpu.VMEM((DSA_HEADS, DSA_TQ, DSA_TQ), jnp.bfloat16),
        ],
        compiler_params=pltpu.CompilerParams(
            dimension_semantics=("parallel", "arbitrary"), vmem_limit_bytes=VMEM_LIMIT),
        name="dsa",
    )(p, p, iwt, ikk, p, svt)


def _memkv_kernel(mem_ref, g_ref, w_ref, o_ref):
    n = _rms(mem_ref[...], g_ref[...]).astype(jnp.bfloat16)
    o_ref[...] = _dot(n, w_ref[...]).astype(o_ref.dtype)


def _memkv(mem2, g_mem, w_kv, batch):
    return pl.pallas_call(
        _memkv_kernel,
        grid=(batch,),
        in_specs=[
            pl.BlockSpec((MEM_LEN, D_MODEL), lambda b: (b, 0)),
            pl.BlockSpec((1, D_MODEL), lambda b: (0, 0)),
            pl.BlockSpec(w_kv.shape, lambda b: (0, 0)),
        ],
        out_specs=pl.BlockSpec((MEM_LEN, w_kv.shape[1]), lambda b: (b, 0)),
        out_shape=jax.ShapeDtypeStruct((batch * MEM_LEN, w_kv.shape[1]), jnp.bfloat16),
        compiler_params=pltpu.CompilerParams(
            dimension_semantics=("parallel",), vmem_limit_bytes=VMEM_LIMIT),
        name="memkv",
    )(mem2, g_mem, w_kv)


def _memattn_kernel(q_ref, kv_ref, o_ref):
    scale = XA_HEAD_DIM ** -0.5
    outs = []
    for h in range(XA_HEADS):
        q = q_ref[:, h * LANES:(h + 1) * LANES]
        k = kv_ref[:, h * LANES:(h + 1) * LANES]
        v = kv_ref[:, (XA_HEADS + h) * LANES:(XA_HEADS + h + 1) * LANES]
        s = _dot_nt(q, k) * scale
        m = jnp.max(s, axis=1, keepdims=True)
        p = jnp.exp(s - m)
        l = jnp.sum(p, axis=1, keepdims=True)
        outs.append(_dot((p / l).astype(jnp.bfloat16), v))
    o_ref[...] = jnp.concatenate(outs, axis=1).astype(o_ref.dtype)


def _mem_attention(p, mkv, batch, seq):
    nq = seq // MEM_TQ
    w = XA_HEADS * XA_HEAD_DIM
    return pl.pallas_call(
        _memattn_kernel,
        grid=(batch, nq),
        in_specs=[
            pl.BlockSpec((MEM_TQ, w), lambda b, qi: (b * nq + qi, 7)),
            pl.BlockSpec((MEM_LEN, 2 * w), lambda b, qi: (b, 0)),
        ],
        out_specs=pl.BlockSpec((MEM_TQ, w), lambda b, qi: (b * nq + qi, 0)),
        out_shape=jax.ShapeDtypeStruct((batch * seq, w), jnp.bfloat16),
        compiler_params=pltpu.CompilerParams(
            dimension_semantics=("parallel", "parallel"), vmem_limit_bytes=VMEM_LIMIT),
        name="memattn",
    )(p, mkv)


def _merge_kernel(x_ref, yd_ref, ys_ref, ym_ref, gmix_ref, wg_ref, bg_ref, wbr_ref, wout_ref,
                  gffn_ref, wr_ref, br_ref, h_ref, hn_ref, lg_ref):
    xf = x_ref[...]
    n = _rms(xf, gmix_ref[...]).astype(jnp.bfloat16)
    merged = jnp.zeros(xf.shape, jnp.float32)
    for i, y_ref in enumerate((yd_ref, ys_ref, ym_ref)):
        gate = jax.nn.sigmoid(_dot(n, wg_ref[i]) + bg_ref[i])
        merged = merged + gate * _dot(y_ref[...], wbr_ref[i])
    h = xf + _dot(merged.astype(jnp.bfloat16), wout_ref[...])
    h_ref[...] = h
    hn = _rms(h, gffn_ref[...]).astype(jnp.bfloat16)
    hn_ref[...] = hn
    lg_ref[...] = _dot(hn, wr_ref[...]) + br_ref[...]


def _merge(x2, y_diff, y_dsa, y_mem, g_mix, w_g, b_g, w_br, w_out, g_ffn, w_r, b_r):
    t = x2.shape[0]
    row = lambda w: pl.BlockSpec((MERGE_TM, w), lambda i: (i, 0))
    full = lambda a: pl.BlockSpec(a.shape, lambda i: (0,) * a.ndim, pipeline_mode=pl.Buffered(1))
    return pl.pallas_call(
        _merge_kernel,
        grid=(t // MERGE_TM,),
        in_specs=[row(D_MODEL), row(512), row(512), row(512), full(g_mix), full(w_g), full(b_g),
                  full(w_br), full(w_out), full(g_ffn), full(w_r), full(b_r)],
        out_specs=[row(D_MODEL), row(D_MODEL), row(LANES)],
        out_shape=[
            jax.ShapeDtypeStruct((t, D_MODEL), jnp.float32),
            jax.ShapeDtypeStruct((t, D_MODEL), jnp.bfloat16),
            jax.ShapeDtypeStruct((t, LANES), jnp.float32),
        ],
        compiler_params=pltpu.CompilerParams(
            dimension_semantics=("parallel",), vmem_limit_bytes=VMEM_LIMIT),
        name="merge",
    )(x2, y_diff, y_dsa, y_mem, g_mix, w_g, b_g, w_br, w_out, g_ffn, w_r, b_r)


def _route_weights(lg):
    lane = lax.broadcasted_iota(jnp.int32, lg.shape, 1).astype(jnp.float32)
    big = float(LANES)
    gl = jnp.where(lane < N_GROUPS, lg, -jnp.inf)
    gmax = jnp.max(gl, axis=1, keepdims=True)
    grp = jnp.min(jnp.where(gl == gmax, lane, big), axis=1, keepdims=True)
    gsum = jnp.sum(jnp.where(lane < N_GROUPS, jnp.exp(gl - gmax), 0.0), axis=1, keepdims=True)
    p_grp = 1.0 / gsum
    lo = N_GROUPS + grp * EXPERTS_PER_GROUP
    el = jnp.where((lane >= lo) & (lane < lo + EXPERTS_PER_GROUP), lg, -jnp.inf)
    e1 = jnp.max(el, axis=1, keepdims=True)
    i1 = jnp.min(jnp.where(el == e1, lane, big), axis=1, keepdims=True)
    el2 = jnp.where(lane == i1, -jnp.inf, el)
    e2 = jnp.max(el2, axis=1, keepdims=True)
    i2 = jnp.min(jnp.where(el2 == e2, lane, big), axis=1, keepdims=True)
    r = jnp.exp(e2 - e1)
    w1 = p_grp / (1.0 + r)
    w2 = p_grp * r / (1.0 + r)
    return jnp.where(lane == i1, w1, 0.0) + jnp.where(lane == i2, w2, 0.0)


def _moe_kernel(hn_ref, lg_ref, h_ref, win_ref, wout_ref, gfin_ref, o_ref, acc, cw):
    e = pl.program_id(1)

    @pl.when(e == 0)
    def _():
        acc[...] = jnp.zeros(acc.shape, jnp.float32)
        cw[...] = _route_weights(lg_ref[...])

    lane = lax.broadcasted_iota(jnp.int32, cw.shape, 1)
    cw_e = jnp.sum(jnp.where(lane == N_GROUPS + e, cw[...], 0.0), axis=1, keepdims=True)
    gu = _dot(hn_ref[...], win_ref[0])
    g = gu[:, :D_EXPERT]
    hid = (g * jax.nn.sigmoid(g)) * gu[:, D_EXPERT:]
    acc[...] += cw_e * _dot(hid.astype(jnp.bfloat16), wout_ref[0])

    @pl.when(e == N_EXPERTS - 1)
    def _():
        o_ref[...] = _rms(h_ref[...] + acc[...], gfin_ref[...])


def _moe(hn, lg, h, w_e_in, w_e_out, g_final):
    t = hn.shape[0]
    return pl.pallas_call(
        _moe_kernel,
        grid=(t // MOE_TM, N_EXPERTS),
        in_specs=[
            pl.BlockSpec((MOE_TM, D_MODEL), lambda i, e: (i, 0)),
            pl.BlockSpec((MOE_TM, LANES), lambda i, e: (i, 0)),
            pl.BlockSpec((MOE_TM, D_MODEL), lambda i, e: (i, 0)),
            pl.BlockSpec((1, D_MODEL, 2 * D_EXPERT), lambda i, e: (e, 0, 0)),
            pl.BlockSpec((1, D_EXPERT, D_MODEL), lambda i, e: (e, 0, 0)),
            pl.BlockSpec((1, D_MODEL), lambda i, e: (0, 0)),
        ],
        out_specs=pl.BlockSpec((MOE_TM, D_MODEL), lambda i, e: (i, 0)),
        out_shape=jax.ShapeDtypeStruct((t, D_MODEL), jnp.float32),
        scratch_shapes=[
            pltpu.VMEM((MOE_TM, D_MODEL), jnp.float32),
            pltpu.VMEM((MOE_TM, LANES), jnp.float32),
        ],
        compiler_params=pltpu.CompilerParams(
            dimension_semantics=("parallel", "arbitrary"), vmem_limit_bytes=VMEM_LIMIT),
        name="moe",
    )(hn, lg, h, w_e_in, w_e_out, g_final)


def kernel(x, positions, mem, g_mix, w_in, b_gate, lambda_q1, lambda_k1, lambda_q2, lambda_k2,
           g_diff_sub, g_mem, w_mem_kv, w_br_diff, w_br_dsa, w_br_mem, w_out, g_ffn,
           w_route_group, b_route_group, w_route_expert, b_route_expert, w_exp_in, w_exp_out,
           g_final):
    b, s, d = x.shape
    t = b * s
    bf = jnp.bfloat16
    top_k = min(TOPK_MAX, s // 4)
    assert d == D_MODEL and s % DIFF_T == 0 and s % DSA_TQ == 0 and top_k <= DSA_TQ
    assert PROJ_TM == DIFF_T and PROJ_TM % DSA_TQ == 0, "proj writes V^T in the attention kernels' key tiles"
    assert g_mix.shape[0] == 1, "single layer"
    lam_init = 0.8 - 0.6 * math.exp(-0.3 * 0)

    wi = w_in[0]
    c = 512
    seg = lambda k: wi[:, k * c:(k + 1) * c]
    o_ik = 7 * c
    w_ik = wi[:, o_ik:o_ik + IDX_DIM]
    w_iw = wi[:, o_ik + IDX_DIM:o_ik + IDX_DIM + IDX_HEADS]
    o_mq = o_ik + IDX_DIM + IDX_HEADS
    w_mq = wi[:, o_mq:o_mq + c]
    w_gl = wi[:, o_mq + c:]
    qs = DIFF_QK_DIM ** -0.5
    qs2 = qs * LOG2E
    w_a = jnp.concatenate([seg(0) * qs2, seg(1), seg(2), seg(3) * qs2, seg(4), seg(5), seg(6) * qs, w_mq],
                          axis=1).astype(bf)
    w_s = jnp.concatenate([w_ik, w_ik, w_iw, jnp.zeros((d, LANES - IDX_HEADS), wi.dtype)], axis=1).astype(bf)
    w_g = w_gl.reshape(d, 3, d).transpose(1, 0, 2).astype(bf)
    b_g = b_gate[0].reshape(3, 1, d)
    w_br = jnp.stack([w_br_diff[0], w_br_dsa[0], w_br_mem[0]]).astype(bf)
    w_r = jnp.concatenate([w_route_group[0], w_route_expert[0],
                           jnp.zeros((d, LANES - N_GROUPS - N_EXPERTS), wi.dtype)], axis=1).astype(bf)
    b_r = jnp.concatenate([b_route_group[0], b_route_expert[0],
                           jnp.zeros((LANES - N_GROUPS - N_EXPERTS,), jnp.float32)]).reshape(1, LANES)

    rot = IDX_DIM // ROPE_FRACTION
    inv_freq = ROPE_THETA ** (-jnp.arange(0, rot, 2, dtype=jnp.float32) / rot)
    inv64 = jnp.concatenate([inv_freq, inv_freq, jnp.zeros((IDX_DIM - rot,), jnp.float32)])
    inv_lane = jnp.concatenate([inv64, inv64]).reshape(1, LANES)

    x2 = x.reshape(t, d)
    pos2 = positions.reshape(t, 1)
    p, ikk, dvt, svt, iwt = _proj(x2, pos2, g_mix, inv_lane, w_a, w_s)

    y_diff = _diff_attention(p, dvt, lambda_q1, lambda_k1, lambda_q2, lambda_k2,
                             g_diff_sub.reshape(DIFF_V_DIM, 1), b, s, lam_init)
    y_dsa = _dsa_attention(p, ikk, iwt, svt, b, s, top_k)
    mkv = _memkv(mem.reshape(b * MEM_LEN, d), g_mem, w_mem_kv[0].astype(bf), b)
    y_mem = _mem_attention(p, mkv, b, s)

    h, hn, lg = _merge(x2, y_diff, y_dsa, y_mem, g_mix, w_g, b_g, w_br, w_out[0].astype(bf), g_ffn, w_r, b_r)
    out = _moe(hn, lg, h, w_exp_in[0].astype(bf), w_exp_out[0].astype(bf), g_final.reshape(1, d))
    return out.reshape(b, s, d)
```

```python
import functools
import math

import jax
import jax.numpy as jnp
from jax import lax
from jax.experimental import pallas as pl
from jax.experimental.pallas import tpu as pltpu

D_MODEL = 1024
MEM_LEN = 256
XA_HEADS = 4
XA_HEAD_DIM = 128
DIFF_HEADS = 4
DIFF_QK_DIM = 64
DIFF_V_DIM = 128
DSA_HEADS = 8
DSA_HEAD_DIM = 64
IDX_HEADS = 8
IDX_DIM = 64
TOPK_MAX = 256
ROPE_THETA = 500000.0
ROPE_FRACTION = 4
N_GROUPS = 4
EXPERTS_PER_GROUP = 4
N_EXPERTS = 16
D_EXPERT = 512
EPS = 1e-6

LANES = 128
SUBLANES = 8
VMEM_LIMIT = 56 * 1024 * 1024
NEG_BIG = -1e30
ONES_ROWS = 16
LOG2E = math.log2(math.e)

PROJ_TM = 512
PROJ_TN = 512
DIFF_T = 512
DSA_TQ = 256
MEM_TQ = 512
MERGE_TM = 512
MOE_TM = 512

_NT = (((1,), (1,)), ((), ()))


def _dot(a, b):
    return jnp.dot(a, b, preferred_element_type=jnp.float32)


def _dot_nt(a, b):
    return lax.dot_general(a, b, _NT, preferred_element_type=jnp.float32)


def _rms(xf, g):
    return xf * lax.rsqrt(jnp.mean(xf * xf, axis=-1, keepdims=True) + EPS) * g


def _proj_kernel(x_ref, pos_ref, g_ref, inv_ref, wa_ref, ws_ref,
                 p_ref, ikk_ref, dvt_ref, svt_ref, iwt_ref, n_scr, cos_scr, sa_scr, sb_scr):
    j = pl.program_id(1)

    def rope(v, reps):
        n = v.shape[1]
        c = jnp.concatenate([cos_scr[...]] * reps, axis=1) if reps > 1 else cos_scr[...]
        a = jnp.concatenate([sa_scr[...]] * reps, axis=1) if reps > 1 else sa_scr[...]
        b = jnp.concatenate([sb_scr[...]] * reps, axis=1) if reps > 1 else sb_scr[...]
        up = pltpu.roll(v, n - 8, 1)
        dn = pltpu.roll(v, 8, 1)
        return v * c + up * a + dn * b

    @pl.when(j == 0)
    def _():
        xf = x_ref[...]
        n_scr[...] = _rms(xf, g_ref[...]).astype(jnp.bfloat16)
        ang = pos_ref[...].astype(jnp.float32) * inv_ref[...]
        lane = lax.broadcasted_iota(jnp.int32, ang.shape, 1) % 64
        cs = jnp.cos(ang)
        sn = jnp.sin(ang)
        cos_scr[...] = jnp.where(lane < 16, cs, 1.0)
        sa_scr[...] = jnp.where(lane < 8, -sn, 0.0)
        sb_scr[...] = jnp.where((lane >= 8) & (lane < 16), sn, 0.0)
        small = _dot(n_scr[...], ws_ref[...])
        ikk_ref[...] = rope(small[:, :LANES], 1).astype(jnp.bfloat16)
        w = small[:, LANES:] * (IDX_HEADS ** -0.5)
        iwt_ref[...] = w.T[:IDX_HEADS, :]

    acc = _dot(n_scr[...], wa_ref[...])
    is_rope = (j == 0) | (j == 1) | (j == 3) | (j == 4) | (j == 6)

    @pl.when(is_rope)
    def _():
        p_ref[...] = rope(acc, PROJ_TN // LANES).astype(jnp.bfloat16)

    @pl.when(jnp.logical_not(is_rope))
    def _():
        p_ref[...] = acc.astype(jnp.bfloat16)

    @pl.when(j == 2)
    def _():
        dvt_ref[0] = acc.T.astype(jnp.bfloat16)

    @pl.when(j == 5)
    def _():
        at = acc.T.astype(jnp.bfloat16)
        for r in range(PROJ_TM // DSA_TQ):
            svt_ref[r] = at[:, r * DSA_TQ:(r + 1) * DSA_TQ]


def _proj(x2, pos2, g_mix, inv_lane, w_a, w_s):
    t = x2.shape[0]
    ncol = w_a.shape[1] // PROJ_TN
    cpt = PROJ_TM // DSA_TQ
    return pl.pallas_call(
        _proj_kernel,
        grid=(t // PROJ_TM, ncol),
        in_specs=[
            pl.BlockSpec((PROJ_TM, D_MODEL), lambda i, j: (i, 0)),
            pl.BlockSpec((PROJ_TM, 1), lambda i, j: (i, 0)),
            pl.BlockSpec((1, D_MODEL), lambda i, j: (0, 0)),
            pl.BlockSpec((1, LANES), lambda i, j: (0, 0)),
            pl.BlockSpec((D_MODEL, PROJ_TN), lambda i, j: (0, j)),
            pl.BlockSpec((D_MODEL, 2 * LANES), lambda i, j: (0, 0)),
        ],
        out_specs=[
            pl.BlockSpec((PROJ_TM, PROJ_TN), lambda i, j: (i, j)),
            pl.BlockSpec((PROJ_TM, LANES), lambda i, j: (i, 0)),
            pl.BlockSpec((1, PROJ_TN, PROJ_TM), lambda i, j: (i, 0, 0)),
            pl.BlockSpec((cpt, PROJ_TN, DSA_TQ), lambda i, j: (i, 0, 0)),
            pl.BlockSpec((IDX_HEADS, PROJ_TM), lambda i, j: (0, i)),
        ],
        out_shape=[
            jax.ShapeDtypeStruct((t, w_a.shape[1]), jnp.bfloat16),
            jax.ShapeDtypeStruct((t, LANES), jnp.bfloat16),
            jax.ShapeDtypeStruct((t // PROJ_TM, PROJ_TN, PROJ_TM), jnp.bfloat16),
            jax.ShapeDtypeStruct((t // DSA_TQ, PROJ_TN, DSA_TQ), jnp.bfloat16),
            jax.ShapeDtypeStruct((IDX_HEADS, t), jnp.float32),
        ],
        scratch_shapes=[
            pltpu.VMEM((PROJ_TM, D_MODEL), jnp.bfloat16),
            pltpu.VMEM((PROJ_TM, LANES), jnp.float32),
            pltpu.VMEM((PROJ_TM, LANES), jnp.float32),
            pltpu.VMEM((PROJ_TM, LANES), jnp.float32),
        ],
        compiler_params=pltpu.CompilerParams(
            dimension_semantics=("parallel", "arbitrary"), vmem_limit_bytes=VMEM_LIMIT),
        name="proj",
    )(x2, pos2, g_mix, inv_lane, w_a, w_s)


def _diff_kernel(q_ref, k_ref, vt_ref, lq1_ref, lk1_ref, lq2_ref, lk2_ref, gs_ref, o_ref,
                 qm, m_s, acc, st_a, st_b, p_scr, *, lam_init):
    qi = pl.program_id(2)
    t = DIFF_T
    q = q_ref[...]
    lane = lax.broadcasted_iota(jnp.int32, q.shape, 1)
    qm[0] = jnp.where(lane < DIFF_QK_DIM, q, jnp.zeros_like(q))
    qm[1] = jnp.where(lane >= DIFF_QK_DIM, q, jnp.zeros_like(q))
    m_s[...] = jnp.full(m_s.shape, NEG_BIG, jnp.float32)
    acc[...] = jnp.zeros(acc.shape, jnp.float32)

    def qk(j, st_ref):
        k = k_ref[pl.ds(pl.multiple_of(j * t, t), t), :]
        for i in range(2):
            st_ref[i] = _dot_nt(k, qm[i])

    def softmax_pv(j, st_ref, diagonal):
        vt = jnp.concatenate([vt_ref[j], jnp.ones((ONES_ROWS, t), jnp.bfloat16)], axis=0)
        if diagonal:
            krow = lax.broadcasted_iota(jnp.int32, (t, t), 0)
            qcol = lax.broadcasted_iota(jnp.int32, (t, t), 1)
            keep = krow <= qcol
        alphas = []
        for i in range(2):
            st = st_ref[i]
            if diagonal:
                st = jnp.where(keep, st, NEG_BIG)
            m_old = m_s[i]
            m_new = jnp.maximum(m_old, jnp.max(st, axis=0, keepdims=True))
            alphas.append(jnp.exp2(m_old - m_new))
            m_s[i] = m_new
            p_scr[i] = jnp.exp2(st - m_new).astype(jnp.bfloat16)
        for i in range(2):
            acc[i] = alphas[i] * acc[i] + _dot(vt, p_scr[i])

    qk(0, st_a)

    def pair(tt, carry):
        j = 2 * tt
        qk(j + 1, st_b)
        softmax_pv(j, st_a, False)
        qk(j + 2, st_a)
        softmax_pv(j + 1, st_b, False)
        return carry

    lax.fori_loop(0, qi // 2, pair, 0)

    @pl.when(qi % 2 == 0)
    def _():
        softmax_pv(qi, st_a, True)

    @pl.when(qi % 2 == 1)
    def _():
        qk(qi, st_b)
        softmax_pv(qi - 1, st_a, False)
        softmax_pv(qi, st_b, True)

    lam = (jnp.exp(jnp.sum(lq1_ref[...] * lk1_ref[...], axis=1, keepdims=True))
           - jnp.exp(jnp.sum(lq2_ref[...] * lk2_ref[...], axis=1, keepdims=True))
           + lam_init)
    dv = DIFF_V_DIM
    ot = (acc[0, :dv, :] / acc[0, dv:dv + 1, :]
          - lam * (acc[1, :dv, :] / acc[1, dv:dv + 1, :]))
    yt = ot * lax.rsqrt(jnp.mean(ot * ot, axis=0, keepdims=True) + EPS) * gs_ref[...]
    o_ref[...] = (yt * (1.0 - lam_init)).T.astype(o_ref.dtype)


def _diff_attention(p, dvt, lq1, lk1, lq2, lk2, g_sub_col, batch, seq, lam_init):
    nb = seq // DIFF_T
    vec = pl.BlockSpec((1, DIFF_QK_DIM), lambda b, h, qi: (0, 0))
    return pl.pallas_call(
        functools.partial(_diff_kernel, lam_init=lam_init),
        grid=(batch, DIFF_HEADS, nb),
        in_specs=[
            pl.BlockSpec((DIFF_T, LANES), lambda b, h, qi: (b * nb + qi, h)),
            pl.BlockSpec((seq, LANES), lambda b, h, qi: (b, DIFF_HEADS + h)),
            pl.BlockSpec((nb, DIFF_V_DIM, DIFF_T), lambda b, h, qi: (b, h, 0)),
            vec, vec, vec, vec,
            pl.BlockSpec((DIFF_V_DIM, 1), lambda b, h, qi: (0, 0)),
        ],
        out_specs=pl.BlockSpec((DIFF_T, LANES), lambda b, h, qi: (b * nb + qi, h)),
        out_shape=jax.ShapeDtypeStruct((batch * seq, DIFF_HEADS * DIFF_V_DIM), jnp.bfloat16),
        scratch_shapes=[
            pltpu.VMEM((2, DIFF_T, LANES), jnp.bfloat16),
            pltpu.VMEM((2, 1, DIFF_T), jnp.float32),
            pltpu.VMEM((2, DIFF_V_DIM + ONES_ROWS, DIFF_T), jnp.float32),
            pltpu.VMEM((2, DIFF_T, DIFF_T), jnp.float32),
            pltpu.VMEM((2, DIFF_T, DIFF_T), jnp.float32),
            pltpu.VMEM((2, DIFF_T, DIFF_T), jnp.bfloat16),
        ],
        compiler_params=pltpu.CompilerParams(
            dimension_semantics=("parallel", "parallel", "arbitrary"),
            vmem_limit_bytes=VMEM_LIMIT),
        name="diffattn",
    )(p, p, dvt, lq1, lk1, lq2, lk2, g_sub_col)


def _key_to_float(key):
    bits = jnp.where(key >= 0, key, key ^ jnp.int32(0x7FFFFFFF))
    return lax.bitcast_convert_type(bits, jnp.float32)


def _count_rows(hit):
    tk, tq = hit.shape
    return jnp.sum(hit.reshape(tk // (4 * SUBLANES), 4 * SUBLANES, tq), axis=0)


def _dsa_kernel(sq_ref, iq_ref, iwt_ref, ikk_ref, sk_ref, svt_ref, o_ref,
                score, hi16, lo16, iqm, sqm, acc, m_s, thr, jcut, st_a, st_b, mx_a, mx_b, p_scr, *, seq, top_k):
    qi = pl.program_id(1)
    tq = DSA_TQ
    nkc = qi + 1
    lane = lax.broadcasted_iota(jnp.int32, (tq, LANES), 1)
    half = lane // DSA_HEAD_DIM

    for h in range(DSA_HEADS):
        pr = h // 2
        iqp = iq_ref[:, pr * LANES:(pr + 1) * LANES]
        sqp = sq_ref[:, pr * LANES:(pr + 1) * LANES]
        iqm[h] = jnp.where(half == h % 2, iqp, jnp.zeros_like(iqp))
        sqm[h] = jnp.where(half == h % 2, sqp, jnp.zeros_like(sqp))

    krow = lax.broadcasted_iota(jnp.int32, (tq, tq), 0)
    qcol = lax.broadcasted_iota(jnp.int32, (tq, tq), 1)

    def idx_logits(c, lg_ref):
        kk = ikk_ref[pl.ds(pl.multiple_of(c * tq, tq), tq), :]
        for h in range(IDX_HEADS):
            lg_ref[h] = _dot_nt(kk, iqm[h])

    def idx_score(c, lg_ref):
        sc = jnp.zeros((tq, tq), jnp.float32)
        for h in range(IDX_HEADS):
            sc = sc + iwt_ref[h:h + 1, :] * jnp.maximum(lg_ref[h], 0.0)
        sc = jnp.where(krow <= qcol + (qi - c) * tq, sc, -jnp.inf)
        score[c] = sc
        bits = lax.bitcast_convert_type(sc, jnp.int32)
        okey = jnp.where(bits >= 0, bits, bits ^ jnp.int32(0x7FFFFFFF))
        hi16[c] = lax.shift_right_arithmetic(okey, 16).astype(jnp.int16)
        lo16[c] = ((okey & 0xFFFF) - 2 ** 15).astype(jnp.int16)

    idx_logits(0, st_a)

    def idx_pair(tt, carry):
        c = 2 * tt
        idx_logits(c + 1, st_b)
        idx_score(c, st_a)
        idx_logits(c + 2, st_a)
        idx_score(c + 1, st_b)
        return carry

    lax.fori_loop(0, qi // 2, idx_pair, 0)

    @pl.when(qi % 2 == 0)
    def _():
        idx_score(qi, st_a)

    @pl.when(qi % 2 == 1)
    def _():
        idx_logits(qi, st_b)
        idx_score(qi - 1, st_a)
        idx_score(qi, st_b)

    zero_cnt = jnp.zeros((4 * SUBLANES, tq), jnp.float32)
    i16_min = -2 ** 15

    def count16(buf, pred):
        def body(c, cnt):
            hit = jnp.where(pred(buf[c]), jnp.int16(1), jnp.int16(0))
            h3 = hit.reshape(tq // (4 * SUBLANES), 4 * SUBLANES, tq)
            part = h3[0]
            for r in range(1, h3.shape[0]):
                part = part + h3[r]
            return cnt + part
        cnt = lax.fori_loop(0, nkc, body, jnp.zeros((4 * SUBLANES, tq), jnp.int16))
        return jnp.sum(cnt.astype(jnp.int32), axis=0, keepdims=True)

    def bisect16(buf, want):
        def bit_step(i, cur):
            cand = cur + lax.shift_left(jnp.int32(1), jnp.int32(15) - i)
            c16 = cand.astype(jnp.int16)
            return jnp.where(count16(buf, lambda blk: blk >= c16) >= want, cand, cur)
        return lax.fori_loop(0, 16, bit_step, jnp.full((1, tq), i16_min, jnp.int32))

    t_hi = bisect16(hi16, top_k)
    t_hi16 = t_hi.astype(jnp.int16)
    rest = top_k - count16(hi16, lambda blk: blk > t_hi16)

    def bucket_only(c, carry):
        lo16[c] = jnp.where(hi16[c] == t_hi16, lo16[c], jnp.int16(i16_min))
        return carry

    lax.fori_loop(0, nkc, bucket_only, 0)
    t_lo = bisect16(lo16, rest)
    key = lax.shift_left(t_hi, 16) + (t_lo - i16_min)
    t_f = _key_to_float(key)

    def count_gt_ge(c, carry):
        gt, ge = carry
        blk = score[c]
        return (gt + _count_rows(jnp.where(blk > t_f, 1.0, 0.0)),
                ge + _count_rows(jnp.where(blk >= t_f, 1.0, 0.0)))

    gt, ge = lax.fori_loop(0, nkc, count_gt_ge, (zero_cnt, zero_cnt))
    n_gt = jnp.sum(gt, axis=0, keepdims=True)
    n_ge = jnp.sum(ge, axis=0, keepdims=True)
    q_pos = qi * tq + lax.broadcasted_iota(jnp.int32, (1, tq), 1)
    few = q_pos < top_k - 1
    thr[...] = jnp.where(few, -jnp.inf, t_f)
    jcut[...] = jnp.where(few, -1, seq)
    need = float(top_k) - n_gt
    split = jnp.logical_and(jnp.logical_not(few), n_ge > float(top_k))

    @pl.when(jnp.max(jnp.where(split, 1.0, 0.0)) > 0.0)
    def _():
        def count_eq_below(jc):
            def body(c, cnt):
                hit = jnp.where((score[c] == t_f) & (c * tq + krow < jc), 1.0, 0.0)
                return cnt + _count_rows(hit)
            return jnp.sum(lax.fori_loop(0, nkc, body, zero_cnt), axis=0, keepdims=True)

        nbits = (seq - 1).bit_length()

        def jbit(i, jv):
            cand = jv + lax.shift_left(jnp.int32(1), jnp.int32(nbits - 1) - i)
            return jnp.where(count_eq_below(cand) < need, cand, jv)

        jv = lax.fori_loop(0, nbits, jbit, jnp.zeros((1, tq), jnp.int32))
        jcut[...] = jnp.where(split, jv, jcut[...])

    m_s[...] = jnp.full(m_s.shape, NEG_BIG, jnp.float32)
    acc[...] = jnp.zeros(acc.shape, jnp.float32)

    def qk(c, slot):
        st_ref, mx_ref = slot
        off = pl.multiple_of(c * tq, tq)
        blk = score[c]
        tt = thr[...]
        keep = (blk > tt) | ((blk == tt) & (c * tq + krow <= jcut[...]))
        bias = jnp.where(keep, 0.0, NEG_BIG)
        for h in range(DSA_HEADS):
            kp = sk_ref[pl.ds(off, tq), (h // 2) * LANES:(h // 2 + 1) * LANES]
            st = _dot_nt(kp, sqm[h]) + bias
            st_ref[h] = st
            mx_ref[h] = jnp.max(st, axis=0, keepdims=True)

    def softmax_pv(c, slot):
        st_ref, mx_ref = slot
        alphas = []
        for h in range(DSA_HEADS):
            m_old = m_s[h]
            m_new = jnp.maximum(m_old, mx_ref[h])
            alphas.append(jnp.exp2(m_old - m_new))
            m_s[h] = m_new
            p_scr[h] = jnp.exp2(st_ref[h] - m_new).astype(jnp.bfloat16)
        ones = jnp.ones((ONES_ROWS, tq), jnp.bfloat16)
        for h in range(DSA_HEADS):
            vt = jnp.concatenate([svt_ref[c, h * DSA_HEAD_DIM:(h + 1) * DSA_HEAD_DIM, :], ones], axis=0)
            acc[h] = alphas[h] * acc[h] + _dot(vt, p_scr[h])

    slot_a, slot_b = (st_a, mx_a), (st_b, mx_b)
    qk(0, slot_a)

    def pair(tt, carry):
        c = 2 * tt
        qk(c + 1, slot_b)
        softmax_pv(c, slot_a)
        qk(c + 2, slot_a)
        softmax_pv(c + 1, slot_b)
        return carry

    lax.fori_loop(0, qi // 2, pair, 0)

    @pl.when(qi % 2 == 0)
    def _():
        softmax_pv(qi, slot_a)

    @pl.when(qi % 2 == 1)
    def _():
        qk(qi, slot_b)
        softmax_pv(qi - 1, slot_a)
        softmax_pv(qi, slot_b)

    dh = DSA_HEAD_DIM
    outs = [acc[h, :dh, :] / acc[h, dh:dh + 1, :] for h in range(DSA_HEADS)]
    o_ref[...] = jnp.concatenate(outs, axis=0).T.astype(o_ref.dtype)


def _dsa_attention(p, ikk, iwt, svt, batch, seq, top_k):
    nq = seq // DSA_TQ
    w = DSA_HEADS * DSA_HEAD_DIM
    once = pl.Buffered(1)
    return pl.pallas_call(
        functools.partial(_dsa_kernel, seq=seq, top_k=top_k),
        grid=(batch, nq),
        in_specs=[
            pl.BlockSpec((DSA_TQ, w), lambda b, qi: (b * nq + qi, 3)),
            pl.BlockSpec((DSA_TQ, w), lambda b, qi: (b * nq + qi, 6)),
            pl.BlockSpec((IDX_HEADS, DSA_TQ), lambda b, qi: (0, b * nq + qi)),
            pl.BlockSpec((seq, LANES), lambda b, qi: (b, 0), pipeline_mode=once),
            pl.BlockSpec((seq, w), lambda b, qi: (b, 4), pipeline_mode=once),
            pl.BlockSpec((nq, w, DSA_TQ), lambda b, qi: (b, 0, 0), pipeline_mode=once),
        ],
        out_specs=pl.BlockSpec((DSA_TQ, w), lambda b, qi: (b * nq + qi, 0)),
        out_shape=jax.ShapeDtypeStruct((batch * seq, w), jnp.bfloat16),
        scratch_shapes=[
            pltpu.VMEM((nq, DSA_TQ, DSA_TQ), jnp.float32),
            pltpu.VMEM((nq, DSA_TQ, DSA_TQ), jnp.int16),
            pltpu.VMEM((nq, DSA_TQ, DSA_TQ), jnp.int16),
            pltpu.VMEM((IDX_HEADS, DSA_TQ, LANES), jnp.bfloat16),
            pltpu.VMEM((DSA_HEADS, DSA_TQ, LANES), jnp.bfloat16),
            pltpu.VMEM((DSA_HEADS, DSA_HEAD_DIM + ONES_ROWS, DSA_TQ), jnp.float32),
            pltpu.VMEM((DSA_HEADS, 1, DSA_TQ), jnp.float32),
            pltpu.VMEM((1, DSA_TQ), jnp.float32),
            pltpu.VMEM((1, DSA_TQ), jnp.int32),
            pltpu.VMEM((DSA_HEADS, DSA_TQ, DSA_TQ), jnp.float32),
            pltpu.VMEM((DSA_HEADS, DSA_TQ, DSA_TQ), jnp.float32),
            pltpu.VMEM((DSA_HEADS, 1, DSA_TQ), jnp.float32),
            pltpu.VMEM((DSA_HEADS, 1, DSA_TQ), jnp.float32),
            pltpu.VMEM((DSA_HEADS, DSA_TQ, DSA_TQ), jnp.bfloat16),
        ],
        compiler_params=pltpu.CompilerParams(
            dimension_semantics=("parallel", "arbitrary"), vmem_limit_bytes=VMEM_LIMIT),
        name="dsa",
    )(p, p, iwt, ikk, p, svt)


def _memkv_kernel(mem_ref, g_ref, w_ref, o_ref):
    n = _rms(mem_ref[...], g_ref[...]).astype(jnp.bfloat16)
    o_ref[...] = _dot(n, w_ref[...]).astype(o_ref.dtype)


def _memkv(mem2, g_mem, w_kv, batch):
    return pl.pallas_call(
        _memkv_kernel,
        grid=(batch,),
        in_specs=[
            pl.BlockSpec((MEM_LEN, D_MODEL), lambda b: (b, 0)),
            pl.BlockSpec((1, D_MODEL), lambda b: (0, 0)),
            pl.BlockSpec(w_kv.shape, lambda b: (0, 0)),
        ],
        out_specs=pl.BlockSpec((MEM_LEN, w_kv.shape[1]), lambda b: (b, 0)),
        out_shape=jax.ShapeDtypeStruct((batch * MEM_LEN, w_kv.shape[1]), jnp.bfloat16),
        compiler_params=pltpu.CompilerParams(
            dimension_semantics=("parallel",), vmem_limit_bytes=VMEM_LIMIT),
        name="memkv",
    )(mem2, g_mem, w_kv)


def _memattn_kernel(q_ref, kv_ref, o_ref):
    scale = XA_HEAD_DIM ** -0.5
    outs = []
    for h in range(XA_HEADS):
        q = q_ref[:, h * LANES:(h + 1) * LANES]
        k = kv_ref[:, h * LANES:(h + 1) * LANES]
        v = kv_ref[:, (XA_HEADS + h) * LANES:(XA_HEADS + h + 1) * LANES]
        s = _dot_nt(q, k) * scale
        m = jnp.max(s, axis=1, keepdims=True)
        p = jnp.exp(s - m)
        l = jnp.sum(p, axis=1, keepdims=True)
        outs.append(_dot((p / l).astype(jnp.bfloat16), v))
    o_ref[...] = jnp.concatenate(outs, axis=1).astype(o_ref.dtype)


def _mem_attention(p, mkv, batch, seq):
    nq = seq // MEM_TQ
    w = XA_HEADS * XA_HEAD_DIM
    return pl.pallas_call(
        _memattn_kernel,
        grid=(batch, nq),
        in_specs=[
            pl.BlockSpec((MEM_TQ, w), lambda b, qi: (b * nq + qi, 7)),
            pl.BlockSpec((MEM_LEN, 2 * w), lambda b, qi: (b, 0)),
        ],
        out_specs=pl.BlockSpec((MEM_TQ, w), lambda b, qi: (b * nq + qi, 0)),
        out_shape=jax.ShapeDtypeStruct((batch * seq, w), jnp.bfloat16),
        compiler_params=pltpu.CompilerParams(
            dimension_semantics=("parallel", "parallel"), vmem_limit_bytes=VMEM_LIMIT),
        name="memattn",
    )(p, mkv)


def _merge_kernel(x_ref, yd_ref, ys_ref, ym_ref, gmix_ref, wg_ref, bg_ref, wbr_ref, wout_ref,
                  gffn_ref, wr_ref, br_ref, h_ref, hn_ref, lg_ref):
    xf = x_ref[...]
    n = _rms(xf, gmix_ref[...]).astype(jnp.bfloat16)
    merged = jnp.zeros(xf.shape, jnp.float32)
    for i, y_ref in enumerate((yd_ref, ys_ref, ym_ref)):
        gate = jax.nn.sigmoid(_dot(n, wg_ref[i]) + bg_ref[i])
        merged = merged + gate * _dot(y_ref[...], wbr_ref[i])
    h = xf + _dot(merged.astype(jnp.bfloat16), wout_ref[...])
    h_ref[...] = h
    hn = _rms(h, gffn_ref[...]).astype(jnp.bfloat16)
    hn_ref[...] = hn
    lg_ref[...] = _dot(hn, wr_ref[...]) + br_ref[...]


def _merge(x2, y_diff, y_dsa, y_mem, g_mix, w_g, b_g, w_br, w_out, g_ffn, w_r, b_r):
    t = x2.shape[0]
    row = lambda w: pl.BlockSpec((MERGE_TM, w), lambda i: (i, 0))
    full = lambda a: pl.BlockSpec(a.shape, lambda i: (0,) * a.ndim, pipeline_mode=pl.Buffered(1))
    return pl.pallas_call(
        _merge_kernel,
        grid=(t // MERGE_TM,),
        in_specs=[row(D_MODEL), row(512), row(512), row(512), full(g_mix), full(w_g), full(b_g),
                  full(w_br), full(w_out), full(g_ffn), full(w_r), full(b_r)],
        out_specs=[row(D_MODEL), row(D_MODEL), row(LANES)],
        out_shape=[
            jax.ShapeDtypeStruct((t, D_MODEL), jnp.float32),
            jax.ShapeDtypeStruct((t, D_MODEL), jnp.bfloat16),
            jax.ShapeDtypeStruct((t, LANES), jnp.float32),
        ],
        compiler_params=pltpu.CompilerParams(
            dimension_semantics=("parallel",), vmem_limit_bytes=VMEM_LIMIT),
        name="merge",
    )(x2, y_diff, y_dsa, y_mem, g_mix, w_g, b_g, w_br, w_out, g_ffn, w_r, b_r)


def _route_weights(lg):
    lane = lax.broadcasted_iota(jnp.int32, lg.shape, 1).astype(jnp.float32)
    big = float(LANES)
    gl = jnp.where(lane < N_GROUPS, lg, -jnp.inf)
    gmax = jnp.max(gl, axis=1, keepdims=True)
    grp = jnp.min(jnp.where(gl == gmax, lane, big), axis=1, keepdims=True)
    gsum = jnp.sum(jnp.where(lane < N_GROUPS, jnp.exp(gl - gmax), 0.0), axis=1, keepdims=True)
    p_grp = 1.0 / gsum
    lo = N_GROUPS + grp * EXPERTS_PER_GROUP
    el = jnp.where((lane >= lo) & (lane < lo + EXPERTS_PER_GROUP), lg, -jnp.inf)
    e1 = jnp.max(el, axis=1, keepdims=True)
    i1 = jnp.min(jnp.where(el == e1, lane, big), axis=1, keepdims=True)
    el2 = jnp.where(lane == i1, -jnp.inf, el)
    e2 = jnp.max(el2, axis=1, keepdims=True)
    i2 = jnp.min(jnp.where(el2 == e2, lane, big), axis=1, keepdims=True)
    r = jnp.exp(e2 - e1)
    w1 = p_grp / (1.0 + r)
    w2 = p_grp * r / (1.0 + r)
    return jnp.where(lane == i1, w1, 0.0) + jnp.where(lane == i2, w2, 0.0)


def _moe_kernel(hn_ref, lg_ref, h_ref, win_ref, wout_ref, gfin_ref, o_ref, acc, cw):
    e = pl.program_id(1)

    @pl.when(e == 0)
    def _():
        acc[...] = jnp.zeros(acc.shape, jnp.float32)
        cw[...] = _route_weights(lg_ref[...])

    lane = lax.broadcasted_iota(jnp.int32, cw.shape, 1)
    cw_e = jnp.sum(jnp.where(lane == N_GROUPS + e, cw[...], 0.0), axis=1, keepdims=True)
    gu = _dot(hn_ref[...], win_ref[0])
    g = gu[:, :D_EXPERT]
    hid = (g * jax.nn.sigmoid(g)) * gu[:, D_EXPERT:]
    acc[...] += cw_e * _dot(hid.astype(jnp.bfloat16), wout_ref[0])

    @pl.when(e == N_EXPERTS - 1)
    def _():
        o_ref[...] = _rms(h_ref[...] + acc[...], gfin_ref[...])


def _moe(hn, lg, h, w_e_in, w_e_out, g_final):
    t = hn.shape[0]
    return pl.pallas_call(
        _moe_kernel,
        grid=(t // MOE_TM, N_EXPERTS),
        in_specs=[
            pl.BlockSpec((MOE_TM, D_MODEL), lambda i, e: (i, 0)),
            pl.BlockSpec((MOE_TM, LANES), lambda i, e: (i, 0)),
            pl.BlockSpec((MOE_TM, D_MODEL), lambda i, e: (i, 0)),
            pl.BlockSpec((1, D_MODEL, 2 * D_EXPERT), lambda i, e: (e, 0, 0)),
            pl.BlockSpec((1, D_EXPERT, D_MODEL), lambda i, e: (e, 0, 0)),
            pl.BlockSpec((1, D_MODEL), lambda i, e: (0, 0)),
        ],
        out_specs=pl.BlockSpec((MOE_TM, D_MODEL), lambda i, e: (i, 0)),
        out_shape=jax.ShapeDtypeStruct((t, D_MODEL), jnp.float32),
        scratch_shapes=[
            pltpu.VMEM((MOE_TM, D_MODEL), jnp.float32),
            pltpu.VMEM((MOE_TM, LANES), jnp.float32),
        ],
        compiler_params=pltpu.CompilerParams(
            dimension_semantics=("parallel", "arbitrary"), vmem_limit_bytes=VMEM_LIMIT),
        name="moe",
    )(hn, lg, h, w_e_in, w_e_out, g_final)


def kernel(x, positions, mem, g_mix, w_in, b_gate, lambda_q1, lambda_k1, lambda_q2, lambda_k2,
           g_diff_sub, g_mem, w_mem_kv, w_br_diff, w_br_dsa, w_br_mem, w_out, g_ffn,
           w_route_group, b_route_group, w_route_expert, b_route_expert, w_exp_in, w_exp_out,
           g_final):
    b, s, d = x.shape
    t = b * s
    bf = jnp.bfloat16
    top_k = min(TOPK_MAX, s // 4)
    assert d == D_MODEL and s % DIFF_T == 0 and s % DSA_TQ == 0 and top_k <= DSA_TQ
    assert PROJ_TM == DIFF_T and PROJ_TM % DSA_TQ == 0, "proj writes V^T in the attention kernels' key tiles"
    assert g_mix.shape[0] == 1, "single layer"
    lam_init = 0.8 - 0.6 * math.exp(-0.3 * 0)

    wi = w_in[0]
    c = 512
    seg = lambda k: wi[:, k * c:(k + 1) * c]
    o_ik = 7 * c
    w_ik = wi[:, o_ik:o_ik + IDX_DIM]
    w_iw = wi[:, o_ik + IDX_DIM:o_ik + IDX_DIM + IDX_HEADS]
    o_mq = o_ik + IDX_DIM + IDX_HEADS
    w_mq = wi[:, o_mq:o_mq + c]
    w_gl = wi[:, o_mq + c:]
    qs = DIFF_QK_DIM ** -0.5
    qs2 = qs * LOG2E
    w_a = jnp.concatenate([seg(0) * qs2, seg(1), seg(2), seg(3) * qs2, seg(4), seg(5), seg(6) * qs, w_mq],
                          axis=1).astype(bf)
    w_s = jnp.concatenate([w_ik, w_ik, w_iw, jnp.zeros((d, LANES - IDX_HEADS), wi.dtype)], axis=1).astype(bf)
    w_g = w_gl.reshape(d, 3, d).transpose(1, 0, 2).astype(bf)
    b_g = b_gate[0].reshape(3, 1, d)
    w_br = jnp.stack([w_br_diff[0], w_br_dsa[0], w_br_mem[0]]).astype(bf)
    w_r = jnp.concatenate([w_route_group[0], w_route_expert[0],
                           jnp.zeros((d, LANES - N_GROUPS - N_EXPERTS), wi.dtype)], axis=1).astype(bf)
    b_r = jnp.concatenate([b_route_group[0], b_route_expert[0],
                           jnp.zeros((LANES - N_GROUPS - N_EXPERTS,), jnp.float32)]).reshape(1, LANES)

    rot = IDX_DIM // ROPE_FRACTION
    inv_freq = ROPE_THETA ** (-jnp.arange(0, rot, 2, dtype=jnp.float32) / rot)
    inv64 = jnp.concatenate([inv_freq, inv_freq, jnp.zeros((IDX_DIM - rot,), jnp.float32)])
    inv_lane = jnp.concatenate([inv64, inv64]).reshape(1, LANES)

    x2 = x.reshape(t, d)
    pos2 = positions.reshape(t, 1)
    p, ikk, dvt, svt, iwt = _proj(x2, pos2, g_mix, inv_lane, w_a, w_s)

    y_diff = _diff_attention(p, dvt, lambda_q1, lambda_k1, lambda_q2, lambda_k2,
                             g_diff_sub.reshape(DIFF_V_DIM, 1), b, s, lam_init)
    y_dsa = _dsa_attention(p, ikk, iwt, svt, b, s, top_k)
    mkv = _memkv(mem.reshape(b * MEM_LEN, d), g_mem, w_mem_kv[0].astype(bf), b)
    y_mem = _mem_attention(p, mkv, b, s)

    h, hn, lg = _merge(x2, y_diff, y_dsa, y_mem, g_mix, w_g, b_g, w_br, w_out[0].astype(bf), g_ffn, w_r, b_r)
    out = _moe(hn, lg, h, w_exp_in[0].astype(bf), w_exp_out[0].astype(bf), g_final.reshape(1, d))
    return out.reshape(b, s, d)
```

```python
import functools
import math

import jax
import jax.numpy as jnp
from jax import lax
from jax.experimental import pallas as pl
from jax.experimental.pallas import tpu as pltpu

D_MODEL = 1024
MEM_LEN = 256
XA_HEADS = 4
XA_HEAD_DIM = 128
DIFF_HEADS = 4
DIFF_QK_DIM = 64
DIFF_V_DIM = 128
DSA_HEADS = 8
DSA_HEAD_DIM = 64
IDX_HEADS = 8
IDX_DIM = 64
TOPK_MAX = 256
ROPE_THETA = 500000.0
ROPE_FRACTION = 4
N_GROUPS = 4
EXPERTS_PER_GROUP = 4
N_EXPERTS = 16
D_EXPERT = 512
EPS = 1e-6

LANES = 128
SUBLANES = 8
VMEM_LIMIT = 56 * 1024 * 1024
NEG_BIG = -1e30
ONES_ROWS = 16
LOG2E = math.log2(math.e)

PROJ_TM = 512
PROJ_TN = 512
DIFF_T = 512
DSA_TQ = 256
MEM_TQ = 512
MERGE_TM = 512
MOE_TM = 512

_NT = (((1,), (1,)), ((), ()))


def _dot(a, b):
    return jnp.dot(a, b, preferred_element_type=jnp.float32)


def _dot_nt(a, b):
    return lax.dot_general(a, b, _NT, preferred_element_type=jnp.float32)


def _rms(xf, g):
    return xf * lax.rsqrt(jnp.mean(xf * xf, axis=-1, keepdims=True) + EPS) * g


def _proj_kernel(x_ref, pos_ref, g_ref, inv_ref, wa_ref, ws_ref,
                 p_ref, ikk_ref, dvt_ref, svt_ref, iwt_ref, n_scr, cos_scr, sa_scr, sb_scr):
    j = pl.program_id(1)

    def rope(v, reps):
        n = v.shape[1]
        c = jnp.concatenate([cos_scr[...]] * reps, axis=1) if reps > 1 else cos_scr[...]
        a = jnp.concatenate([sa_scr[...]] * reps, axis=1) if reps > 1 else sa_scr[...]
        b = jnp.concatenate([sb_scr[...]] * reps, axis=1) if reps > 1 else sb_scr[...]
        up = pltpu.roll(v, n - 8, 1)
        dn = pltpu.roll(v, 8, 1)
        return v * c + up * a + dn * b

    @pl.when(j == 0)
    def _():
        xf = x_ref[...]
        n_scr[...] = _rms(xf, g_ref[...]).astype(jnp.bfloat16)
        ang = pos_ref[...].astype(jnp.float32) * inv_ref[...]
        lane = lax.broadcasted_iota(jnp.int32, ang.shape, 1) % 64
        cs = jnp.cos(ang)
        sn = jnp.sin(ang)
        cos_scr[...] = jnp.where(lane < 16, cs, 1.0)
        sa_scr[...] = jnp.where(lane < 8, -sn, 0.0)
        sb_scr[...] = jnp.where((lane >= 8) & (lane < 16), sn, 0.0)
        small = _dot(n_scr[...], ws_ref[...])
        ikk_ref[...] = rope(small[:, :LANES], 1).astype(jnp.bfloat16)
        w = small[:, LANES:] * (IDX_HEADS ** -0.5)
        iwt_ref[...] = w.T[:IDX_HEADS, :]

    acc = _dot(n_scr[...], wa_ref[...])
    is_rope = (j == 0) | (j == 1) | (j == 3) | (j == 4) | (j == 6)

    @pl.when(is_rope)
    def _():
        p_ref[...] = rope(acc, PROJ_TN // LANES).astype(jnp.bfloat16)

    @pl.when(jnp.logical_not(is_rope))
    def _():
        p_ref[...] = acc.astype(jnp.bfloat16)

    @pl.when(j == 2)
    def _():
        dvt_ref[0] = acc.T.astype(jnp.bfloat16)

    @pl.when(j == 5)
    def _():
        at = acc.T.astype(jnp.bfloat16)
        for r in range(PROJ_TM // DSA_TQ):
            svt_ref[r] = at[:, r * DSA_TQ:(r + 1) * DSA_TQ]


def _proj(x2, pos2, g_mix, inv_lane, w_a, w_s):
    t = x2.shape[0]
    ncol = w_a.shape[1] // PROJ_TN
    cpt = PROJ_TM // DSA_TQ
    return pl.pallas_call(
        _proj_kernel,
        grid=(t // PROJ_TM, ncol),
        in_specs=[
            pl.BlockSpec((PROJ_TM, D_MODEL), lambda i, j: (i, 0)),
            pl.BlockSpec((PROJ_TM, 1), lambda i, j: (i, 0)),
            pl.BlockSpec((1, D_MODEL), lambda i, j: (0, 0)),
            pl.BlockSpec((1, LANES), lambda i, j: (0, 0)),
            pl.BlockSpec((D_MODEL, PROJ_TN), lambda i, j: (0, j)),
            pl.BlockSpec((D_MODEL, 2 * LANES), lambda i, j: (0, 0)),
        ],
        out_specs=[
            pl.BlockSpec((PROJ_TM, PROJ_TN), lambda i, j: (i, j)),
            pl.BlockSpec((PROJ_TM, LANES), lambda i, j: (i, 0)),
            pl.BlockSpec((1, PROJ_TN, PROJ_TM), lambda i, j: (i, 0, 0)),
            pl.BlockSpec((cpt, PROJ_TN, DSA_TQ), lambda i, j: (i, 0, 0)),
            pl.BlockSpec((IDX_HEADS, PROJ_TM), lambda i, j: (0, i)),
        ],
        out_shape=[
            jax.ShapeDtypeStruct((t, w_a.shape[1]), jnp.bfloat16),
            jax.ShapeDtypeStruct((t, LANES), jnp.bfloat16),
            jax.ShapeDtypeStruct((t // PROJ_TM, PROJ_TN, PROJ_TM), jnp.bfloat16),
            jax.ShapeDtypeStruct((t // DSA_TQ, PROJ_TN, DSA_TQ), jnp.bfloat16),
            jax.ShapeDtypeStruct((IDX_HEADS, t), jnp.float32),
        ],
        scratch_shapes=[
            pltpu.VMEM((PROJ_TM, D_MODEL), jnp.bfloat16),
            pltpu.VMEM((PROJ_TM, LANES), jnp.float32),
            pltpu.VMEM((PROJ_TM, LANES), jnp.float32),
            pltpu.VMEM((PROJ_TM, LANES), jnp.float32),
        ],
        compiler_params=pltpu.CompilerParams(
            dimension_semantics=("parallel", "arbitrary"), vmem_limit_bytes=VMEM_LIMIT),
        name="proj",
    )(x2, pos2, g_mix, inv_lane, w_a, w_s)


def _diff_kernel(q_ref, k_ref, vt_ref, lq1_ref, lk1_ref, lq2_ref, lk2_ref, gs_ref, o_ref,
                 qm, m_s, acc, st_a, st_b, p_scr, *, lam_init):
    qi = pl.program_id(2)
    t = DIFF_T
    q = q_ref[...]
    lane = lax.broadcasted_iota(jnp.int32, q.shape, 1)
    qm[0] = jnp.where(lane < DIFF_QK_DIM, q, jnp.zeros_like(q))
    qm[1] = jnp.where(lane >= DIFF_QK_DIM, q, jnp.zeros_like(q))
    m_s[...] = jnp.full(m_s.shape, NEG_BIG, jnp.float32)
    acc[...] = jnp.zeros(acc.shape, jnp.float32)

    def qk(j, st_ref):
        k = k_ref[pl.ds(pl.multiple_of(j * t, t), t), :]
        for i in range(2):
            st_ref[i] = _dot_nt(k, qm[i])

    def softmax_pv(j, st_ref, diagonal):
        vt = jnp.concatenate([vt_ref[j], jnp.ones((ONES_ROWS, t), jnp.bfloat16)], axis=0)
        if diagonal:
            krow = lax.broadcasted_iota(jnp.int32, (t, t), 0)
            qcol = lax.broadcasted_iota(jnp.int32, (t, t), 1)
            keep = krow <= qcol
        alphas = []
        for i in range(2):
            st = st_ref[i]
            if diagonal:
                st = jnp.where(keep, st, NEG_BIG)
            m_old = m_s[i]
            m_new = jnp.maximum(m_old, jnp.max(st, axis=0, keepdims=True))
            alphas.append(jnp.exp2(m_old - m_new))
            m_s[i] = m_new
            p_scr[i] = jnp.exp2(st - m_new).astype(jnp.bfloat16)
        for i in range(2):
            acc[i] = alphas[i] * acc[i] + _dot(vt, p_scr[i])

    qk(0, st_a)

    def pair(tt, carry):
        j = 2 * tt
        qk(j + 1, st_b)
        softmax_pv(j, st_a, False)
        qk(j + 2, st_a)
        softmax_pv(j + 1, st_b, False)
        return carry

    lax.fori_loop(0, qi // 2, pair, 0)

    @pl.when(qi % 2 == 0)
    def _():
        softmax_pv(qi, st_a, True)

    @pl.when(qi % 2 == 1)
    def _():
        qk(qi, st_b)
        softmax_pv(qi - 1, st_a, False)
        softmax_pv(qi, st_b, True)

    lam = (jnp.exp(jnp.sum(lq1_ref[...] * lk1_ref[...], axis=1, keepdims=True))
           - jnp.exp(jnp.sum(lq2_ref[...] * lk2_ref[...], axis=1, keepdims=True))
           + lam_init)
    dv = DIFF_V_DIM
    ot = (acc[0, :dv, :] / acc[0, dv:dv + 1, :]
          - lam * (acc[1, :dv, :] / acc[1, dv:dv + 1, :]))
    yt = ot * lax.rsqrt(jnp.mean(ot * ot, axis=0, keepdims=True) + EPS) * gs_ref[...]
    o_ref[...] = (yt * (1.0 - lam_init)).T.astype(o_ref.dtype)


def _diff_attention(p, dvt, lq1, lk1, lq2, lk2, g_sub_col, batch, seq, lam_init):
    nb = seq // DIFF_T
    vec = pl.BlockSpec((1, DIFF_QK_DIM), lambda b, h, qi: (0, 0))
    return pl.pallas_call(
        functools.partial(_diff_kernel, lam_init=lam_init),
        grid=(batch, DIFF_HEADS, nb),
        in_specs=[
            pl.BlockSpec((DIFF_T, LANES), lambda b, h, qi: (b * nb + qi, h)),
            pl.BlockSpec((seq, LANES), lambda b, h, qi: (b, DIFF_HEADS + h)),
            pl.BlockSpec((nb, DIFF_V_DIM, DIFF_T), lambda b, h, qi: (b, h, 0)),
            vec, vec, vec, vec,
            pl.BlockSpec((DIFF_V_DIM, 1), lambda b, h, qi: (0, 0)),
        ],
        out_specs=pl.BlockSpec((DIFF_T, LANES), lambda b, h, qi: (b * nb + qi, h)),
        out_shape=jax.ShapeDtypeStruct((batch * seq, DIFF_HEADS * DIFF_V_DIM), jnp.bfloat16),
        scratch_shapes=[
            pltpu.VMEM((2, DIFF_T, LANES), jnp.bfloat16),
            pltpu.VMEM((2, 1, DIFF_T), jnp.float32),
            pltpu.VMEM((2, DIFF_V_DIM + ONES_ROWS, DIFF_T), jnp.float32),
            pltpu.VMEM((2, DIFF_T, DIFF_T), jnp.float32),
            pltpu.VMEM((2, DIFF_T, DIFF_T), jnp.float32),
            pltpu.VMEM((2, DIFF_T, DIFF_T), jnp.bfloat16),
        ],
        compiler_params=pltpu.CompilerParams(
            dimension_semantics=("parallel", "parallel", "arbitrary"),
            vmem_limit_bytes=VMEM_LIMIT),
        name="diffattn",
    )(p, p, dvt, lq1, lk1, lq2, lk2, g_sub_col)


def _key_to_float(key):
    bits = jnp.where(key >= 0, key, key ^ jnp.int32(0x7FFFFFFF))
    return lax.bitcast_convert_type(bits, jnp.float32)


def _count_rows(hit):
    tk, tq = hit.shape
    return jnp.sum(hit.reshape(tk // (4 * SUBLANES), 4 * SUBLANES, tq), axis=0)


def _dsa_kernel(sq_ref, iq_ref, iwt_ref, ikk_ref, sk_ref, svt_ref, o_ref,
                score, hi16, lo16, iqm, sqm, acc, m_s, thr, st_a, st_b, mx_a, mx_b, p_scr, *, seq, top_k):
    qi = pl.program_id(1)
    tq = DSA_TQ
    nkc = qi + 1
    lane = lax.broadcasted_iota(jnp.int32, (tq, LANES), 1)
    half = lane // DSA_HEAD_DIM

    for h in range(DSA_HEADS):
        pr = h // 2
        iqp = iq_ref[:, pr * LANES:(pr + 1) * LANES]
        sqp = sq_ref[:, pr * LANES:(pr + 1) * LANES]
        iqm[h] = jnp.where(half == h % 2, iqp, jnp.zeros_like(iqp))
        sqm[h] = jnp.where(half == h % 2, sqp, jnp.zeros_like(sqp))

    krow = lax.broadcasted_iota(jnp.int32, (tq, tq), 0)
    qcol = lax.broadcasted_iota(jnp.int32, (tq, tq), 1)

    def idx_logits(c, lg_ref):
        kk = ikk_ref[pl.ds(pl.multiple_of(c * tq, tq), tq), :]
        for h in range(IDX_HEADS):
            lg_ref[h] = _dot_nt(kk, iqm[h])

    def idx_score(c, lg_ref):
        sc = jnp.zeros((tq, tq), jnp.float32)
        for h in range(IDX_HEADS):
            sc = sc + iwt_ref[h:h + 1, :] * jnp.maximum(lg_ref[h], 0.0)
        sc = jnp.where(krow <= qcol + (qi - c) * tq, sc, -jnp.inf)
        score[c] = sc
        bits = lax.bitcast_convert_type(sc, jnp.int32)
        okey = jnp.where(bits >= 0, bits, bits ^ jnp.int32(0x7FFFFFFF))
        hi16[c] = lax.shift_right_arithmetic(okey, 16).astype(jnp.int16)
        lo16[c] = ((okey & 0xFFFF) - 2 ** 15).astype(jnp.int16)

    idx_logits(0, st_a)

    def idx_pair(tt, carry):
        c = 2 * tt
        idx_logits(c + 1, st_b)
        idx_score(c, st_a)
        idx_logits(c + 2, st_a)
        idx_score(c + 1, st_b)
        return carry

    lax.fori_loop(0, qi // 2, idx_pair, 0)

    @pl.when(qi % 2 == 0)
    def _():
        idx_score(qi, st_a)

    @pl.when(qi % 2 == 1)
    def _():
        idx_logits(qi, st_b)
        idx_score(qi - 1, st_a)
        idx_score(qi, st_b)

    zero_cnt = jnp.zeros((4 * SUBLANES, tq), jnp.float32)
    i16_min = -2 ** 15

    def count16(buf, pred):
        def body(c, cnt):
            hit = jnp.where(pred(buf[c]), jnp.int16(1), jnp.int16(0))
            h3 = hit.reshape(tq // (4 * SUBLANES), 4 * SUBLANES, tq)
            part = h3[0]
            for r in range(1, h3.shape[0]):
                part = part + h3[r]
            return cnt + part
        cnt = lax.fori_loop(0, nkc, body, jnp.zeros((4 * SUBLANES, tq), jnp.int16))
        return jnp.sum(cnt.astype(jnp.int32), axis=0, keepdims=True)

    def bisect16(buf, want):
        def bit_step(i, cur):
            cand = cur + lax.shift_left(jnp.int32(1), jnp.int32(15) - i)
            c16 = cand.astype(jnp.int16)
            return jnp.where(count16(buf, lambda blk: blk >= c16) >= want, cand, cur)
        return lax.fori_loop(0, 16, bit_step, jnp.full((1, tq), i16_min, jnp.int32))

    t_hi = bisect16(hi16, top_k)
    t_hi16 = t_hi.astype(jnp.int16)
    rest = top_k - count16(hi16, lambda blk: blk > t_hi16)

    def bucket_only(c, carry):
        lo16[c] = jnp.where(hi16[c] == t_hi16, lo16[c], jnp.int16(i16_min))
        return carry

    lax.fori_loop(0, nkc, bucket_only, 0)
    t_lo = bisect16(lo16, rest)
    key = lax.shift_left(t_hi, 16) + (t_lo - i16_min)
    t_f = _key_to_float(key)

    def count_gt_ge(c, carry):
        gt, ge = carry
        blk = score[c]
        return (gt + _count_rows(jnp.where(blk > t_f, 1.0, 0.0)),
                ge + _count_rows(jnp.where(blk >= t_f, 1.0, 0.0)))

    gt, ge = lax.fori_loop(0, nkc, count_gt_ge, (zero_cnt, zero_cnt))
    n_gt = jnp.sum(gt, axis=0, keepdims=True)
    n_ge = jnp.sum(ge, axis=0, keepdims=True)
    q_pos = qi * tq + lax.broadcasted_iota(jnp.int32, (1, tq), 1)
    few = q_pos < top_k - 1
    thr[...] = jnp.where(few, float(jnp.finfo(jnp.float32).min), t_f)
    need = float(top_k) - n_gt
    split = jnp.logical_and(jnp.logical_not(few), n_ge > float(top_k))

    @pl.when(jnp.max(jnp.where(split, 1.0, 0.0)) > 0.0)
    def _():
        def count_eq_below(jc):
            def body(c, cnt):
                hit = jnp.where((score[c] == t_f) & (c * tq + krow < jc), 1.0, 0.0)
                return cnt + _count_rows(hit)
            return jnp.sum(lax.fori_loop(0, nkc, body, zero_cnt), axis=0, keepdims=True)

        nbits = (seq - 1).bit_length()

        def jbit(i, jv):
            cand = jv + lax.shift_left(jnp.int32(1), jnp.int32(nbits - 1) - i)
            return jnp.where(count_eq_below(cand) < need, cand, jv)

        jv = lax.fori_loop(0, nbits, jbit, jnp.zeros((1, tq), jnp.int32))

        def drop_ties(c, carry):
            blk = score[c]
            score[c] = jnp.where(split & (blk == t_f) & (c * tq + krow > jv), -jnp.inf, blk)
            return carry

        lax.fori_loop(0, nkc, drop_ties, 0)

    m_s[...] = jnp.full(m_s.shape, NEG_BIG, jnp.float32)
    acc[...] = jnp.zeros(acc.shape, jnp.float32)

    def qk(c, slot):
        st_ref, mx_ref = slot
        off = pl.multiple_of(c * tq, tq)
        bias = jnp.where(score[c] >= thr[...], 0.0, NEG_BIG)
        for h in range(DSA_HEADS):
            kp = sk_ref[pl.ds(off, tq), (h // 2) * LANES:(h // 2 + 1) * LANES]
            st = _dot_nt(kp, sqm[h]) + bias
            st_ref[h] = st
            mx_ref[h] = jnp.max(st, axis=0, keepdims=True)

    def softmax_pv(c, slot):
        st_ref, mx_ref = slot
        alphas = []
        for h in range(DSA_HEADS):
            m_old = m_s[h]
            m_new = jnp.maximum(m_old, mx_ref[h])
            alphas.append(jnp.exp2(m_old - m_new))
            m_s[h] = m_new
            p_scr[h] = jnp.exp2(st_ref[h] - m_new).astype(jnp.bfloat16)
        ones = jnp.ones((ONES_ROWS, tq), jnp.bfloat16)
        for h in range(DSA_HEADS):
            vt = jnp.concatenate([svt_ref[c, h * DSA_HEAD_DIM:(h + 1) * DSA_HEAD_DIM, :], ones], axis=0)
            acc[h] = alphas[h] * acc[h] + _dot(vt, p_scr[h])

    slot_a, slot_b = (st_a, mx_a), (st_b, mx_b)
    qk(0, slot_a)

    def pair(tt, carry):
        c = 2 * tt
        qk(c + 1, slot_b)
        softmax_pv(c, slot_a)
        qk(c + 2, slot_a)
        softmax_pv(c + 1, slot_b)
        return carry

    lax.fori_loop(0, qi // 2, pair, 0)

    @pl.when(qi % 2 == 0)
    def _():
        softmax_pv(qi, slot_a)

    @pl.when(qi % 2 == 1)
    def _():
        qk(qi, slot_b)
        softmax_pv(qi - 1, slot_a)
        softmax_pv(qi, slot_b)

    dh = DSA_HEAD_DIM
    outs = [acc[h, :dh, :] / acc[h, dh:dh + 1, :] for h in range(DSA_HEADS)]
    o_ref[...] = jnp.concatenate(outs, axis=0).T.astype(o_ref.dtype)


def _dsa_attention(p, ikk, iwt, svt, batch, seq, top_k):
    nq = seq // DSA_TQ
    w = DSA_HEADS * DSA_HEAD_DIM
    once = pl.Buffered(1)
    return pl.pallas_call(
        functools.partial(_dsa_kernel, seq=seq, top_k=top_k),
        grid=(batch, nq),
        in_specs=[
            pl.BlockSpec((DSA_TQ, w), lambda b, qi: (b * nq + qi, 3)),
            pl.BlockSpec((DSA_TQ, w), lambda b, qi: (b * nq + qi, 6)),
            pl.BlockSpec((IDX_HEADS, DSA_TQ), lambda b, qi: (0, b * nq + qi)),
            pl.BlockSpec((seq, LANES), lambda b, qi: (b, 0), pipeline_mode=once),
            pl.BlockSpec((seq, w), lambda b, qi: (b, 4), pipeline_mode=once),
            pl.BlockSpec((nq, w, DSA_TQ), lambda b, qi: (b, 0, 0), pipeline_mode=once),
        ],
        out_specs=pl.BlockSpec((DSA_TQ, w), lambda b, qi: (b * nq + qi, 0)),
        out_shape=jax.ShapeDtypeStruct((batch * seq, w), jnp.bfloat16),
        scratch_shapes=[
            pltpu.VMEM((nq, DSA_TQ, DSA_TQ), jnp.float32),
            pltpu.VMEM((nq, DSA_TQ, DSA_TQ), jnp.int16),
            pltpu.VMEM((nq, DSA_TQ, DSA_TQ), jnp.int16),
            pltpu.VMEM((IDX_HEADS, DSA_TQ, LANES), jnp.bfloat16),
            pltpu.VMEM((DSA_HEADS, DSA_TQ, LANES), jnp.bfloat16),
            pltpu.VMEM((DSA_HEADS, DSA_HEAD_DIM + ONES_ROWS, DSA_TQ), jnp.float32),
            pltpu.VMEM((DSA_HEADS, 1, DSA_TQ), jnp.float32),
            pltpu.VMEM((1, DSA_TQ), jnp.float32),
            pltpu.VMEM((DSA_HEADS, DSA_TQ, DSA_TQ), jnp.float32),
            pltpu.VMEM((DSA_HEADS, DSA_TQ, DSA_TQ), jnp.float32),
            pltpu.VMEM((DSA_HEADS, 1, DSA_TQ), jnp.float32),
            pltpu.VMEM((DSA_HEADS, 1, DSA_TQ), jnp.float32),
            pltpu.VMEM((DSA_HEADS, DSA_TQ, DSA_TQ), jnp.bfloat16),
        ],
        compiler_params=pltpu.CompilerParams(
            dimension_semantics=("parallel", "arbitrary"), vmem_limit_bytes=VMEM_LIMIT),
        name="dsa",
    )(p, p, iwt, ikk, p, svt)


def _memkv_kernel(mem_ref, g_ref, w_ref, o_ref):
    n = _rms(mem_ref[...], g_ref[...]).astype(jnp.bfloat16)
    o_ref[...] = _dot(n, w_ref[...]).astype(o_ref.dtype)


def _memkv(mem2, g_mem, w_kv, batch):
    return pl.pallas_call(
        _memkv_kernel,
        grid=(batch,),
        in_specs=[
            pl.BlockSpec((MEM_LEN, D_MODEL), lambda b: (b, 0)),
            pl.BlockSpec((1, D_MODEL), lambda b: (0, 0)),
            pl.BlockSpec(w_kv.shape, lambda b: (0, 0)),
        ],
        out_specs=pl.BlockSpec((MEM_LEN, w_kv.shape[1]), lambda b: (b, 0)),
        out_shape=jax.ShapeDtypeStruct((batch * MEM_LEN, w_kv.shape[1]), jnp.bfloat16),
        compiler_params=pltpu.CompilerParams(
            dimension_semantics=("parallel",), vmem_limit_bytes=VMEM_LIMIT),
        name="memkv",
    )(mem2, g_mem, w_kv)


def _memattn_kernel(q_ref, kv_ref, o_ref):
    scale = XA_HEAD_DIM ** -0.5
    outs = []
    for h in range(XA_HEADS):
        q = q_ref[:, h * LANES:(h + 1) * LANES]
        k = kv_ref[:, h * LANES:(h + 1) * LANES]
        v = kv_ref[:, (XA_HEADS + h) * LANES:(XA_HEADS + h + 1) * LANES]
        s = _dot_nt(q, k) * scale
        m = jnp.max(s, axis=1, keepdims=True)
        p = jnp.exp(s - m)
        l = jnp.sum(p, axis=1, keepdims=True)
        outs.append(_dot((p / l).astype(jnp.bfloat16), v))
    o_ref[...] = jnp.concatenate(outs, axis=1).astype(o_ref.dtype)


def _mem_attention(p, mkv, batch, seq):
    nq = seq // MEM_TQ
    w = XA_HEADS * XA_HEAD_DIM
    return pl.pallas_call(
        _memattn_kernel,
        grid=(batch, nq),
        in_specs=[
            pl.BlockSpec((MEM_TQ, w), lambda b, qi: (b * nq + qi, 7)),
            pl.BlockSpec((MEM_LEN, 2 * w), lambda b, qi: (b, 0)),
        ],
        out_specs=pl.BlockSpec((MEM_TQ, w), lambda b, qi: (b * nq + qi, 0)),
        out_shape=jax.ShapeDtypeStruct((batch * seq, w), jnp.bfloat16),
        compiler_params=pltpu.CompilerParams(
            dimension_semantics=("parallel", "parallel"), vmem_limit_bytes=VMEM_LIMIT),
        name="memattn",
    )(p, mkv)


def _merge_kernel(x_ref, yd_ref, ys_ref, ym_ref, gmix_ref, wg_ref, bg_ref, wbr_ref, wout_ref,
                  gffn_ref, wr_ref, br_ref, h_ref, hn_ref, lg_ref):
    xf = x_ref[...]
    n = _rms(xf, gmix_ref[...]).astype(jnp.bfloat16)
    merged = jnp.zeros(xf.shape, jnp.float32)
    for i, y_ref in enumerate((yd_ref, ys_ref, ym_ref)):
        gate = jax.nn.sigmoid(_dot(n, wg_ref[i]) + bg_ref[i])
        merged = merged + gate * _dot(y_ref[...], wbr_ref[i])
    h = xf + _dot(merged.astype(jnp.bfloat16), wout_ref[...])
    h_ref[...] = h
    hn = _rms(h, gffn_ref[...]).astype(jnp.bfloat16)
    hn_ref[...] = hn
    lg_ref[...] = _dot(hn, wr_ref[...]) + br_ref[...]


def _merge(x2, y_diff, y_dsa, y_mem, g_mix, w_g, b_g, w_br, w_out, g_ffn, w_r, b_r):
    t = x2.shape[0]
    row = lambda w: pl.BlockSpec((MERGE_TM, w), lambda i: (i, 0))
    full = lambda a: pl.BlockSpec(a.shape, lambda i: (0,) * a.ndim, pipeline_mode=pl.Buffered(1))
    return pl.pallas_call(
        _merge_kernel,
        grid=(t // MERGE_TM,),
        in_specs=[row(D_MODEL), row(512), row(512), row(512), full(g_mix), full(w_g), full(b_g),
                  full(w_br), full(w_out), full(g_ffn), full(w_r), full(b_r)],
        out_specs=[row(D_MODEL), row(D_MODEL), row(LANES)],
        out_shape=[
            jax.ShapeDtypeStruct((t, D_MODEL), jnp.float32),
            jax.ShapeDtypeStruct((t, D_MODEL), jnp.bfloat16),
            jax.ShapeDtypeStruct((t, LANES), jnp.float32),
        ],
        compiler_params=pltpu.CompilerParams(
            dimension_semantics=("parallel",), vmem_limit_bytes=VMEM_LIMIT),
        name="merge",
    )(x2, y_diff, y_dsa, y_mem, g_mix, w_g, b_g, w_br, w_out, g_ffn, w_r, b_r)


def _route_weights(lg):
    lane = lax.broadcasted_iota(jnp.int32, lg.shape, 1).astype(jnp.float32)
    big = float(LANES)
    gl = jnp.where(lane < N_GROUPS, lg, -jnp.inf)
    gmax = jnp.max(gl, axis=1, keepdims=True)
    grp = jnp.min(jnp.where(gl == gmax, lane, big), axis=1, keepdims=True)
    gsum = jnp.sum(jnp.where(lane < N_GROUPS, jnp.exp(gl - gmax), 0.0), axis=1, keepdims=True)
    p_grp = 1.0 / gsum
    lo = N_GROUPS + grp * EXPERTS_PER_GROUP
    el = jnp.where((lane >= lo) & (lane < lo + EXPERTS_PER_GROUP), lg, -jnp.inf)
    e1 = jnp.max(el, axis=1, keepdims=True)
    i1 = jnp.min(jnp.where(el == e1, lane, big), axis=1, keepdims=True)
    el2 = jnp.where(lane == i1, -jnp.inf, el)
    e2 = jnp.max(el2, axis=1, keepdims=True)
    i2 = jnp.min(jnp.where(el2 == e2, lane, big), axis=1, keepdims=True)
    r = jnp.exp(e2 - e1)
    w1 = p_grp / (1.0 + r)
    w2 = p_grp * r / (1.0 + r)
    return jnp.where(lane == i1, w1, 0.0) + jnp.where(lane == i2, w2, 0.0)


def _moe_kernel(hn_ref, lg_ref, h_ref, win_ref, wout_ref, gfin_ref, o_ref, acc, cw):
    e = pl.program_id(1)

    @pl.when(e == 0)
    def _():
        acc[...] = jnp.zeros(acc.shape, jnp.float32)
        cw[...] = _route_weights(lg_ref[...])

    lane = lax.broadcasted_iota(jnp.int32, cw.shape, 1)
    cw_e = jnp.sum(jnp.where(lane == N_GROUPS + e, cw[...], 0.0), axis=1, keepdims=True)
    gu = _dot(hn_ref[...], win_ref[0])
    g = gu[:, :D_EXPERT]
    hid = (g * jax.nn.sigmoid(g)) * gu[:, D_EXPERT:]
    acc[...] += cw_e * _dot(hid.astype(jnp.bfloat16), wout_ref[0])

    @pl.when(e == N_EXPERTS - 1)
    def _():
        o_ref[...] = _rms(h_ref[...] + acc[...], gfin_ref[...])


def _moe(hn, lg, h, w_e_in, w_e_out, g_final):
    t = hn.shape[0]
    return pl.pallas_call(
        _moe_kernel,
        grid=(t // MOE_TM, N_EXPERTS),
        in_specs=[
            pl.BlockSpec((MOE_TM, D_MODEL), lambda i, e: (i, 0)),
            pl.BlockSpec((MOE_TM, LANES), lambda i, e: (i, 0)),
            pl.BlockSpec((MOE_TM, D_MODEL), lambda i, e: (i, 0)),
            pl.BlockSpec((1, D_MODEL, 2 * D_EXPERT), lambda i, e: (e, 0, 0)),
            pl.BlockSpec((1, D_EXPERT, D_MODEL), lambda i, e: (e, 0, 0)),
            pl.BlockSpec((1, D_MODEL), lambda i, e: (0, 0)),
        ],
        out_specs=pl.BlockSpec((MOE_TM, D_MODEL), lambda i, e: (i, 0)),
        out_shape=jax.ShapeDtypeStruct((t, D_MODEL), jnp.float32),
        scratch_shapes=[
            pltpu.VMEM((MOE_TM, D_MODEL), jnp.float32),
            pltpu.VMEM((MOE_TM, LANES), jnp.float32),
        ],
        compiler_params=pltpu.CompilerParams(
            dimension_semantics=("parallel", "arbitrary"), vmem_limit_bytes=VMEM_LIMIT),
        name="moe",
    )(hn, lg, h, w_e_in, w_e_out, g_final)


def kernel(x, positions, mem, g_mix, w_in, b_gate, lambda_q1, lambda_k1, lambda_q2, lambda_k2,
           g_diff_sub, g_mem, w_mem_kv, w_br_diff, w_br_dsa, w_br_mem, w_out, g_ffn,
           w_route_group, b_route_group, w_route_expert, b_route_expert, w_exp_in, w_exp_out,
           g_final):
    b, s, d = x.shape
    t = b * s
    bf = jnp.bfloat16
    top_k = min(TOPK_MAX, s // 4)
    assert d == D_MODEL and s % DIFF_T == 0 and s % DSA_TQ == 0 and top_k <= DSA_TQ
    assert PROJ_TM == DIFF_T and PROJ_TM % DSA_TQ == 0, "proj writes V^T in the attention kernels' key tiles"
    assert g_mix.shape[0] == 1, "single layer"
    lam_init = 0.8 - 0.6 * math.exp(-0.3 * 0)

    wi = w_in[0]
    c = 512
    seg = lambda k: wi[:, k * c:(k + 1) * c]
    o_ik = 7 * c
    w_ik = wi[:, o_ik:o_ik + IDX_DIM]
    w_iw = wi[:, o_ik + IDX_DIM:o_ik + IDX_DIM + IDX_HEADS]
    o_mq = o_ik + IDX_DIM + IDX_HEADS
    w_mq = wi[:, o_mq:o_mq + c]
    w_gl = wi[:, o_mq + c:]
    qs = DIFF_QK_DIM ** -0.5
    qs2 = qs * LOG2E
    w_a = jnp.concatenate([seg(0) * qs2, seg(1), seg(2), seg(3) * qs2, seg(4), seg(5), seg(6) * qs, w_mq],
                          axis=1).astype(bf)
    w_s = jnp.concatenate([w_ik, w_ik, w_iw, jnp.zeros((d, LANES - IDX_HEADS), wi.dtype)], axis=1).astype(bf)
    w_g = w_gl.reshape(d, 3, d).transpose(1, 0, 2).astype(bf)
    b_g = b_gate[0].reshape(3, 1, d)
    w_br = jnp.stack([w_br_diff[0], w_br_dsa[0], w_br_mem[0]]).astype(bf)
    w_r = jnp.concatenate([w_route_group[0], w_route_expert[0],
                           jnp.zeros((d, LANES - N_GROUPS - N_EXPERTS), wi.dtype)], axis=1).astype(bf)
    b_r = jnp.concatenate([b_route_group[0], b_route_expert[0],
                           jnp.zeros((LANES - N_GROUPS - N_EXPERTS,), jnp.float32)]).reshape(1, LANES)

    rot = IDX_DIM // ROPE_FRACTION
    inv_freq = ROPE_THETA ** (-jnp.arange(0, rot, 2, dtype=jnp.float32) / rot)
    inv64 = jnp.concatenate([inv_freq, inv_freq, jnp.zeros((IDX_DIM - rot,), jnp.float32)])
    inv_lane = jnp.concatenate([inv64, inv64]).reshape(1, LANES)

    x2 = x.reshape(t, d)
    pos2 = positions.reshape(t, 1)
    p, ikk, dvt, svt, iwt = _proj(x2, pos2, g_mix, inv_lane, w_a, w_s)

    y_diff = _diff_attention(p, dvt, lambda_q1, lambda_k1, lambda_q2, lambda_k2,
                             g_diff_sub.reshape(DIFF_V_DIM, 1), b, s, lam_init)
    y_dsa = _dsa_attention(p, ikk, iwt, svt, b, s, top_k)
    mkv = _memkv(mem.reshape(b * MEM_LEN, d), g_mem, w_mem_kv[0].astype(bf), b)
    y_mem = _mem_attention(p, mkv, b, s)

    h, hn, lg = _merge(x2, y_diff, y_dsa, y_mem, g_mix, w_g, b_g, w_br, w_out[0].astype(bf), g_ffn, w_r, b_r)
    out = _moe(hn, lg, h, w_exp_in[0].astype(bf), w_exp_out[0].astype(bf), g_final.reshape(1, d))
    return out.reshape(b, s, d)
```

```python
import functools
import math

import jax
import jax.numpy as jnp
from jax import lax
from jax.experimental import pallas as pl
from jax.experimental.pallas import tpu as pltpu

D_MODEL = 1024
MEM_LEN = 256
XA_HEADS = 4
XA_HEAD_DIM = 128
DIFF_HEADS = 4
DIFF_QK_DIM = 64
DIFF_V_DIM = 128
DSA_HEADS = 8
DSA_HEAD_DIM = 64
IDX_HEADS = 8
IDX_DIM = 64
TOPK_MAX = 256
ROPE_THETA = 500000.0
ROPE_FRACTION = 4
N_GROUPS = 4
EXPERTS_PER_GROUP = 4
N_EXPERTS = 16
D_EXPERT = 512
EPS = 1e-6

LANES = 128
SUBLANES = 8
VMEM_LIMIT = 56 * 1024 * 1024
NEG_BIG = -1e30
ONES_ROWS = 16
LOG2E = math.log2(math.e)

PROJ_TM = 512
PROJ_TN = 512
DIFF_T = 512
DSA_TQ = 256
MEM_TQ = 512
MERGE_TM = 512

_NT = (((1,), (1,)), ((), ()))


def _dot(a, b):
    return jnp.dot(a, b, preferred_element_type=jnp.float32)


def _dot_nt(a, b):
    return lax.dot_general(a, b, _NT, preferred_element_type=jnp.float32)


def _rms(xf, g):
    return xf * lax.rsqrt(jnp.mean(xf * xf, axis=-1, keepdims=True) + EPS) * g


def _proj_kernel(x_ref, pos_ref, g_ref, inv_ref, wa_ref, ws_ref,
                 p_ref, ikk_ref, dvt_ref, svt_ref, iwt_ref, n_scr, cos_scr, sa_scr, sb_scr):
    j = pl.program_id(1)

    def rope(v, reps):
        n = v.shape[1]
        c = jnp.concatenate([cos_scr[...]] * reps, axis=1) if reps > 1 else cos_scr[...]
        a = jnp.concatenate([sa_scr[...]] * reps, axis=1) if reps > 1 else sa_scr[...]
        b = jnp.concatenate([sb_scr[...]] * reps, axis=1) if reps > 1 else sb_scr[...]
        up = pltpu.roll(v, n - 8, 1)
        dn = pltpu.roll(v, 8, 1)
        return v * c + up * a + dn * b

    @pl.when(j == 0)
    def _():
        xf = x_ref[...]
        n_scr[...] = _rms(xf, g_ref[...]).astype(jnp.bfloat16)
        ang = pos_ref[...].astype(jnp.float32) * inv_ref[...]
        lane = lax.broadcasted_iota(jnp.int32, ang.shape, 1) % 64
        cs = jnp.cos(ang)
        sn = jnp.sin(ang)
        cos_scr[...] = jnp.where(lane < 16, cs, 1.0)
        sa_scr[...] = jnp.where(lane < 8, -sn, 0.0)
        sb_scr[...] = jnp.where((lane >= 8) & (lane < 16), sn, 0.0)
        small = _dot(n_scr[...], ws_ref[...])
        ikk_ref[...] = rope(small[:, :LANES], 1).astype(jnp.bfloat16)
        w = small[:, LANES:] * (IDX_HEADS ** -0.5)
        iwt_ref[...] = w.T[:IDX_HEADS, :]

    acc = _dot(n_scr[...], wa_ref[...])
    is_rope = (j == 0) | (j == 1) | (j == 3) | (j == 4) | (j == 6)

    @pl.when(is_rope)
    def _():
        p_ref[...] = rope(acc, PROJ_TN // LANES).astype(jnp.bfloat16)

    @pl.when(jnp.logical_not(is_rope))
    def _():
        p_ref[...] = acc.astype(jnp.bfloat16)

    @pl.when(j == 2)
    def _():
        dvt_ref[0] = acc.T.astype(jnp.bfloat16)

    @pl.when(j == 5)
    def _():
        at = acc.T.astype(jnp.bfloat16)
        for r in range(PROJ_TM // DSA_TQ):
            svt_ref[r] = at[:, r * DSA_TQ:(r + 1) * DSA_TQ]


def _proj(x2, pos2, g_mix, inv_lane, w_a, w_s):
    t = x2.shape[0]
    ncol = w_a.shape[1] // PROJ_TN
    cpt = PROJ_TM // DSA_TQ
    return pl.pallas_call(
        _proj_kernel,
        grid=(t // PROJ_TM, ncol),
        in_specs=[
            pl.BlockSpec((PROJ_TM, D_MODEL), lambda i, j: (i, 0)),
            pl.BlockSpec((PROJ_TM, 1), lambda i, j: (i, 0)),
            pl.BlockSpec((1, D_MODEL), lambda i, j: (0, 0)),
            pl.BlockSpec((1, LANES), lambda i, j: (0, 0)),
            pl.BlockSpec((D_MODEL, PROJ_TN), lambda i, j: (0, j)),
            pl.BlockSpec((D_MODEL, 2 * LANES), lambda i, j: (0, 0)),
        ],
        out_specs=[
            pl.BlockSpec((PROJ_TM, PROJ_TN), lambda i, j: (i, j)),
            pl.BlockSpec((PROJ_TM, LANES), lambda i, j: (i, 0)),
            pl.BlockSpec((1, PROJ_TN, PROJ_TM), lambda i, j: (i, 0, 0)),
            pl.BlockSpec((cpt, PROJ_TN, DSA_TQ), lambda i, j: (i, 0, 0)),
            pl.BlockSpec((IDX_HEADS, PROJ_TM), lambda i, j: (0, i)),
        ],
        out_shape=[
            jax.ShapeDtypeStruct((t, w_a.shape[1]), jnp.bfloat16),
            jax.ShapeDtypeStruct((t, LANES), jnp.bfloat16),
            jax.ShapeDtypeStruct((t // PROJ_TM, PROJ_TN, PROJ_TM), jnp.bfloat16),
            jax.ShapeDtypeStruct((t // DSA_TQ, PROJ_TN, DSA_TQ), jnp.bfloat16),
            jax.ShapeDtypeStruct((IDX_HEADS, t), jnp.float32),
        ],
        scratch_shapes=[
            pltpu.VMEM((PROJ_TM, D_MODEL), jnp.bfloat16),
            pltpu.VMEM((PROJ_TM, LANES), jnp.float32),
            pltpu.VMEM((PROJ_TM, LANES), jnp.float32),
            pltpu.VMEM((PROJ_TM, LANES), jnp.float32),
        ],
        compiler_params=pltpu.CompilerParams(
            dimension_semantics=("parallel", "arbitrary"), vmem_limit_bytes=VMEM_LIMIT),
        name="proj",
    )(x2, pos2, g_mix, inv_lane, w_a, w_s)


def _diff_kernel(q_ref, k_ref, vt_ref, lq1_ref, lk1_ref, lq2_ref, lk2_ref, gs_ref, o_ref,
                 qm, m_s, acc, st_a, st_b, p_scr, *, lam_init):
    qi = pl.program_id(2)
    t = DIFF_T
    q = q_ref[...]
    lane = lax.broadcasted_iota(jnp.int32, q.shape, 1)
    qm[0] = jnp.where(lane < DIFF_QK_DIM, q, jnp.zeros_like(q))
    qm[1] = jnp.where(lane >= DIFF_QK_DIM, q, jnp.zeros_like(q))
    m_s[...] = jnp.full(m_s.shape, NEG_BIG, jnp.float32)
    acc[...] = jnp.zeros(acc.shape, jnp.float32)

    def qk(j, st_ref):
        k = k_ref[pl.ds(pl.multiple_of(j * t, t), t), :]
        for i in range(2):
            st_ref[i] = _dot_nt(k, qm[i])

    def softmax_pv(j, st_ref, diagonal):
        vt = jnp.concatenate([vt_ref[j], jnp.ones((ONES_ROWS, t), jnp.bfloat16)], axis=0)
        if diagonal:
            krow = lax.broadcasted_iota(jnp.int32, (t, t), 0)
            qcol = lax.broadcasted_iota(jnp.int32, (t, t), 1)
            keep = krow <= qcol
        alphas = []
        for i in range(2):
            st = st_ref[i]
            if diagonal:
                st = jnp.where(keep, st, NEG_BIG)
            m_old = m_s[i]
            m_new = jnp.maximum(m_old, jnp.max(st, axis=0, keepdims=True))
            alphas.append(jnp.exp2(m_old - m_new))
            m_s[i] = m_new
            p_scr[i] = jnp.exp2(st - m_new).astype(jnp.bfloat16)
        for i in range(2):
            acc[i] = alphas[i] * acc[i] + _dot(vt, p_scr[i])

    qk(0, st_a)

    def pair(tt, carry):
        j = 2 * tt
        qk(j + 1, st_b)
        softmax_pv(j, st_a, False)
        qk(j + 2, st_a)
        softmax_pv(j + 1, st_b, False)
        return carry

    lax.fori_loop(0, qi // 2, pair, 0)

    @pl.when(qi % 2 == 0)
    def _():
        softmax_pv(qi, st_a, True)

    @pl.when(qi % 2 == 1)
    def _():
        qk(qi, st_b)
        softmax_pv(qi - 1, st_a, False)
        softmax_pv(qi, st_b, True)

    lam = (jnp.exp(jnp.sum(lq1_ref[...] * lk1_ref[...], axis=1, keepdims=True))
           - jnp.exp(jnp.sum(lq2_ref[...] * lk2_ref[...], axis=1, keepdims=True))
           + lam_init)
    dv = DIFF_V_DIM
    ot = (acc[0, :dv, :] / acc[0, dv:dv + 1, :]
          - lam * (acc[1, :dv, :] / acc[1, dv:dv + 1, :]))
    yt = ot * lax.rsqrt(jnp.mean(ot * ot, axis=0, keepdims=True) + EPS) * gs_ref[...]
    o_ref[...] = (yt * (1.0 - lam_init)).T.astype(o_ref.dtype)


def _diff_attention(p, dvt, lq1, lk1, lq2, lk2, g_sub_col, batch, seq, lam_init):
    nb = seq // DIFF_T
    vec = pl.BlockSpec((1, DIFF_QK_DIM), lambda b, h, qi: (0, 0))
    return pl.pallas_call(
        functools.partial(_diff_kernel, lam_init=lam_init),
        grid=(batch, DIFF_HEADS, nb),
        in_specs=[
            pl.BlockSpec((DIFF_T, LANES), lambda b, h, qi: (b * nb + qi, h)),
            pl.BlockSpec((seq, LANES), lambda b, h, qi: (b, DIFF_HEADS + h)),
            pl.BlockSpec((nb, DIFF_V_DIM, DIFF_T), lambda b, h, qi: (b, h, 0)),
            vec, vec, vec, vec,
            pl.BlockSpec((DIFF_V_DIM, 1), lambda b, h, qi: (0, 0)),
        ],
        out_specs=pl.BlockSpec((DIFF_T, LANES), lambda b, h, qi: (b * nb + qi, h)),
        out_shape=jax.ShapeDtypeStruct((batch * seq, DIFF_HEADS * DIFF_V_DIM), jnp.bfloat16),
        scratch_shapes=[
            pltpu.VMEM((2, DIFF_T, LANES), jnp.bfloat16),
            pltpu.VMEM((2, 1, DIFF_T), jnp.float32),
            pltpu.VMEM((2, DIFF_V_DIM + ONES_ROWS, DIFF_T), jnp.float32),
            pltpu.VMEM((2, DIFF_T, DIFF_T), jnp.float32),
            pltpu.VMEM((2, DIFF_T, DIFF_T), jnp.float32),
            pltpu.VMEM((2, DIFF_T, DIFF_T), jnp.bfloat16),
        ],
        compiler_params=pltpu.CompilerParams(
            dimension_semantics=("parallel", "parallel", "arbitrary"),
            vmem_limit_bytes=VMEM_LIMIT),
        name="diffattn",
    )(p, p, dvt, lq1, lk1, lq2, lk2, g_sub_col)


def _key_to_float(key):
    bits = jnp.where(key >= 0, key, key ^ jnp.int32(0x7FFFFFFF))
    return lax.bitcast_convert_type(bits, jnp.float32)


def _count_rows(hit):
    tk, tq = hit.shape
    return jnp.sum(hit.reshape(tk // (4 * SUBLANES), 4 * SUBLANES, tq), axis=0)


def _dsa_kernel(sq_ref, iq_ref, iwt_ref, ikk_ref, sk_ref, svt_ref, o_ref,
                score, hi16, lo16, iqm, sqm, acc, m_s, thr, st_a, st_b, mx_a, mx_b, p_scr, *, seq, top_k):
    qi = pl.program_id(1)
    tq = DSA_TQ
    nkc = qi + 1
    lane = lax.broadcasted_iota(jnp.int32, (tq, LANES), 1)
    half = lane // DSA_HEAD_DIM

    for h in range(DSA_HEADS):
        pr = h // 2
        iqp = iq_ref[:, pr * LANES:(pr + 1) * LANES]
        sqp = sq_ref[:, pr * LANES:(pr + 1) * LANES]
        iqm[h] = jnp.where(half == h % 2, iqp, jnp.zeros_like(iqp))
        sqm[h] = jnp.where(half == h % 2, sqp, jnp.zeros_like(sqp))

    krow = lax.broadcasted_iota(jnp.int32, (tq, tq), 0)
    qcol = lax.broadcasted_iota(jnp.int32, (tq, tq), 1)

    def idx_logits(c, lg_ref):
        kk = ikk_ref[pl.ds(pl.multiple_of(c * tq, tq), tq), :]
        for h in range(IDX_HEADS):
            lg_ref[h] = _dot_nt(kk, iqm[h])

    def idx_score(c, lg_ref):
        sc = jnp.zeros((tq, tq), jnp.float32)
        for h in range(IDX_HEADS):
            sc = sc + iwt_ref[h:h + 1, :] * jnp.maximum(lg_ref[h], 0.0)
        sc = jnp.where(krow <= qcol + (qi - c) * tq, sc, -jnp.inf)
        score[c] = sc
        bits = lax.bitcast_convert_type(sc, jnp.int32)
        okey = jnp.where(bits >= 0, bits, bits ^ jnp.int32(0x7FFFFFFF))
        hi16[c] = lax.shift_right_arithmetic(okey, 16).astype(jnp.int16)
        lo16[c] = ((okey & 0xFFFF) - 2 ** 15).astype(jnp.int16)

    idx_logits(0, st_a)

    def idx_pair(tt, carry):
        c = 2 * tt
        idx_logits(c + 1, st_b)
        idx_score(c, st_a)
        idx_logits(c + 2, st_a)
        idx_score(c + 1, st_b)
        return carry

    lax.fori_loop(0, qi // 2, idx_pair, 0)

    @pl.when(qi % 2 == 0)
    def _():
        idx_score(qi, st_a)

    @pl.when(qi % 2 == 1)
    def _():
        idx_logits(qi, st_b)
        idx_score(qi - 1, st_a)
        idx_score(qi, st_b)

    zero_cnt = jnp.zeros((4 * SUBLANES, tq), jnp.float32)
    i16_min = -2 ** 15

    def count16(buf, pred):
        def body(c, cnt):
            hit = jnp.where(pred(buf[c]), jnp.int16(1), jnp.int16(0))
            h3 = hit.reshape(tq // (4 * SUBLANES), 4 * SUBLANES, tq)
            part = h3[0]
            for r in range(1, h3.shape[0]):
                part = part + h3[r]
            return cnt + part
        cnt = lax.fori_loop(0, nkc, body, jnp.zeros((4 * SUBLANES, tq), jnp.int16))
        return jnp.sum(cnt.astype(jnp.int32), axis=0, keepdims=True)

    def bisect16(buf, want):
        def bit_step(i, cur):
            cand = cur + lax.shift_left(jnp.int32(1), jnp.int32(15) - i)
            c16 = cand.astype(jnp.int16)
            return jnp.where(count16(buf, lambda blk: blk >= c16) >= want, cand, cur)
        return lax.fori_loop(0, 16, bit_step, jnp.full((1, tq), i16_min, jnp.int32))

    t_hi = bisect16(hi16, top_k)
    t_hi16 = t_hi.astype(jnp.int16)
    rest = top_k - count16(hi16, lambda blk: blk > t_hi16)

    def bucket_only(c, carry):
        lo16[c] = jnp.where(hi16[c] == t_hi16, lo16[c], jnp.int16(i16_min))
        return carry

    lax.fori_loop(0, nkc, bucket_only, 0)
    t_lo = bisect16(lo16, rest)
    key = lax.shift_left(t_hi, 16) + (t_lo - i16_min)
    t_f = _key_to_float(key)

    def count_gt_ge(c, carry):
        gt, ge = carry
        blk = score[c]
        return (gt + _count_rows(jnp.where(blk > t_f, 1.0, 0.0)),
                ge + _count_rows(jnp.where(blk >= t_f, 1.0, 0.0)))

    gt, ge = lax.fori_loop(0, nkc, count_gt_ge, (zero_cnt, zero_cnt))
    n_gt = jnp.sum(gt, axis=0, keepdims=True)
    n_ge = jnp.sum(ge, axis=0, keepdims=True)
    q_pos = qi * tq + lax.broadcasted_iota(jnp.int32, (1, tq), 1)
    few = q_pos < top_k - 1
    thr[...] = jnp.where(few, float(jnp.finfo(jnp.float32).min), t_f)
    need = float(top_k) - n_gt
    split = jnp.logical_and(jnp.logical_not(few), n_ge > float(top_k))

    @pl.when(jnp.max(jnp.where(split, 1.0, 0.0)) > 0.0)
    def _():
        def count_eq_below(jc):
            def body(c, cnt):
                hit = jnp.where((score[c] == t_f) & (c * tq + krow < jc), 1.0, 0.0)
                return cnt + _count_rows(hit)
            return jnp.sum(lax.fori_loop(0, nkc, body, zero_cnt), axis=0, keepdims=True)

        nbits = (seq - 1).bit_length()

        def jbit(i, jv):
            cand = jv + lax.shift_left(jnp.int32(1), jnp.int32(nbits - 1) - i)
            return jnp.where(count_eq_below(cand) < need, cand, jv)

        jv = lax.fori_loop(0, nbits, jbit, jnp.zeros((1, tq), jnp.int32))

        def drop_ties(c, carry):
            blk = score[c]
            score[c] = jnp.where(split & (blk == t_f) & (c * tq + krow > jv), -jnp.inf, blk)
            return carry

        lax.fori_loop(0, nkc, drop_ties, 0)

    m_s[...] = jnp.full(m_s.shape, NEG_BIG, jnp.float32)
    acc[...] = jnp.zeros(acc.shape, jnp.float32)

    def qk(c, slot):
        st_ref, mx_ref = slot
        off = pl.multiple_of(c * tq, tq)
        bias = jnp.where(score[c] >= thr[...], 0.0, NEG_BIG)
        for h in range(DSA_HEADS):
            kp = sk_ref[pl.ds(off, tq), (h // 2) * LANES:(h // 2 + 1) * LANES]
            st = _dot_nt(kp, sqm[h]) + bias
            st_ref[h] = st
            mx_ref[h] = jnp.max(st, axis=0, keepdims=True)

    def softmax_pv(c, slot):
        st_ref, mx_ref = slot
        alphas = []
        for h in range(DSA_HEADS):
            m_old = m_s[h]
            m_new = jnp.maximum(m_old, mx_ref[h])
            alphas.append(jnp.exp2(m_old - m_new))
            m_s[h] = m_new
            p_scr[h] = jnp.exp2(st_ref[h] - m_new).astype(jnp.bfloat16)
        ones = jnp.ones((ONES_ROWS, tq), jnp.bfloat16)
        for h in range(DSA_HEADS):
            vt = jnp.concatenate([svt_ref[c, h * DSA_HEAD_DIM:(h + 1) * DSA_HEAD_DIM, :], ones], axis=0)
            acc[h] = alphas[h] * acc[h] + _dot(vt, p_scr[h])

    slot_a, slot_b = (st_a, mx_a), (st_b, mx_b)
    qk(0, slot_a)

    def pair(tt, carry):
        c = 2 * tt
        qk(c + 1, slot_b)
        softmax_pv(c, slot_a)
        qk(c + 2, slot_a)
        softmax_pv(c + 1, slot_b)
        return carry

    lax.fori_loop(0, qi // 2, pair, 0)

    @pl.when(qi % 2 == 0)
    def _():
        softmax_pv(qi, slot_a)

    @pl.when(qi % 2 == 1)
    def _():
        qk(qi, slot_b)
        softmax_pv(qi - 1, slot_a)
        softmax_pv(qi, slot_b)

    dh = DSA_HEAD_DIM
    outs = [acc[h, :dh, :] / acc[h, dh:dh + 1, :] for h in range(DSA_HEADS)]
    o_ref[...] = jnp.concatenate(outs, axis=0).T.astype(o_ref.dtype)


def _dsa_attention(p, ikk, iwt, svt, batch, seq, top_k):
    nq = seq // DSA_TQ
    w = DSA_HEADS * DSA_HEAD_DIM
    once = pl.Buffered(1)
    return pl.pallas_call(
        functools.partial(_dsa_kernel, seq=seq, top_k=top_k),
        grid=(batch, nq),
        in_specs=[
            pl.BlockSpec((DSA_TQ, w), lambda b, qi: (b * nq + qi, 3)),
            pl.BlockSpec((DSA_TQ, w), lambda b, qi: (b * nq + qi, 6)),
            pl.BlockSpec((IDX_HEADS, DSA_TQ), lambda b, qi: (0, b * nq + qi)),
            pl.BlockSpec((seq, LANES), lambda b, qi: (b, 0), pipeline_mode=once),
            pl.BlockSpec((seq, w), lambda b, qi: (b, 4), pipeline_mode=once),
            pl.BlockSpec((nq, w, DSA_TQ), lambda b, qi: (b, 0, 0), pipeline_mode=once),
        ],
        out_specs=pl.BlockSpec((DSA_TQ, w), lambda b, qi: (b * nq + qi, 0)),
        out_shape=jax.ShapeDtypeStruct((batch * seq, w), jnp.bfloat16),
        scratch_shapes=[
            pltpu.VMEM((nq, DSA_TQ, DSA_TQ), jnp.float32),
            pltpu.VMEM((nq, DSA_TQ, DSA_TQ), jnp.int16),
            pltpu.VMEM((nq, DSA_TQ, DSA_TQ), jnp.int16),
            pltpu.VMEM((IDX_HEADS, DSA_TQ, LANES), jnp.bfloat16),
            pltpu.VMEM((DSA_HEADS, DSA_TQ, LANES), jnp.bfloat16),
            pltpu.VMEM((DSA_HEADS, DSA_HEAD_DIM + ONES_ROWS, DSA_TQ), jnp.float32),
            pltpu.VMEM((DSA_HEADS, 1, DSA_TQ), jnp.float32),
            pltpu.VMEM((1, DSA_TQ), jnp.float32),
            pltpu.VMEM((DSA_HEADS, DSA_TQ, DSA_TQ), jnp.float32),
            pltpu.VMEM((DSA_HEADS, DSA_TQ, DSA_TQ), jnp.float32),
            pltpu.VMEM((DSA_HEADS, 1, DSA_TQ), jnp.float32),
            pltpu.VMEM((DSA_HEADS, 1, DSA_TQ), jnp.float32),
            pltpu.VMEM((DSA_HEADS, DSA_TQ, DSA_TQ), jnp.bfloat16),
        ],
        compiler_params=pltpu.CompilerParams(
            dimension_semantics=("parallel", "arbitrary"), vmem_limit_bytes=VMEM_LIMIT),
        name="dsa",
    )(p, p, iwt, ikk, p, svt)


def _memkv_kernel(mem_ref, g_ref, w_ref, o_ref):
    n = _rms(mem_ref[...], g_ref[...]).astype(jnp.bfloat16)
    o_ref[...] = _dot(n, w_ref[...]).astype(o_ref.dtype)


def _memkv(mem2, g_mem, w_kv, batch):
    return pl.pallas_call(
        _memkv_kernel,
        grid=(batch,),
        in_specs=[
            pl.BlockSpec((MEM_LEN, D_MODEL), lambda b: (b, 0)),
            pl.BlockSpec((1, D_MODEL), lambda b: (0, 0)),
            pl.BlockSpec(w_kv.shape, lambda b: (0, 0)),
        ],
        out_specs=pl.BlockSpec((MEM_LEN, w_kv.shape[1]), lambda b: (b, 0)),
        out_shape=jax.ShapeDtypeStruct((batch * MEM_LEN, w_kv.shape[1]), jnp.bfloat16),
        compiler_params=pltpu.CompilerParams(
            dimension_semantics=("parallel",), vmem_limit_bytes=VMEM_LIMIT),
        name="memkv",
    )(mem2, g_mem, w_kv)


def _memattn_kernel(q_ref, kv_ref, o_ref):
    scale = XA_HEAD_DIM ** -0.5
    outs = []
    for h in range(XA_HEADS):
        q = q_ref[:, h * LANES:(h + 1) * LANES]
        k = kv_ref[:, h * LANES:(h + 1) * LANES]
        v = kv_ref[:, (XA_HEADS + h) * LANES:(XA_HEADS + h + 1) * LANES]
        s = _dot_nt(q, k) * scale
        m = jnp.max(s, axis=1, keepdims=True)
        p = jnp.exp(s - m)
        l = jnp.sum(p, axis=1, keepdims=True)
        outs.append(_dot((p / l).astype(jnp.bfloat16), v))
    o_ref[...] = jnp.concatenate(outs, axis=1).astype(o_ref.dtype)


def _mem_attention(p, mkv, batch, seq):
    nq = seq // MEM_TQ
    w = XA_HEADS * XA_HEAD_DIM
    return pl.pallas_call(
        _memattn_kernel,
        grid=(batch, nq),
        in_specs=[
            pl.BlockSpec((MEM_TQ, w), lambda b, qi: (b * nq + qi, 7)),
            pl.BlockSpec((MEM_LEN, 2 * w), lambda b, qi: (b, 0)),
        ],
        out_specs=pl.BlockSpec((MEM_TQ, w), lambda b, qi: (b * nq + qi, 0)),
        out_shape=jax.ShapeDtypeStruct((batch * seq, w), jnp.bfloat16),
        compiler_params=pltpu.CompilerParams(
            dimension_semantics=("parallel", "parallel"), vmem_limit_bytes=VMEM_LIMIT),
        name="memattn",
    )(p, mkv)


def _merge_kernel(x_ref, yd_ref, ys_ref, ym_ref, gmix_ref, wg_ref, bg_ref, wbr_ref, wout_ref,
                  gffn_ref, wr_ref, br_ref, h_ref, hn_ref, lg_ref):
    xf = x_ref[...]
    n = _rms(xf, gmix_ref[...]).astype(jnp.bfloat16)
    merged = jnp.zeros(xf.shape, jnp.float32)
    for i, y_ref in enumerate((yd_ref, ys_ref, ym_ref)):
        gate = jax.nn.sigmoid(_dot(n, wg_ref[i]) + bg_ref[i])
        merged = merged + gate * _dot(y_ref[...], wbr_ref[i])
    h = xf + _dot(merged.astype(jnp.bfloat16), wout_ref[...])
    h_ref[...] = h
    hn = _rms(h, gffn_ref[...]).astype(jnp.bfloat16)
    hn_ref[...] = hn
    lg_ref[...] = _dot(hn, wr_ref[...]) + br_ref[...]


def _merge(x2, y_diff, y_dsa, y_mem, g_mix, w_g, b_g, w_br, w_out, g_ffn, w_r, b_r):
    t = x2.shape[0]
    row = lambda w: pl.BlockSpec((MERGE_TM, w), lambda i: (i, 0))
    full = lambda a: pl.BlockSpec(a.shape, lambda i: (0,) * a.ndim, pipeline_mode=pl.Buffered(1))
    return pl.pallas_call(
        _merge_kernel,
        grid=(t // MERGE_TM,),
        in_specs=[row(D_MODEL), row(512), row(512), row(512), full(g_mix), full(w_g), full(b_g),
                  full(w_br), full(w_out), full(g_ffn), full(w_r), full(b_r)],
        out_specs=[row(D_MODEL), row(D_MODEL), row(LANES)],
        out_shape=[
            jax.ShapeDtypeStruct((t, D_MODEL), jnp.float32),
            jax.ShapeDtypeStruct((t, D_MODEL), jnp.bfloat16),
            jax.ShapeDtypeStruct((t, LANES), jnp.float32),
        ],
        compiler_params=pltpu.CompilerParams(
            dimension_semantics=("parallel",), vmem_limit_bytes=VMEM_LIMIT),
        name="merge",
    )(x2, y_diff, y_dsa, y_mem, g_mix, w_g, b_g, w_br, w_out, g_ffn, w_r, b_r)


MOE_ST = 2048
MOE_SUB = 512
MOE_ALIGN = 16
MOE_BLK = 256
MOE_NSUB = MOE_ST // MOE_SUB
MOE_ZROWS = MOE_SUB + N_GROUPS * MOE_ALIGN
MOE_ZK = -(-MOE_ZROWS // LANES) * LANES
MOE_GROWS = MOE_ST + MOE_NSUB * N_GROUPS * MOE_ALIGN + MOE_BLK
_SEG_ROWS, _SEG_LOCAL, _SEG_GLOBAL = 0, 16, 32
_GRP_BASE, _GRP_ROWS = 48, 52


def _route(lg):
    lane = lax.broadcasted_iota(jnp.int32, lg.shape, 1).astype(jnp.float32)
    big = float(LANES)
    gl = jnp.where(lane < N_GROUPS, lg, -jnp.inf)
    gmax = jnp.max(gl, axis=1, keepdims=True)
    grp = jnp.min(jnp.where(gl == gmax, lane, big), axis=1, keepdims=True)
    gsum = jnp.sum(jnp.where(lane < N_GROUPS, jnp.exp(gl - gmax), 0.0), axis=1, keepdims=True)
    p_grp = 1.0 / gsum
    lo = N_GROUPS + grp * EXPERTS_PER_GROUP
    el = jnp.where((lane >= lo) & (lane < lo + EXPERTS_PER_GROUP), lg, -jnp.inf)
    e1 = jnp.max(el, axis=1, keepdims=True)
    i1 = jnp.min(jnp.where(el == e1, lane, big), axis=1, keepdims=True)
    el2 = jnp.where(lane == i1, -jnp.inf, el)
    e2 = jnp.max(el2, axis=1, keepdims=True)
    i2 = jnp.min(jnp.where(el2 == e2, lane, big), axis=1, keepdims=True)
    r = jnp.exp(e2 - e1)
    w1 = p_grp / (1.0 + r)
    w2 = p_grp * r / (1.0 + r)
    return jnp.where(lane == i1, w1, 0.0) + jnp.where(lane == i2, w2, 0.0), grp


def _split3(x):
    a = x.astype(jnp.bfloat16)
    r = x - a.astype(jnp.float32)
    b = r.astype(jnp.bfloat16)
    c = (r - b.astype(jnp.float32)).astype(jnp.bfloat16)
    return a, b, c


def _moe_kernel(hn_ref, lg_ref, h_ref, gfin_ref, win_hbm, wout_hbm, o_ref,
                g_buf, ys, cw_s, dest_s, z_buf, cwz_buf, zy_buf, win_buf, wout_buf, sem, seg):
    j = pl.program_id(1)
    sub, al, bf = MOE_SUB, MOE_ALIGN, jnp.bfloat16
    lane_row = lax.broadcasted_iota(jnp.int32, (1, LANES), 1)

    def copy_rows(src_ref, dst_ref, src, dst, nblk):
        def body(i, carry):
            s0 = pl.multiple_of(src + i * al, al)
            d0 = pl.multiple_of(dst + i * al, al)
            for s_ref, d_ref in zip(src_ref, dst_ref):
                d_ref[pl.ds(d0, al), :] = s_ref[pl.ds(s0, al), :]
            return carry
        lax.fori_loop(0, nblk, body, 0)

    @pl.when(j == 0)
    def _():
        cw, grp = _route(lg_ref[...])
        lane = lax.broadcasted_iota(jnp.int32, (sub, LANES), 1).astype(jnp.float32)
        hots = [lane == grp[u * sub:(u + 1) * sub] for u in range(MOE_NSUB)]
        for u in range(MOE_NSUB):
            n_vec = jnp.sum(jnp.where(hots[u], 1.0, 0.0), axis=0, keepdims=True)
            off = jnp.int32(0)
            for g in range(N_GROUPS):
                n = jnp.sum(jnp.where(lane_row == g, n_vec, 0.0)).astype(jnp.int32)
                rows = ((n + (al - 1)) // al) * al
                seg[_SEG_ROWS + u * N_GROUPS + g] = rows
                seg[_SEG_LOCAL + u * N_GROUPS + g] = off
                off = off + rows
        base = jnp.int32(0)
        for g in range(N_GROUPS):
            seg[_GRP_BASE + g] = base
            pos = base
            for u in range(MOE_NSUB):
                seg[_SEG_GLOBAL + u * N_GROUPS + g] = pos
                pos = pos + seg[_SEG_ROWS + u * N_GROUPS + g]
            seg[_GRP_ROWS + g] = pos - base
            base = pos

        ri = lax.broadcasted_iota(jnp.int32, (sub, sub), 0)
        ci = lax.broadcasted_iota(jnp.int32, (sub, sub), 1)
        earlier = jnp.where(ci < ri, 1.0, 0.0).astype(bf)
        zrow = lax.broadcasted_iota(jnp.int32, (MOE_ZROWS, sub), 0).astype(jnp.float32)
        zy_buf[...] = jnp.zeros(zy_buf.shape, jnp.float32)
        for u in range(MOE_NSUB):
            hot = hots[u]
            before = _dot(earlier, jnp.where(hot, 1.0, 0.0).astype(bf))
            start = jnp.zeros((1, LANES), jnp.float32)
            for g in range(N_GROUPS):
                start = jnp.where(lane_row == g, seg[_SEG_LOCAL + u * N_GROUPS + g].astype(jnp.float32), start)
            dest = jnp.sum(jnp.where(hot, before + start, 0.0), axis=1, keepdims=True)
            dest_b = jnp.broadcast_to(dest, (sub, LANES))
            dest_s[u] = dest_b
            perm = jnp.where(zrow == dest_b.T[0:1, :], 1.0, 0.0).astype(bf)
            z_buf[...] = _dot(perm, hn_ref[u * sub:(u + 1) * sub, :]).astype(bf)
            c3 = _split3(cw[u * sub:(u + 1) * sub])
            cwz_buf[...] = _dot(perm, c3[0]) + _dot(perm, c3[1]) + _dot(perm, c3[2])
            for g in range(N_GROUPS):
                k = u * N_GROUPS + g
                copy_rows((z_buf, cwz_buf), (g_buf, cw_s), seg[_SEG_LOCAL + k], seg[_SEG_GLOBAL + k],
                          seg[_SEG_ROWS + k] // al)

        def weights(e, slot):
            return (pltpu.make_async_copy(win_hbm.at[e], win_buf.at[slot], sem.at[0, slot]),
                    pltpu.make_async_copy(wout_hbm.at[e], wout_buf.at[slot], sem.at[1, slot]))

        for cp in weights(0, 0):
            cp.start()

        def expert(e, carry):
            slot = e % 2
            g = e // EXPERTS_PER_GROUP

            @pl.when(e + 1 < N_EXPERTS)
            def _():
                for cp in weights(e + 1, 1 - slot):
                    cp.start()

            for cp in weights(e, slot):
                cp.wait()
            start = seg[_GRP_BASE + g]
            rows = seg[_GRP_ROWS + g]
            nb = rows // MOE_BLK
            rem = rows - nb * MOE_BLK
            first = e % EXPERTS_PER_GROUP == 0

            def block(row0, n):
                row0 = pl.multiple_of(row0, al)
                gu = _dot(g_buf[pl.ds(row0, n), :], win_buf[slot])
                gt = gu[:, :D_EXPERT]
                hid = (gt * jax.nn.sigmoid(gt)) * gu[:, D_EXPERT:]
                y = _dot(hid.astype(bf), wout_buf[slot])
                ln = lax.broadcasted_iota(jnp.int32, (n, LANES), 1)
                cw_e = jnp.sum(jnp.where(ln == N_GROUPS + e, cw_s[pl.ds(row0, n), :], 0.0), axis=1, keepdims=True)

                @pl.when(first)
                def _():
                    ys[pl.ds(row0, n), :] = cw_e * y

                @pl.when(jnp.logical_not(first))
                def _():
                    ys[pl.ds(row0, n), :] += cw_e * y

            def full_block(b, c):
                block(start + b * MOE_BLK, MOE_BLK)
                return c

            lax.fori_loop(0, nb, full_block, 0)

            @pl.when(rem > MOE_BLK // 2)
            def _():
                block(start + nb * MOE_BLK, MOE_BLK)

            @pl.when((rem > 0) & (rem <= MOE_BLK // 2))
            def _():
                block(start + nb * MOE_BLK, MOE_BLK // 2)

            return carry

        lax.fori_loop(0, N_EXPERTS, expert, 0)

    @pl.when(j > 0)
    def _():
        u = j - 1
        for g in range(N_GROUPS):
            k = u * N_GROUPS + g
            copy_rows((ys,), (zy_buf,), seg[_SEG_GLOBAL + k], seg[_SEG_LOCAL + k], seg[_SEG_ROWS + k] // al)
        zy = zy_buf[...]
        hi = zy.astype(bf)
        lo = (zy - hi.astype(jnp.float32)).astype(bf)
        dest = jnp.concatenate([dest_s[u]] * (MOE_ZK // LANES), axis=1)
        col = lax.broadcasted_iota(jnp.int32, (sub, MOE_ZK), 1).astype(jnp.float32)
        unperm = jnp.where(col == dest, 1.0, 0.0).astype(bf)
        moe = _dot(unperm, hi) + _dot(unperm, lo)
        o_ref[...] = _rms(h_ref[...] + moe, gfin_ref[...])


def _moe(hn, lg, h, w_e_in, w_e_out, g_final):
    t = hn.shape[0]
    row = lambda s, j: (s * MOE_NSUB + jnp.maximum(j - 1, 0), 0)
    return pl.pallas_call(
        _moe_kernel,
        grid=(t // MOE_ST, 1 + MOE_NSUB),
        in_specs=[
            pl.BlockSpec((MOE_ST, D_MODEL), lambda s, j: (s, 0)),
            pl.BlockSpec((MOE_ST, LANES), lambda s, j: (s, 0)),
            pl.BlockSpec((MOE_SUB, D_MODEL), row),
            pl.BlockSpec((1, D_MODEL), lambda s, j: (0, 0)),
            pl.BlockSpec(memory_space=pl.ANY),
            pl.BlockSpec(memory_space=pl.ANY),
        ],
        out_specs=pl.BlockSpec((MOE_SUB, D_MODEL), row),
        out_shape=jax.ShapeDtypeStruct((t, D_MODEL), jnp.float32),
        scratch_shapes=[
            pltpu.VMEM((MOE_GROWS, D_MODEL), jnp.bfloat16),
            pltpu.VMEM((MOE_GROWS, D_MODEL), jnp.float32),
            pltpu.VMEM((MOE_GROWS, LANES), jnp.float32),
            pltpu.VMEM((MOE_NSUB, MOE_SUB, LANES), jnp.float32),
            pltpu.VMEM((MOE_ZROWS, D_MODEL), jnp.bfloat16),
            pltpu.VMEM((MOE_ZROWS, LANES), jnp.float32),
            pltpu.VMEM((MOE_ZK, D_MODEL), jnp.float32),
            pltpu.VMEM((2, D_MODEL, 2 * D_EXPERT), jnp.bfloat16),
            pltpu.VMEM((2, D_EXPERT, D_MODEL), jnp.bfloat16),
            pltpu.SemaphoreType.DMA((2, 2)),
            pltpu.SMEM((64,), jnp.int32),
        ],
        compiler_params=pltpu.CompilerParams(
            dimension_semantics=("parallel", "arbitrary"), vmem_limit_bytes=VMEM_LIMIT),
        name="moe",
    )(hn, lg, h, g_final, w_e_in, w_e_out)


def kernel(x, positions, mem, g_mix, w_in, b_gate, lambda_q1, lambda_k1, lambda_q2, lambda_k2,
           g_diff_sub, g_mem, w_mem_kv, w_br_diff, w_br_dsa, w_br_mem, w_out, g_ffn,
           w_route_group, b_route_group, w_route_expert, b_route_expert, w_exp_in, w_exp_out,
           g_final):
    b, s, d = x.shape
    t = b * s
    bf = jnp.bfloat16
    top_k = min(TOPK_MAX, s // 4)
    assert d == D_MODEL and s % DIFF_T == 0 and s % DSA_TQ == 0 and top_k <= DSA_TQ
    assert PROJ_TM == DIFF_T and PROJ_TM % DSA_TQ == 0, "proj writes V^T in the attention kernels' key tiles"
    assert g_mix.shape[0] == 1, "single layer"
    lam_init = 0.8 - 0.6 * math.exp(-0.3 * 0)

    wi = w_in[0]
    c = 512
    seg = lambda k: wi[:, k * c:(k + 1) * c]
    o_ik = 7 * c
    w_ik = wi[:, o_ik:o_ik + IDX_DIM]
    w_iw = wi[:, o_ik + IDX_DIM:o_ik + IDX_DIM + IDX_HEADS]
    o_mq = o_ik + IDX_DIM + IDX_HEADS
    w_mq = wi[:, o_mq:o_mq + c]
    w_gl = wi[:, o_mq + c:]
    qs = DIFF_QK_DIM ** -0.5
    qs2 = qs * LOG2E
    w_a = jnp.concatenate([seg(0) * qs2, seg(1), seg(2), seg(3) * qs2, seg(4), seg(5), seg(6) * qs, w_mq],
                          axis=1).astype(bf)
    w_s = jnp.concatenate([w_ik, w_ik, w_iw, jnp.zeros((d, LANES - IDX_HEADS), wi.dtype)], axis=1).astype(bf)
    w_g = w_gl.reshape(d, 3, d).transpose(1, 0, 2).astype(bf)
    b_g = b_gate[0].reshape(3, 1, d)
    w_br = jnp.stack([w_br_diff[0], w_br_dsa[0], w_br_mem[0]]).astype(bf)
    w_r = jnp.concatenate([w_route_group[0], w_route_expert[0],
                           jnp.zeros((d, LANES - N_GROUPS - N_EXPERTS), wi.dtype)], axis=1).astype(bf)
    b_r = jnp.concatenate([b_route_group[0], b_route_expert[0],
                           jnp.zeros((LANES - N_GROUPS - N_EXPERTS,), jnp.float32)]).reshape(1, LANES)

    rot = IDX_DIM // ROPE_FRACTION
    inv_freq = ROPE_THETA ** (-jnp.arange(0, rot, 2, dtype=jnp.float32) / rot)
    inv64 = jnp.concatenate([inv_freq, inv_freq, jnp.zeros((IDX_DIM - rot,), jnp.float32)])
    inv_lane = jnp.concatenate([inv64, inv64]).reshape(1, LANES)

    x2 = x.reshape(t, d)
    pos2 = positions.reshape(t, 1)
    p, ikk, dvt, svt, iwt = _proj(x2, pos2, g_mix, inv_lane, w_a, w_s)

    y_diff = _diff_attention(p, dvt, lambda_q1, lambda_k1, lambda_q2, lambda_k2,
                             g_diff_sub.reshape(DIFF_V_DIM, 1), b, s, lam_init)
    y_dsa = _dsa_attention(p, ikk, iwt, svt, b, s, top_k)
    mkv = _memkv(mem.reshape(b * MEM_LEN, d), g_mem, w_mem_kv[0].astype(bf), b)
    y_mem = _mem_attention(p, mkv, b, s)

    h, hn, lg = _merge(x2, y_diff, y_dsa, y_mem, g_mix, w_g, b_g, w_br, w_out[0].astype(bf), g_ffn, w_r, b_r)
    out = _moe(hn, lg, h, w_exp_in[0].astype(bf), w_exp_out[0].astype(bf), g_final.reshape(1, d))
    return out.reshape(b, s, d)
```

```python
import functools
import math

import jax
import jax.numpy as jnp
from jax import lax
from jax.experimental import pallas as pl
from jax.experimental.pallas import tpu as pltpu

D_MODEL = 1024
MEM_LEN = 256
XA_HEADS = 4
XA_HEAD_DIM = 128
DIFF_HEADS = 4
DIFF_QK_DIM = 64
DIFF_V_DIM = 128
DSA_HEADS = 8
DSA_HEAD_DIM = 64
IDX_HEADS = 8
IDX_DIM = 64
TOPK_MAX = 256
ROPE_THETA = 500000.0
ROPE_FRACTION = 4
N_GROUPS = 4
EXPERTS_PER_GROUP = 4
N_EXPERTS = 16
D_EXPERT = 512
EPS = 1e-6

LANES = 128
SUBLANES = 8
VMEM_LIMIT = 56 * 1024 * 1024
NEG_BIG = -1e30
ONES_ROWS = 16
LOG2E = math.log2(math.e)

PROJ_TM = 512
PROJ_TN = 512
DIFF_T = 512
DSA_TQ = 256
DSA_TK = 512
MEM_TQ = 512
MERGE_TM = 512

_NT = (((1,), (1,)), ((), ()))


def _dot(a, b):
    return jnp.dot(a, b, preferred_element_type=jnp.float32)


def _dot_nt(a, b):
    return lax.dot_general(a, b, _NT, preferred_element_type=jnp.float32)


def _rms(xf, g):
    return xf * lax.rsqrt(jnp.mean(xf * xf, axis=-1, keepdims=True) + EPS) * g


def _proj_kernel(x_ref, pos_ref, g_ref, inv_ref, wa_ref, ws_ref,
                 p_ref, ikk_ref, dvt_ref, svt_ref, iwt_ref, n_scr, cos_scr, sa_scr, sb_scr):
    j = pl.program_id(1)

    def rope(v, reps):
        n = v.shape[1]
        c = jnp.concatenate([cos_scr[...]] * reps, axis=1) if reps > 1 else cos_scr[...]
        a = jnp.concatenate([sa_scr[...]] * reps, axis=1) if reps > 1 else sa_scr[...]
        b = jnp.concatenate([sb_scr[...]] * reps, axis=1) if reps > 1 else sb_scr[...]
        up = pltpu.roll(v, n - 8, 1)
        dn = pltpu.roll(v, 8, 1)
        return v * c + up * a + dn * b

    @pl.when(j == 0)
    def _():
        xf = x_ref[...]
        n_scr[...] = _rms(xf, g_ref[...]).astype(jnp.bfloat16)
        ang = pos_ref[...].astype(jnp.float32) * inv_ref[...]
        lane = lax.broadcasted_iota(jnp.int32, ang.shape, 1) % 64
        cs = jnp.cos(ang)
        sn = jnp.sin(ang)
        cos_scr[...] = jnp.where(lane < 16, cs, 1.0)
        sa_scr[...] = jnp.where(lane < 8, -sn, 0.0)
        sb_scr[...] = jnp.where((lane >= 8) & (lane < 16), sn, 0.0)
        small = _dot(n_scr[...], ws_ref[...])
        ikk_ref[...] = rope(small[:, :LANES], 1).astype(jnp.bfloat16)
        w = small[:, LANES:] * (IDX_HEADS ** -0.5)
        iwt_ref[...] = w.T[:IDX_HEADS, :]

    acc = _dot(n_scr[...], wa_ref[...])
    is_rope = (j == 0) | (j == 1) | (j == 3) | (j == 4) | (j == 6)

    @pl.when(is_rope)
    def _():
        p_ref[...] = rope(acc, PROJ_TN // LANES).astype(jnp.bfloat16)

    @pl.when(jnp.logical_not(is_rope))
    def _():
        p_ref[...] = acc.astype(jnp.bfloat16)

    @pl.when(j == 2)
    def _():
        dvt_ref[0] = acc.T.astype(jnp.bfloat16)

    @pl.when(j == 5)
    def _():
        svt_ref[0] = acc.T.astype(jnp.bfloat16)


def _proj(x2, pos2, g_mix, inv_lane, w_a, w_s):
    t = x2.shape[0]
    ncol = w_a.shape[1] // PROJ_TN
    return pl.pallas_call(
        _proj_kernel,
        grid=(t // PROJ_TM, ncol),
        in_specs=[
            pl.BlockSpec((PROJ_TM, D_MODEL), lambda i, j: (i, 0)),
            pl.BlockSpec((PROJ_TM, 1), lambda i, j: (i, 0)),
            pl.BlockSpec((1, D_MODEL), lambda i, j: (0, 0)),
            pl.BlockSpec((1, LANES), lambda i, j: (0, 0)),
            pl.BlockSpec((D_MODEL, PROJ_TN), lambda i, j: (0, j)),
            pl.BlockSpec((D_MODEL, 2 * LANES), lambda i, j: (0, 0)),
        ],
        out_specs=[
            pl.BlockSpec((PROJ_TM, PROJ_TN), lambda i, j: (i, j)),
            pl.BlockSpec((PROJ_TM, LANES), lambda i, j: (i, 0)),
            pl.BlockSpec((1, PROJ_TN, PROJ_TM), lambda i, j: (i, 0, 0)),
            pl.BlockSpec((1, PROJ_TN, PROJ_TM), lambda i, j: (i, 0, 0)),
            pl.BlockSpec((IDX_HEADS, PROJ_TM), lambda i, j: (0, i)),
        ],
        out_shape=[
            jax.ShapeDtypeStruct((t, w_a.shape[1]), jnp.bfloat16),
            jax.ShapeDtypeStruct((t, LANES), jnp.bfloat16),
            jax.ShapeDtypeStruct((t // PROJ_TM, PROJ_TN, PROJ_TM), jnp.bfloat16),
            jax.ShapeDtypeStruct((t // PROJ_TM, PROJ_TN, PROJ_TM), jnp.bfloat16),
            jax.ShapeDtypeStruct((IDX_HEADS, t), jnp.float32),
        ],
        scratch_shapes=[
            pltpu.VMEM((PROJ_TM, D_MODEL), jnp.bfloat16),
            pltpu.VMEM((PROJ_TM, LANES), jnp.float32),
            pltpu.VMEM((PROJ_TM, LANES), jnp.float32),
            pltpu.VMEM((PROJ_TM, LANES), jnp.float32),
        ],
        compiler_params=pltpu.CompilerParams(
            dimension_semantics=("parallel", "arbitrary"), vmem_limit_bytes=VMEM_LIMIT),
        name="proj",
    )(x2, pos2, g_mix, inv_lane, w_a, w_s)


def _diff_kernel(q_ref, k_ref, vt_ref, lq1_ref, lk1_ref, lq2_ref, lk2_ref, gs_ref, o_ref,
                 qm, m_s, acc, st_a, st_b, p_scr, *, lam_init):
    qi = pl.program_id(2)
    t = DIFF_T
    q = q_ref[...]
    lane = lax.broadcasted_iota(jnp.int32, q.shape, 1)
    qm[0] = jnp.where(lane < DIFF_QK_DIM, q, jnp.zeros_like(q))
    qm[1] = jnp.where(lane >= DIFF_QK_DIM, q, jnp.zeros_like(q))
    m_s[...] = jnp.full(m_s.shape, NEG_BIG, jnp.float32)
    acc[...] = jnp.zeros(acc.shape, jnp.float32)

    def qk(j, st_ref):
        k = k_ref[pl.ds(pl.multiple_of(j * t, t), t), :]
        for i in range(2):
            st_ref[i] = _dot_nt(k, qm[i])

    def softmax_pv(j, st_ref, diagonal):
        vt = jnp.concatenate([vt_ref[j], jnp.ones((ONES_ROWS, t), jnp.bfloat16)], axis=0)
        if diagonal:
            krow = lax.broadcasted_iota(jnp.int32, (t, t), 0)
            qcol = lax.broadcasted_iota(jnp.int32, (t, t), 1)
            keep = krow <= qcol
        alphas = []
        for i in range(2):
            st = st_ref[i]
            if diagonal:
                st = jnp.where(keep, st, NEG_BIG)
            m_old = m_s[i]
            m_new = jnp.maximum(m_old, jnp.max(st, axis=0, keepdims=True))
            alphas.append(jnp.exp2(m_old - m_new))
            m_s[i] = m_new
            p_scr[i] = jnp.exp2(st - m_new).astype(jnp.bfloat16)
        for i in range(2):
            acc[i] = alphas[i] * acc[i] + _dot(vt, p_scr[i])

    qk(0, st_a)

    def pair(tt, carry):
        j = 2 * tt
        qk(j + 1, st_b)
        softmax_pv(j, st_a, False)
        qk(j + 2, st_a)
        softmax_pv(j + 1, st_b, False)
        return carry

    lax.fori_loop(0, qi // 2, pair, 0)

    @pl.when(qi % 2 == 0)
    def _():
        softmax_pv(qi, st_a, True)

    @pl.when(qi % 2 == 1)
    def _():
        qk(qi, st_b)
        softmax_pv(qi - 1, st_a, False)
        softmax_pv(qi, st_b, True)

    lam = (jnp.exp(jnp.sum(lq1_ref[...] * lk1_ref[...], axis=1, keepdims=True))
           - jnp.exp(jnp.sum(lq2_ref[...] * lk2_ref[...], axis=1, keepdims=True))
           + lam_init)
    dv = DIFF_V_DIM
    ot = (acc[0, :dv, :] / acc[0, dv:dv + 1, :]
          - lam * (acc[1, :dv, :] / acc[1, dv:dv + 1, :]))
    yt = ot * lax.rsqrt(jnp.mean(ot * ot, axis=0, keepdims=True) + EPS) * gs_ref[...]
    o_ref[...] = (yt * (1.0 - lam_init)).T.astype(o_ref.dtype)


def _diff_attention(p, dvt, lq1, lk1, lq2, lk2, g_sub_col, batch, seq, lam_init):
    nb = seq // DIFF_T
    vec = pl.BlockSpec((1, DIFF_QK_DIM), lambda b, h, qi: (0, 0))
    return pl.pallas_call(
        functools.partial(_diff_kernel, lam_init=lam_init),
        grid=(batch, DIFF_HEADS, nb),
        in_specs=[
            pl.BlockSpec((DIFF_T, LANES), lambda b, h, qi: (b * nb + qi, h)),
            pl.BlockSpec((seq, LANES), lambda b, h, qi: (b, DIFF_HEADS + h)),
            pl.BlockSpec((nb, DIFF_V_DIM, DIFF_T), lambda b, h, qi: (b, h, 0)),
            vec, vec, vec, vec,
            pl.BlockSpec((DIFF_V_DIM, 1), lambda b, h, qi: (0, 0)),
        ],
        out_specs=pl.BlockSpec((DIFF_T, LANES), lambda b, h, qi: (b * nb + qi, h)),
        out_shape=jax.ShapeDtypeStruct((batch * seq, DIFF_HEADS * DIFF_V_DIM), jnp.bfloat16),
        scratch_shapes=[
            pltpu.VMEM((2, DIFF_T, LANES), jnp.bfloat16),
            pltpu.VMEM((2, 1, DIFF_T), jnp.float32),
            pltpu.VMEM((2, DIFF_V_DIM + ONES_ROWS, DIFF_T), jnp.float32),
            pltpu.VMEM((2, DIFF_T, DIFF_T), jnp.float32),
            pltpu.VMEM((2, DIFF_T, DIFF_T), jnp.float32),
            pltpu.VMEM((2, DIFF_T, DIFF_T), jnp.bfloat16),
        ],
        compiler_params=pltpu.CompilerParams(
            dimension_semantics=("parallel", "parallel", "arbitrary"),
            vmem_limit_bytes=VMEM_LIMIT),
        name="diffattn",
    )(p, p, dvt, lq1, lk1, lq2, lk2, g_sub_col)


def _key_to_float(key):
    bits = jnp.where(key >= 0, key, key ^ jnp.int32(0x7FFFFFFF))
    return lax.bitcast_convert_type(bits, jnp.float32)


def _count_rows(hit):
    tk, tq = hit.shape
    return jnp.sum(hit.reshape(tk // (4 * SUBLANES), 4 * SUBLANES, tq), axis=0)


def _dsa_kernel(sq_ref, iq_ref, iwt_ref, ikk_ref, sk_ref, svt_ref, o_ref,
                score, hi16, lo16, iqm, sqm, acc, m_s, thr, st_a, st_b, mx_a, mx_b, p_scr, *, seq, top_k):
    qi = pl.program_id(1)
    tq, tk = DSA_TQ, DSA_TK
    last = (qi * tq) // tk
    nkc = last + 1
    lane = lax.broadcasted_iota(jnp.int32, (tq, LANES), 1)
    half = lane // DSA_HEAD_DIM

    for h in range(DSA_HEADS):
        pr = h // 2
        iqp = iq_ref[:, pr * LANES:(pr + 1) * LANES]
        sqp = sq_ref[:, pr * LANES:(pr + 1) * LANES]
        iqm[h] = jnp.where(half == h % 2, iqp, jnp.zeros_like(iqp))
        sqm[h] = jnp.where(half == h % 2, sqp, jnp.zeros_like(sqp))

    krow = lax.broadcasted_iota(jnp.int32, (tk, tq), 0)
    qcol = lax.broadcasted_iota(jnp.int32, (tk, tq), 1)

    def idx_logits(c, lg_ref):
        kk = ikk_ref[pl.ds(pl.multiple_of(c * tk, tk), tk), :]
        for h in range(IDX_HEADS):
            lg_ref[h] = _dot_nt(kk, iqm[h])

    def idx_score(c, lg_ref):
        sc = jnp.zeros((tk, tq), jnp.float32)
        for h in range(IDX_HEADS):
            sc = sc + iwt_ref[h:h + 1, :] * jnp.maximum(lg_ref[h], 0.0)
        sc = jnp.where(krow <= qcol + (qi * tq - c * tk), sc, -jnp.inf)
        score[c] = sc
        bits = lax.bitcast_convert_type(sc, jnp.int32)
        okey = jnp.where(bits >= 0, bits, bits ^ jnp.int32(0x7FFFFFFF))
        hi16[c] = lax.shift_right_arithmetic(okey, 16).astype(jnp.int16)
        lo16[c] = ((okey & 0xFFFF) - 2 ** 15).astype(jnp.int16)

    idx_logits(0, st_a)

    def idx_pair(tt, carry):
        c = 2 * tt
        idx_logits(c + 1, st_b)
        idx_score(c, st_a)
        idx_logits(c + 2, st_a)
        idx_score(c + 1, st_b)
        return carry

    lax.fori_loop(0, last // 2, idx_pair, 0)

    @pl.when(last % 2 == 0)
    def _():
        idx_score(last, st_a)

    @pl.when(last % 2 == 1)
    def _():
        idx_logits(last, st_b)
        idx_score(last - 1, st_a)
        idx_score(last, st_b)

    zero_cnt = jnp.zeros((4 * SUBLANES, tq), jnp.float32)
    i16_min = -2 ** 15

    def count16(buf, pred):
        def body(c, cnt):
            hit = jnp.where(pred(buf[c]), jnp.int16(1), jnp.int16(0))
            h3 = hit.reshape(tk // (4 * SUBLANES), 4 * SUBLANES, tq)
            part = h3[0]
            for r in range(1, h3.shape[0]):
                part = part + h3[r]
            return cnt + part
        cnt = lax.fori_loop(0, nkc, body, jnp.zeros((4 * SUBLANES, tq), jnp.int16))
        return jnp.sum(cnt.astype(jnp.int32), axis=0, keepdims=True)

    def bisect16(buf, want):
        def bit_step(i, cur):
            cand = cur + lax.shift_left(jnp.int32(1), jnp.int32(15) - i)
            c16 = cand.astype(jnp.int16)
            return jnp.where(count16(buf, lambda blk: blk >= c16) >= want, cand, cur)
        return lax.fori_loop(0, 16, bit_step, jnp.full((1, tq), i16_min, jnp.int32))

    t_hi = bisect16(hi16, top_k)
    t_hi16 = t_hi.astype(jnp.int16)
    rest = top_k - count16(hi16, lambda blk: blk > t_hi16)

    def bucket_only(c, carry):
        lo16[c] = jnp.where(hi16[c] == t_hi16, lo16[c], jnp.int16(i16_min))
        return carry

    lax.fori_loop(0, nkc, bucket_only, 0)
    t_lo = bisect16(lo16, rest)
    key = lax.shift_left(t_hi, 16) + (t_lo - i16_min)
    t_f = _key_to_float(key)

    def count_gt_ge(c, carry):
        gt, ge = carry
        blk = score[c]
        return (gt + _count_rows(jnp.where(blk > t_f, 1.0, 0.0)),
                ge + _count_rows(jnp.where(blk >= t_f, 1.0, 0.0)))

    gt, ge = lax.fori_loop(0, nkc, count_gt_ge, (zero_cnt, zero_cnt))
    n_gt = jnp.sum(gt, axis=0, keepdims=True)
    n_ge = jnp.sum(ge, axis=0, keepdims=True)
    q_pos = qi * tq + lax.broadcasted_iota(jnp.int32, (1, tq), 1)
    few = q_pos < top_k - 1
    thr[...] = jnp.where(few, float(jnp.finfo(jnp.float32).min), t_f)
    need = float(top_k) - n_gt
    split = jnp.logical_and(jnp.logical_not(few), n_ge > float(top_k))

    @pl.when(jnp.max(jnp.where(split, 1.0, 0.0)) > 0.0)
    def _():
        def count_eq_below(jc):
            def body(c, cnt):
                hit = jnp.where((score[c] == t_f) & (c * tk + krow < jc), 1.0, 0.0)
                return cnt + _count_rows(hit)
            return jnp.sum(lax.fori_loop(0, nkc, body, zero_cnt), axis=0, keepdims=True)

        nbits = (seq - 1).bit_length()

        def jbit(i, jv):
            cand = jv + lax.shift_left(jnp.int32(1), jnp.int32(nbits - 1) - i)
            return jnp.where(count_eq_below(cand) < need, cand, jv)

        jv = lax.fori_loop(0, nbits, jbit, jnp.zeros((1, tq), jnp.int32))

        def drop_ties(c, carry):
            blk = score[c]
            score[c] = jnp.where(split & (blk == t_f) & (c * tk + krow > jv), -jnp.inf, blk)
            return carry

        lax.fori_loop(0, nkc, drop_ties, 0)

    m_s[...] = jnp.full(m_s.shape, NEG_BIG, jnp.float32)
    acc[...] = jnp.zeros(acc.shape, jnp.float32)

    def qk(c, slot):
        st_ref, mx_ref = slot
        off = pl.multiple_of(c * tk, tk)
        bias = jnp.where(score[c] >= thr[...], 0.0, NEG_BIG)
        for h in range(DSA_HEADS):
            kp = sk_ref[pl.ds(off, tk), (h // 2) * LANES:(h // 2 + 1) * LANES]
            st = _dot_nt(kp, sqm[h]) + bias
            st_ref[h] = st
            mx_ref[h] = jnp.max(st, axis=0, keepdims=True)

    def softmax_pv(c, slot):
        st_ref, mx_ref = slot
        alphas = []
        for h in range(DSA_HEADS):
            m_old = m_s[h]
            m_new = jnp.maximum(m_old, mx_ref[h])
            alphas.append(jnp.exp2(m_old - m_new))
            m_s[h] = m_new
            p_scr[h] = jnp.exp2(st_ref[h] - m_new).astype(jnp.bfloat16)
        ones = jnp.ones((ONES_ROWS, tk), jnp.bfloat16)
        for h in range(DSA_HEADS):
            vt = jnp.concatenate([svt_ref[c, h * DSA_HEAD_DIM:(h + 1) * DSA_HEAD_DIM, :], ones], axis=0)
            acc[h] = alphas[h] * acc[h] + _dot(vt, p_scr[h])

    slot_a, slot_b = (st_a, mx_a), (st_b, mx_b)
    qk(0, slot_a)

    def pair(tt, carry):
        c = 2 * tt
        qk(c + 1, slot_b)
        softmax_pv(c, slot_a)
        qk(c + 2, slot_a)
        softmax_pv(c + 1, slot_b)
        return carry

    lax.fori_loop(0, last // 2, pair, 0)

    @pl.when(last % 2 == 0)
    def _():
        softmax_pv(last, slot_a)

    @pl.when(last % 2 == 1)
    def _():
        qk(last, slot_b)
        softmax_pv(last - 1, slot_a)
        softmax_pv(last, slot_b)

    dh = DSA_HEAD_DIM
    outs = [acc[h, :dh, :] / acc[h, dh:dh + 1, :] for h in range(DSA_HEADS)]
    o_ref[...] = jnp.concatenate(outs, axis=0).T.astype(o_ref.dtype)


def _dsa_attention(p, ikk, iwt, svt, batch, seq, top_k):
    nq = seq // DSA_TQ
    w = DSA_HEADS * DSA_HEAD_DIM
    once = pl.Buffered(1)
    return pl.pallas_call(
        functools.partial(_dsa_kernel, seq=seq, top_k=top_k),
        grid=(batch, nq),
        in_specs=[
            pl.BlockSpec((DSA_TQ, w), lambda b, qi: (b * nq + qi, 3)),
            pl.BlockSpec((DSA_TQ, w), lambda b, qi: (b * nq + qi, 6)),
            pl.BlockSpec((IDX_HEADS, DSA_TQ), lambda b, qi: (0, b * nq + qi)),
            pl.BlockSpec((seq, LANES), lambda b, qi: (b, 0), pipeline_mode=once),
            pl.BlockSpec((seq, w), lambda b, qi: (b, 4), pipeline_mode=once),
            pl.BlockSpec((seq // DSA_TK, w, DSA_TK), lambda b, qi: (b, 0, 0), pipeline_mode=once),
        ],
        out_specs=pl.BlockSpec((DSA_TQ, w), lambda b, qi: (b * nq + qi, 0)),
        out_shape=jax.ShapeDtypeStruct((batch * seq, w), jnp.bfloat16),
        scratch_shapes=[
            pltpu.VMEM((seq // DSA_TK, DSA_TK, DSA_TQ), jnp.float32),
            pltpu.VMEM((seq // DSA_TK, DSA_TK, DSA_TQ), jnp.int16),
            pltpu.VMEM((seq // DSA_TK, DSA_TK, DSA_TQ), jnp.int16),
            pltpu.VMEM((IDX_HEADS, DSA_TQ, LANES), jnp.bfloat16),
            pltpu.VMEM((DSA_HEADS, DSA_TQ, LANES), jnp.bfloat16),
            pltpu.VMEM((DSA_HEADS, DSA_HEAD_DIM + ONES_ROWS, DSA_TQ), jnp.float32),
            pltpu.VMEM((DSA_HEADS, 1, DSA_TQ), jnp.float32),
            pltpu.VMEM((1, DSA_TQ), jnp.float32),
            pltpu.VMEM((DSA_HEADS, DSA_TK, DSA_TQ), jnp.float32),
            pltpu.VMEM((DSA_HEADS, DSA_TK, DSA_TQ), jnp.float32),
            pltpu.VMEM((DSA_HEADS, 1, DSA_TQ), jnp.float32),
            pltpu.VMEM((DSA_HEADS, 1, DSA_TQ), jnp.float32),
            pltpu.VMEM((DSA_HEADS, DSA_TK, DSA_TQ), jnp.bfloat16),
        ],
        compiler_params=pltpu.CompilerParams(
            dimension_semantics=("parallel", "arbitrary"), vmem_limit_bytes=VMEM_LIMIT),
        name="dsa",
    )(p, p, iwt, ikk, p, svt)


def _memkv_kernel(mem_ref, g_ref, w_ref, o_ref):
    n = _rms(mem_ref[...], g_ref[...]).astype(jnp.bfloat16)
    o_ref[...] = _dot(n, w_ref[...]).astype(o_ref.dtype)


def _memkv(mem2, g_mem, w_kv, batch):
    return pl.pallas_call(
        _memkv_kernel,
        grid=(batch,),
        in_specs=[
            pl.BlockSpec((MEM_LEN, D_MODEL), lambda b: (b, 0)),
            pl.BlockSpec((1, D_MODEL), lambda b: (0, 0)),
            pl.BlockSpec(w_kv.shape, lambda b: (0, 0)),
        ],
        out_specs=pl.BlockSpec((MEM_LEN, w_kv.shape[1]), lambda b: (b, 0)),
        out_shape=jax.ShapeDtypeStruct((batch * MEM_LEN, w_kv.shape[1]), jnp.bfloat16),
        compiler_params=pltpu.CompilerParams(
            dimension_semantics=("parallel",), vmem_limit_bytes=VMEM_LIMIT),
        name="memkv",
    )(mem2, g_mem, w_kv)


def _memattn_kernel(q_ref, kv_ref, o_ref):
    scale = XA_HEAD_DIM ** -0.5
    outs = []
    for h in range(XA_HEADS):
        q = q_ref[:, h * LANES:(h + 1) * LANES]
        k = kv_ref[:, h * LANES:(h + 1) * LANES]
        v = kv_ref[:, (XA_HEADS + h) * LANES:(XA_HEADS + h + 1) * LANES]
        s = _dot_nt(q, k) * scale
        m = jnp.max(s, axis=1, keepdims=True)
        p = jnp.exp(s - m)
        l = jnp.sum(p, axis=1, keepdims=True)
        outs.append(_dot((p / l).astype(jnp.bfloat16), v))
    o_ref[...] = jnp.concatenate(outs, axis=1).astype(o_ref.dtype)


def _mem_attention(p, mkv, batch, seq):
    nq = seq // MEM_TQ
    w = XA_HEADS * XA_HEAD_DIM
    return pl.pallas_call(
        _memattn_kernel,
        grid=(batch, nq),
        in_specs=[
            pl.BlockSpec((MEM_TQ, w), lambda b, qi: (b * nq + qi, 7)),
            pl.BlockSpec((MEM_LEN, 2 * w), lambda b, qi: (b, 0)),
        ],
        out_specs=pl.BlockSpec((MEM_TQ, w), lambda b, qi: (b * nq + qi, 0)),
        out_shape=jax.ShapeDtypeStruct((batch * seq, w), jnp.bfloat16),
        compiler_params=pltpu.CompilerParams(
            dimension_semantics=("parallel", "parallel"), vmem_limit_bytes=VMEM_LIMIT),
        name="memattn",
    )(p, mkv)


def _merge_kernel(x_ref, yd_ref, ys_ref, ym_ref, gmix_ref, wg_ref, bg_ref, wbr_ref, wout_ref,
                  gffn_ref, wr_ref, br_ref, h_ref, hn_ref, lg_ref):
    xf = x_ref[...]
    n = _rms(xf, gmix_ref[...]).astype(jnp.bfloat16)
    merged = jnp.zeros(xf.shape, jnp.float32)
    for i, y_ref in enumerate((yd_ref, ys_ref, ym_ref)):
        gate = jax.nn.sigmoid(_dot(n, wg_ref[i]) + bg_ref[i])
        merged = merged + gate * _dot(y_ref[...], wbr_ref[i])
    h = xf + _dot(merged.astype(jnp.bfloat16), wout_ref[...])
    h_ref[...] = h
    hn = _rms(h, gffn_ref[...]).astype(jnp.bfloat16)
    hn_ref[...] = hn
    lg_ref[...] = _dot(hn, wr_ref[...]) + br_ref[...]


def _merge(x2, y_diff, y_dsa, y_mem, g_mix, w_g, b_g, w_br, w_out, g_ffn, w_r, b_r):
    t = x2.shape[0]
    row = lambda w: pl.BlockSpec((MERGE_TM, w), lambda i: (i, 0))
    full = lambda a: pl.BlockSpec(a.shape, lambda i: (0,) * a.ndim, pipeline_mode=pl.Buffered(1))
    return pl.pallas_call(
        _merge_kernel,
        grid=(t // MERGE_TM,),
        in_specs=[row(D_MODEL), row(512), row(512), row(512), full(g_mix), full(w_g), full(b_g),
                  full(w_br), full(w_out), full(g_ffn), full(w_r), full(b_r)],
        out_specs=[row(D_MODEL), row(D_MODEL), row(LANES)],
        out_shape=[
            jax.ShapeDtypeStruct((t, D_MODEL), jnp.float32),
            jax.ShapeDtypeStruct((t, D_MODEL), jnp.bfloat16),
            jax.ShapeDtypeStruct((t, LANES), jnp.float32),
        ],
        compiler_params=pltpu.CompilerParams(
            dimension_semantics=("parallel",), vmem_limit_bytes=VMEM_LIMIT),
        name="merge",
    )(x2, y_diff, y_dsa, y_mem, g_mix, w_g, b_g, w_br, w_out, g_ffn, w_r, b_r)


MOE_ST = 2048
MOE_SUB = 512
MOE_ALIGN = 16
MOE_BLK = 256
MOE_NSUB = MOE_ST // MOE_SUB
MOE_ZROWS = MOE_SUB + N_GROUPS * MOE_ALIGN
MOE_ZK = -(-MOE_ZROWS // LANES) * LANES
MOE_GROWS = MOE_ST + MOE_NSUB * N_GROUPS * MOE_ALIGN + MOE_BLK
_SEG_ROWS, _SEG_LOCAL, _SEG_GLOBAL = 0, 16, 32
_GRP_BASE, _GRP_ROWS = 48, 52


def _route(lg):
    lane = lax.broadcasted_iota(jnp.int32, lg.shape, 1).astype(jnp.float32)
    big = float(LANES)
    gl = jnp.where(lane < N_GROUPS, lg, -jnp.inf)
    gmax = jnp.max(gl, axis=1, keepdims=True)
    grp = jnp.min(jnp.where(gl == gmax, lane, big), axis=1, keepdims=True)
    gsum = jnp.sum(jnp.where(lane < N_GROUPS, jnp.exp(gl - gmax), 0.0), axis=1, keepdims=True)
    p_grp = 1.0 / gsum
    lo = N_GROUPS + grp * EXPERTS_PER_GROUP
    el = jnp.where((lane >= lo) & (lane < lo + EXPERTS_PER_GROUP), lg, -jnp.inf)
    e1 = jnp.max(el, axis=1, keepdims=True)
    i1 = jnp.min(jnp.where(el == e1, lane, big), axis=1, keepdims=True)
    el2 = jnp.where(lane == i1, -jnp.inf, el)
    e2 = jnp.max(el2, axis=1, keepdims=True)
    i2 = jnp.min(jnp.where(el2 == e2, lane, big), axis=1, keepdims=True)
    r = jnp.exp(e2 - e1)
    w1 = p_grp / (1.0 + r)
    w2 = p_grp * r / (1.0 + r)
    return jnp.where(lane == i1, w1, 0.0) + jnp.where(lane == i2, w2, 0.0), grp


def _split3(x):
    a = x.astype(jnp.bfloat16)
    r = x - a.astype(jnp.float32)
    b = r.astype(jnp.bfloat16)
    c = (r - b.astype(jnp.float32)).astype(jnp.bfloat16)
    return a, b, c


def _moe_kernel(hn_ref, lg_ref, h_ref, gfin_ref, win_hbm, wout_hbm, o_ref,
                g_buf, ys, cw_s, dest_s, z_buf, cwz_buf, zy_buf, win_buf, wout_buf, sem, seg):
    j = pl.program_id(1)
    sub, al, bf = MOE_SUB, MOE_ALIGN, jnp.bfloat16
    lane_row = lax.broadcasted_iota(jnp.int32, (1, LANES), 1)

    def copy_rows(src_ref, dst_ref, src, dst, nblk):
        def body(i, carry):
            s0 = pl.multiple_of(src + i * al, al)
            d0 = pl.multiple_of(dst + i * al, al)
            for s_ref, d_ref in zip(src_ref, dst_ref):
                d_ref[pl.ds(d0, al), :] = s_ref[pl.ds(s0, al), :]
            return carry
        lax.fori_loop(0, nblk, body, 0)

    @pl.when(j == 0)
    def _():
        cw, grp = _route(lg_ref[...])
        lane = lax.broadcasted_iota(jnp.int32, (sub, LANES), 1).astype(jnp.float32)
        hots = [lane == grp[u * sub:(u + 1) * sub] for u in range(MOE_NSUB)]
        for u in range(MOE_NSUB):
            n_vec = jnp.sum(jnp.where(hots[u], 1.0, 0.0), axis=0, keepdims=True)
            off = jnp.int32(0)
            for g in range(N_GROUPS):
                n = jnp.sum(jnp.where(lane_row == g, n_vec, 0.0)).astype(jnp.int32)
                rows = ((n + (al - 1)) // al) * al
                seg[_SEG_ROWS + u * N_GROUPS + g] = rows
                seg[_SEG_LOCAL + u * N_GROUPS + g] = off
                off = off + rows
        base = jnp.int32(0)
        for g in range(N_GROUPS):
            seg[_GRP_BASE + g] = base
            pos = base
            for u in range(MOE_NSUB):
                seg[_SEG_GLOBAL + u * N_GROUPS + g] = pos
                pos = pos + seg[_SEG_ROWS + u * N_GROUPS + g]
            seg[_GRP_ROWS + g] = pos - base
            base = pos

        ri = lax.broadcasted_iota(jnp.int32, (sub, sub), 0)
        ci = lax.broadcasted_iota(jnp.int32, (sub, sub), 1)
        earlier = jnp.where(ci < ri, 1.0, 0.0).astype(bf)
        zrow = lax.broadcasted_iota(jnp.int32, (MOE_ZROWS, sub), 0).astype(jnp.float32)
        zy_buf[...] = jnp.zeros(zy_buf.shape, jnp.float32)
        for u in range(MOE_NSUB):
            hot = hots[u]
            before = _dot(earlier, jnp.where(hot, 1.0, 0.0).astype(bf))
            start = jnp.zeros((1, LANES), jnp.float32)
            for g in range(N_GROUPS):
                start = jnp.where(lane_row == g, seg[_SEG_LOCAL + u * N_GROUPS + g].astype(jnp.float32), start)
            dest = jnp.sum(jnp.where(hot, before + start, 0.0), axis=1, keepdims=True)
            dest_b = jnp.broadcast_to(dest, (sub, LANES))
            dest_s[u] = dest_b
            perm = jnp.where(zrow == dest_b.T[0:1, :], 1.0, 0.0).astype(bf)
            z_buf[...] = _dot(perm, hn_ref[u * sub:(u + 1) * sub, :]).astype(bf)
            c3 = _split3(cw[u * sub:(u + 1) * sub])
            cwz_buf[...] = _dot(perm, c3[0]) + _dot(perm, c3[1]) + _dot(perm, c3[2])
            for g in range(N_GROUPS):
                k = u * N_GROUPS + g
                copy_rows((z_buf, cwz_buf), (g_buf, cw_s), seg[_SEG_LOCAL + k], seg[_SEG_GLOBAL + k],
                          seg[_SEG_ROWS + k] // al)

        def weights(e, slot):
            return (pltpu.make_async_copy(win_hbm.at[e], win_buf.at[slot], sem.at[0, slot]),
                    pltpu.make_async_copy(wout_hbm.at[e], wout_buf.at[slot], sem.at[1, slot]))

        for cp in weights(0, 0):
            cp.start()

        def expert(e, carry):
            slot = e % 2
            g = e // EXPERTS_PER_GROUP

            @pl.when(e + 1 < N_EXPERTS)
            def _():
                for cp in weights(e + 1, 1 - slot):
                    cp.start()

            for cp in weights(e, slot):
                cp.wait()
            start = seg[_GRP_BASE + g]
            rows = seg[_GRP_ROWS + g]
            big = 2 * MOE_BLK
            nbig = rows // big
            rem = rows - nbig * big
            first = e % EXPERTS_PER_GROUP == 0

            def block(row0, n):
                row0 = pl.multiple_of(row0, al)
                gu = _dot(g_buf[pl.ds(row0, n), :], win_buf[slot])
                gt = gu[:, :D_EXPERT]
                hid = (gt * jax.nn.sigmoid(gt)) * gu[:, D_EXPERT:]
                y = _dot(hid.astype(bf), wout_buf[slot])
                ln = lax.broadcasted_iota(jnp.int32, (n, LANES), 1)
                cw_e = jnp.sum(jnp.where(ln == N_GROUPS + e, cw_s[pl.ds(row0, n), :], 0.0), axis=1, keepdims=True)

                @pl.when(first)
                def _():
                    ys[pl.ds(row0, n), :] = cw_e * y

                @pl.when(jnp.logical_not(first))
                def _():
                    ys[pl.ds(row0, n), :] += cw_e * y

            def big_block(b, c):
                block(start + b * big, big)
                return c

            lax.fori_loop(0, nbig, big_block, 0)

            tail = start + nbig * big

            @pl.when(rem > MOE_BLK)
            def _():
                block(tail, big)

            @pl.when((rem > MOE_BLK // 2) & (rem <= MOE_BLK))
            def _():
                block(tail, MOE_BLK)

            @pl.when((rem > 0) & (rem <= MOE_BLK // 2))
            def _():
                block(tail, MOE_BLK // 2)

            return carry

        lax.fori_loop(0, N_EXPERTS, expert, 0)

    @pl.when(j > 0)
    def _():
        u = j - 1
        for g in range(N_GROUPS):
            k = u * N_GROUPS + g
            copy_rows((ys,), (zy_buf,), seg[_SEG_GLOBAL + k], seg[_SEG_LOCAL + k], seg[_SEG_ROWS + k] // al)
        zy = zy_buf[...]
        hi = zy.astype(bf)
        lo = (zy - hi.astype(jnp.float32)).astype(bf)
        dest = jnp.concatenate([dest_s[u]] * (MOE_ZK // LANES), axis=1)
        col = lax.broadcasted_iota(jnp.int32, (sub, MOE_ZK), 1).astype(jnp.float32)
        unperm = jnp.where(col == dest, 1.0, 0.0).astype(bf)
        moe = _dot(unperm, hi) + _dot(unperm, lo)
        o_ref[...] = _rms(h_ref[...] + moe, gfin_ref[...])


def _moe(hn, lg, h, w_e_in, w_e_out, g_final):
    t = hn.shape[0]
    row = lambda s, j: (s * MOE_NSUB + jnp.maximum(j - 1, 0), 0)
    return pl.pallas_call(
        _moe_kernel,
        grid=(t // MOE_ST, 1 + MOE_NSUB),
        in_specs=[
            pl.BlockSpec((MOE_ST, D_MODEL), lambda s, j: (s, 0)),
            pl.BlockSpec((MOE_ST, LANES), lambda s, j: (s, 0)),
            pl.BlockSpec((MOE_SUB, D_MODEL), row),
            pl.BlockSpec((1, D_MODEL), lambda s, j: (0, 0)),
            pl.BlockSpec(memory_space=pl.ANY),
            pl.BlockSpec(memory_space=pl.ANY),
        ],
        out_specs=pl.BlockSpec((MOE_SUB, D_MODEL), row),
        out_shape=jax.ShapeDtypeStruct((t, D_MODEL), jnp.float32),
        scratch_shapes=[
            pltpu.VMEM((MOE_GROWS, D_MODEL), jnp.bfloat16),
            pltpu.VMEM((MOE_GROWS, D_MODEL), jnp.float32),
            pltpu.VMEM((MOE_GROWS, LANES), jnp.float32),
            pltpu.VMEM((MOE_NSUB, MOE_SUB, LANES), jnp.float32),
            pltpu.VMEM((MOE_ZROWS, D_MODEL), jnp.bfloat16),
            pltpu.VMEM((MOE_ZROWS, LANES), jnp.float32),
            pltpu.VMEM((MOE_ZK, D_MODEL), jnp.float32),
            pltpu.VMEM((2, D_MODEL, 2 * D_EXPERT), jnp.bfloat16),
            pltpu.VMEM((2, D_EXPERT, D_MODEL), jnp.bfloat16),
            pltpu.SemaphoreType.DMA((2, 2)),
            pltpu.SMEM((64,), jnp.int32),
        ],
        compiler_params=pltpu.CompilerParams(
            dimension_semantics=("parallel", "arbitrary"), vmem_limit_bytes=VMEM_LIMIT),
        name="moe",
    )(hn, lg, h, g_final, w_e_in, w_e_out)


def kernel(x, positions, mem, g_mix, w_in, b_gate, lambda_q1, lambda_k1, lambda_q2, lambda_k2,
           g_diff_sub, g_mem, w_mem_kv, w_br_diff, w_br_dsa, w_br_mem, w_out, g_ffn,
           w_route_group, b_route_group, w_route_expert, b_route_expert, w_exp_in, w_exp_out,
           g_final):
    b, s, d = x.shape
    t = b * s
    bf = jnp.bfloat16
    top_k = min(TOPK_MAX, s // 4)
    assert d == D_MODEL and s % DIFF_T == 0 and s % DSA_TQ == 0 and top_k <= DSA_TQ
    assert PROJ_TM == DIFF_T == DSA_TK and DSA_TK % DSA_TQ == 0, "proj writes V^T in the attention kernels' key tiles"
    assert g_mix.shape[0] == 1, "single layer"
    lam_init = 0.8 - 0.6 * math.exp(-0.3 * 0)

    wi = w_in[0]
    c = 512
    seg = lambda k: wi[:, k * c:(k + 1) * c]
    o_ik = 7 * c
    w_ik = wi[:, o_ik:o_ik + IDX_DIM]
    w_iw = wi[:, o_ik + IDX_DIM:o_ik + IDX_DIM + IDX_HEADS]
    o_mq = o_ik + IDX_DIM + IDX_HEADS
    w_mq = wi[:, o_mq:o_mq + c]
    w_gl = wi[:, o_mq + c:]
    qs = DIFF_QK_DIM ** -0.5
    qs2 = qs * LOG2E
    w_a = jnp.concatenate([seg(0) * qs2, seg(1), seg(2), seg(3) * qs2, seg(4), seg(5), seg(6) * qs, w_mq],
                          axis=1).astype(bf)
    w_s = jnp.concatenate([w_ik, w_ik, w_iw, jnp.zeros((d, LANES - IDX_HEADS), wi.dtype)], axis=1).astype(bf)
    w_g = w_gl.reshape(d, 3, d).transpose(1, 0, 2).astype(bf)
    b_g = b_gate[0].reshape(3, 1, d)
    w_br = jnp.stack([w_br_diff[0], w_br_dsa[0], w_br_mem[0]]).astype(bf)
    w_r = jnp.concatenate([w_route_group[0], w_route_expert[0],
                           jnp.zeros((d, LANES - N_GROUPS - N_EXPERTS), wi.dtype)], axis=1).astype(bf)
    b_r = jnp.concatenate([b_route_group[0], b_route_expert[0],
                           jnp.zeros((LANES - N_GROUPS - N_EXPERTS,), jnp.float32)]).reshape(1, LANES)

    rot = IDX_DIM // ROPE_FRACTION
    inv_freq = ROPE_THETA ** (-jnp.arange(0, rot, 2, dtype=jnp.float32) / rot)
    inv64 = jnp.concatenate([inv_freq, inv_freq, jnp.zeros((IDX_DIM - rot,), jnp.float32)])
    inv_lane = jnp.concatenate([inv64, inv64]).reshape(1, LANES)

    x2 = x.reshape(t, d)
    pos2 = positions.reshape(t, 1)
    p, ikk, dvt, svt, iwt = _proj(x2, pos2, g_mix, inv_lane, w_a, w_s)

    y_diff = _diff_attention(p, dvt, lambda_q1, lambda_k1, lambda_q2, lambda_k2,
                             g_diff_sub.reshape(DIFF_V_DIM, 1), b, s, lam_init)
    y_dsa = _dsa_attention(p, ikk, iwt, svt, b, s, top_k)
    mkv = _memkv(mem.reshape(b * MEM_LEN, d), g_mem, w_mem_kv[0].astype(bf), b)
    y_mem = _mem_attention(p, mkv, b, s)

    h, hn, lg = _merge(x2, y_diff, y_dsa, y_mem, g_mix, w_g, b_g, w_br, w_out[0].astype(bf), g_ffn, w_r, b_r)
    out = _moe(hn, lg, h, w_exp_in[0].astype(bf), w_exp_out[0].astype(bf), g_final.reshape(1, d))
    return out.reshape(b, s, d)
```

```python
import functools
import math

import jax
import jax.numpy as jnp
from jax import lax
from jax.experimental import pallas as pl
from jax.experimental.pallas import tpu as pltpu

D_MODEL = 1024
MEM_LEN = 256
XA_HEADS = 4
XA_HEAD_DIM = 128
DIFF_HEADS = 4
DIFF_QK_DIM = 64
DIFF_V_DIM = 128
DSA_HEADS = 8
DSA_HEAD_DIM = 64
IDX_HEADS = 8
IDX_DIM = 64
TOPK_MAX = 256
ROPE_THETA = 500000.0
ROPE_FRACTION = 4
N_GROUPS = 4
EXPERTS_PER_GROUP = 4
N_EXPERTS = 16
D_EXPERT = 512
EPS = 1e-6

LANES = 128
SUBLANES = 8
VMEM_LIMIT = 56 * 1024 * 1024
NEG_BIG = -1e30
ONES_ROWS = 16
LOG2E = math.log2(math.e)

PROJ_TM = 512
PROJ_TN = 512
DIFF_T = 512
DSA_TQ = 256
DSA_TK = 512
MEM_TQ = 512
MERGE_TM = 512

_NT = (((1,), (1,)), ((), ()))


def _dot(a, b):
    return jnp.dot(a, b, preferred_element_type=jnp.float32)


def _dot_nt(a, b):
    return lax.dot_general(a, b, _NT, preferred_element_type=jnp.float32)


def _rms(xf, g):
    return xf * lax.rsqrt(jnp.mean(xf * xf, axis=-1, keepdims=True) + EPS) * g


def _proj_kernel(x_ref, pos_ref, g_ref, inv_ref, wa_ref, ws_ref,
                 p_ref, ikk_ref, dvt_ref, svt_ref, iwt_ref, n_scr, cos_scr, sa_scr, sb_scr):
    j = pl.program_id(1)

    def rope(v, reps):
        n = v.shape[1]
        c = jnp.concatenate([cos_scr[...]] * reps, axis=1) if reps > 1 else cos_scr[...]
        a = jnp.concatenate([sa_scr[...]] * reps, axis=1) if reps > 1 else sa_scr[...]
        b = jnp.concatenate([sb_scr[...]] * reps, axis=1) if reps > 1 else sb_scr[...]
        up = pltpu.roll(v, n - 8, 1)
        dn = pltpu.roll(v, 8, 1)
        return v * c + up * a + dn * b

    @pl.when(j == 0)
    def _():
        xf = x_ref[...]
        n_scr[...] = _rms(xf, g_ref[...]).astype(jnp.bfloat16)
        ang = pos_ref[...].astype(jnp.float32) * inv_ref[...]
        lane = lax.broadcasted_iota(jnp.int32, ang.shape, 1) % 64
        cs = jnp.cos(ang)
        sn = jnp.sin(ang)
        cos_scr[...] = jnp.where(lane < 16, cs, 1.0)
        sa_scr[...] = jnp.where(lane < 8, -sn, 0.0)
        sb_scr[...] = jnp.where((lane >= 8) & (lane < 16), sn, 0.0)
        small = _dot(n_scr[...], ws_ref[...])
        ikk_ref[...] = rope(small[:, :LANES], 1).astype(jnp.bfloat16)
        w = small[:, LANES:] * (IDX_HEADS ** -0.5)
        iwt_ref[...] = w.T[:IDX_HEADS, :]

    acc = _dot(n_scr[...], wa_ref[...])
    is_rope = (j == 0) | (j == 1) | (j == 3) | (j == 4) | (j == 6)

    @pl.when(is_rope)
    def _():
        p_ref[...] = rope(acc, PROJ_TN // LANES).astype(jnp.bfloat16)

    @pl.when(jnp.logical_not(is_rope))
    def _():
        p_ref[...] = acc.astype(jnp.bfloat16)

    @pl.when(j == 2)
    def _():
        dvt_ref[0] = acc.T.astype(jnp.bfloat16)

    @pl.when(j == 5)
    def _():
        svt_ref[0] = acc.T.astype(jnp.bfloat16)


def _proj(x2, pos2, g_mix, inv_lane, w_a, w_s):
    t = x2.shape[0]
    ncol = w_a.shape[1] // PROJ_TN
    return pl.pallas_call(
        _proj_kernel,
        grid=(t // PROJ_TM, ncol),
        in_specs=[
            pl.BlockSpec((PROJ_TM, D_MODEL), lambda i, j: (i, 0)),
            pl.BlockSpec((PROJ_TM, 1), lambda i, j: (i, 0)),
            pl.BlockSpec((1, D_MODEL), lambda i, j: (0, 0)),
            pl.BlockSpec((1, LANES), lambda i, j: (0, 0)),
            pl.BlockSpec((D_MODEL, PROJ_TN), lambda i, j: (0, j)),
            pl.BlockSpec((D_MODEL, 2 * LANES), lambda i, j: (0, 0)),
        ],
        out_specs=[
            pl.BlockSpec((PROJ_TM, PROJ_TN), lambda i, j: (i, j)),
            pl.BlockSpec((PROJ_TM, LANES), lambda i, j: (i, 0)),
            pl.BlockSpec((1, PROJ_TN, PROJ_TM), lambda i, j: (i, 0, 0)),
            pl.BlockSpec((1, PROJ_TN, PROJ_TM), lambda i, j: (i, 0, 0)),
            pl.BlockSpec((IDX_HEADS, PROJ_TM), lambda i, j: (0, i)),
        ],
        out_shape=[
            jax.ShapeDtypeStruct((t, w_a.shape[1]), jnp.bfloat16),
            jax.ShapeDtypeStruct((t, LANES), jnp.bfloat16),
            jax.ShapeDtypeStruct((t // PROJ_TM, PROJ_TN, PROJ_TM), jnp.bfloat16),
            jax.ShapeDtypeStruct((t // PROJ_TM, PROJ_TN, PROJ_TM), jnp.bfloat16),
            jax.ShapeDtypeStruct((IDX_HEADS, t), jnp.float32),
        ],
        scratch_shapes=[
            pltpu.VMEM((PROJ_TM, D_MODEL), jnp.bfloat16),
            pltpu.VMEM((PROJ_TM, LANES), jnp.float32),
            pltpu.VMEM((PROJ_TM, LANES), jnp.float32),
            pltpu.VMEM((PROJ_TM, LANES), jnp.float32),
        ],
        compiler_params=pltpu.CompilerParams(
            dimension_semantics=("parallel", "arbitrary"), vmem_limit_bytes=VMEM_LIMIT),
        name="proj",
    )(x2, pos2, g_mix, inv_lane, w_a, w_s)


def _diff_kernel(q_ref, k_ref, vt_ref, lq1_ref, lk1_ref, lq2_ref, lk2_ref, gs_ref, o_ref,
                 qm, m_s, acc, st_a, st_b, mx_a, mx_b, p_scr, *, lam_init):
    qi = pl.program_id(2)
    t = DIFF_T
    q = q_ref[...]
    lane = lax.broadcasted_iota(jnp.int32, q.shape, 1)
    qm[0] = jnp.where(lane < DIFF_QK_DIM, q, jnp.zeros_like(q))
    qm[1] = jnp.where(lane >= DIFF_QK_DIM, q, jnp.zeros_like(q))
    m_s[...] = jnp.full(m_s.shape, NEG_BIG, jnp.float32)
    acc[...] = jnp.zeros(acc.shape, jnp.float32)

    def qk(j, slot, diagonal=False):
        st_ref, mx_ref = slot
        k = k_ref[pl.ds(pl.multiple_of(j * t, t), t), :]
        if diagonal:
            krow = lax.broadcasted_iota(jnp.int32, (t, t), 0)
            qcol = lax.broadcasted_iota(jnp.int32, (t, t), 1)
            keep = krow <= qcol
        for i in range(2):
            st = _dot_nt(k, qm[i])
            if diagonal:
                st = jnp.where(keep, st, NEG_BIG)
            st_ref[i] = st
            mx_ref[i] = jnp.max(st, axis=0, keepdims=True)

    def softmax_pv(j, slot):
        st_ref, mx_ref = slot
        vt = jnp.concatenate([vt_ref[j], jnp.ones((ONES_ROWS, t), jnp.bfloat16)], axis=0)
        alphas = []
        for i in range(2):
            m_old = m_s[i]
            m_new = jnp.maximum(m_old, mx_ref[i])
            alphas.append(jnp.exp2(m_old - m_new))
            m_s[i] = m_new
            p_scr[i] = jnp.exp2(st_ref[i] - m_new).astype(jnp.bfloat16)
        for i in range(2):
            acc[i] = alphas[i] * acc[i] + _dot(vt, p_scr[i])

    slot_a, slot_b = (st_a, mx_a), (st_b, mx_b)

    @pl.when(qi == 0)
    def _():
        qk(0, slot_a, diagonal=True)
        softmax_pv(0, slot_a)

    @pl.when(qi > 0)
    def _():
        qk(0, slot_a)

    def pair(tt, carry):
        j = 2 * tt
        qk(j + 1, slot_b)
        softmax_pv(j, slot_a)
        qk(j + 2, slot_a)
        softmax_pv(j + 1, slot_b)
        return carry

    lax.fori_loop(0, (qi - 1) // 2, pair, 0)

    @pl.when(qi % 2 == 1)
    def _():
        qk(qi, slot_b, diagonal=True)
        softmax_pv(qi - 1, slot_a)
        softmax_pv(qi, slot_b)

    @pl.when((qi % 2 == 0) & (qi > 0))
    def _():
        qk(qi - 1, slot_b)
        softmax_pv(qi - 2, slot_a)
        qk(qi, slot_a, diagonal=True)
        softmax_pv(qi - 1, slot_b)
        softmax_pv(qi, slot_a)

    lam = (jnp.exp(jnp.sum(lq1_ref[...] * lk1_ref[...], axis=1, keepdims=True))
           - jnp.exp(jnp.sum(lq2_ref[...] * lk2_ref[...], axis=1, keepdims=True))
           + lam_init)
    dv = DIFF_V_DIM
    ot = (acc[0, :dv, :] / acc[0, dv:dv + 1, :]
          - lam * (acc[1, :dv, :] / acc[1, dv:dv + 1, :]))
    yt = ot * lax.rsqrt(jnp.mean(ot * ot, axis=0, keepdims=True) + EPS) * gs_ref[...]
    o_ref[...] = (yt * (1.0 - lam_init)).T.astype(o_ref.dtype)


def _diff_attention(p, dvt, lq1, lk1, lq2, lk2, g_sub_col, batch, seq, lam_init):
    nb = seq // DIFF_T
    vec = pl.BlockSpec((1, DIFF_QK_DIM), lambda b, h, qi: (0, 0))
    return pl.pallas_call(
        functools.partial(_diff_kernel, lam_init=lam_init),
        grid=(batch, DIFF_HEADS, nb),
        in_specs=[
            pl.BlockSpec((DIFF_T, LANES), lambda b, h, qi: (b * nb + qi, h)),
            pl.BlockSpec((seq, LANES), lambda b, h, qi: (b, DIFF_HEADS + h)),
            pl.BlockSpec((nb, DIFF_V_DIM, DIFF_T), lambda b, h, qi: (b, h, 0)),
            vec, vec, vec, vec,
            pl.BlockSpec((DIFF_V_DIM, 1), lambda b, h, qi: (0, 0)),
        ],
        out_specs=pl.BlockSpec((DIFF_T, LANES), lambda b, h, qi: (b * nb + qi, h)),
        out_shape=jax.ShapeDtypeStruct((batch * seq, DIFF_HEADS * DIFF_V_DIM), jnp.bfloat16),
        scratch_shapes=[
            pltpu.VMEM((2, DIFF_T, LANES), jnp.bfloat16),
            pltpu.VMEM((2, 1, DIFF_T), jnp.float32),
            pltpu.VMEM((2, DIFF_V_DIM + ONES_ROWS, DIFF_T), jnp.float32),
            pltpu.VMEM((2, DIFF_T, DIFF_T), jnp.float32),
            pltpu.VMEM((2, DIFF_T, DIFF_T), jnp.float32),
            pltpu.VMEM((2, 1, DIFF_T), jnp.float32),
            pltpu.VMEM((2, 1, DIFF_T), jnp.float32),
            pltpu.VMEM((2, DIFF_T, DIFF_T), jnp.bfloat16),
        ],
        compiler_params=pltpu.CompilerParams(
            dimension_semantics=("parallel", "parallel", "arbitrary"),
            vmem_limit_bytes=VMEM_LIMIT),
        name="diffattn",
    )(p, p, dvt, lq1, lk1, lq2, lk2, g_sub_col)


def _key_to_float(key):
    bits = jnp.where(key >= 0, key, key ^ jnp.int32(0x7FFFFFFF))
    return lax.bitcast_convert_type(bits, jnp.float32)


def _count_rows(hit):
    tk, tq = hit.shape
    return jnp.sum(hit.reshape(tk // (4 * SUBLANES), 4 * SUBLANES, tq), axis=0)


def _dsa_kernel(sq_ref, iq_ref, iwt_ref, ikk_ref, sk_ref, svt_ref, o_ref,
                score, hi16, lo16, gmax16, iqm, sqm, acc, m_s, thr, st_a, st_b, mx_a, mx_b, p_scr, *, seq, top_k):
    qi = pl.program_id(1)
    tq, tk = DSA_TQ, DSA_TK
    last = (qi * tq) // tk
    nkc = last + 1
    lane = lax.broadcasted_iota(jnp.int32, (tq, LANES), 1)
    half = lane // DSA_HEAD_DIM

    for h in range(DSA_HEADS):
        pr = h // 2
        iqp = iq_ref[:, pr * LANES:(pr + 1) * LANES]
        sqp = sq_ref[:, pr * LANES:(pr + 1) * LANES]
        iqm[h] = jnp.where(half == h % 2, iqp, jnp.zeros_like(iqp))
        sqm[h] = jnp.where(half == h % 2, sqp, jnp.zeros_like(sqp))

    krow = lax.broadcasted_iota(jnp.int32, (tk, tq), 0)
    qcol = lax.broadcasted_iota(jnp.int32, (tk, tq), 1)

    def idx_logits(c, lg_ref):
        kk = ikk_ref[pl.ds(pl.multiple_of(c * tk, tk), tk), :]
        for h in range(IDX_HEADS):
            lg_ref[h] = _dot_nt(kk, iqm[h])

    def idx_score(c, lg_ref):
        sc = jnp.zeros((tk, tq), jnp.float32)
        for h in range(IDX_HEADS):
            sc = sc + iwt_ref[h:h + 1, :] * jnp.maximum(lg_ref[h], 0.0)
        sc = jnp.where(krow <= qcol + (qi * tq - c * tk), sc, -jnp.inf)
        score[c] = sc
        bits = lax.bitcast_convert_type(sc, jnp.int32)
        okey = jnp.where(bits >= 0, bits, bits ^ jnp.int32(0x7FFFFFFF))
        hi = lax.shift_right_arithmetic(okey, 16).astype(jnp.int16)
        hi16[c] = hi
        lo16[c] = ((okey & 0xFFFF) - 2 ** 15).astype(jnp.int16)
        gm = gmax16[0]
        gmax16[0] = jnp.where(hi > gm, hi, gm)

    gmax16[...] = jnp.full(gmax16.shape, -2 ** 15, jnp.int16)
    idx_logits(0, st_a)

    def idx_pair(tt, carry):
        c = 2 * tt
        idx_logits(c + 1, st_b)
        idx_score(c, st_a)
        idx_logits(c + 2, st_a)
        idx_score(c + 1, st_b)
        return carry

    lax.fori_loop(0, last // 2, idx_pair, 0)

    @pl.when(last % 2 == 0)
    def _():
        idx_score(last, st_a)

    @pl.when(last % 2 == 1)
    def _():
        idx_logits(last, st_b)
        idx_score(last - 1, st_a)
        idx_score(last, st_b)

    zero_cnt = jnp.zeros((4 * SUBLANES, tq), jnp.float32)
    i16_min = -2 ** 15

    def count16(buf, pred, n=nkc):
        def body(c, cnt):
            hit = jnp.where(pred(buf[c]), jnp.int16(1), jnp.int16(0))
            h3 = hit.reshape(tk // (4 * SUBLANES), 4 * SUBLANES, tq)
            part = h3[0]
            for r in range(1, h3.shape[0]):
                part = part + h3[r]
            return cnt + part
        cnt = lax.fori_loop(0, n, body, jnp.zeros((4 * SUBLANES, tq), jnp.int16))
        return jnp.sum(cnt.astype(jnp.int32), axis=0, keepdims=True)

    def bisect16(buf, want, lo=None, hi=None, nbits=16, n=nkc):
        lo = jnp.full((1, tq), i16_min, jnp.int32) if lo is None else lo

        def bit_step(i, cur):
            cand = cur + lax.shift_left(jnp.int32(1), nbits - 1 - i)
            c16 = cand.astype(jnp.int16)
            ok = count16(buf, lambda blk: blk >= c16, n) >= want
            if hi is not None:
                ok = ok & (cand <= hi)
            return jnp.where(ok, cand, cur)
        return lax.fori_loop(0, nbits, bit_step, lo)

    g_lo = bisect16(gmax16, top_k, n=1)
    g_hi = jnp.max(gmax16[0].astype(jnp.int32), axis=0, keepdims=True)
    span_bits = 32 - jnp.min(lax.clz(g_hi - g_lo))
    t_hi = bisect16(hi16, top_k, lo=g_lo, hi=g_hi, nbits=span_bits)
    t_hi16 = t_hi.astype(jnp.int16)
    rest = top_k - count16(hi16, lambda blk: blk > t_hi16)

    def bucket_only(c, carry):
        lo16[c] = jnp.where(hi16[c] == t_hi16, lo16[c], jnp.int16(i16_min))
        return carry

    lax.fori_loop(0, nkc, bucket_only, 0)
    t_lo = bisect16(lo16, rest)
    key = lax.shift_left(t_hi, 16) + (t_lo - i16_min)
    t_f = _key_to_float(key)

    def count_gt_ge(c, carry):
        gt, ge = carry
        blk = score[c]
        return (gt + _count_rows(jnp.where(blk > t_f, 1.0, 0.0)),
                ge + _count_rows(jnp.where(blk >= t_f, 1.0, 0.0)))

    gt, ge = lax.fori_loop(0, nkc, count_gt_ge, (zero_cnt, zero_cnt))
    n_gt = jnp.sum(gt, axis=0, keepdims=True)
    n_ge = jnp.sum(ge, axis=0, keepdims=True)
    q_pos = qi * tq + lax.broadcasted_iota(jnp.int32, (1, tq), 1)
    few = q_pos < top_k - 1
    thr[...] = jnp.where(few, float(jnp.finfo(jnp.float32).min), t_f)
    need = float(top_k) - n_gt
    split = jnp.logical_and(jnp.logical_not(few), n_ge > float(top_k))

    @pl.when(jnp.max(jnp.where(split, 1.0, 0.0)) > 0.0)
    def _():
        def count_eq_below(jc):
            def body(c, cnt):
                hit = jnp.where((score[c] == t_f) & (c * tk + krow < jc), 1.0, 0.0)
                return cnt + _count_rows(hit)
            return jnp.sum(lax.fori_loop(0, nkc, body, zero_cnt), axis=0, keepdims=True)

        nbits = (seq - 1).bit_length()

        def jbit(i, jv):
            cand = jv + lax.shift_left(jnp.int32(1), jnp.int32(nbits - 1) - i)
            return jnp.where(count_eq_below(cand) < need, cand, jv)

        jv = lax.fori_loop(0, nbits, jbit, jnp.zeros((1, tq), jnp.int32))

        def drop_ties(c, carry):
            blk = score[c]
            score[c] = jnp.where(split & (blk == t_f) & (c * tk + krow > jv), -jnp.inf, blk)
            return carry

        lax.fori_loop(0, nkc, drop_ties, 0)

    m_s[...] = jnp.full(m_s.shape, NEG_BIG, jnp.float32)
    acc[...] = jnp.zeros(acc.shape, jnp.float32)

    def qk(c, slot):
        st_ref, mx_ref = slot
        off = pl.multiple_of(c * tk, tk)
        bias = jnp.where(score[c] >= thr[...], 0.0, NEG_BIG)
        for h in range(DSA_HEADS):
            kp = sk_ref[pl.ds(off, tk), (h // 2) * LANES:(h // 2 + 1) * LANES]
            st = _dot_nt(kp, sqm[h]) + bias
            st_ref[h] = st
            mx_ref[h] = jnp.max(st, axis=0, keepdims=True)

    def softmax_pv(c, slot):
        st_ref, mx_ref = slot
        alphas = []
        for h in range(DSA_HEADS):
            m_old = m_s[h]
            m_new = jnp.maximum(m_old, mx_ref[h])
            alphas.append(jnp.exp2(m_old - m_new))
            m_s[h] = m_new
            p_scr[h] = jnp.exp2(st_ref[h] - m_new).astype(jnp.bfloat16)
        ones = jnp.ones((ONES_ROWS, tk), jnp.bfloat16)
        for h in range(DSA_HEADS):
            vt = jnp.concatenate([svt_ref[c, h * DSA_HEAD_DIM:(h + 1) * DSA_HEAD_DIM, :], ones], axis=0)
            acc[h] = alphas[h] * acc[h] + _dot(vt, p_scr[h])

    slot_a, slot_b = (st_a, mx_a), (st_b, mx_b)
    qk(0, slot_a)

    def pair(tt, carry):
        c = 2 * tt
        qk(c + 1, slot_b)
        softmax_pv(c, slot_a)
        qk(c + 2, slot_a)
        softmax_pv(c + 1, slot_b)
        return carry

    lax.fori_loop(0, last // 2, pair, 0)

    @pl.when(last % 2 == 0)
    def _():
        softmax_pv(last, slot_a)

    @pl.when(last % 2 == 1)
    def _():
        qk(last, slot_b)
        softmax_pv(last - 1, slot_a)
        softmax_pv(last, slot_b)

    dh = DSA_HEAD_DIM
    outs = [acc[h, :dh, :] / acc[h, dh:dh + 1, :] for h in range(DSA_HEADS)]
    o_ref[...] = jnp.concatenate(outs, axis=0).T.astype(o_ref.dtype)


def _dsa_attention(p, ikk, iwt, svt, batch, seq, top_k):
    nq = seq // DSA_TQ
    w = DSA_HEADS * DSA_HEAD_DIM
    once = pl.Buffered(1)
    return pl.pallas_call(
        functools.partial(_dsa_kernel, seq=seq, top_k=top_k),
        grid=(batch, nq),
        in_specs=[
            pl.BlockSpec((DSA_TQ, w), lambda b, qi: (b * nq + qi, 3)),
            pl.BlockSpec((DSA_TQ, w), lambda b, qi: (b * nq + qi, 6)),
            pl.BlockSpec((IDX_HEADS, DSA_TQ), lambda b, qi: (0, b * nq + qi)),
            pl.BlockSpec((seq, LANES), lambda b, qi: (b, 0), pipeline_mode=once),
            pl.BlockSpec((seq, w), lambda b, qi: (b, 4), pipeline_mode=once),
            pl.BlockSpec((seq // DSA_TK, w, DSA_TK), lambda b, qi: (b, 0, 0), pipeline_mode=once),
        ],
        out_specs=pl.BlockSpec((DSA_TQ, w), lambda b, qi: (b * nq + qi, 0)),
        out_shape=jax.ShapeDtypeStruct((batch * seq, w), jnp.bfloat16),
        scratch_shapes=[
            pltpu.VMEM((seq // DSA_TK, DSA_TK, DSA_TQ), jnp.float32),
            pltpu.VMEM((seq // DSA_TK, DSA_TK, DSA_TQ), jnp.int16),
            pltpu.VMEM((seq // DSA_TK, DSA_TK, DSA_TQ), jnp.int16),
            pltpu.VMEM((1, DSA_TK, DSA_TQ), jnp.int16),
            pltpu.VMEM((IDX_HEADS, DSA_TQ, LANES), jnp.bfloat16),
            pltpu.VMEM((DSA_HEADS, DSA_TQ, LANES), jnp.bfloat16),
            pltpu.VMEM((DSA_HEADS, DSA_HEAD_DIM + ONES_ROWS, DSA_TQ), jnp.float32),
            pltpu.VMEM((DSA_HEADS, 1, DSA_TQ), jnp.float32),
            pltpu.VMEM((1, DSA_TQ), jnp.float32),
            pltpu.VMEM((DSA_HEADS, DSA_TK, DSA_TQ), jnp.float32),
            pltpu.VMEM((DSA_HEADS, DSA_TK, DSA_TQ), jnp.float32),
            pltpu.VMEM((DSA_HEADS, 1, DSA_TQ), jnp.float32),
            pltpu.VMEM((DSA_HEADS, 1, DSA_TQ), jnp.float32),
            pltpu.VMEM((DSA_HEADS, DSA_TK, DSA_TQ), jnp.bfloat16),
        ],
        compiler_params=pltpu.CompilerParams(
            dimension_semantics=("parallel", "arbitrary"), vmem_limit_bytes=VMEM_LIMIT),
        name="dsa",
    )(p, p, iwt, ikk, p, svt)


def _memkv_kernel(mem_ref, g_ref, w_ref, o_ref):
    n = _rms(mem_ref[...], g_ref[...]).astype(jnp.bfloat16)
    o_ref[...] = _dot(n, w_ref[...]).astype(o_ref.dtype)


def _memkv(mem2, g_mem, w_kv, batch):
    return pl.pallas_call(
        _memkv_kernel,
        grid=(batch,),
        in_specs=[
            pl.BlockSpec((MEM_LEN, D_MODEL), lambda b: (b, 0)),
            pl.BlockSpec((1, D_MODEL), lambda b: (0, 0)),
            pl.BlockSpec(w_kv.shape, lambda b: (0, 0)),
        ],
        out_specs=pl.BlockSpec((MEM_LEN, w_kv.shape[1]), lambda b: (b, 0)),
        out_shape=jax.ShapeDtypeStruct((batch * MEM_LEN, w_kv.shape[1]), jnp.bfloat16),
        compiler_params=pltpu.CompilerParams(
            dimension_semantics=("parallel",), vmem_limit_bytes=VMEM_LIMIT),
        name="memkv",
    )(mem2, g_mem, w_kv)


def _memattn_kernel(q_ref, kv_ref, o_ref):
    scale = XA_HEAD_DIM ** -0.5
    outs = []
    for h in range(XA_HEADS):
        q = q_ref[:, h * LANES:(h + 1) * LANES]
        k = kv_ref[:, h * LANES:(h + 1) * LANES]
        v = kv_ref[:, (XA_HEADS + h) * LANES:(XA_HEADS + h + 1) * LANES]
        s = _dot_nt(q, k) * scale
        m = jnp.max(s, axis=1, keepdims=True)
        p = jnp.exp(s - m)
        l = jnp.sum(p, axis=1, keepdims=True)
        outs.append(_dot((p / l).astype(jnp.bfloat16), v))
    o_ref[...] = jnp.concatenate(outs, axis=1).astype(o_ref.dtype)


def _mem_attention(p, mkv, batch, seq):
    nq = seq // MEM_TQ
    w = XA_HEADS * XA_HEAD_DIM
    return pl.pallas_call(
        _memattn_kernel,
        grid=(batch, nq),
        in_specs=[
            pl.BlockSpec((MEM_TQ, w), lambda b, qi: (b * nq + qi, 7)),
            pl.BlockSpec((MEM_LEN, 2 * w), lambda b, qi: (b, 0)),
        ],
        out_specs=pl.BlockSpec((MEM_TQ, w), lambda b, qi: (b * nq + qi, 0)),
        out_shape=jax.ShapeDtypeStruct((batch * seq, w), jnp.bfloat16),
        compiler_params=pltpu.CompilerParams(
            dimension_semantics=("parallel", "parallel"), vmem_limit_bytes=VMEM_LIMIT),
        name="memattn",
    )(p, mkv)


def _merge_kernel(x_ref, yd_ref, ys_ref, ym_ref, gmix_ref, wg_ref, bg_ref, wbr_ref, wout_ref,
                  gffn_ref, wr_ref, br_ref, h_ref, hn_ref, lg_ref):
    xf = x_ref[...]
    n = _rms(xf, gmix_ref[...]).astype(jnp.bfloat16)
    merged = jnp.zeros(xf.shape, jnp.float32)
    for i, y_ref in enumerate((yd_ref, ys_ref, ym_ref)):
        gate = jax.nn.sigmoid(_dot(n, wg_ref[i]) + bg_ref[i])
        merged = merged + gate * _dot(y_ref[...], wbr_ref[i])
    h = xf + _dot(merged.astype(jnp.bfloat16), wout_ref[...])
    h_ref[...] = h
    hn = _rms(h, gffn_ref[...]).astype(jnp.bfloat16)
    hn_ref[...] = hn
    lg_ref[...] = _dot(hn, wr_ref[...]) + br_ref[...]


def _merge(x2, y_diff, y_dsa, y_mem, g_mix, w_g, b_g, w_br, w_out, g_ffn, w_r, b_r):
    t = x2.shape[0]
    row = lambda w: pl.BlockSpec((MERGE_TM, w), lambda i: (i, 0))
    full = lambda a: pl.BlockSpec(a.shape, lambda i: (0,) * a.ndim, pipeline_mode=pl.Buffered(1))
    return pl.pallas_call(
        _merge_kernel,
        grid=(t // MERGE_TM,),
        in_specs=[row(D_MODEL), row(512), row(512), row(512), full(g_mix), full(w_g), full(b_g),
                  full(w_br), full(w_out), full(g_ffn), full(w_r), full(b_r)],
        out_specs=[row(D_MODEL), row(D_MODEL), row(LANES)],
        out_shape=[
            jax.ShapeDtypeStruct((t, D_MODEL), jnp.float32),
            jax.ShapeDtypeStruct((t, D_MODEL), jnp.bfloat16),
            jax.ShapeDtypeStruct((t, LANES), jnp.float32),
        ],
        compiler_params=pltpu.CompilerParams(
            dimension_semantics=("parallel",), vmem_limit_bytes=VMEM_LIMIT),
        name="merge",
    )(x2, y_diff, y_dsa, y_mem, g_mix, w_g, b_g, w_br, w_out, g_ffn, w_r, b_r)


MOE_ST = 2048
MOE_SUB = 512
MOE_ALIGN = 16
MOE_BLK = 256
MOE_NSUB = MOE_ST // MOE_SUB
MOE_ZROWS = MOE_SUB + N_GROUPS * MOE_ALIGN
MOE_ZK = -(-MOE_ZROWS // LANES) * LANES
MOE_GROWS = MOE_ST + MOE_NSUB * N_GROUPS * MOE_ALIGN + MOE_BLK
_SEG_ROWS, _SEG_LOCAL, _SEG_GLOBAL = 0, 16, 32
_GRP_BASE, _GRP_ROWS = 48, 52


def _route(lg):
    lane = lax.broadcasted_iota(jnp.int32, lg.shape, 1).astype(jnp.float32)
    big = float(LANES)
    gl = jnp.where(lane < N_GROUPS, lg, -jnp.inf)
    gmax = jnp.max(gl, axis=1, keepdims=True)
    grp = jnp.min(jnp.where(gl == gmax, lane, big), axis=1, keepdims=True)
    gsum = jnp.sum(jnp.where(lane < N_GROUPS, jnp.exp(gl - gmax), 0.0), axis=1, keepdims=True)
    p_grp = 1.0 / gsum
    lo = N_GROUPS + grp * EXPERTS_PER_GROUP
    el = jnp.where((lane >= lo) & (lane < lo + EXPERTS_PER_GROUP), lg, -jnp.inf)
    e1 = jnp.max(el, axis=1, keepdims=True)
    i1 = jnp.min(jnp.where(el == e1, lane, big), axis=1, keepdims=True)
    el2 = jnp.where(lane == i1, -jnp.inf, el)
    e2 = jnp.max(el2, axis=1, keepdims=True)
    i2 = jnp.min(jnp.where(el2 == e2, lane, big), axis=1, keepdims=True)
    r = jnp.exp(e2 - e1)
    w1 = p_grp / (1.0 + r)
    w2 = p_grp * r / (1.0 + r)
    return jnp.where(lane == i1, w1, 0.0) + jnp.where(lane == i2, w2, 0.0), grp


def _split3(x):
    a = x.astype(jnp.bfloat16)
    r = x - a.astype(jnp.float32)
    b = r.astype(jnp.bfloat16)
    c = (r - b.astype(jnp.float32)).astype(jnp.bfloat16)
    return a, b, c


def _moe_kernel(hn_ref, lg_ref, h_ref, gfin_ref, win_hbm, wout_hbm, o_ref,
                g_buf, ys, cw_s, dest_s, z_buf, cwz_buf, zy_buf, win_buf, wout_buf, sem, seg):
    j = pl.program_id(1)
    sub, al, bf = MOE_SUB, MOE_ALIGN, jnp.bfloat16
    lane_row = lax.broadcasted_iota(jnp.int32, (1, LANES), 1)

    def copy_rows(src_ref, dst_ref, src, dst, nblk):
        def body(i, carry):
            s0 = pl.multiple_of(src + i * al, al)
            d0 = pl.multiple_of(dst + i * al, al)
            for s_ref, d_ref in zip(src_ref, dst_ref):
                d_ref[pl.ds(d0, al), :] = s_ref[pl.ds(s0, al), :]
            return carry
        lax.fori_loop(0, nblk, body, 0)

    @pl.when(j == 0)
    def _():
        cw, grp = _route(lg_ref[...])
        lane = lax.broadcasted_iota(jnp.int32, (sub, LANES), 1).astype(jnp.float32)
        hots = [lane == grp[u * sub:(u + 1) * sub] for u in range(MOE_NSUB)]
        for u in range(MOE_NSUB):
            n_vec = jnp.sum(jnp.where(hots[u], 1.0, 0.0), axis=0, keepdims=True)
            off = jnp.int32(0)
            for g in range(N_GROUPS):
                n = jnp.sum(jnp.where(lane_row == g, n_vec, 0.0)).astype(jnp.int32)
                rows = ((n + (al - 1)) // al) * al
                seg[_SEG_ROWS + u * N_GROUPS + g] = rows
                seg[_SEG_LOCAL + u * N_GROUPS + g] = off
                off = off + rows
        base = jnp.int32(0)
        for g in range(N_GROUPS):
            seg[_GRP_BASE + g] = base
            pos = base
            for u in range(MOE_NSUB):
                seg[_SEG_GLOBAL + u * N_GROUPS + g] = pos
                pos = pos + seg[_SEG_ROWS + u * N_GROUPS + g]
            seg[_GRP_ROWS + g] = pos - base
            base = pos

        ri = lax.broadcasted_iota(jnp.int32, (sub, sub), 0)
        ci = lax.broadcasted_iota(jnp.int32, (sub, sub), 1)
        earlier = jnp.where(ci < ri, 1.0, 0.0).astype(bf)
        zrow = lax.broadcasted_iota(jnp.int32, (MOE_ZROWS, sub), 0).astype(jnp.float32)
        zy_buf[...] = jnp.zeros(zy_buf.shape, jnp.float32)
        for u in range(MOE_NSUB):
            hot = hots[u]
            before = _dot(earlier, jnp.where(hot, 1.0, 0.0).astype(bf))
            start = jnp.zeros((1, LANES), jnp.float32)
            for g in range(N_GROUPS):
                start = jnp.where(lane_row == g, seg[_SEG_LOCAL + u * N_GROUPS + g].astype(jnp.float32), start)
            dest = jnp.sum(jnp.where(hot, before + start, 0.0), axis=1, keepdims=True)
            dest_b = jnp.broadcast_to(dest, (sub, LANES))
            dest_s[u] = dest_b
            perm = jnp.where(zrow == dest_b.T[0:1, :], 1.0, 0.0).astype(bf)
            z_buf[...] = _dot(perm, hn_ref[u * sub:(u + 1) * sub, :]).astype(bf)
            c3 = _split3(cw[u * sub:(u + 1) * sub])
            cwz_buf[...] = _dot(perm, c3[0]) + _dot(perm, c3[1]) + _dot(perm, c3[2])
            for g in range(N_GROUPS):
                k = u * N_GROUPS + g
                copy_rows((z_buf, cwz_buf), (g_buf, cw_s), seg[_SEG_LOCAL + k], seg[_SEG_GLOBAL + k],
                          seg[_SEG_ROWS + k] // al)

        def weights(e, slot):
            return (pltpu.make_async_copy(win_hbm.at[e], win_buf.at[slot], sem.at[0, slot]),
                    pltpu.make_async_copy(wout_hbm.at[e], wout_buf.at[slot], sem.at[1, slot]))

        for cp in weights(0, 0):
            cp.start()

        def expert(e, carry):
            slot = e % 2
            g = e // EXPERTS_PER_GROUP

            @pl.when(e + 1 < N_EXPERTS)
            def _():
                for cp in weights(e + 1, 1 - slot):
                    cp.start()

            for cp in weights(e, slot):
                cp.wait()
            start = seg[_GRP_BASE + g]
            rows = seg[_GRP_ROWS + g]
            big = 2 * MOE_BLK
            nbig = rows // big
            rem = rows - nbig * big
            first = e % EXPERTS_PER_GROUP == 0

            def block(row0, n):
                row0 = pl.multiple_of(row0, al)
                gu = _dot(g_buf[pl.ds(row0, n), :], win_buf[slot])
                gt = gu[:, :D_EXPERT]
                hid = (gt * jax.nn.sigmoid(gt)) * gu[:, D_EXPERT:]
                y = _dot(hid.astype(bf), wout_buf[slot])
                ln = lax.broadcasted_iota(jnp.int32, (n, LANES), 1)
                cw_e = jnp.sum(jnp.where(ln == N_GROUPS + e, cw_s[pl.ds(row0, n), :], 0.0), axis=1, keepdims=True)

                @pl.when(first)
                def _():
                    ys[pl.ds(row0, n), :] = cw_e * y

                @pl.when(jnp.logical_not(first))
                def _():
                    ys[pl.ds(row0, n), :] += cw_e * y

            def big_block(b, c):
                block(start + b * big, big)
                return c

            lax.fori_loop(0, nbig, big_block, 0)

            tail = start + nbig * big

            @pl.when(rem > MOE_BLK)
            def _():
                block(tail, big)

            @pl.when((rem > MOE_BLK // 2) & (rem <= MOE_BLK))
            def _():
                block(tail, MOE_BLK)

            @pl.when((rem > 0) & (rem <= MOE_BLK // 2))
            def _():
                block(tail, MOE_BLK // 2)

            return carry

        lax.fori_loop(0, N_EXPERTS, expert, 0)

    @pl.when(j > 0)
    def _():
        u = j - 1
        for g in range(N_GROUPS):
            k = u * N_GROUPS + g
            copy_rows((ys,), (zy_buf,), seg[_SEG_GLOBAL + k], seg[_SEG_LOCAL + k], seg[_SEG_ROWS + k] // al)
        zy = zy_buf[...]
        hi = zy.astype(bf)
        lo = (zy - hi.astype(jnp.float32)).astype(bf)
        dest = jnp.concatenate([dest_s[u]] * (MOE_ZK // LANES), axis=1)
        col = lax.broadcasted_iota(jnp.int32, (sub, MOE_ZK), 1).astype(jnp.float32)
        unperm = jnp.where(col == dest, 1.0, 0.0).astype(bf)
        moe = _dot(unperm, hi) + _dot(unperm, lo)
        o_ref[...] = _rms(h_ref[...] + moe, gfin_ref[...])


def _moe(hn, lg, h, w_e_in, w_e_out, g_final):
    t = hn.shape[0]
    row = lambda s, j: (s * MOE_NSUB + jnp.maximum(j - 1, 0), 0)
    return pl.pallas_call(
        _moe_kernel,
        grid=(t // MOE_ST, 1 + MOE_NSUB),
        in_specs=[
            pl.BlockSpec((MOE_ST, D_MODEL), lambda s, j: (s, 0)),
            pl.BlockSpec((MOE_ST, LANES), lambda s, j: (s, 0)),
            pl.BlockSpec((MOE_SUB, D_MODEL), row),
            pl.BlockSpec((1, D_MODEL), lambda s, j: (0, 0)),
            pl.BlockSpec(memory_space=pl.ANY),
            pl.BlockSpec(memory_space=pl.ANY),
        ],
        out_specs=pl.BlockSpec((MOE_SUB, D_MODEL), row),
        out_shape=jax.ShapeDtypeStruct((t, D_MODEL), jnp.float32),
        scratch_shapes=[
            pltpu.VMEM((MOE_GROWS, D_MODEL), jnp.bfloat16),
            pltpu.VMEM((MOE_GROWS, D_MODEL), jnp.float32),
            pltpu.VMEM((MOE_GROWS, LANES), jnp.float32),
            pltpu.VMEM((MOE_NSUB, MOE_SUB, LANES), jnp.float32),
            pltpu.VMEM((MOE_ZROWS, D_MODEL), jnp.bfloat16),
            pltpu.VMEM((MOE_ZROWS, LANES), jnp.float32),
            pltpu.VMEM((MOE_ZK, D_MODEL), jnp.float32),
            pltpu.VMEM((2, D_MODEL, 2 * D_EXPERT), jnp.bfloat16),
            pltpu.VMEM((2, D_EXPERT, D_MODEL), jnp.bfloat16),
            pltpu.SemaphoreType.DMA((2, 2)),
            pltpu.SMEM((64,), jnp.int32),
        ],
        compiler_params=pltpu.CompilerParams(
            dimension_semantics=("parallel", "arbitrary"), vmem_limit_bytes=VMEM_LIMIT),
        name="moe",
    )(hn, lg, h, g_final, w_e_in, w_e_out)


def kernel(x, positions, mem, g_mix, w_in, b_gate, lambda_q1, lambda_k1, lambda_q2, lambda_k2,
           g_diff_sub, g_mem, w_mem_kv, w_br_diff, w_br_dsa, w_br_mem, w_out, g_ffn,
           w_route_group, b_route_group, w_route_expert, b_route_expert, w_exp_in, w_exp_out,
           g_final):
    b, s, d = x.shape
    t = b * s
    bf = jnp.bfloat16
    top_k = min(TOPK_MAX, s // 4)
    assert d == D_MODEL and s % DIFF_T == 0 and s % DSA_TQ == 0 and top_k <= DSA_TQ
    assert PROJ_TM == DIFF_T == DSA_TK and DSA_TK % DSA_TQ == 0, "proj writes V^T in the attention kernels' key tiles"
    assert g_mix.shape[0] == 1, "single layer"
    lam_init = 0.8 - 0.6 * math.exp(-0.3 * 0)

    wi = w_in[0]
    c = 512
    seg = lambda k: wi[:, k * c:(k + 1) * c]
    o_ik = 7 * c
    w_ik = wi[:, o_ik:o_ik + IDX_DIM]
    w_iw = wi[:, o_ik + IDX_DIM:o_ik + IDX_DIM + IDX_HEADS]
    o_mq = o_ik + IDX_DIM + IDX_HEADS
    w_mq = wi[:, o_mq:o_mq + c]
    w_gl = wi[:, o_mq + c:]
    qs = DIFF_QK_DIM ** -0.5
    qs2 = qs * LOG2E
    w_a = jnp.concatenate([seg(0) * qs2, seg(1), seg(2), seg(3) * qs2, seg(4), seg(5), seg(6) * qs, w_mq],
                          axis=1).astype(bf)
    w_s = jnp.concatenate([w_ik, w_ik, w_iw, jnp.zeros((d, LANES - IDX_HEADS), wi.dtype)], axis=1).astype(bf)
    w_g = w_gl.reshape(d, 3, d).transpose(1, 0, 2).astype(bf)
    b_g = b_gate[0].reshape(3, 1, d)
    w_br = jnp.stack([w_br_diff[0], w_br_dsa[0], w_br_mem[0]]).astype(bf)
    w_r = jnp.concatenate([w_route_group[0], w_route_expert[0],
                           jnp.zeros((d, LANES - N_GROUPS - N_EXPERTS), wi.dtype)], axis=1).astype(bf)
    b_r = jnp.concatenate([b_route_group[0], b_route_expert[0],
                           jnp.zeros((LANES - N_GROUPS - N_EXPERTS,), jnp.float32)]).reshape(1, LANES)

    rot = IDX_DIM // ROPE_FRACTION
    inv_freq = ROPE_THETA ** (-jnp.arange(0, rot, 2, dtype=jnp.float32) / rot)
    inv64 = jnp.concatenate([inv_freq, inv_freq, jnp.zeros((IDX_DIM - rot,), jnp.float32)])
    inv_lane = jnp.concatenate([inv64, inv64]).reshape(1, LANES)

    x2 = x.reshape(t, d)
    pos2 = positions.reshape(t, 1)
    p, ikk, dvt, svt, iwt = _proj(x2, pos2, g_mix, inv_lane, w_a, w_s)

    y_diff = _diff_attention(p, dvt, lambda_q1, lambda_k1, lambda_q2, lambda_k2,
                             g_diff_sub.reshape(DIFF_V_DIM, 1), b, s, lam_init)
    y_dsa = _dsa_attention(p, ikk, iwt, svt, b, s, top_k)
    mkv = _memkv(mem.reshape(b * MEM_LEN, d), g_mem, w_mem_kv[0].astype(bf), b)
    y_mem = _mem_attention(p, mkv, b, s)

    h, hn, lg = _merge(x2, y_diff, y_dsa, y_mem, g_mix, w_g, b_g, w_br, w_out[0].astype(bf), g_ffn, w_r, b_r)
    out = _moe(hn, lg, h, w_exp_in[0].astype(bf), w_exp_out[0].astype(bf), g_final.reshape(1, d))
    return out.reshape(b, s, d)
```

```python
import functools
import math

import jax
import jax.numpy as jnp
import numpy as np
from jax import lax
from jax.experimental import pallas as pl
from jax.experimental.pallas import tpu as pltpu

D_MODEL = 1024
MEM_LEN = 256
XA_HEADS = 4
XA_HEAD_DIM = 128
DIFF_HEADS = 4
DIFF_QK_DIM = 64
DIFF_V_DIM = 128
DSA_HEADS = 8
DSA_HEAD_DIM = 64
IDX_HEADS = 8
IDX_DIM = 64
TOPK_MAX = 256
ROPE_THETA = 500000.0
ROPE_FRACTION = 4
N_GROUPS = 4
EXPERTS_PER_GROUP = 4
N_EXPERTS = 16
D_EXPERT = 512
EPS = 1e-6

LANES = 128
SUBLANES = 8
VMEM_LIMIT = 56 * 1024 * 1024
NEG_BIG = -1e30
ONES_ROWS = 16
LOG2E = math.log2(math.e)

PROJ_TM = 512
PROJ_TN = 512
DIFF_T = 512
DSA_TQ = 256
DSA_TK = 512
MEM_TQ = 512
MERGE_TM = 512

_NT = (((1,), (1,)), ((), ()))


def _dot(a, b):
    return jnp.dot(a, b, preferred_element_type=jnp.float32)


def _dot_nt(a, b):
    return lax.dot_general(a, b, _NT, preferred_element_type=jnp.float32)


def _rms(xf, g):
    return xf * lax.rsqrt(jnp.mean(xf * xf, axis=-1, keepdims=True) + EPS) * g


def _pair_lane_source():
    half, rot = IDX_DIM, IDX_DIM // ROPE_FRACTION
    hr = rot // 2
    src = np.zeros(LANES, np.int32)
    src[0:hr] = np.arange(0, hr)
    src[hr:rot] = half + np.arange(0, hr)
    src[rot:half] = np.arange(rot, half)
    src[half:half + hr] = np.arange(hr, rot)
    src[half + hr:half + rot] = half + np.arange(hr, rot)
    src[half + rot:] = half + np.arange(rot, half)
    return src


def _first_head_lanes(shape):
    half, rot = IDX_DIM, IDX_DIM // ROPE_FRACTION
    lane = lax.broadcasted_iota(jnp.int32, shape, len(shape) - 1) % LANES
    return (lane < rot // 2) | ((lane >= rot) & (lane < half + rot // 2))


def _rope_expanders():
    rot = IDX_DIM // ROPE_FRACTION
    e = np.zeros((2, LANES, LANES), np.float32)
    for l in list(range(rot)) + list(range(IDX_DIM, IDX_DIM + rot)):
        e[0, l % (rot // 2), l] = 1.0
        e[1, l % (rot // 2), l] = -1.0 if l < rot else 1.0
    return e


def _proj_kernel(x_ref, posr_ref, g_ref, inv_ref, e_ref, wa_ref, ws_ref,
                 p_ref, ikk_ref, dvt_ref, svt_ref, iwt_ref, n_scr, cos_scr, sin_scr):
    j = pl.program_id(1)

    def rope(v):
        k = v.shape[1] // LANES
        blocks = [pltpu.roll(v[:, b * LANES:(b + 1) * LANES], LANES // 2, 1) for b in range(k)]
        partner = jnp.concatenate(blocks, axis=1) if k > 1 else blocks[0]
        c = jnp.concatenate([cos_scr[...]] * k, axis=1) if k > 1 else cos_scr[...]
        s = jnp.concatenate([sin_scr[...]] * k, axis=1) if k > 1 else sin_scr[...]
        return v * c + partner * s

    @pl.when(j == 0)
    def _():
        xf = x_ref[...]
        n_scr[...] = _rms(xf, g_ref[...]).astype(jnp.bfloat16)
        tm = xf.shape[0]
        ang = posr_ref[0].astype(jnp.float32) * inv_ref[...]
        pad = jnp.zeros((LANES - ang.shape[0], tm), jnp.float32)
        ct = jnp.concatenate([jnp.cos(ang), pad], axis=0).T
        st = jnp.concatenate([jnp.sin(ang), pad], axis=0).T
        crot = sum(_dot(part, e_ref[0]) for part in _split3(ct))
        lane = lax.broadcasted_iota(jnp.int32, crot.shape, 1) % IDX_DIM
        cos_scr[...] = jnp.where(lane < IDX_DIM // ROPE_FRACTION, crot, 1.0)
        sin_scr[...] = sum(_dot(part, e_ref[1]) for part in _split3(st))
        small = _dot(n_scr[...], ws_ref[...])
        ikk_ref[...] = rope(small[:, :LANES]).astype(jnp.bfloat16)
        w = small[:, LANES:] * (IDX_HEADS ** -0.5)
        iwt_ref[...] = w.T[:IDX_HEADS, :]

    acc = _dot(n_scr[...], wa_ref[...])
    is_rope = (j == 0) | (j == 1) | (j == 3) | (j == 4) | (j == 6)

    @pl.when(is_rope)
    def _():
        p_ref[...] = rope(acc).astype(jnp.bfloat16)

    @pl.when(jnp.logical_not(is_rope))
    def _():
        p_ref[...] = acc.astype(jnp.bfloat16)

    @pl.when(j == 2)
    def _():
        dvt_ref[0] = acc.T.astype(jnp.bfloat16)

    @pl.when(j == 5)
    def _():
        svt_ref[0] = acc.T.astype(jnp.bfloat16)


def _proj(x2, pos_rows, g_mix, inv_col, expand, w_a, w_s):
    t = x2.shape[0]
    ncol = w_a.shape[1] // PROJ_TN
    return pl.pallas_call(
        _proj_kernel,
        grid=(t // PROJ_TM, ncol),
        in_specs=[
            pl.BlockSpec((PROJ_TM, D_MODEL), lambda i, j: (i, 0)),
            pl.BlockSpec((1, 1, PROJ_TM), lambda i, j: (i, 0, 0)),
            pl.BlockSpec((1, D_MODEL), lambda i, j: (0, 0)),
            pl.BlockSpec(inv_col.shape, lambda i, j: (0, 0)),
            pl.BlockSpec(expand.shape, lambda i, j: (0, 0, 0)),
            pl.BlockSpec((D_MODEL, PROJ_TN), lambda i, j: (0, j)),
            pl.BlockSpec((D_MODEL, 2 * LANES), lambda i, j: (0, 0)),
        ],
        out_specs=[
            pl.BlockSpec((PROJ_TM, PROJ_TN), lambda i, j: (i, j)),
            pl.BlockSpec((PROJ_TM, LANES), lambda i, j: (i, 0)),
            pl.BlockSpec((1, PROJ_TN, PROJ_TM), lambda i, j: (i, 0, 0)),
            pl.BlockSpec((1, PROJ_TN, PROJ_TM), lambda i, j: (i, 0, 0)),
            pl.BlockSpec((IDX_HEADS, PROJ_TM), lambda i, j: (0, i)),
        ],
        out_shape=[
            jax.ShapeDtypeStruct((t, w_a.shape[1]), jnp.bfloat16),
            jax.ShapeDtypeStruct((t, LANES), jnp.bfloat16),
            jax.ShapeDtypeStruct((t // PROJ_TM, PROJ_TN, PROJ_TM), jnp.bfloat16),
            jax.ShapeDtypeStruct((t // PROJ_TM, PROJ_TN, PROJ_TM), jnp.bfloat16),
            jax.ShapeDtypeStruct((IDX_HEADS, t), jnp.float32),
        ],
        scratch_shapes=[
            pltpu.VMEM((PROJ_TM, D_MODEL), jnp.bfloat16),
            pltpu.VMEM((PROJ_TM, LANES), jnp.float32),
            pltpu.VMEM((PROJ_TM, LANES), jnp.float32),
        ],
        compiler_params=pltpu.CompilerParams(
            dimension_semantics=("parallel", "arbitrary"), vmem_limit_bytes=VMEM_LIMIT),
        name="proj",
    )(x2, pos_rows, g_mix, inv_col, expand, w_a, w_s)


def _diff_kernel(q_ref, k_ref, vt_ref, lq1_ref, lk1_ref, lq2_ref, lk2_ref, gs_ref, o_ref,
                 qm, m_s, acc, st_a, st_b, p_scr, *, lam_init):
    qi = pl.program_id(2)
    t = DIFF_T
    q = q_ref[...]
    first = _first_head_lanes(q.shape)
    qm[0] = jnp.where(first, q, jnp.zeros_like(q))
    qm[1] = jnp.where(first, jnp.zeros_like(q), q)
    m_s[...] = jnp.full(m_s.shape, NEG_BIG, jnp.float32)
    acc[...] = jnp.zeros(acc.shape, jnp.float32)

    def qk(j, st_ref):
        k = k_ref[pl.ds(pl.multiple_of(j * t, t), t), :]
        for i in range(2):
            st_ref[i] = _dot_nt(k, qm[i])

    def softmax_pv(j, st_ref, diagonal):
        vt = jnp.concatenate([vt_ref[j], jnp.ones((ONES_ROWS, t), jnp.bfloat16)], axis=0)
        if diagonal:
            krow = lax.broadcasted_iota(jnp.int32, (t, t), 0)
            qcol = lax.broadcasted_iota(jnp.int32, (t, t), 1)
            keep = krow <= qcol
        alphas = []
        for i in range(2):
            st = st_ref[i]
            if diagonal:
                st = jnp.where(keep, st, NEG_BIG)
            m_old = m_s[i]
            m_new = jnp.maximum(m_old, jnp.max(st, axis=0, keepdims=True))
            alphas.append(jnp.exp2(m_old - m_new))
            m_s[i] = m_new
            p_scr[i] = jnp.exp2(st - m_new).astype(jnp.bfloat16)
        for i in range(2):
            acc[i] = alphas[i] * acc[i] + _dot(vt, p_scr[i])

    qk(0, st_a)

    def pair(tt, carry):
        j = 2 * tt
        qk(j + 1, st_b)
        softmax_pv(j, st_a, False)
        qk(j + 2, st_a)
        softmax_pv(j + 1, st_b, False)
        return carry

    lax.fori_loop(0, qi // 2, pair, 0)

    @pl.when(qi % 2 == 0)
    def _():
        softmax_pv(qi, st_a, True)

    @pl.when(qi % 2 == 1)
    def _():
        qk(qi, st_b)
        softmax_pv(qi - 1, st_a, False)
        softmax_pv(qi, st_b, True)

    lam = (jnp.exp(jnp.sum(lq1_ref[...] * lk1_ref[...], axis=1, keepdims=True))
           - jnp.exp(jnp.sum(lq2_ref[...] * lk2_ref[...], axis=1, keepdims=True))
           + lam_init)
    dv = DIFF_V_DIM
    ot = (acc[0, :dv, :] / acc[0, dv:dv + 1, :]
          - lam * (acc[1, :dv, :] / acc[1, dv:dv + 1, :]))
    yt = ot * lax.rsqrt(jnp.mean(ot * ot, axis=0, keepdims=True) + EPS) * gs_ref[...]
    o_ref[...] = (yt * (1.0 - lam_init)).T.astype(o_ref.dtype)


def _diff_attention(p, dvt, lq1, lk1, lq2, lk2, g_sub_col, batch, seq, lam_init):
    nb = seq // DIFF_T
    vec = pl.BlockSpec((1, DIFF_QK_DIM), lambda b, h, qi: (0, 0))
    return pl.pallas_call(
        functools.partial(_diff_kernel, lam_init=lam_init),
        grid=(batch, DIFF_HEADS, nb),
        in_specs=[
            pl.BlockSpec((DIFF_T, LANES), lambda b, h, qi: (b * nb + qi, h)),
            pl.BlockSpec((seq, LANES), lambda b, h, qi: (b, DIFF_HEADS + h)),
            pl.BlockSpec((nb, DIFF_V_DIM, DIFF_T), lambda b, h, qi: (b, h, 0)),
            vec, vec, vec, vec,
            pl.BlockSpec((DIFF_V_DIM, 1), lambda b, h, qi: (0, 0)),
        ],
        out_specs=pl.BlockSpec((DIFF_T, LANES), lambda b, h, qi: (b * nb + qi, h)),
        out_shape=jax.ShapeDtypeStruct((batch * seq, DIFF_HEADS * DIFF_V_DIM), jnp.bfloat16),
        scratch_shapes=[
            pltpu.VMEM((2, DIFF_T, LANES), jnp.bfloat16),
            pltpu.VMEM((2, 1, DIFF_T), jnp.float32),
            pltpu.VMEM((2, DIFF_V_DIM + ONES_ROWS, DIFF_T), jnp.float32),
            pltpu.VMEM((2, DIFF_T, DIFF_T), jnp.float32),
            pltpu.VMEM((2, DIFF_T, DIFF_T), jnp.float32),
            pltpu.VMEM((2, DIFF_T, DIFF_T), jnp.bfloat16),
        ],
        compiler_params=pltpu.CompilerParams(
            dimension_semantics=("parallel", "parallel", "arbitrary"),
            vmem_limit_bytes=VMEM_LIMIT),
        name="diffattn",
    )(p, p, dvt, lq1, lk1, lq2, lk2, g_sub_col)


def _key_to_float(key):
    bits = jnp.where(key >= 0, key, key ^ jnp.int32(0x7FFFFFFF))
    return lax.bitcast_convert_type(bits, jnp.float32)


def _count_rows(hit):
    tk, tq = hit.shape
    return jnp.sum(hit.reshape(tk // (4 * SUBLANES), 4 * SUBLANES, tq), axis=0)


def _dsa_kernel(sq_ref, iq_ref, iwt_ref, ikk_ref, sk_ref, svt_ref, o_ref,
                score, hi16, lo16, iqm, sqm, acc, m_s, thr, st_a, st_b, mx_a, mx_b, p_scr, *, seq, top_k):
    qi = pl.program_id(1)
    tq, tk = DSA_TQ, DSA_TK
    last = (qi * tq) // tk
    nkc = last + 1
    first = _first_head_lanes((tq, LANES))

    for h in range(DSA_HEADS):
        pr = h // 2
        iqp = iq_ref[:, pr * LANES:(pr + 1) * LANES]
        sqp = sq_ref[:, pr * LANES:(pr + 1) * LANES]
        mine = first if h % 2 == 0 else jnp.logical_not(first)
        iqm[h] = jnp.where(mine, iqp, jnp.zeros_like(iqp))
        sqm[h] = jnp.where(mine, sqp, jnp.zeros_like(sqp))

    krow = lax.broadcasted_iota(jnp.int32, (tk, tq), 0)
    qcol = lax.broadcasted_iota(jnp.int32, (tk, tq), 1)

    def idx_logits(c, lg_ref):
        kk = ikk_ref[pl.ds(pl.multiple_of(c * tk, tk), tk), :]
        for h in range(IDX_HEADS):
            lg_ref[h] = _dot_nt(kk, iqm[h])

    def idx_score(c, lg_ref):
        sc = jnp.zeros((tk, tq), jnp.float32)
        for h in range(IDX_HEADS):
            sc = sc + iwt_ref[h:h + 1, :] * jnp.maximum(lg_ref[h], 0.0)
        sc = jnp.where(krow <= qcol + (qi * tq - c * tk), sc, -jnp.inf)
        score[c] = sc
        bits = lax.bitcast_convert_type(sc, jnp.int32)
        okey = jnp.where(bits >= 0, bits, bits ^ jnp.int32(0x7FFFFFFF))
        hi16[c] = lax.shift_right_arithmetic(okey, 16).astype(jnp.int16)
        lo16[c] = ((okey & 0xFFFF) - 2 ** 15).astype(jnp.int16)

    idx_logits(0, st_a)

    def idx_pair(tt, carry):
        c = 2 * tt
        idx_logits(c + 1, st_b)
        idx_score(c, st_a)
        idx_logits(c + 2, st_a)
        idx_score(c + 1, st_b)
        return carry

    lax.fori_loop(0, last // 2, idx_pair, 0)

    @pl.when(last % 2 == 0)
    def _():
        idx_score(last, st_a)

    @pl.when(last % 2 == 1)
    def _():
        idx_logits(last, st_b)
        idx_score(last - 1, st_a)
        idx_score(last, st_b)

    zero_cnt = jnp.zeros((4 * SUBLANES, tq), jnp.float32)
    i16_min = -2 ** 15

    def count16(buf, pred):
        def body(c, cnt):
            hit = jnp.where(pred(buf[c]), jnp.int16(1), jnp.int16(0))
            h3 = hit.reshape(tk // (4 * SUBLANES), 4 * SUBLANES, tq)
            part = h3[0]
            for r in range(1, h3.shape[0]):
                part = part + h3[r]
            return cnt + part
        cnt = lax.fori_loop(0, nkc, body, jnp.zeros((4 * SUBLANES, tq), jnp.int16))
        return jnp.sum(cnt.astype(jnp.int32), axis=0, keepdims=True)

    def bisect16(buf, want):
        def bit_step(i, cur):
            cand = cur + lax.shift_left(jnp.int32(1), jnp.int32(15) - i)
            c16 = cand.astype(jnp.int16)
            return jnp.where(count16(buf, lambda blk: blk >= c16) >= want, cand, cur)
        return lax.fori_loop(0, 16, bit_step, jnp.full((1, tq), i16_min, jnp.int32))

    t_hi = bisect16(hi16, top_k)
    t_hi16 = t_hi.astype(jnp.int16)
    rest = top_k - count16(hi16, lambda blk: blk > t_hi16)

    def bucket_only(c, carry):
        lo16[c] = jnp.where(hi16[c] == t_hi16, lo16[c], jnp.int16(i16_min))
        return carry

    lax.fori_loop(0, nkc, bucket_only, 0)
    t_lo = bisect16(lo16, rest)
    key = lax.shift_left(t_hi, 16) + (t_lo - i16_min)
    t_f = _key_to_float(key)

    def count_gt_ge(c, carry):
        gt, ge = carry
        blk = score[c]
        return (gt + _count_rows(jnp.where(blk > t_f, 1.0, 0.0)),
                ge + _count_rows(jnp.where(blk >= t_f, 1.0, 0.0)))

    gt, ge = lax.fori_loop(0, nkc, count_gt_ge, (zero_cnt, zero_cnt))
    n_gt = jnp.sum(gt, axis=0, keepdims=True)
    n_ge = jnp.sum(ge, axis=0, keepdims=True)
    q_pos = qi * tq + lax.broadcasted_iota(jnp.int32, (1, tq), 1)
    few = q_pos < top_k - 1
    thr[...] = jnp.where(few, float(jnp.finfo(jnp.float32).min), t_f)
    need = float(top_k) - n_gt
    split = jnp.logical_and(jnp.logical_not(few), n_ge > float(top_k))

    @pl.when(jnp.max(jnp.where(split, 1.0, 0.0)) > 0.0)
    def _():
        def count_eq_below(jc):
            def body(c, cnt):
                hit = jnp.where((score[c] == t_f) & (c * tk + krow < jc), 1.0, 0.0)
                return cnt + _count_rows(hit)
            return jnp.sum(lax.fori_loop(0, nkc, body, zero_cnt), axis=0, keepdims=True)

        nbits = (seq - 1).bit_length()

        def jbit(i, jv):
            cand = jv + lax.shift_left(jnp.int32(1), jnp.int32(nbits - 1) - i)
            return jnp.where(count_eq_below(cand) < need, cand, jv)

        jv = lax.fori_loop(0, nbits, jbit, jnp.zeros((1, tq), jnp.int32))

        def drop_ties(c, carry):
            blk = score[c]
            score[c] = jnp.where(split & (blk == t_f) & (c * tk + krow > jv), -jnp.inf, blk)
            return carry

        lax.fori_loop(0, nkc, drop_ties, 0)

    m_s[...] = jnp.full(m_s.shape, NEG_BIG, jnp.float32)
    acc[...] = jnp.zeros(acc.shape, jnp.float32)

    def qk(c, slot):
        st_ref, mx_ref = slot
        off = pl.multiple_of(c * tk, tk)
        bias = jnp.where(score[c] >= thr[...], 0.0, NEG_BIG)
        for h in range(DSA_HEADS):
            kp = sk_ref[pl.ds(off, tk), (h // 2) * LANES:(h // 2 + 1) * LANES]
            st = _dot_nt(kp, sqm[h]) + bias
            st_ref[h] = st
            mx_ref[h] = jnp.max(st, axis=0, keepdims=True)

    def softmax_pv(c, slot):
        st_ref, mx_ref = slot
        alphas = []
        for h in range(DSA_HEADS):
            m_old = m_s[h]
            m_new = jnp.maximum(m_old, mx_ref[h])
            alphas.append(jnp.exp2(m_old - m_new))
            m_s[h] = m_new
            p_scr[h] = jnp.exp2(st_ref[h] - m_new).astype(jnp.bfloat16)
        ones = jnp.ones((ONES_ROWS, tk), jnp.bfloat16)
        for h in range(DSA_HEADS):
            vt = jnp.concatenate([svt_ref[c, h * DSA_HEAD_DIM:(h + 1) * DSA_HEAD_DIM, :], ones], axis=0)
            acc[h] = alphas[h] * acc[h] + _dot(vt, p_scr[h])

    slot_a, slot_b = (st_a, mx_a), (st_b, mx_b)
    qk(0, slot_a)

    def pair(tt, carry):
        c = 2 * tt
        qk(c + 1, slot_b)
        softmax_pv(c, slot_a)
        qk(c + 2, slot_a)
        softmax_pv(c + 1, slot_b)
        return carry

    lax.fori_loop(0, last // 2, pair, 0)

    @pl.when(last % 2 == 0)
    def _():
        softmax_pv(last, slot_a)

    @pl.when(last % 2 == 1)
    def _():
        qk(last, slot_b)
        softmax_pv(last - 1, slot_a)
        softmax_pv(last, slot_b)

    dh = DSA_HEAD_DIM
    outs = [acc[h, :dh, :] / acc[h, dh:dh + 1, :] for h in range(DSA_HEADS)]
    o_ref[...] = jnp.concatenate(outs, axis=0).T.astype(o_ref.dtype)


def _dsa_attention(p, ikk, iwt, svt, batch, seq, top_k):
    nq = seq // DSA_TQ
    w = DSA_HEADS * DSA_HEAD_DIM
    once = pl.Buffered(1)
    return pl.pallas_call(
        functools.partial(_dsa_kernel, seq=seq, top_k=top_k),
        grid=(batch, nq),
        in_specs=[
            pl.BlockSpec((DSA_TQ, w), lambda b, qi: (b * nq + qi, 3)),
            pl.BlockSpec((DSA_TQ, w), lambda b, qi: (b * nq + qi, 6)),
            pl.BlockSpec((IDX_HEADS, DSA_TQ), lambda b, qi: (0, b * nq + qi)),
            pl.BlockSpec((seq, LANES), lambda b, qi: (b, 0), pipeline_mode=once),
            pl.BlockSpec((seq, w), lambda b, qi: (b, 4), pipeline_mode=once),
            pl.BlockSpec((seq // DSA_TK, w, DSA_TK), lambda b, qi: (b, 0, 0), pipeline_mode=once),
        ],
        out_specs=pl.BlockSpec((DSA_TQ, w), lambda b, qi: (b * nq + qi, 0)),
        out_shape=jax.ShapeDtypeStruct((batch * seq, w), jnp.bfloat16),
        scratch_shapes=[
            pltpu.VMEM((seq // DSA_TK, DSA_TK, DSA_TQ), jnp.float32),
            pltpu.VMEM((seq // DSA_TK, DSA_TK, DSA_TQ), jnp.int16),
            pltpu.VMEM((seq // DSA_TK, DSA_TK, DSA_TQ), jnp.int16),
            pltpu.VMEM((IDX_HEADS, DSA_TQ, LANES), jnp.bfloat16),
            pltpu.VMEM((DSA_HEADS, DSA_TQ, LANES), jnp.bfloat16),
            pltpu.VMEM((DSA_HEADS, DSA_HEAD_DIM + ONES_ROWS, DSA_TQ), jnp.float32),
            pltpu.VMEM((DSA_HEADS, 1, DSA_TQ), jnp.float32),
            pltpu.VMEM((1, DSA_TQ), jnp.float32),
            pltpu.VMEM((DSA_HEADS, DSA_TK, DSA_TQ), jnp.float32),
            pltpu.VMEM((DSA_HEADS, DSA_TK, DSA_TQ), jnp.float32),
            pltpu.VMEM((DSA_HEADS, 1, DSA_TQ), jnp.float32),
            pltpu.VMEM((DSA_HEADS, 1, DSA_TQ), jnp.float32),
            pltpu.VMEM((DSA_HEADS, DSA_TK, DSA_TQ), jnp.bfloat16),
        ],
        compiler_params=pltpu.CompilerParams(
            dimension_semantics=("parallel", "arbitrary"), vmem_limit_bytes=VMEM_LIMIT),
        name="dsa",
    )(p, p, iwt, ikk, p, svt)


def _memkv_kernel(mem_ref, g_ref, w_ref, o_ref):
    n = _rms(mem_ref[...], g_ref[...]).astype(jnp.bfloat16)
    o_ref[...] = _dot(n, w_ref[...]).astype(o_ref.dtype)


def _memkv(mem2, g_mem, w_kv, batch):
    return pl.pallas_call(
        _memkv_kernel,
        grid=(batch,),
        in_specs=[
            pl.BlockSpec((MEM_LEN, D_MODEL), lambda b: (b, 0)),
            pl.BlockSpec((1, D_MODEL), lambda b: (0, 0)),
            pl.BlockSpec(w_kv.shape, lambda b: (0, 0)),
        ],
        out_specs=pl.BlockSpec((MEM_LEN, w_kv.shape[1]), lambda b: (b, 0)),
        out_shape=jax.ShapeDtypeStruct((batch * MEM_LEN, w_kv.shape[1]), jnp.bfloat16),
        compiler_params=pltpu.CompilerParams(
            dimension_semantics=("parallel",), vmem_limit_bytes=VMEM_LIMIT),
        name="memkv",
    )(mem2, g_mem, w_kv)


def _memattn_kernel(q_ref, kv_ref, o_ref):
    scale = XA_HEAD_DIM ** -0.5
    outs = []
    for h in range(XA_HEADS):
        q = q_ref[:, h * LANES:(h + 1) * LANES]
        k = kv_ref[:, h * LANES:(h + 1) * LANES]
        v = kv_ref[:, (XA_HEADS + h) * LANES:(XA_HEADS + h + 1) * LANES]
        s = _dot_nt(q, k) * scale
        m = jnp.max(s, axis=1, keepdims=True)
        p = jnp.exp(s - m)
        l = jnp.sum(p, axis=1, keepdims=True)
        outs.append(_dot((p / l).astype(jnp.bfloat16), v))
    o_ref[...] = jnp.concatenate(outs, axis=1).astype(o_ref.dtype)


def _mem_attention(p, mkv, batch, seq):
    nq = seq // MEM_TQ
    w = XA_HEADS * XA_HEAD_DIM
    return pl.pallas_call(
        _memattn_kernel,
        grid=(batch, nq),
        in_specs=[
            pl.BlockSpec((MEM_TQ, w), lambda b, qi: (b * nq + qi, 7)),
            pl.BlockSpec((MEM_LEN, 2 * w), lambda b, qi: (b, 0)),
        ],
        out_specs=pl.BlockSpec((MEM_TQ, w), lambda b, qi: (b * nq + qi, 0)),
        out_shape=jax.ShapeDtypeStruct((batch * seq, w), jnp.bfloat16),
        compiler_params=pltpu.CompilerParams(
            dimension_semantics=("parallel", "parallel"), vmem_limit_bytes=VMEM_LIMIT),
        name="memattn",
    )(p, mkv)


def _merge_kernel(x_ref, yd_ref, ys_ref, ym_ref, gmix_ref, wg_ref, bg_ref, wbr_ref, wout_ref,
                  gffn_ref, wr_ref, br_ref, h_ref, hn_ref, lg_ref):
    xf = x_ref[...]
    n = _rms(xf, gmix_ref[...]).astype(jnp.bfloat16)
    merged = jnp.zeros(xf.shape, jnp.float32)
    for i, y_ref in enumerate((yd_ref, ys_ref, ym_ref)):
        gate = jax.nn.sigmoid(_dot(n, wg_ref[i]) + bg_ref[i])
        merged = merged + gate * _dot(y_ref[...], wbr_ref[i])
    h = xf + _dot(merged.astype(jnp.bfloat16), wout_ref[...])
    h_ref[...] = h
    hn = _rms(h, gffn_ref[...]).astype(jnp.bfloat16)
    hn_ref[...] = hn
    lg_ref[...] = _dot(hn, wr_ref[...]) + br_ref[...]


def _merge(x2, y_diff, y_dsa, y_mem, g_mix, w_g, b_g, w_br, w_out, g_ffn, w_r, b_r):
    t = x2.shape[0]
    row = lambda w: pl.BlockSpec((MERGE_TM, w), lambda i: (i, 0))
    full = lambda a: pl.BlockSpec(a.shape, lambda i: (0,) * a.ndim, pipeline_mode=pl.Buffered(1))
    return pl.pallas_call(
        _merge_kernel,
        grid=(t // MERGE_TM,),
        in_specs=[row(D_MODEL), row(512), row(512), row(512), full(g_mix), full(w_g), full(b_g),
                  full(w_br), full(w_out), full(g_ffn), full(w_r), full(b_r)],
        out_specs=[row(D_MODEL), row(D_MODEL), row(LANES)],
        out_shape=[
            jax.ShapeDtypeStruct((t, D_MODEL), jnp.float32),
            jax.ShapeDtypeStruct((t, D_MODEL), jnp.bfloat16),
            jax.ShapeDtypeStruct((t, LANES), jnp.float32),
        ],
        compiler_params=pltpu.CompilerParams(
            dimension_semantics=("parallel",), vmem_limit_bytes=VMEM_LIMIT),
        name="merge",
    )(x2, y_diff, y_dsa, y_mem, g_mix, w_g, b_g, w_br, w_out, g_ffn, w_r, b_r)


MOE_ST = 2048
MOE_SUB = 512
MOE_ALIGN = 16
MOE_BLK = 256
MOE_NSUB = MOE_ST // MOE_SUB
MOE_ZROWS = MOE_SUB + N_GROUPS * MOE_ALIGN
MOE_ZK = -(-MOE_ZROWS // LANES) * LANES
MOE_GROWS = MOE_ST + MOE_NSUB * N_GROUPS * MOE_ALIGN + MOE_BLK
_SEG_ROWS, _SEG_LOCAL, _SEG_GLOBAL = 0, 16, 32
_GRP_BASE, _GRP_ROWS = 48, 52


def _route(lg):
    lane = lax.broadcasted_iota(jnp.int32, lg.shape, 1).astype(jnp.float32)
    big = float(LANES)
    gl = jnp.where(lane < N_GROUPS, lg, -jnp.inf)
    gmax = jnp.max(gl, axis=1, keepdims=True)
    grp = jnp.min(jnp.where(gl == gmax, lane, big), axis=1, keepdims=True)
    gsum = jnp.sum(jnp.where(lane < N_GROUPS, jnp.exp(gl - gmax), 0.0), axis=1, keepdims=True)
    p_grp = 1.0 / gsum
    lo = N_GROUPS + grp * EXPERTS_PER_GROUP
    el = jnp.where((lane >= lo) & (lane < lo + EXPERTS_PER_GROUP), lg, -jnp.inf)
    e1 = jnp.max(el, axis=1, keepdims=True)
    i1 = jnp.min(jnp.where(el == e1, lane, big), axis=1, keepdims=True)
    el2 = jnp.where(lane == i1, -jnp.inf, el)
    e2 = jnp.max(el2, axis=1, keepdims=True)
    i2 = jnp.min(jnp.where(el2 == e2, lane, big), axis=1, keepdims=True)
    r = jnp.exp(e2 - e1)
    w1 = p_grp / (1.0 + r)
    w2 = p_grp * r / (1.0 + r)
    return jnp.where(lane == i1, w1, 0.0) + jnp.where(lane == i2, w2, 0.0), grp


def _split3(x):
    a = x.astype(jnp.bfloat16)
    r = x - a.astype(jnp.float32)
    b = r.astype(jnp.bfloat16)
    c = (r - b.astype(jnp.float32)).astype(jnp.bfloat16)
    return a, b, c


def _moe_kernel(hn_ref, lg_ref, h_ref, gfin_ref, win_hbm, wout_hbm, o_ref,
                g_buf, ys, cw_s, dest_s, z_buf, cwz_buf, zy_buf, win_buf, wout_buf, sem, seg):
    j = pl.program_id(1)
    sub, al, bf = MOE_SUB, MOE_ALIGN, jnp.bfloat16
    lane_row = lax.broadcasted_iota(jnp.int32, (1, LANES), 1)

    def copy_rows(src_ref, dst_ref, src, dst, nblk):
        def body(i, carry):
            s0 = pl.multiple_of(src + i * al, al)
            d0 = pl.multiple_of(dst + i * al, al)
            for s_ref, d_ref in zip(src_ref, dst_ref):
                d_ref[pl.ds(d0, al), :] = s_ref[pl.ds(s0, al), :]
            return carry
        lax.fori_loop(0, nblk, body, 0)

    @pl.when(j == 0)
    def _():
        cw, grp = _route(lg_ref[...])
        lane = lax.broadcasted_iota(jnp.int32, (sub, LANES), 1).astype(jnp.float32)
        hots = [lane == grp[u * sub:(u + 1) * sub] for u in range(MOE_NSUB)]
        for u in range(MOE_NSUB):
            n_vec = jnp.sum(jnp.where(hots[u], 1.0, 0.0), axis=0, keepdims=True)
            off = jnp.int32(0)
            for g in range(N_GROUPS):
                n = jnp.sum(jnp.where(lane_row == g, n_vec, 0.0)).astype(jnp.int32)
                rows = ((n + (al - 1)) // al) * al
                seg[_SEG_ROWS + u * N_GROUPS + g] = rows
                seg[_SEG_LOCAL + u * N_GROUPS + g] = off
                off = off + rows
        base = jnp.int32(0)
        for g in range(N_GROUPS):
            seg[_GRP_BASE + g] = base
            pos = base
            for u in range(MOE_NSUB):
                seg[_SEG_GLOBAL + u * N_GROUPS + g] = pos
                pos = pos + seg[_SEG_ROWS + u * N_GROUPS + g]
            seg[_GRP_ROWS + g] = pos - base
            base = pos

        ri = lax.broadcasted_iota(jnp.int32, (sub, sub), 0)
        ci = lax.broadcasted_iota(jnp.int32, (sub, sub), 1)
        earlier = jnp.where(ci < ri, 1.0, 0.0).astype(bf)
        zrow = lax.broadcasted_iota(jnp.int32, (MOE_ZROWS, sub), 0).astype(jnp.float32)
        zy_buf[...] = jnp.zeros(zy_buf.shape, jnp.float32)
        for u in range(MOE_NSUB):
            hot = hots[u]
            before = _dot(earlier, jnp.where(hot, 1.0, 0.0).astype(bf))
            start = jnp.zeros((1, LANES), jnp.float32)
            for g in range(N_GROUPS):
                start = jnp.where(lane_row == g, seg[_SEG_LOCAL + u * N_GROUPS + g].astype(jnp.float32), start)
            dest = jnp.sum(jnp.where(hot, before + start, 0.0), axis=1, keepdims=True)
            dest_b = jnp.broadcast_to(dest, (sub, LANES))
            dest_s[u] = dest_b
            perm = jnp.where(zrow == dest_b.T[0:1, :], 1.0, 0.0).astype(bf)
            z_buf[...] = _dot(perm, hn_ref[u * sub:(u + 1) * sub, :]).astype(bf)
            c3 = _split3(cw[u * sub:(u + 1) * sub])
            cwz_buf[...] = _dot(perm, c3[0]) + _dot(perm, c3[1]) + _dot(perm, c3[2])
            for g in range(N_GROUPS):
                k = u * N_GROUPS + g
                copy_rows((z_buf, cwz_buf), (g_buf, cw_s), seg[_SEG_LOCAL + k], seg[_SEG_GLOBAL + k],
                          seg[_SEG_ROWS + k] // al)

        def weights(e, slot):
            return (pltpu.make_async_copy(win_hbm.at[e], win_buf.at[slot], sem.at[0, slot]),
                    pltpu.make_async_copy(wout_hbm.at[e], wout_buf.at[slot], sem.at[1, slot]))

        for cp in weights(0, 0):
            cp.start()

        def expert(e, carry):
            slot = e % 2
            g = e // EXPERTS_PER_GROUP

            @pl.when(e + 1 < N_EXPERTS)
            def _():
                for cp in weights(e + 1, 1 - slot):
                    cp.start()

            for cp in weights(e, slot):
                cp.wait()
            start = seg[_GRP_BASE + g]
            rows = seg[_GRP_ROWS + g]
            big = 2 * MOE_BLK
            nbig = rows // big
            rem = rows - nbig * big
            first = e % EXPERTS_PER_GROUP == 0

            def block(row0, n):
                row0 = pl.multiple_of(row0, al)
                gu = _dot(g_buf[pl.ds(row0, n), :], win_buf[slot])
                gt = gu[:, :D_EXPERT]
                hid = (gt * jax.nn.sigmoid(gt)) * gu[:, D_EXPERT:]
                y = _dot(hid.astype(bf), wout_buf[slot])
                ln = lax.broadcasted_iota(jnp.int32, (n, LANES), 1)
                cw_e = jnp.sum(jnp.where(ln == N_GROUPS + e, cw_s[pl.ds(row0, n), :], 0.0), axis=1, keepdims=True)

                @pl.when(first)
                def _():
                    ys[pl.ds(row0, n), :] = cw_e * y

                @pl.when(jnp.logical_not(first))
                def _():
                    ys[pl.ds(row0, n), :] += cw_e * y

            def big_block(b, c):
                block(start + b * big, big)
                return c

            lax.fori_loop(0, nbig, big_block, 0)

            tail = start + nbig * big

            @pl.when(rem > MOE_BLK)
            def _():
                block(tail, big)

            @pl.when((rem > MOE_BLK // 2) & (rem <= MOE_BLK))
            def _():
                block(tail, MOE_BLK)

            @pl.when((rem > 0) & (rem <= MOE_BLK // 2))
            def _():
                block(tail, MOE_BLK // 2)

            return carry

        lax.fori_loop(0, N_EXPERTS, expert, 0)

    @pl.when(j > 0)
    def _():
        u = j - 1
        for g in range(N_GROUPS):
            k = u * N_GROUPS + g
            copy_rows((ys,), (zy_buf,), seg[_SEG_GLOBAL + k], seg[_SEG_LOCAL + k], seg[_SEG_ROWS + k] // al)
        zy = zy_buf[...]
        hi = zy.astype(bf)
        lo = (zy - hi.astype(jnp.float32)).astype(bf)
        dest = jnp.concatenate([dest_s[u]] * (MOE_ZK // LANES), axis=1)
        col = lax.broadcasted_iota(jnp.int32, (sub, MOE_ZK), 1).astype(jnp.float32)
        unperm = jnp.where(col == dest, 1.0, 0.0).astype(bf)
        moe = _dot(unperm, hi) + _dot(unperm, lo)
        o_ref[...] = _rms(h_ref[...] + moe, gfin_ref[...])


def _moe(hn, lg, h, w_e_in, w_e_out, g_final):
    t = hn.shape[0]
    row = lambda s, j: (s * MOE_NSUB + jnp.maximum(j - 1, 0), 0)
    return pl.pallas_call(
        _moe_kernel,
        grid=(t // MOE_ST, 1 + MOE_NSUB),
        in_specs=[
            pl.BlockSpec((MOE_ST, D_MODEL), lambda s, j: (s, 0)),
            pl.BlockSpec((MOE_ST, LANES), lambda s, j: (s, 0)),
            pl.BlockSpec((MOE_SUB, D_MODEL), row),
            pl.BlockSpec((1, D_MODEL), lambda s, j: (0, 0)),
            pl.BlockSpec(memory_space=pl.ANY),
            pl.BlockSpec(memory_space=pl.ANY),
        ],
        out_specs=pl.BlockSpec((MOE_SUB, D_MODEL), row),
        out_shape=jax.ShapeDtypeStruct((t, D_MODEL), jnp.float32),
        scratch_shapes=[
            pltpu.VMEM((MOE_GROWS, D_MODEL), jnp.bfloat16),
            pltpu.VMEM((MOE_GROWS, D_MODEL), jnp.float32),
            pltpu.VMEM((MOE_GROWS, LANES), jnp.float32),
            pltpu.VMEM((MOE_NSUB, MOE_SUB, LANES), jnp.float32),
            pltpu.VMEM((MOE_ZROWS, D_MODEL), jnp.bfloat16),
            pltpu.VMEM((MOE_ZROWS, LANES), jnp.float32),
            pltpu.VMEM((MOE_ZK, D_MODEL), jnp.float32),
            pltpu.VMEM((2, D_MODEL, 2 * D_EXPERT), jnp.bfloat16),
            pltpu.VMEM((2, D_EXPERT, D_MODEL), jnp.bfloat16),
            pltpu.SemaphoreType.DMA((2, 2)),
            pltpu.SMEM((64,), jnp.int32),
        ],
        compiler_params=pltpu.CompilerParams(
            dimension_semantics=("parallel", "arbitrary"), vmem_limit_bytes=VMEM_LIMIT),
        name="moe",
    )(hn, lg, h, g_final, w_e_in, w_e_out)


def kernel(x, positions, mem, g_mix, w_in, b_gate, lambda_q1, lambda_k1, lambda_q2, lambda_k2,
           g_diff_sub, g_mem, w_mem_kv, w_br_diff, w_br_dsa, w_br_mem, w_out, g_ffn,
           w_route_group, b_route_group, w_route_expert, b_route_expert, w_exp_in, w_exp_out,
           g_final):
    b, s, d = x.shape
    t = b * s
    bf = jnp.bfloat16
    top_k = min(TOPK_MAX, s // 4)
    assert d == D_MODEL and s % DIFF_T == 0 and s % DSA_TQ == 0 and top_k <= DSA_TQ
    assert PROJ_TM == DIFF_T == DSA_TK and DSA_TK % DSA_TQ == 0, "proj writes V^T in the attention kernels' key tiles"
    assert g_mix.shape[0] == 1, "single layer"
    lam_init = 0.8 - 0.6 * math.exp(-0.3 * 0)

    wi = w_in[0]
    c = 512
    seg = lambda k: wi[:, k * c:(k + 1) * c]
    o_ik = 7 * c
    w_ik = wi[:, o_ik:o_ik + IDX_DIM]
    w_iw = wi[:, o_ik + IDX_DIM:o_ik + IDX_DIM + IDX_HEADS]
    o_mq = o_ik + IDX_DIM + IDX_HEADS
    w_mq = wi[:, o_mq:o_mq + c]
    w_gl = wi[:, o_mq + c:]
    qs = DIFF_QK_DIM ** -0.5
    qs2 = qs * LOG2E
    src = _pair_lane_source()
    src_seg = np.concatenate([blk * LANES + src for blk in range(c // LANES)])
    rseg = lambda k: seg(k)[:, src_seg]
    w_a = jnp.concatenate([rseg(0) * qs2, rseg(1), seg(2), rseg(3) * qs2, rseg(4), seg(5), rseg(6) * qs, w_mq],
                          axis=1).astype(bf)
    w_s = jnp.concatenate([jnp.concatenate([w_ik, w_ik], axis=1)[:, src], w_iw,
                           jnp.zeros((d, LANES - IDX_HEADS), wi.dtype)], axis=1).astype(bf)
    w_g = w_gl.reshape(d, 3, d).transpose(1, 0, 2).astype(bf)
    b_g = b_gate[0].reshape(3, 1, d)
    w_br = jnp.stack([w_br_diff[0], w_br_dsa[0], w_br_mem[0]]).astype(bf)
    w_r = jnp.concatenate([w_route_group[0], w_route_expert[0],
                           jnp.zeros((d, LANES - N_GROUPS - N_EXPERTS), wi.dtype)], axis=1).astype(bf)
    b_r = jnp.concatenate([b_route_group[0], b_route_expert[0],
                           jnp.zeros((LANES - N_GROUPS - N_EXPERTS,), jnp.float32)]).reshape(1, LANES)

    rot = IDX_DIM // ROPE_FRACTION
    inv_freq = ROPE_THETA ** (-jnp.arange(0, rot, 2, dtype=jnp.float32) / rot)
    inv_col = inv_freq.reshape(rot // 2, 1)
    expand = jnp.asarray(_rope_expanders(), bf)

    x2 = x.reshape(t, d)
    pos_rows = positions.reshape(t // PROJ_TM, 1, PROJ_TM)
    p, ikk, dvt, svt, iwt = _proj(x2, pos_rows, g_mix, inv_col, expand, w_a, w_s)

    y_diff = _diff_attention(p, dvt, lambda_q1, lambda_k1, lambda_q2, lambda_k2,
                             g_diff_sub.reshape(DIFF_V_DIM, 1), b, s, lam_init)
    y_dsa = _dsa_attention(p, ikk, iwt, svt, b, s, top_k)
    mkv = _memkv(mem.reshape(b * MEM_LEN, d), g_mem, w_mem_kv[0].astype(bf), b)
    y_mem = _mem_attention(p, mkv, b, s)

    h, hn, lg = _merge(x2, y_diff, y_dsa, y_mem, g_mix, w_g, b_g, w_br, w_out[0].astype(bf), g_ffn, w_r, b_r)
    out = _moe(hn, lg, h, w_exp_in[0].astype(bf), w_exp_out[0].astype(bf), g_final.reshape(1, d))
    return out.reshape(b, s, d)
```

```python
import functools
import math

import jax
import jax.numpy as jnp
import numpy as np
from jax import lax
from jax.experimental import pallas as pl
from jax.experimental.pallas import tpu as pltpu

D_MODEL = 1024
MEM_LEN = 256
XA_HEADS = 4
XA_HEAD_DIM = 128
DIFF_HEADS = 4
DIFF_QK_DIM = 64
DIFF_V_DIM = 128
DSA_HEADS = 8
DSA_HEAD_DIM = 64
IDX_HEADS = 8
IDX_DIM = 64
TOPK_MAX = 256
ROPE_THETA = 500000.0
ROPE_FRACTION = 4
N_GROUPS = 4
EXPERTS_PER_GROUP = 4
N_EXPERTS = 16
D_EXPERT = 512
EPS = 1e-6

LANES = 128
SUBLANES = 8
VMEM_LIMIT = 56 * 1024 * 1024
NEG_BIG = -1e30
ONES_ROWS = 16
LOG2E = math.log2(math.e)

PROJ_TM = 512
PROJ_TN = 512
DIFF_T = 512
DSA_TQ = 256
DSA_TK = 512
MERGE_TM = 512

_NT = (((1,), (1,)), ((), ()))


def _dot(a, b):
    return jnp.dot(a, b, preferred_element_type=jnp.float32)


def _dot_nt(a, b):
    return lax.dot_general(a, b, _NT, preferred_element_type=jnp.float32)


def _rms(xf, g):
    return xf * lax.rsqrt(jnp.mean(xf * xf, axis=-1, keepdims=True) + EPS) * g


def _pair_lane_source():
    half, rot = IDX_DIM, IDX_DIM // ROPE_FRACTION
    hr = rot // 2
    src = np.zeros(LANES, np.int32)
    src[0:hr] = np.arange(0, hr)
    src[hr:rot] = half + np.arange(0, hr)
    src[rot:half] = np.arange(rot, half)
    src[half:half + hr] = np.arange(hr, rot)
    src[half + hr:half + rot] = half + np.arange(hr, rot)
    src[half + rot:] = half + np.arange(rot, half)
    return src


def _first_head_lanes(shape):
    half, rot = IDX_DIM, IDX_DIM // ROPE_FRACTION
    lane = lax.broadcasted_iota(jnp.int32, shape, len(shape) - 1) % LANES
    return (lane < rot // 2) | ((lane >= rot) & (lane < half + rot // 2))


def _rope_expanders():
    rot = IDX_DIM // ROPE_FRACTION
    e = np.zeros((2, LANES, LANES), np.float32)
    for l in list(range(rot)) + list(range(IDX_DIM, IDX_DIM + rot)):
        e[0, l % (rot // 2), l] = 1.0
        e[1, l % (rot // 2), l] = -1.0 if l < rot else 1.0
    return e


def _proj_kernel(x_ref, posr_ref, g_ref, inv_ref, e_ref, wa_ref, ws_ref,
                 p_ref, ikk_ref, dvt_ref, svt_ref, iwt_ref, n_scr, cos_scr, sin_scr):
    j = pl.program_id(1)

    def rope(v):
        k = v.shape[1] // LANES
        blocks = [pltpu.roll(v[:, b * LANES:(b + 1) * LANES], LANES // 2, 1) for b in range(k)]
        partner = jnp.concatenate(blocks, axis=1) if k > 1 else blocks[0]
        c = jnp.concatenate([cos_scr[...]] * k, axis=1) if k > 1 else cos_scr[...]
        s = jnp.concatenate([sin_scr[...]] * k, axis=1) if k > 1 else sin_scr[...]
        return v * c + partner * s

    @pl.when(j == 0)
    def _():
        xf = x_ref[...]
        n_scr[...] = _rms(xf, g_ref[...]).astype(jnp.bfloat16)
        tm = xf.shape[0]
        ang = posr_ref[0].astype(jnp.float32) * inv_ref[...]
        pad = jnp.zeros((LANES - ang.shape[0], tm), jnp.float32)
        ct = jnp.concatenate([jnp.cos(ang), pad], axis=0).T
        st = jnp.concatenate([jnp.sin(ang), pad], axis=0).T
        crot = sum(_dot(part, e_ref[0]) for part in _split3(ct))
        lane = lax.broadcasted_iota(jnp.int32, crot.shape, 1) % IDX_DIM
        cos_scr[...] = jnp.where(lane < IDX_DIM // ROPE_FRACTION, crot, 1.0)
        sin_scr[...] = sum(_dot(part, e_ref[1]) for part in _split3(st))
        small = _dot(n_scr[...], ws_ref[...])
        ikk_ref[...] = rope(small[:, :LANES]).astype(jnp.bfloat16)
        w = small[:, LANES:] * (IDX_HEADS ** -0.5)
        iwt_ref[...] = w.T[:IDX_HEADS, :]

    acc = _dot(n_scr[...], wa_ref[...])
    is_rope = (j == 0) | (j == 1) | (j == 3) | (j == 4) | (j == 6)

    @pl.when(is_rope)
    def _():
        p_ref[...] = rope(acc).astype(jnp.bfloat16)

    @pl.when(jnp.logical_not(is_rope))
    def _():
        p_ref[...] = acc.astype(jnp.bfloat16)

    @pl.when(j == 2)
    def _():
        dvt_ref[0] = acc.T.astype(jnp.bfloat16)

    @pl.when(j == 5)
    def _():
        svt_ref[0] = acc.T.astype(jnp.bfloat16)


def _proj(x2, pos_rows, g_mix, inv_col, expand, w_a, w_s):
    t = x2.shape[0]
    ncol = w_a.shape[1] // PROJ_TN
    return pl.pallas_call(
        _proj_kernel,
        grid=(t // PROJ_TM, ncol),
        in_specs=[
            pl.BlockSpec((PROJ_TM, D_MODEL), lambda i, j: (i, 0)),
            pl.BlockSpec((1, 1, PROJ_TM), lambda i, j: (i, 0, 0)),
            pl.BlockSpec((1, D_MODEL), lambda i, j: (0, 0)),
            pl.BlockSpec(inv_col.shape, lambda i, j: (0, 0)),
            pl.BlockSpec(expand.shape, lambda i, j: (0, 0, 0)),
            pl.BlockSpec((D_MODEL, PROJ_TN), lambda i, j: (0, j)),
            pl.BlockSpec((D_MODEL, 2 * LANES), lambda i, j: (0, 0)),
        ],
        out_specs=[
            pl.BlockSpec((PROJ_TM, PROJ_TN), lambda i, j: (i, j)),
            pl.BlockSpec((PROJ_TM, LANES), lambda i, j: (i, 0)),
            pl.BlockSpec((1, PROJ_TN, PROJ_TM), lambda i, j: (i, 0, 0)),
            pl.BlockSpec((1, PROJ_TN, PROJ_TM), lambda i, j: (i, 0, 0)),
            pl.BlockSpec((IDX_HEADS, PROJ_TM), lambda i, j: (0, i)),
        ],
        out_shape=[
            jax.ShapeDtypeStruct((t, w_a.shape[1]), jnp.bfloat16),
            jax.ShapeDtypeStruct((t, LANES), jnp.bfloat16),
            jax.ShapeDtypeStruct((t // PROJ_TM, PROJ_TN, PROJ_TM), jnp.bfloat16),
            jax.ShapeDtypeStruct((t // PROJ_TM, PROJ_TN, PROJ_TM), jnp.bfloat16),
            jax.ShapeDtypeStruct((IDX_HEADS, t), jnp.float32),
        ],
        scratch_shapes=[
            pltpu.VMEM((PROJ_TM, D_MODEL), jnp.bfloat16),
            pltpu.VMEM((PROJ_TM, LANES), jnp.float32),
            pltpu.VMEM((PROJ_TM, LANES), jnp.float32),
        ],
        compiler_params=pltpu.CompilerParams(
            dimension_semantics=("parallel", "arbitrary"), vmem_limit_bytes=VMEM_LIMIT),
        name="proj",
    )(x2, pos_rows, g_mix, inv_col, expand, w_a, w_s)


def _diff_kernel(q_ref, k_ref, vt_ref, lq1_ref, lk1_ref, lq2_ref, lk2_ref, gs_ref, o_ref,
                 qm, m_s, acc, st_a, st_b, p_scr, *, lam_init):
    qi = pl.program_id(2)
    t = DIFF_T
    q = q_ref[...]
    first = _first_head_lanes(q.shape)
    qm[0] = jnp.where(first, q, jnp.zeros_like(q))
    qm[1] = jnp.where(first, jnp.zeros_like(q), q)
    m_s[...] = jnp.full(m_s.shape, NEG_BIG, jnp.float32)
    acc[...] = jnp.zeros(acc.shape, jnp.float32)

    def qk(j, st_ref):
        k = k_ref[pl.ds(pl.multiple_of(j * t, t), t), :]
        for i in range(2):
            st_ref[i] = _dot_nt(k, qm[i])

    def softmax_pv(j, st_ref, diagonal):
        vt = jnp.concatenate([vt_ref[j], jnp.ones((ONES_ROWS, t), jnp.bfloat16)], axis=0)
        if diagonal:
            krow = lax.broadcasted_iota(jnp.int32, (t, t), 0)
            qcol = lax.broadcasted_iota(jnp.int32, (t, t), 1)
            keep = krow <= qcol
        alphas = []
        for i in range(2):
            st = st_ref[i]
            if diagonal:
                st = jnp.where(keep, st, NEG_BIG)
            m_old = m_s[i]
            m_new = jnp.maximum(m_old, jnp.max(st, axis=0, keepdims=True))
            alphas.append(jnp.exp2(m_old - m_new))
            m_s[i] = m_new
            p_scr[i] = jnp.exp2(st - m_new).astype(jnp.bfloat16)
        for i in range(2):
            acc[i] = alphas[i] * acc[i] + _dot(vt, p_scr[i])

    qk(0, st_a)

    def pair(tt, carry):
        j = 2 * tt
        qk(j + 1, st_b)
        softmax_pv(j, st_a, False)
        qk(j + 2, st_a)
        softmax_pv(j + 1, st_b, False)
        return carry

    lax.fori_loop(0, qi // 2, pair, 0)

    @pl.when(qi % 2 == 0)
    def _():
        softmax_pv(qi, st_a, True)

    @pl.when(qi % 2 == 1)
    def _():
        qk(qi, st_b)
        softmax_pv(qi - 1, st_a, False)
        softmax_pv(qi, st_b, True)

    lam = (jnp.exp(jnp.sum(lq1_ref[...] * lk1_ref[...], axis=1, keepdims=True))
           - jnp.exp(jnp.sum(lq2_ref[...] * lk2_ref[...], axis=1, keepdims=True))
           + lam_init)
    dv = DIFF_V_DIM
    ot = (acc[0, :dv, :] / acc[0, dv:dv + 1, :]
          - lam * (acc[1, :dv, :] / acc[1, dv:dv + 1, :]))
    yt = ot * lax.rsqrt(jnp.mean(ot * ot, axis=0, keepdims=True) + EPS) * gs_ref[...]
    o_ref[...] = (yt * (1.0 - lam_init)).T.astype(o_ref.dtype)


def _diff_attention(p, dvt, lq1, lk1, lq2, lk2, g_sub_col, batch, seq, lam_init):
    nb = seq // DIFF_T
    vec = pl.BlockSpec((1, DIFF_QK_DIM), lambda b, h, qi: (0, 0))
    return pl.pallas_call(
        functools.partial(_diff_kernel, lam_init=lam_init),
        grid=(batch, DIFF_HEADS, nb),
        in_specs=[
            pl.BlockSpec((DIFF_T, LANES), lambda b, h, qi: (b * nb + qi, h)),
            pl.BlockSpec((seq, LANES), lambda b, h, qi: (b, DIFF_HEADS + h)),
            pl.BlockSpec((nb, DIFF_V_DIM, DIFF_T), lambda b, h, qi: (b, h, 0)),
            vec, vec, vec, vec,
            pl.BlockSpec((DIFF_V_DIM, 1), lambda b, h, qi: (0, 0)),
        ],
        out_specs=pl.BlockSpec((DIFF_T, LANES), lambda b, h, qi: (b * nb + qi, h)),
        out_shape=jax.ShapeDtypeStruct((batch * seq, DIFF_HEADS * DIFF_V_DIM), jnp.bfloat16),
        scratch_shapes=[
            pltpu.VMEM((2, DIFF_T, LANES), jnp.bfloat16),
            pltpu.VMEM((2, 1, DIFF_T), jnp.float32),
            pltpu.VMEM((2, DIFF_V_DIM + ONES_ROWS, DIFF_T), jnp.float32),
            pltpu.VMEM((2, DIFF_T, DIFF_T), jnp.float32),
            pltpu.VMEM((2, DIFF_T, DIFF_T), jnp.float32),
            pltpu.VMEM((2, DIFF_T, DIFF_T), jnp.bfloat16),
        ],
        compiler_params=pltpu.CompilerParams(
            dimension_semantics=("parallel", "parallel", "arbitrary"),
            vmem_limit_bytes=VMEM_LIMIT),
        name="diffattn",
    )(p, p, dvt, lq1, lk1, lq2, lk2, g_sub_col)


def _key_to_float(key):
    bits = jnp.where(key >= 0, key, key ^ jnp.int32(0x7FFFFFFF))
    return lax.bitcast_convert_type(bits, jnp.float32)


def _count_rows(hit):
    tk, tq = hit.shape
    return jnp.sum(hit.reshape(tk // (4 * SUBLANES), 4 * SUBLANES, tq), axis=0)


def _dsa_kernel(sq_ref, iq_ref, iwt_ref, ikk_ref, sk_ref, svt_ref, o_ref,
                score, hi16, lo16, iqm, sqm, acc, m_s, thr, st_a, st_b, mx_a, mx_b, p_scr, *, seq, top_k):
    qi = pl.program_id(1)
    tq, tk = DSA_TQ, DSA_TK
    last = (qi * tq) // tk
    nkc = last + 1
    first = _first_head_lanes((tq, LANES))

    for h in range(DSA_HEADS):
        pr = h // 2
        iqp = iq_ref[:, pr * LANES:(pr + 1) * LANES]
        sqp = sq_ref[:, pr * LANES:(pr + 1) * LANES]
        mine = first if h % 2 == 0 else jnp.logical_not(first)
        iqm[h] = jnp.where(mine, iqp, jnp.zeros_like(iqp))
        sqm[h] = jnp.where(mine, sqp, jnp.zeros_like(sqp))

    krow = lax.broadcasted_iota(jnp.int32, (tk, tq), 0)
    qcol = lax.broadcasted_iota(jnp.int32, (tk, tq), 1)

    def idx_logits(c, lg_ref):
        kk = ikk_ref[pl.ds(pl.multiple_of(c * tk, tk), tk), :]
        for h in range(IDX_HEADS):
            lg_ref[h] = _dot_nt(kk, iqm[h])

    def idx_score(c, lg_ref):
        sc = jnp.zeros((tk, tq), jnp.float32)
        for h in range(IDX_HEADS):
            sc = sc + iwt_ref[h:h + 1, :] * jnp.maximum(lg_ref[h], 0.0)
        sc = jnp.where(krow <= qcol + (qi * tq - c * tk), sc, -jnp.inf)
        score[c] = sc
        bits = lax.bitcast_convert_type(sc, jnp.int32)
        okey = jnp.where(bits >= 0, bits, bits ^ jnp.int32(0x7FFFFFFF))
        hi16[c] = lax.shift_right_arithmetic(okey, 16).astype(jnp.int16)
        lo16[c] = ((okey & 0xFFFF) - 2 ** 15).astype(jnp.int16)

    idx_logits(0, st_a)

    def idx_pair(tt, carry):
        c = 2 * tt
        idx_logits(c + 1, st_b)
        idx_score(c, st_a)
        idx_logits(c + 2, st_a)
        idx_score(c + 1, st_b)
        return carry

    lax.fori_loop(0, last // 2, idx_pair, 0)

    @pl.when(last % 2 == 0)
    def _():
        idx_score(last, st_a)

    @pl.when(last % 2 == 1)
    def _():
        idx_logits(last, st_b)
        idx_score(last - 1, st_a)
        idx_score(last, st_b)

    zero_cnt = jnp.zeros((4 * SUBLANES, tq), jnp.float32)
    i16_min = -2 ** 15

    def count16(buf, pred):
        def body(c, cnt):
            hit = jnp.where(pred(buf[c]), jnp.int16(1), jnp.int16(0))
            h3 = hit.reshape(tk // (4 * SUBLANES), 4 * SUBLANES, tq)
            part = h3[0]
            for r in range(1, h3.shape[0]):
                part = part + h3[r]
            return cnt + part
        cnt = lax.fori_loop(0, nkc, body, jnp.zeros((4 * SUBLANES, tq), jnp.int16))
        return jnp.sum(cnt.astype(jnp.int32), axis=0, keepdims=True)

    def bisect16(buf, want):
        def bit_step(i, cur):
            cand = cur + lax.shift_left(jnp.int32(1), jnp.int32(15) - i)
            c16 = cand.astype(jnp.int16)
            return jnp.where(count16(buf, lambda blk: blk >= c16) >= want, cand, cur)
        return lax.fori_loop(0, 16, bit_step, jnp.full((1, tq), i16_min, jnp.int32))

    t_hi = bisect16(hi16, top_k)
    t_hi16 = t_hi.astype(jnp.int16)
    rest = top_k - count16(hi16, lambda blk: blk > t_hi16)

    def bucket_only(c, carry):
        lo16[c] = jnp.where(hi16[c] == t_hi16, lo16[c], jnp.int16(i16_min))
        return carry

    lax.fori_loop(0, nkc, bucket_only, 0)
    t_lo = bisect16(lo16, rest)
    key = lax.shift_left(t_hi, 16) + (t_lo - i16_min)
    t_f = _key_to_float(key)

    def count_gt_ge(c, carry):
        gt, ge = carry
        blk = score[c]
        return (gt + _count_rows(jnp.where(blk > t_f, 1.0, 0.0)),
                ge + _count_rows(jnp.where(blk >= t_f, 1.0, 0.0)))

    gt, ge = lax.fori_loop(0, nkc, count_gt_ge, (zero_cnt, zero_cnt))
    n_gt = jnp.sum(gt, axis=0, keepdims=True)
    n_ge = jnp.sum(ge, axis=0, keepdims=True)
    q_pos = qi * tq + lax.broadcasted_iota(jnp.int32, (1, tq), 1)
    few = q_pos < top_k - 1
    thr[...] = jnp.where(few, float(jnp.finfo(jnp.float32).min), t_f)
    need = float(top_k) - n_gt
    split = jnp.logical_and(jnp.logical_not(few), n_ge > float(top_k))

    @pl.when(jnp.max(jnp.where(split, 1.0, 0.0)) > 0.0)
    def _():
        def count_eq_below(jc):
            def body(c, cnt):
                hit = jnp.where((score[c] == t_f) & (c * tk + krow < jc), 1.0, 0.0)
                return cnt + _count_rows(hit)
            return jnp.sum(lax.fori_loop(0, nkc, body, zero_cnt), axis=0, keepdims=True)

        nbits = (seq - 1).bit_length()

        def jbit(i, jv):
            cand = jv + lax.shift_left(jnp.int32(1), jnp.int32(nbits - 1) - i)
            return jnp.where(count_eq_below(cand) < need, cand, jv)

        jv = lax.fori_loop(0, nbits, jbit, jnp.zeros((1, tq), jnp.int32))

        def drop_ties(c, carry):
            blk = score[c]
            score[c] = jnp.where(split & (blk == t_f) & (c * tk + krow > jv), -jnp.inf, blk)
            return carry

        lax.fori_loop(0, nkc, drop_ties, 0)

    m_s[...] = jnp.full(m_s.shape, NEG_BIG, jnp.float32)
    acc[...] = jnp.zeros(acc.shape, jnp.float32)

    def qk(c, slot):
        st_ref, mx_ref = slot
        off = pl.multiple_of(c * tk, tk)
        bias = jnp.where(score[c] >= thr[...], 0.0, NEG_BIG)
        for h in range(DSA_HEADS):
            kp = sk_ref[pl.ds(off, tk), (h // 2) * LANES:(h // 2 + 1) * LANES]
            st = _dot_nt(kp, sqm[h]) + bias
            st_ref[h] = st
            mx_ref[h] = jnp.max(st, axis=0, keepdims=True)

    def softmax_pv(c, slot):
        st_ref, mx_ref = slot
        alphas = []
        for h in range(DSA_HEADS):
            m_old = m_s[h]
            m_new = jnp.maximum(m_old, mx_ref[h])
            alphas.append(jnp.exp2(m_old - m_new))
            m_s[h] = m_new
            p_scr[h] = jnp.exp2(st_ref[h] - m_new).astype(jnp.bfloat16)
        ones = jnp.ones((ONES_ROWS, tk), jnp.bfloat16)
        for h in range(DSA_HEADS):
            vt = jnp.concatenate([svt_ref[c, h * DSA_HEAD_DIM:(h + 1) * DSA_HEAD_DIM, :], ones], axis=0)
            acc[h] = alphas[h] * acc[h] + _dot(vt, p_scr[h])

    slot_a, slot_b = (st_a, mx_a), (st_b, mx_b)
    qk(0, slot_a)

    def pair(tt, carry):
        c = 2 * tt
        qk(c + 1, slot_b)
        softmax_pv(c, slot_a)
        qk(c + 2, slot_a)
        softmax_pv(c + 1, slot_b)
        return carry

    lax.fori_loop(0, last // 2, pair, 0)

    @pl.when(last % 2 == 0)
    def _():
        softmax_pv(last, slot_a)

    @pl.when(last % 2 == 1)
    def _():
        qk(last, slot_b)
        softmax_pv(last - 1, slot_a)
        softmax_pv(last, slot_b)

    dh = DSA_HEAD_DIM
    outs = [acc[h, :dh, :] / acc[h, dh:dh + 1, :] for h in range(DSA_HEADS)]
    o_ref[...] = jnp.concatenate(outs, axis=0).T.astype(o_ref.dtype)


def _dsa_attention(p, ikk, iwt, svt, batch, seq, top_k):
    nq = seq // DSA_TQ
    w = DSA_HEADS * DSA_HEAD_DIM
    once = pl.Buffered(1)
    return pl.pallas_call(
        functools.partial(_dsa_kernel, seq=seq, top_k=top_k),
        grid=(batch, nq),
        in_specs=[
            pl.BlockSpec((DSA_TQ, w), lambda b, qi: (b * nq + qi, 3)),
            pl.BlockSpec((DSA_TQ, w), lambda b, qi: (b * nq + qi, 6)),
            pl.BlockSpec((IDX_HEADS, DSA_TQ), lambda b, qi: (0, b * nq + qi)),
            pl.BlockSpec((seq, LANES), lambda b, qi: (b, 0), pipeline_mode=once),
            pl.BlockSpec((seq, w), lambda b, qi: (b, 4), pipeline_mode=once),
            pl.BlockSpec((seq // DSA_TK, w, DSA_TK), lambda b, qi: (b, 0, 0), pipeline_mode=once),
        ],
        out_specs=pl.BlockSpec((DSA_TQ, w), lambda b, qi: (b * nq + qi, 0)),
        out_shape=jax.ShapeDtypeStruct((batch * seq, w), jnp.bfloat16),
        scratch_shapes=[
            pltpu.VMEM((seq // DSA_TK, DSA_TK, DSA_TQ), jnp.float32),
            pltpu.VMEM((seq // DSA_TK, DSA_TK, DSA_TQ), jnp.int16),
            pltpu.VMEM((seq // DSA_TK, DSA_TK, DSA_TQ), jnp.int16),
            pltpu.VMEM((IDX_HEADS, DSA_TQ, LANES), jnp.bfloat16),
            pltpu.VMEM((DSA_HEADS, DSA_TQ, LANES), jnp.bfloat16),
            pltpu.VMEM((DSA_HEADS, DSA_HEAD_DIM + ONES_ROWS, DSA_TQ), jnp.float32),
            pltpu.VMEM((DSA_HEADS, 1, DSA_TQ), jnp.float32),
            pltpu.VMEM((1, DSA_TQ), jnp.float32),
            pltpu.VMEM((DSA_HEADS, DSA_TK, DSA_TQ), jnp.float32),
            pltpu.VMEM((DSA_HEADS, DSA_TK, DSA_TQ), jnp.float32),
            pltpu.VMEM((DSA_HEADS, 1, DSA_TQ), jnp.float32),
            pltpu.VMEM((DSA_HEADS, 1, DSA_TQ), jnp.float32),
            pltpu.VMEM((DSA_HEADS, DSA_TK, DSA_TQ), jnp.bfloat16),
        ],
        compiler_params=pltpu.CompilerParams(
            dimension_semantics=("parallel", "arbitrary"), vmem_limit_bytes=VMEM_LIMIT),
        name="dsa",
    )(p, p, iwt, ikk, p, svt)


def _memkv_kernel(mem_ref, g_ref, w_ref, o_ref):
    n = _rms(mem_ref[...], g_ref[...]).astype(jnp.bfloat16)
    o_ref[...] = _dot(n, w_ref[...]).astype(o_ref.dtype)


def _memkv(mem2, g_mem, w_kv, batch):
    return pl.pallas_call(
        _memkv_kernel,
        grid=(batch,),
        in_specs=[
            pl.BlockSpec((MEM_LEN, D_MODEL), lambda b: (b, 0)),
            pl.BlockSpec((1, D_MODEL), lambda b: (0, 0)),
            pl.BlockSpec(w_kv.shape, lambda b: (0, 0)),
        ],
        out_specs=pl.BlockSpec((MEM_LEN, w_kv.shape[1]), lambda b: (b, 0)),
        out_shape=jax.ShapeDtypeStruct((batch * MEM_LEN, w_kv.shape[1]), jnp.bfloat16),
        compiler_params=pltpu.CompilerParams(
            dimension_semantics=("parallel",), vmem_limit_bytes=VMEM_LIMIT),
        name="memkv",
    )(mem2, g_mem, w_kv)


def _mem_attention(q, kv):
    scale = XA_HEAD_DIM ** -0.5
    outs = []
    for h in range(XA_HEADS):
        qh = q[:, h * LANES:(h + 1) * LANES]
        k = kv[:, h * LANES:(h + 1) * LANES]
        v = kv[:, (XA_HEADS + h) * LANES:(XA_HEADS + h + 1) * LANES]
        s = _dot_nt(qh, k) * scale
        m = jnp.max(s, axis=1, keepdims=True)
        p = jnp.exp(s - m)
        l = jnp.sum(p, axis=1, keepdims=True)
        outs.append(_dot((p / l).astype(jnp.bfloat16), v))
    return jnp.concatenate(outs, axis=1).astype(jnp.bfloat16)


def _merge_kernel(x_ref, yd_ref, ys_ref, mq_ref, mkv_ref, gmix_ref, wg_ref, bg_ref, wbr_ref, wout_ref,
                  gffn_ref, wr_ref, br_ref, h_ref, hn_ref, lg_ref):
    xf = x_ref[...]
    n = _rms(xf, gmix_ref[...]).astype(jnp.bfloat16)
    branches = (yd_ref[...], ys_ref[...], _mem_attention(mq_ref[...], mkv_ref[...]))
    merged = jnp.zeros(xf.shape, jnp.float32)
    for i, y in enumerate(branches):
        gate = jax.nn.sigmoid(_dot(n, wg_ref[i]) + bg_ref[i])
        merged = merged + gate * _dot(y, wbr_ref[i])
    h = xf + _dot(merged.astype(jnp.bfloat16), wout_ref[...])
    h_ref[...] = h
    hn = _rms(h, gffn_ref[...]).astype(jnp.bfloat16)
    hn_ref[...] = hn
    lg_ref[...] = _dot(hn, wr_ref[...]) + br_ref[...]


def _merge(x2, y_diff, y_dsa, p, mkv, seq, g_mix, w_g, b_g, w_br, w_out, g_ffn, w_r, b_r):
    t = x2.shape[0]
    tiles_per_seq = seq // MERGE_TM
    row = lambda w: pl.BlockSpec((MERGE_TM, w), lambda i: (i, 0))
    full = lambda a: pl.BlockSpec(a.shape, lambda i: (0,) * a.ndim, pipeline_mode=pl.Buffered(1))
    return pl.pallas_call(
        _merge_kernel,
        grid=(t // MERGE_TM,),
        in_specs=[row(D_MODEL), row(512), row(512),
                  pl.BlockSpec((MERGE_TM, 512), lambda i: (i, 7)),
                  pl.BlockSpec((MEM_LEN, mkv.shape[1]), lambda i: (i // tiles_per_seq, 0)),
                  full(g_mix), full(w_g), full(b_g), full(w_br), full(w_out), full(g_ffn), full(w_r), full(b_r)],
        out_specs=[row(D_MODEL), row(D_MODEL), row(LANES)],
        out_shape=[
            jax.ShapeDtypeStruct((t, D_MODEL), jnp.float32),
            jax.ShapeDtypeStruct((t, D_MODEL), jnp.bfloat16),
            jax.ShapeDtypeStruct((t, LANES), jnp.float32),
        ],
        compiler_params=pltpu.CompilerParams(
            dimension_semantics=("parallel",), vmem_limit_bytes=VMEM_LIMIT),
        name="merge",
    )(x2, y_diff, y_dsa, p, mkv, g_mix, w_g, b_g, w_br, w_out, g_ffn, w_r, b_r)


MOE_ST = 2048
MOE_SUB = 512
MOE_ALIGN = 16
MOE_BLK = 256
MOE_NSUB = MOE_ST // MOE_SUB
MOE_ZROWS = MOE_SUB + N_GROUPS * MOE_ALIGN
MOE_ZK = -(-MOE_ZROWS // LANES) * LANES
MOE_GROWS = MOE_ST + MOE_NSUB * N_GROUPS * MOE_ALIGN + MOE_BLK
_SEG_ROWS, _SEG_LOCAL, _SEG_GLOBAL = 0, 16, 32
_GRP_BASE, _GRP_ROWS = 48, 52


def _route(lg):
    lane = lax.broadcasted_iota(jnp.int32, lg.shape, 1).astype(jnp.float32)
    big = float(LANES)
    gl = jnp.where(lane < N_GROUPS, lg, -jnp.inf)
    gmax = jnp.max(gl, axis=1, keepdims=True)
    grp = jnp.min(jnp.where(gl == gmax, lane, big), axis=1, keepdims=True)
    gsum = jnp.sum(jnp.where(lane < N_GROUPS, jnp.exp(gl - gmax), 0.0), axis=1, keepdims=True)
    p_grp = 1.0 / gsum
    lo = N_GROUPS + grp * EXPERTS_PER_GROUP
    el = jnp.where((lane >= lo) & (lane < lo + EXPERTS_PER_GROUP), lg, -jnp.inf)
    e1 = jnp.max(el, axis=1, keepdims=True)
    i1 = jnp.min(jnp.where(el == e1, lane, big), axis=1, keepdims=True)
    el2 = jnp.where(lane == i1, -jnp.inf, el)
    e2 = jnp.max(el2, axis=1, keepdims=True)
    i2 = jnp.min(jnp.where(el2 == e2, lane, big), axis=1, keepdims=True)
    r = jnp.exp(e2 - e1)
    w1 = p_grp / (1.0 + r)
    w2 = p_grp * r / (1.0 + r)
    return jnp.where(lane == i1, w1, 0.0) + jnp.where(lane == i2, w2, 0.0), grp


def _split3(x):
    a = x.astype(jnp.bfloat16)
    r = x - a.astype(jnp.float32)
    b = r.astype(jnp.bfloat16)
    c = (r - b.astype(jnp.float32)).astype(jnp.bfloat16)
    return a, b, c


def _moe_kernel(hn_ref, lg_ref, h_ref, gfin_ref, win_hbm, wout_hbm, o_ref,
                g_buf, ys, cw_s, dest_s, z_buf, cwz_buf, zy_buf, win_buf, wout_buf, sem, seg):
    j = pl.program_id(1)
    sub, al, bf = MOE_SUB, MOE_ALIGN, jnp.bfloat16
    lane_row = lax.broadcasted_iota(jnp.int32, (1, LANES), 1)

    def copy_rows(src_ref, dst_ref, src, dst, nblk):
        def body(i, carry):
            s0 = pl.multiple_of(src + i * al, al)
            d0 = pl.multiple_of(dst + i * al, al)
            for s_ref, d_ref in zip(src_ref, dst_ref):
                d_ref[pl.ds(d0, al), :] = s_ref[pl.ds(s0, al), :]
            return carry
        lax.fori_loop(0, nblk, body, 0)

    @pl.when(j == 0)
    def _():
        cw, grp = _route(lg_ref[...])
        lane = lax.broadcasted_iota(jnp.int32, (sub, LANES), 1).astype(jnp.float32)
        hots = [lane == grp[u * sub:(u + 1) * sub] for u in range(MOE_NSUB)]
        for u in range(MOE_NSUB):
            n_vec = jnp.sum(jnp.where(hots[u], 1.0, 0.0), axis=0, keepdims=True)
            off = jnp.int32(0)
            for g in range(N_GROUPS):
                n = jnp.sum(jnp.where(lane_row == g, n_vec, 0.0)).astype(jnp.int32)
                rows = ((n + (al - 1)) // al) * al
                seg[_SEG_ROWS + u * N_GROUPS + g] = rows
                seg[_SEG_LOCAL + u * N_GROUPS + g] = off
                off = off + rows
        base = jnp.int32(0)
        for g in range(N_GROUPS):
            seg[_GRP_BASE + g] = base
            pos = base
            for u in range(MOE_NSUB):
                seg[_SEG_GLOBAL + u * N_GROUPS + g] = pos
                pos = pos + seg[_SEG_ROWS + u * N_GROUPS + g]
            seg[_GRP_ROWS + g] = pos - base
            base = pos

        ri = lax.broadcasted_iota(jnp.int32, (sub, sub), 0)
        ci = lax.broadcasted_iota(jnp.int32, (sub, sub), 1)
        earlier = jnp.where(ci < ri, 1.0, 0.0).astype(bf)
        zrow = lax.broadcasted_iota(jnp.int32, (MOE_ZROWS, sub), 0).astype(jnp.float32)
        zy_buf[...] = jnp.zeros(zy_buf.shape, jnp.float32)
        for u in range(MOE_NSUB):
            hot = hots[u]
            before = _dot(earlier, jnp.where(hot, 1.0, 0.0).astype(bf))
            start = jnp.zeros((1, LANES), jnp.float32)
            for g in range(N_GROUPS):
                start = jnp.where(lane_row == g, seg[_SEG_LOCAL + u * N_GROUPS + g].astype(jnp.float32), start)
            dest = jnp.sum(jnp.where(hot, before + start, 0.0), axis=1, keepdims=True)
            dest_b = jnp.broadcast_to(dest, (sub, LANES))
            dest_s[u] = dest_b
            perm = jnp.where(zrow == dest_b.T[0:1, :], 1.0, 0.0).astype(bf)
            z_buf[...] = _dot(perm, hn_ref[u * sub:(u + 1) * sub, :]).astype(bf)
            c3 = _split3(cw[u * sub:(u + 1) * sub])
            cwz_buf[...] = _dot(perm, c3[0]) + _dot(perm, c3[1]) + _dot(perm, c3[2])
            for g in range(N_GROUPS):
                k = u * N_GROUPS + g
                copy_rows((z_buf, cwz_buf), (g_buf, cw_s), seg[_SEG_LOCAL + k], seg[_SEG_GLOBAL + k],
                          seg[_SEG_ROWS + k] // al)

        def weights(e, slot):
            return (pltpu.make_async_copy(win_hbm.at[e], win_buf.at[slot], sem.at[0, slot]),
                    pltpu.make_async_copy(wout_hbm.at[e], wout_buf.at[slot], sem.at[1, slot]))

        for cp in weights(0, 0):
            cp.start()

        def expert(e, carry):
            slot = e % 2
            g = e // EXPERTS_PER_GROUP

            @pl.when(e + 1 < N_EXPERTS)
            def _():
                for cp in weights(e + 1, 1 - slot):
                    cp.start()

            for cp in weights(e, slot):
                cp.wait()
            start = seg[_GRP_BASE + g]
            rows = seg[_GRP_ROWS + g]
            big = 2 * MOE_BLK
            nbig = rows // big
            rem = rows - nbig * big
            first = e % EXPERTS_PER_GROUP == 0

            def block(row0, n):
                row0 = pl.multiple_of(row0, al)
                gu = _dot(g_buf[pl.ds(row0, n), :], win_buf[slot])
                gt = gu[:, :D_EXPERT]
                hid = (gt * jax.nn.sigmoid(gt)) * gu[:, D_EXPERT:]
                y = _dot(hid.astype(bf), wout_buf[slot])
                ln = lax.broadcasted_iota(jnp.int32, (n, LANES), 1)
                cw_e = jnp.sum(jnp.where(ln == N_GROUPS + e, cw_s[pl.ds(row0, n), :], 0.0), axis=1, keepdims=True)

                @pl.when(first)
                def _():
                    ys[pl.ds(row0, n), :] = cw_e * y

                @pl.when(jnp.logical_not(first))
                def _():
                    ys[pl.ds(row0, n), :] += cw_e * y

            def big_block(b, c):
                block(start + b * big, big)
                return c

            lax.fori_loop(0, nbig, big_block, 0)

            tail = start + nbig * big

            @pl.when(rem > MOE_BLK)
            def _():
                block(tail, big)

            @pl.when((rem > MOE_BLK // 2) & (rem <= MOE_BLK))
            def _():
                block(tail, MOE_BLK)

            @pl.when((rem > 0) & (rem <= MOE_BLK // 2))
            def _():
                block(tail, MOE_BLK // 2)

            return carry

        lax.fori_loop(0, N_EXPERTS, expert, 0)

    @pl.when(j > 0)
    def _():
        u = j - 1
        for g in range(N_GROUPS):
            k = u * N_GROUPS + g
            copy_rows((ys,), (zy_buf,), seg[_SEG_GLOBAL + k], seg[_SEG_LOCAL + k], seg[_SEG_ROWS + k] // al)
        zy = zy_buf[...]
        hi = zy.astype(bf)
        lo = (zy - hi.astype(jnp.float32)).astype(bf)
        dest = jnp.concatenate([dest_s[u]] * (MOE_ZK // LANES), axis=1)
        col = lax.broadcasted_iota(jnp.int32, (sub, MOE_ZK), 1).astype(jnp.float32)
        unperm = jnp.where(col == dest, 1.0, 0.0).astype(bf)
        moe = _dot(unperm, hi) + _dot(unperm, lo)
        o_ref[...] = _rms(h_ref[...] + moe, gfin_ref[...])


def _moe(hn, lg, h, w_e_in, w_e_out, g_final):
    t = hn.shape[0]
    row = lambda s, j: (s * MOE_NSUB + jnp.maximum(j - 1, 0), 0)
    return pl.pallas_call(
        _moe_kernel,
        grid=(t // MOE_ST, 1 + MOE_NSUB),
        in_specs=[
            pl.BlockSpec((MOE_ST, D_MODEL), lambda s, j: (s, 0)),
            pl.BlockSpec((MOE_ST, LANES), lambda s, j: (s, 0)),
            pl.BlockSpec((MOE_SUB, D_MODEL), row),
            pl.BlockSpec((1, D_MODEL), lambda s, j: (0, 0)),
            pl.BlockSpec(memory_space=pl.ANY),
            pl.BlockSpec(memory_space=pl.ANY),
        ],
        out_specs=pl.BlockSpec((MOE_SUB, D_MODEL), row),
        out_shape=jax.ShapeDtypeStruct((t, D_MODEL), jnp.float32),
        scratch_shapes=[
            pltpu.VMEM((MOE_GROWS, D_MODEL), jnp.bfloat16),
            pltpu.VMEM((MOE_GROWS, D_MODEL), jnp.float32),
            pltpu.VMEM((MOE_GROWS, LANES), jnp.float32),
            pltpu.VMEM((MOE_NSUB, MOE_SUB, LANES), jnp.float32),
            pltpu.VMEM((MOE_ZROWS, D_MODEL), jnp.bfloat16),
            pltpu.VMEM((MOE_ZROWS, LANES), jnp.float32),
            pltpu.VMEM((MOE_ZK, D_MODEL), jnp.float32),
            pltpu.VMEM((2, D_MODEL, 2 * D_EXPERT), jnp.bfloat16),
            pltpu.VMEM((2, D_EXPERT, D_MODEL), jnp.bfloat16),
            pltpu.SemaphoreType.DMA((2, 2)),
            pltpu.SMEM((64,), jnp.int32),
        ],
        compiler_params=pltpu.CompilerParams(
            dimension_semantics=("parallel", "arbitrary"), vmem_limit_bytes=VMEM_LIMIT),
        name="moe",
    )(hn, lg, h, g_final, w_e_in, w_e_out)


def kernel(x, positions, mem, g_mix, w_in, b_gate, lambda_q1, lambda_k1, lambda_q2, lambda_k2,
           g_diff_sub, g_mem, w_mem_kv, w_br_diff, w_br_dsa, w_br_mem, w_out, g_ffn,
           w_route_group, b_route_group, w_route_expert, b_route_expert, w_exp_in, w_exp_out,
           g_final):
    b, s, d = x.shape
    t = b * s
    bf = jnp.bfloat16
    top_k = min(TOPK_MAX, s // 4)
    assert d == D_MODEL and s % DIFF_T == 0 and s % DSA_TQ == 0 and top_k <= DSA_TQ
    assert PROJ_TM == DIFF_T == DSA_TK and DSA_TK % DSA_TQ == 0, "proj writes V^T in the attention kernels' key tiles"
    assert g_mix.shape[0] == 1, "single layer"
    lam_init = 0.8 - 0.6 * math.exp(-0.3 * 0)

    wi = w_in[0]
    c = 512
    seg = lambda k: wi[:, k * c:(k + 1) * c]
    o_ik = 7 * c
    w_ik = wi[:, o_ik:o_ik + IDX_DIM]
    w_iw = wi[:, o_ik + IDX_DIM:o_ik + IDX_DIM + IDX_HEADS]
    o_mq = o_ik + IDX_DIM + IDX_HEADS
    w_mq = wi[:, o_mq:o_mq + c]
    w_gl = wi[:, o_mq + c:]
    qs = DIFF_QK_DIM ** -0.5
    qs2 = qs * LOG2E
    src = _pair_lane_source()
    src_seg = np.concatenate([blk * LANES + src for blk in range(c // LANES)])
    rseg = lambda k: seg(k)[:, src_seg]
    w_a = jnp.concatenate([rseg(0) * qs2, rseg(1), seg(2), rseg(3) * qs2, rseg(4), seg(5), rseg(6) * qs, w_mq],
                          axis=1).astype(bf)
    w_s = jnp.concatenate([jnp.concatenate([w_ik, w_ik], axis=1)[:, src], w_iw,
                           jnp.zeros((d, LANES - IDX_HEADS), wi.dtype)], axis=1).astype(bf)
    w_g = w_gl.reshape(d, 3, d).transpose(1, 0, 2).astype(bf)
    b_g = b_gate[0].reshape(3, 1, d)
    w_br = jnp.stack([w_br_diff[0], w_br_dsa[0], w_br_mem[0]]).astype(bf)
    w_r = jnp.concatenate([w_route_group[0], w_route_expert[0],
                           jnp.zeros((d, LANES - N_GROUPS - N_EXPERTS), wi.dtype)], axis=1).astype(bf)
    b_r = jnp.concatenate([b_route_group[0], b_route_expert[0],
                           jnp.zeros((LANES - N_GROUPS - N_EXPERTS,), jnp.float32)]).reshape(1, LANES)

    rot = IDX_DIM // ROPE_FRACTION
    inv_freq = ROPE_THETA ** (-jnp.arange(0, rot, 2, dtype=jnp.float32) / rot)
    inv_col = inv_freq.reshape(rot // 2, 1)
    expand = jnp.asarray(_rope_expanders(), bf)

    x2 = x.reshape(t, d)
    pos_rows = positions.reshape(t // PROJ_TM, 1, PROJ_TM)
    p, ikk, dvt, svt, iwt = _proj(x2, pos_rows, g_mix, inv_col, expand, w_a, w_s)

    y_diff = _diff_attention(p, dvt, lambda_q1, lambda_k1, lambda_q2, lambda_k2,
                             g_diff_sub.reshape(DIFF_V_DIM, 1), b, s, lam_init)
    y_dsa = _dsa_attention(p, ikk, iwt, svt, b, s, top_k)
    mkv = _memkv(mem.reshape(b * MEM_LEN, d), g_mem, w_mem_kv[0].astype(bf), b)
    h, hn, lg = _merge(x2, y_diff, y_dsa, p, mkv, s, g_mix, w_g, b_g, w_br, w_out[0].astype(bf), g_ffn, w_r, b_r)
    out = _moe(hn, lg, h, w_exp_in[0].astype(bf), w_exp_out[0].astype(bf), g_final.reshape(1, d))
    return out.reshape(b, s, d)
```

```python
import functools
import math

import jax
import jax.numpy as jnp
import numpy as np
from jax import lax
from jax.experimental import pallas as pl
from jax.experimental.pallas import tpu as pltpu

D_MODEL = 1024
MEM_LEN = 256
XA_HEADS = 4
XA_HEAD_DIM = 128
DIFF_HEADS = 4
DIFF_QK_DIM = 64
DIFF_V_DIM = 128
DSA_HEADS = 8
DSA_HEAD_DIM = 64
IDX_HEADS = 8
IDX_DIM = 64
TOPK_MAX = 256
ROPE_THETA = 500000.0
ROPE_FRACTION = 4
N_GROUPS = 4
EXPERTS_PER_GROUP = 4
N_EXPERTS = 16
D_EXPERT = 512
EPS = 1e-6

LANES = 128
SUBLANES = 8
VMEM_LIMIT = 56 * 1024 * 1024
NEG_BIG = -1e30
ONES_ROWS = 16
LOG2E = math.log2(math.e)

PROJ_TM = 512
PROJ_TN = 512
DIFF_T = 512
DSA_TQ = 256
DSA_TK = 512
MERGE_TM = 512

_NT = (((1,), (1,)), ((), ()))


def _dot(a, b):
    return jnp.dot(a, b, preferred_element_type=jnp.float32)


def _dot_nt(a, b):
    return lax.dot_general(a, b, _NT, preferred_element_type=jnp.float32)


def _rms(xf, g):
    return xf * lax.rsqrt(jnp.mean(xf * xf, axis=-1, keepdims=True) + EPS) * g


def _pair_lane_source():
    half, rot = IDX_DIM, IDX_DIM // ROPE_FRACTION
    hr = rot // 2
    src = np.zeros(LANES, np.int32)
    src[0:hr] = np.arange(0, hr)
    src[hr:rot] = half + np.arange(0, hr)
    src[rot:half] = np.arange(rot, half)
    src[half:half + hr] = np.arange(hr, rot)
    src[half + hr:half + rot] = half + np.arange(hr, rot)
    src[half + rot:] = half + np.arange(rot, half)
    return src


def _first_head_lanes(shape):
    half, rot = IDX_DIM, IDX_DIM // ROPE_FRACTION
    lane = lax.broadcasted_iota(jnp.int32, shape, len(shape) - 1) % LANES
    return (lane < rot // 2) | ((lane >= rot) & (lane < half + rot // 2))


def _rope_expanders():
    rot = IDX_DIM // ROPE_FRACTION
    e = np.zeros((2, LANES, LANES), np.float32)
    for l in list(range(rot)) + list(range(IDX_DIM, IDX_DIM + rot)):
        e[0, l % (rot // 2), l] = 1.0
        e[1, l % (rot // 2), l] = -1.0 if l < rot else 1.0
    return e


_ROPE_SEGMENTS = (0, 1, 3, 4, 6)


def _proj_kernel(x_ref, posr_ref, g_ref, inv_ref, e_ref, wa_ref, ws_ref,
                 p_ref, ikk_ref, dvt_ref, svt_ref, iwt_ref):
    xf = x_ref[...]
    tm = xf.shape[0]
    n = _rms(xf, g_ref[...]).astype(jnp.bfloat16)
    ang = posr_ref[0].astype(jnp.float32) * inv_ref[...]
    pad = jnp.zeros((LANES - ang.shape[0], tm), jnp.float32)
    ct = jnp.concatenate([jnp.cos(ang), pad], axis=0).T
    st = jnp.concatenate([jnp.sin(ang), pad], axis=0).T
    crot = sum(_dot(part, e_ref[0]) for part in _split3(ct))
    lane = lax.broadcasted_iota(jnp.int32, crot.shape, 1) % IDX_DIM
    cos_t = jnp.where(lane < IDX_DIM // ROPE_FRACTION, crot, 1.0)
    sin_t = sum(_dot(part, e_ref[1]) for part in _split3(st))

    def rope(v):
        k = v.shape[1] // LANES
        blocks = [pltpu.roll(v[:, b * LANES:(b + 1) * LANES], LANES // 2, 1) for b in range(k)]
        partner = jnp.concatenate(blocks, axis=1) if k > 1 else blocks[0]
        c = jnp.concatenate([cos_t] * k, axis=1) if k > 1 else cos_t
        s = jnp.concatenate([sin_t] * k, axis=1) if k > 1 else sin_t
        return v * c + partner * s

    small = _dot(n, ws_ref[...])
    ikk_ref[...] = rope(small[:, :LANES]).astype(jnp.bfloat16)
    iwt_ref[...] = (small[:, LANES:] * (IDX_HEADS ** -0.5)).T[:IDX_HEADS, :]

    for j in range(wa_ref.shape[1] // PROJ_TN):
        cols = slice(j * PROJ_TN, (j + 1) * PROJ_TN)
        acc = _dot(n, wa_ref[:, cols])
        p_ref[:, cols] = (rope(acc) if j in _ROPE_SEGMENTS else acc).astype(jnp.bfloat16)
        if j == 2:
            dvt_ref[0] = acc.T.astype(jnp.bfloat16)
        if j == 5:
            svt_ref[0] = acc.T.astype(jnp.bfloat16)


def _proj(x2, pos_rows, g_mix, inv_col, expand, w_a, w_s):
    t = x2.shape[0]
    once = pl.Buffered(1)
    return pl.pallas_call(
        _proj_kernel,
        grid=(t // PROJ_TM,),
        in_specs=[
            pl.BlockSpec((PROJ_TM, D_MODEL), lambda i: (i, 0)),
            pl.BlockSpec((1, 1, PROJ_TM), lambda i: (i, 0, 0)),
            pl.BlockSpec((1, D_MODEL), lambda i: (0, 0)),
            pl.BlockSpec(inv_col.shape, lambda i: (0, 0)),
            pl.BlockSpec(expand.shape, lambda i: (0, 0, 0)),
            pl.BlockSpec(w_a.shape, lambda i: (0, 0), pipeline_mode=once),
            pl.BlockSpec(w_s.shape, lambda i: (0, 0), pipeline_mode=once),
        ],
        out_specs=[
            pl.BlockSpec((PROJ_TM, w_a.shape[1]), lambda i: (i, 0)),
            pl.BlockSpec((PROJ_TM, LANES), lambda i: (i, 0)),
            pl.BlockSpec((1, PROJ_TN, PROJ_TM), lambda i: (i, 0, 0)),
            pl.BlockSpec((1, PROJ_TN, PROJ_TM), lambda i: (i, 0, 0)),
            pl.BlockSpec((IDX_HEADS, PROJ_TM), lambda i: (0, i)),
        ],
        out_shape=[
            jax.ShapeDtypeStruct((t, w_a.shape[1]), jnp.bfloat16),
            jax.ShapeDtypeStruct((t, LANES), jnp.bfloat16),
            jax.ShapeDtypeStruct((t // PROJ_TM, PROJ_TN, PROJ_TM), jnp.bfloat16),
            jax.ShapeDtypeStruct((t // PROJ_TM, PROJ_TN, PROJ_TM), jnp.bfloat16),
            jax.ShapeDtypeStruct((IDX_HEADS, t), jnp.float32),
        ],
        compiler_params=pltpu.CompilerParams(
            dimension_semantics=("parallel",), vmem_limit_bytes=VMEM_LIMIT),
        name="proj",
    )(x2, pos_rows, g_mix, inv_col, expand, w_a, w_s)


def _diff_kernel(q_ref, k_ref, vt_ref, lq1_ref, lk1_ref, lq2_ref, lk2_ref, gs_ref, o_ref,
                 qm, m_s, acc, st_a, st_b, p_scr, *, lam_init):
    qi = pl.program_id(2)
    t = DIFF_T
    q = q_ref[...]
    first = _first_head_lanes(q.shape)
    qm[0] = jnp.where(first, q, jnp.zeros_like(q))
    qm[1] = jnp.where(first, jnp.zeros_like(q), q)
    m_s[...] = jnp.full(m_s.shape, NEG_BIG, jnp.float32)
    acc[...] = jnp.zeros(acc.shape, jnp.float32)

    def qk(j, st_ref):
        k = k_ref[pl.ds(pl.multiple_of(j * t, t), t), :]
        for i in range(2):
            st_ref[i] = _dot_nt(k, qm[i])

    def softmax_pv(j, st_ref, diagonal):
        vt = jnp.concatenate([vt_ref[j], jnp.ones((ONES_ROWS, t), jnp.bfloat16)], axis=0)
        if diagonal:
            krow = lax.broadcasted_iota(jnp.int32, (t, t), 0)
            qcol = lax.broadcasted_iota(jnp.int32, (t, t), 1)
            keep = krow <= qcol
        alphas = []
        for i in range(2):
            st = st_ref[i]
            if diagonal:
                st = jnp.where(keep, st, NEG_BIG)
            m_old = m_s[i]
            m_new = jnp.maximum(m_old, jnp.max(st, axis=0, keepdims=True))
            alphas.append(jnp.exp2(m_old - m_new))
            m_s[i] = m_new
            p_scr[i] = jnp.exp2(st - m_new).astype(jnp.bfloat16)
        for i in range(2):
            acc[i] = alphas[i] * acc[i] + _dot(vt, p_scr[i])

    qk(0, st_a)

    def pair(tt, carry):
        j = 2 * tt
        qk(j + 1, st_b)
        softmax_pv(j, st_a, False)
        qk(j + 2, st_a)
        softmax_pv(j + 1, st_b, False)
        return carry

    lax.fori_loop(0, qi // 2, pair, 0)

    @pl.when(qi % 2 == 0)
    def _():
        softmax_pv(qi, st_a, True)

    @pl.when(qi % 2 == 1)
    def _():
        qk(qi, st_b)
        softmax_pv(qi - 1, st_a, False)
        softmax_pv(qi, st_b, True)

    lam = (jnp.exp(jnp.sum(lq1_ref[...] * lk1_ref[...], axis=1, keepdims=True))
           - jnp.exp(jnp.sum(lq2_ref[...] * lk2_ref[...], axis=1, keepdims=True))
           + lam_init)
    dv = DIFF_V_DIM
    ot = (acc[0, :dv, :] / acc[0, dv:dv + 1, :]
          - lam * (acc[1, :dv, :] / acc[1, dv:dv + 1, :]))
    yt = ot * lax.rsqrt(jnp.mean(ot * ot, axis=0, keepdims=True) + EPS) * gs_ref[...]
    o_ref[...] = (yt * (1.0 - lam_init)).T.astype(o_ref.dtype)


def _diff_attention(p, dvt, lq1, lk1, lq2, lk2, g_sub_col, batch, seq, lam_init):
    nb = seq // DIFF_T
    vec = pl.BlockSpec((1, DIFF_QK_DIM), lambda b, h, qi: (0, 0))
    return pl.pallas_call(
        functools.partial(_diff_kernel, lam_init=lam_init),
        grid=(batch, DIFF_HEADS, nb),
        in_specs=[
            pl.BlockSpec((DIFF_T, LANES), lambda b, h, qi: (b * nb + qi, h)),
            pl.BlockSpec((seq, LANES), lambda b, h, qi: (b, DIFF_HEADS + h)),
            pl.BlockSpec((nb, DIFF_V_DIM, DIFF_T), lambda b, h, qi: (b, h, 0)),
            vec, vec, vec, vec,
            pl.BlockSpec((DIFF_V_DIM, 1), lambda b, h, qi: (0, 0)),
        ],
        out_specs=pl.BlockSpec((DIFF_T, LANES), lambda b, h, qi: (b * nb + qi, h)),
        out_shape=jax.ShapeDtypeStruct((batch * seq, DIFF_HEADS * DIFF_V_DIM), jnp.bfloat16),
        scratch_shapes=[
            pltpu.VMEM((2, DIFF_T, LANES), jnp.bfloat16),
            pltpu.VMEM((2, 1, DIFF_T), jnp.float32),
            pltpu.VMEM((2, DIFF_V_DIM + ONES_ROWS, DIFF_T), jnp.float32),
            pltpu.VMEM((2, DIFF_T, DIFF_T), jnp.float32),
            pltpu.VMEM((2, DIFF_T, DIFF_T), jnp.float32),
            pltpu.VMEM((2, DIFF_T, DIFF_T), jnp.bfloat16),
        ],
        compiler_params=pltpu.CompilerParams(
            dimension_semantics=("parallel", "parallel", "arbitrary"),
            vmem_limit_bytes=VMEM_LIMIT),
        name="diffattn",
    )(p, p, dvt, lq1, lk1, lq2, lk2, g_sub_col)


def _key_to_float(key):
    bits = jnp.where(key >= 0, key, key ^ jnp.int32(0x7FFFFFFF))
    return lax.bitcast_convert_type(bits, jnp.float32)


def _count_rows(hit):
    tk, tq = hit.shape
    return jnp.sum(hit.reshape(tk // (4 * SUBLANES), 4 * SUBLANES, tq), axis=0)


def _dsa_kernel(sq_ref, iq_ref, iwt_ref, ikk_ref, sk_ref, svt_ref, o_ref,
                score, hi16, lo16, iqm, sqm, acc, m_s, thr, st_a, st_b, mx_a, mx_b, p_scr, *, seq, top_k):
    qi = pl.program_id(1)
    tq, tk = DSA_TQ, DSA_TK
    last = (qi * tq) // tk
    nkc = last + 1
    first = _first_head_lanes((tq, LANES))

    for h in range(DSA_HEADS):
        pr = h // 2
        iqp = iq_ref[:, pr * LANES:(pr + 1) * LANES]
        sqp = sq_ref[:, pr * LANES:(pr + 1) * LANES]
        mine = first if h % 2 == 0 else jnp.logical_not(first)
        iqm[h] = jnp.where(mine, iqp, jnp.zeros_like(iqp))
        sqm[h] = jnp.where(mine, sqp, jnp.zeros_like(sqp))

    krow = lax.broadcasted_iota(jnp.int32, (tk, tq), 0)
    qcol = lax.broadcasted_iota(jnp.int32, (tk, tq), 1)

    def idx_logits(c, lg_ref):
        kk = ikk_ref[pl.ds(pl.multiple_of(c * tk, tk), tk), :]
        for h in range(IDX_HEADS):
            lg_ref[h] = _dot_nt(kk, iqm[h])

    def idx_score(c, lg_ref):
        sc = jnp.zeros((tk, tq), jnp.float32)
        for h in range(IDX_HEADS):
            sc = sc + iwt_ref[h:h + 1, :] * jnp.maximum(lg_ref[h], 0.0)
        sc = jnp.where(krow <= qcol + (qi * tq - c * tk), sc, -jnp.inf)
        score[c] = sc
        bits = lax.bitcast_convert_type(sc, jnp.int32)
        okey = jnp.where(bits >= 0, bits, bits ^ jnp.int32(0x7FFFFFFF))
        hi16[c] = lax.shift_right_arithmetic(okey, 16).astype(jnp.int16)
        lo16[c] = ((okey & 0xFFFF) - 2 ** 15).astype(jnp.int16)

    idx_logits(0, st_a)

    def idx_pair(tt, carry):
        c = 2 * tt
        idx_logits(c + 1, st_b)
        idx_score(c, st_a)
        idx_logits(c + 2, st_a)
        idx_score(c + 1, st_b)
        return carry

    lax.fori_loop(0, last // 2, idx_pair, 0)

    @pl.when(last % 2 == 0)
    def _():
        idx_score(last, st_a)

    @pl.when(last % 2 == 1)
    def _():
        idx_logits(last, st_b)
        idx_score(last - 1, st_a)
        idx_score(last, st_b)

    zero_cnt = jnp.zeros((4 * SUBLANES, tq), jnp.float32)
    i16_min = -2 ** 15

    def count16(buf, pred):
        def body(c, cnt):
            hit = jnp.where(pred(buf[c]), jnp.int16(1), jnp.int16(0))
            h3 = hit.reshape(tk // (4 * SUBLANES), 4 * SUBLANES, tq)
            part = h3[0]
            for r in range(1, h3.shape[0]):
                part = part + h3[r]
            return cnt + part
        cnt = lax.fori_loop(0, nkc, body, jnp.zeros((4 * SUBLANES, tq), jnp.int16))
        return jnp.sum(cnt.astype(jnp.int32), axis=0, keepdims=True)

    def bisect16(buf, want):
        def bit_step(i, cur):
            cand = cur + lax.shift_left(jnp.int32(1), jnp.int32(15) - i)
            c16 = cand.astype(jnp.int16)
            return jnp.where(count16(buf, lambda blk: blk >= c16) >= want, cand, cur)
        return lax.fori_loop(0, 16, bit_step, jnp.full((1, tq), i16_min, jnp.int32))

    t_hi = bisect16(hi16, top_k)
    t_hi16 = t_hi.astype(jnp.int16)
    rest = top_k - count16(hi16, lambda blk: blk > t_hi16)

    def bucket_only(c, carry):
        lo16[c] = jnp.where(hi16[c] == t_hi16, lo16[c], jnp.int16(i16_min))
        return carry

    lax.fori_loop(0, nkc, bucket_only, 0)
    t_lo = bisect16(lo16, rest)
    key = lax.shift_left(t_hi, 16) + (t_lo - i16_min)
    t_f = _key_to_float(key)

    def count_gt_ge(c, carry):
        gt, ge = carry
        blk = score[c]
        return (gt + _count_rows(jnp.where(blk > t_f, 1.0, 0.0)),
                ge + _count_rows(jnp.where(blk >= t_f, 1.0, 0.0)))

    gt, ge = lax.fori_loop(0, nkc, count_gt_ge, (zero_cnt, zero_cnt))
    n_gt = jnp.sum(gt, axis=0, keepdims=True)
    n_ge = jnp.sum(ge, axis=0, keepdims=True)
    q_pos = qi * tq + lax.broadcasted_iota(jnp.int32, (1, tq), 1)
    few = q_pos < top_k - 1
    thr[...] = jnp.where(few, float(jnp.finfo(jnp.float32).min), t_f)
    need = float(top_k) - n_gt
    split = jnp.logical_and(jnp.logical_not(few), n_ge > float(top_k))

    @pl.when(jnp.max(jnp.where(split, 1.0, 0.0)) > 0.0)
    def _():
        def count_eq_below(jc):
            def body(c, cnt):
                hit = jnp.where((score[c] == t_f) & (c * tk + krow < jc), 1.0, 0.0)
                return cnt + _count_rows(hit)
            return jnp.sum(lax.fori_loop(0, nkc, body, zero_cnt), axis=0, keepdims=True)

        nbits = (seq - 1).bit_length()

        def jbit(i, jv):
            cand = jv + lax.shift_left(jnp.int32(1), jnp.int32(nbits - 1) - i)
            return jnp.where(count_eq_below(cand) < need, cand, jv)

        jv = lax.fori_loop(0, nbits, jbit, jnp.zeros((1, tq), jnp.int32))

        def drop_ties(c, carry):
            blk = score[c]
            score[c] = jnp.where(split & (blk == t_f) & (c * tk + krow > jv), -jnp.inf, blk)
            return carry

        lax.fori_loop(0, nkc, drop_ties, 0)

    m_s[...] = jnp.full(m_s.shape, NEG_BIG, jnp.float32)
    acc[...] = jnp.zeros(acc.shape, jnp.float32)

    def qk(c, slot):
        st_ref, mx_ref = slot
        off = pl.multiple_of(c * tk, tk)
        bias = jnp.where(score[c] >= thr[...], 0.0, NEG_BIG)
        for h in range(DSA_HEADS):
            kp = sk_ref[pl.ds(off, tk), (h // 2) * LANES:(h // 2 + 1) * LANES]
            st = _dot_nt(kp, sqm[h]) + bias
            st_ref[h] = st
            mx_ref[h] = jnp.max(st, axis=0, keepdims=True)

    def softmax_pv(c, slot):
        st_ref, mx_ref = slot
        alphas = []
        for h in range(DSA_HEADS):
            m_old = m_s[h]
            m_new = jnp.maximum(m_old, mx_ref[h])
            alphas.append(jnp.exp2(m_old - m_new))
            m_s[h] = m_new
            p_scr[h] = jnp.exp2(st_ref[h] - m_new).astype(jnp.bfloat16)
        ones = jnp.ones((ONES_ROWS, tk), jnp.bfloat16)
        for h in range(DSA_HEADS):
            vt = jnp.concatenate([svt_ref[c, h * DSA_HEAD_DIM:(h + 1) * DSA_HEAD_DIM, :], ones], axis=0)
            acc[h] = alphas[h] * acc[h] + _dot(vt, p_scr[h])

    slot_a, slot_b = (st_a, mx_a), (st_b, mx_b)
    qk(0, slot_a)

    def pair(tt, carry):
        c = 2 * tt
        qk(c + 1, slot_b)
        softmax_pv(c, slot_a)
        qk(c + 2, slot_a)
        softmax_pv(c + 1, slot_b)
        return carry

    lax.fori_loop(0, last // 2, pair, 0)

    @pl.when(last % 2 == 0)
    def _():
        softmax_pv(last, slot_a)

    @pl.when(last % 2 == 1)
    def _():
        qk(last, slot_b)
        softmax_pv(last - 1, slot_a)
        softmax_pv(last, slot_b)

    dh = DSA_HEAD_DIM
    outs = [acc[h, :dh, :] / acc[h, dh:dh + 1, :] for h in range(DSA_HEADS)]
    o_ref[...] = jnp.concatenate(outs, axis=0).T.astype(o_ref.dtype)


def _dsa_attention(p, ikk, iwt, svt, batch, seq, top_k):
    nq = seq // DSA_TQ
    w = DSA_HEADS * DSA_HEAD_DIM
    once = pl.Buffered(1)
    return pl.pallas_call(
        functools.partial(_dsa_kernel, seq=seq, top_k=top_k),
        grid=(batch, nq),
        in_specs=[
            pl.BlockSpec((DSA_TQ, w), lambda b, qi: (b * nq + qi, 3)),
            pl.BlockSpec((DSA_TQ, w), lambda b, qi: (b * nq + qi, 6)),
            pl.BlockSpec((IDX_HEADS, DSA_TQ), lambda b, qi: (0, b * nq + qi)),
            pl.BlockSpec((seq, LANES), lambda b, qi: (b, 0), pipeline_mode=once),
            pl.BlockSpec((seq, w), lambda b, qi: (b, 4), pipeline_mode=once),
            pl.BlockSpec((seq // DSA_TK, w, DSA_TK), lambda b, qi: (b, 0, 0), pipeline_mode=once),
        ],
        out_specs=pl.BlockSpec((DSA_TQ, w), lambda b, qi: (b * nq + qi, 0)),
        out_shape=jax.ShapeDtypeStruct((batch * seq, w), jnp.bfloat16),
        scratch_shapes=[
            pltpu.VMEM((seq // DSA_TK, DSA_TK, DSA_TQ), jnp.float32),
            pltpu.VMEM((seq // DSA_TK, DSA_TK, DSA_TQ), jnp.int16),
            pltpu.VMEM((seq // DSA_TK, DSA_TK, DSA_TQ), jnp.int16),
            pltpu.VMEM((IDX_HEADS, DSA_TQ, LANES), jnp.bfloat16),
            pltpu.VMEM((DSA_HEADS, DSA_TQ, LANES), jnp.bfloat16),
            pltpu.VMEM((DSA_HEADS, DSA_HEAD_DIM + ONES_ROWS, DSA_TQ), jnp.float32),
            pltpu.VMEM((DSA_HEADS, 1, DSA_TQ), jnp.float32),
            pltpu.VMEM((1, DSA_TQ), jnp.float32),
            pltpu.VMEM((DSA_HEADS, DSA_TK, DSA_TQ), jnp.float32),
            pltpu.VMEM((DSA_HEADS, DSA_TK, DSA_TQ), jnp.float32),
            pltpu.VMEM((DSA_HEADS, 1, DSA_TQ), jnp.float32),
            pltpu.VMEM((DSA_HEADS, 1, DSA_TQ), jnp.float32),
            pltpu.VMEM((DSA_HEADS, DSA_TK, DSA_TQ), jnp.bfloat16),
        ],
        compiler_params=pltpu.CompilerParams(
            dimension_semantics=("parallel", "arbitrary"), vmem_limit_bytes=VMEM_LIMIT),
        name="dsa",
    )(p, p, iwt, ikk, p, svt)


def _memkv_kernel(mem_ref, g_ref, w_ref, o_ref):
    n = _rms(mem_ref[...], g_ref[...]).astype(jnp.bfloat16)
    o_ref[...] = _dot(n, w_ref[...]).astype(o_ref.dtype)


def _memkv(mem2, g_mem, w_kv, batch):
    return pl.pallas_call(
        _memkv_kernel,
        grid=(batch,),
        in_specs=[
            pl.BlockSpec((MEM_LEN, D_MODEL), lambda b: (b, 0)),
            pl.BlockSpec((1, D_MODEL), lambda b: (0, 0)),
            pl.BlockSpec(w_kv.shape, lambda b: (0, 0)),
        ],
        out_specs=pl.BlockSpec((MEM_LEN, w_kv.shape[1]), lambda b: (b, 0)),
        out_shape=jax.ShapeDtypeStruct((batch * MEM_LEN, w_kv.shape[1]), jnp.bfloat16),
        compiler_params=pltpu.CompilerParams(
            dimension_semantics=("parallel",), vmem_limit_bytes=VMEM_LIMIT),
        name="memkv",
    )(mem2, g_mem, w_kv)


def _mem_attention(q, kv):
    scale = XA_HEAD_DIM ** -0.5
    outs = []
    for h in range(XA_HEADS):
        qh = q[:, h * LANES:(h + 1) * LANES]
        k = kv[:, h * LANES:(h + 1) * LANES]
        v = kv[:, (XA_HEADS + h) * LANES:(XA_HEADS + h + 1) * LANES]
        s = _dot_nt(qh, k) * scale
        m = jnp.max(s, axis=1, keepdims=True)
        p = jnp.exp(s - m)
        l = jnp.sum(p, axis=1, keepdims=True)
        outs.append(_dot((p / l).astype(jnp.bfloat16), v))
    return jnp.concatenate(outs, axis=1).astype(jnp.bfloat16)


def _merge_kernel(x_ref, yd_ref, ys_ref, mq_ref, mkv_ref, gmix_ref, wg_ref, bg_ref, wbr_ref, wout_ref,
                  gffn_ref, wr_ref, br_ref, h_ref, hn_ref, lg_ref):
    xf = x_ref[...]
    n = _rms(xf, gmix_ref[...]).astype(jnp.bfloat16)
    branches = (yd_ref[...], ys_ref[...], _mem_attention(mq_ref[...], mkv_ref[...]))
    merged = jnp.zeros(xf.shape, jnp.float32)
    for i, y in enumerate(branches):
        gate = jax.nn.sigmoid(_dot(n, wg_ref[i]) + bg_ref[i])
        merged = merged + gate * _dot(y, wbr_ref[i])
    h = xf + _dot(merged.astype(jnp.bfloat16), wout_ref[...])
    h_ref[...] = h
    hn = _rms(h, gffn_ref[...]).astype(jnp.bfloat16)
    hn_ref[...] = hn
    lg_ref[...] = _dot(hn, wr_ref[...]) + br_ref[...]


def _merge(x2, y_diff, y_dsa, p, mkv, seq, g_mix, w_g, b_g, w_br, w_out, g_ffn, w_r, b_r):
    t = x2.shape[0]
    tiles_per_seq = seq // MERGE_TM
    row = lambda w: pl.BlockSpec((MERGE_TM, w), lambda i: (i, 0))
    full = lambda a: pl.BlockSpec(a.shape, lambda i: (0,) * a.ndim, pipeline_mode=pl.Buffered(1))
    return pl.pallas_call(
        _merge_kernel,
        grid=(t // MERGE_TM,),
        in_specs=[row(D_MODEL), row(512), row(512),
                  pl.BlockSpec((MERGE_TM, 512), lambda i: (i, 7)),
                  pl.BlockSpec((MEM_LEN, mkv.shape[1]), lambda i: (i // tiles_per_seq, 0)),
                  full(g_mix), full(w_g), full(b_g), full(w_br), full(w_out), full(g_ffn), full(w_r), full(b_r)],
        out_specs=[row(D_MODEL), row(D_MODEL), row(LANES)],
        out_shape=[
            jax.ShapeDtypeStruct((t, D_MODEL), jnp.float32),
            jax.ShapeDtypeStruct((t, D_MODEL), jnp.bfloat16),
            jax.ShapeDtypeStruct((t, LANES), jnp.float32),
        ],
        compiler_params=pltpu.CompilerParams(
            dimension_semantics=("parallel",), vmem_limit_bytes=VMEM_LIMIT),
        name="merge",
    )(x2, y_diff, y_dsa, p, mkv, g_mix, w_g, b_g, w_br, w_out, g_ffn, w_r, b_r)


MOE_ST = 2048
MOE_SUB = 512
MOE_ALIGN = 16
MOE_BLK = 256
MOE_NSUB = MOE_ST // MOE_SUB
MOE_ZROWS = MOE_SUB + N_GROUPS * MOE_ALIGN
MOE_ZK = -(-MOE_ZROWS // LANES) * LANES
MOE_GROWS = MOE_ST + MOE_NSUB * N_GROUPS * MOE_ALIGN + MOE_BLK
_SEG_ROWS, _SEG_LOCAL, _SEG_GLOBAL = 0, 16, 32
_GRP_BASE, _GRP_ROWS = 48, 52


def _route(lg):
    lane = lax.broadcasted_iota(jnp.int32, lg.shape, 1).astype(jnp.float32)
    big = float(LANES)
    gl = jnp.where(lane < N_GROUPS, lg, -jnp.inf)
    gmax = jnp.max(gl, axis=1, keepdims=True)
    grp = jnp.min(jnp.where(gl == gmax, lane, big), axis=1, keepdims=True)
    gsum = jnp.sum(jnp.where(lane < N_GROUPS, jnp.exp(gl - gmax), 0.0), axis=1, keepdims=True)
    p_grp = 1.0 / gsum
    lo = N_GROUPS + grp * EXPERTS_PER_GROUP
    el = jnp.where((lane >= lo) & (lane < lo + EXPERTS_PER_GROUP), lg, -jnp.inf)
    e1 = jnp.max(el, axis=1, keepdims=True)
    i1 = jnp.min(jnp.where(el == e1, lane, big), axis=1, keepdims=True)
    el2 = jnp.where(lane == i1, -jnp.inf, el)
    e2 = jnp.max(el2, axis=1, keepdims=True)
    i2 = jnp.min(jnp.where(el2 == e2, lane, big), axis=1, keepdims=True)
    r = jnp.exp(e2 - e1)
    w1 = p_grp / (1.0 + r)
    w2 = p_grp * r / (1.0 + r)
    return jnp.where(lane == i1, w1, 0.0) + jnp.where(lane == i2, w2, 0.0), grp


def _split3(x):
    a = x.astype(jnp.bfloat16)
    r = x - a.astype(jnp.float32)
    b = r.astype(jnp.bfloat16)
    c = (r - b.astype(jnp.float32)).astype(jnp.bfloat16)
    return a, b, c


def _moe_kernel(hn_ref, lg_ref, h_ref, gfin_ref, win_hbm, wout_hbm, o_ref,
                g_buf, ys, cw_s, dest_s, z_buf, cwz_buf, zy_buf, win_buf, wout_buf, sem, seg):
    j = pl.program_id(1)
    sub, al, bf = MOE_SUB, MOE_ALIGN, jnp.bfloat16
    lane_row = lax.broadcasted_iota(jnp.int32, (1, LANES), 1)

    def copy_rows(src_ref, dst_ref, src, dst, nblk):
        def body(i, carry):
            s0 = pl.multiple_of(src + i * al, al)
            d0 = pl.multiple_of(dst + i * al, al)
            for s_ref, d_ref in zip(src_ref, dst_ref):
                d_ref[pl.ds(d0, al), :] = s_ref[pl.ds(s0, al), :]
            return carry
        lax.fori_loop(0, nblk, body, 0)

    @pl.when(j == 0)
    def _():
        cw, grp = _route(lg_ref[...])
        lane = lax.broadcasted_iota(jnp.int32, (sub, LANES), 1).astype(jnp.float32)
        hots = [lane == grp[u * sub:(u + 1) * sub] for u in range(MOE_NSUB)]
        for u in range(MOE_NSUB):
            n_vec = jnp.sum(jnp.where(hots[u], 1.0, 0.0), axis=0, keepdims=True)
            off = jnp.int32(0)
            for g in range(N_GROUPS):
                n = jnp.sum(jnp.where(lane_row == g, n_vec, 0.0)).astype(jnp.int32)
                rows = ((n + (al - 1)) // al) * al
                seg[_SEG_ROWS + u * N_GROUPS + g] = rows
                seg[_SEG_LOCAL + u * N_GROUPS + g] = off
                off = off + rows
        base = jnp.int32(0)
        for g in range(N_GROUPS):
            seg[_GRP_BASE + g] = base
            pos = base
            for u in range(MOE_NSUB):
                seg[_SEG_GLOBAL + u * N_GROUPS + g] = pos
                pos = pos + seg[_SEG_ROWS + u * N_GROUPS + g]
            seg[_GRP_ROWS + g] = pos - base
            base = pos

        ri = lax.broadcasted_iota(jnp.int32, (sub, sub), 0)
        ci = lax.broadcasted_iota(jnp.int32, (sub, sub), 1)
        earlier = jnp.where(ci < ri, 1.0, 0.0).astype(bf)
        zrow = lax.broadcasted_iota(jnp.int32, (MOE_ZROWS, sub), 0).astype(jnp.float32)
        zy_buf[...] = jnp.zeros(zy_buf.shape, jnp.float32)
        for u in range(MOE_NSUB):
            hot = hots[u]
            before = _dot(earlier, jnp.where(hot, 1.0, 0.0).astype(bf))
            start = jnp.zeros((1, LANES), jnp.float32)
            for g in range(N_GROUPS):
                start = jnp.where(lane_row == g, seg[_SEG_LOCAL + u * N_GROUPS + g].astype(jnp.float32), start)
            dest = jnp.sum(jnp.where(hot, before + start, 0.0), axis=1, keepdims=True)
            dest_b = jnp.broadcast_to(dest, (sub, LANES))
            dest_s[u] = dest_b
            perm = jnp.where(zrow == dest_b.T[0:1, :], 1.0, 0.0).astype(bf)
            z_buf[...] = _dot(perm, hn_ref[u * sub:(u + 1) * sub, :]).astype(bf)
            c3 = _split3(cw[u * sub:(u + 1) * sub])
            cwz_buf[...] = _dot(perm, c3[0]) + _dot(perm, c3[1]) + _dot(perm, c3[2])
            for g in range(N_GROUPS):
                k = u * N_GROUPS + g
                copy_rows((z_buf, cwz_buf), (g_buf, cw_s), seg[_SEG_LOCAL + k], seg[_SEG_GLOBAL + k],
                          seg[_SEG_ROWS + k] // al)

        def weights(e, slot):
            return (pltpu.make_async_copy(win_hbm.at[e], win_buf.at[slot], sem.at[0, slot]),
                    pltpu.make_async_copy(wout_hbm.at[e], wout_buf.at[slot], sem.at[1, slot]))

        for cp in weights(0, 0):
            cp.start()

        def expert(e, carry):
            slot = e % 2
            g = e // EXPERTS_PER_GROUP

            @pl.when(e + 1 < N_EXPERTS)
            def _():
                for cp in weights(e + 1, 1 - slot):
                    cp.start()

            for cp in weights(e, slot):
                cp.wait()
            start = seg[_GRP_BASE + g]
            rows = seg[_GRP_ROWS + g]
            big = 2 * MOE_BLK
            nbig = rows // big
            rem = rows - nbig * big
            first = e % EXPERTS_PER_GROUP == 0

            def block(row0, n):
                row0 = pl.multiple_of(row0, al)
                gu = _dot(g_buf[pl.ds(row0, n), :], win_buf[slot])
                gt = gu[:, :D_EXPERT]
                hid = (gt * jax.nn.sigmoid(gt)) * gu[:, D_EXPERT:]
                y = _dot(hid.astype(bf), wout_buf[slot])
                ln = lax.broadcasted_iota(jnp.int32, (n, LANES), 1)
                cw_e = jnp.sum(jnp.where(ln == N_GROUPS + e, cw_s[pl.ds(row0, n), :], 0.0), axis=1, keepdims=True)

                @pl.when(first)
                def _():
                    ys[pl.ds(row0, n), :] = cw_e * y

                @pl.when(jnp.logical_not(first))
                def _():
                    ys[pl.ds(row0, n), :] += cw_e * y

            def big_block(b, c):
                block(start + b * big, big)
                return c

            lax.fori_loop(0, nbig, big_block, 0)

            tail = start + nbig * big

            @pl.when(rem > MOE_BLK)
            def _():
                block(tail, big)

            @pl.when((rem > MOE_BLK // 2) & (rem <= MOE_BLK))
            def _():
                block(tail, MOE_BLK)

            @pl.when((rem > 0) & (rem <= MOE_BLK // 2))
            def _():
                block(tail, MOE_BLK // 2)

            return carry

        lax.fori_loop(0, N_EXPERTS, expert, 0)

    @pl.when(j > 0)
    def _():
        u = j - 1
        for g in range(N_GROUPS):
            k = u * N_GROUPS + g
            copy_rows((ys,), (zy_buf,), seg[_SEG_GLOBAL + k], seg[_SEG_LOCAL + k], seg[_SEG_ROWS + k] // al)
        zy = zy_buf[...]
        hi = zy.astype(bf)
        lo = (zy - hi.astype(jnp.float32)).astype(bf)
        dest = jnp.concatenate([dest_s[u]] * (MOE_ZK // LANES), axis=1)
        col = lax.broadcasted_iota(jnp.int32, (sub, MOE_ZK), 1).astype(jnp.float32)
        unperm = jnp.where(col == dest, 1.0, 0.0).astype(bf)
        moe = _dot(unperm, hi) + _dot(unperm, lo)
        o_ref[...] = _rms(h_ref[...] + moe, gfin_ref[...])


def _moe(hn, lg, h, w_e_in, w_e_out, g_final):
    t = hn.shape[0]
    row = lambda s, j: (s * MOE_NSUB + jnp.maximum(j - 1, 0), 0)
    return pl.pallas_call(
        _moe_kernel,
        grid=(t // MOE_ST, 1 + MOE_NSUB),
        in_specs=[
            pl.BlockSpec((MOE_ST, D_MODEL), lambda s, j: (s, 0)),
            pl.BlockSpec((MOE_ST, LANES), lambda s, j: (s, 0)),
            pl.BlockSpec((MOE_SUB, D_MODEL), row),
            pl.BlockSpec((1, D_MODEL), lambda s, j: (0, 0)),
            pl.BlockSpec(memory_space=pl.ANY),
            pl.BlockSpec(memory_space=pl.ANY),
        ],
        out_specs=pl.BlockSpec((MOE_SUB, D_MODEL), row),
        out_shape=jax.ShapeDtypeStruct((t, D_MODEL), jnp.float32),
        scratch_shapes=[
            pltpu.VMEM((MOE_GROWS, D_MODEL), jnp.bfloat16),
            pltpu.VMEM((MOE_GROWS, D_MODEL), jnp.float32),
            pltpu.VMEM((MOE_GROWS, LANES), jnp.float32),
            pltpu.VMEM((MOE_NSUB, MOE_SUB, LANES), jnp.float32),
            pltpu.VMEM((MOE_ZROWS, D_MODEL), jnp.bfloat16),
            pltpu.VMEM((MOE_ZROWS, LANES), jnp.float32),
            pltpu.VMEM((MOE_ZK, D_MODEL), jnp.float32),
            pltpu.VMEM((2, D_MODEL, 2 * D_EXPERT), jnp.bfloat16),
            pltpu.VMEM((2, D_EXPERT, D_MODEL), jnp.bfloat16),
            pltpu.SemaphoreType.DMA((2, 2)),
            pltpu.SMEM((64,), jnp.int32),
        ],
        compiler_params=pltpu.CompilerParams(
            dimension_semantics=("parallel", "arbitrary"), vmem_limit_bytes=VMEM_LIMIT),
        name="moe",
    )(hn, lg, h, g_final, w_e_in, w_e_out)


def kernel(x, positions, mem, g_mix, w_in, b_gate, lambda_q1, lambda_k1, lambda_q2, lambda_k2,
           g_diff_sub, g_mem, w_mem_kv, w_br_diff, w_br_dsa, w_br_mem, w_out, g_ffn,
           w_route_group, b_route_group, w_route_expert, b_route_expert, w_exp_in, w_exp_out,
           g_final):
    b, s, d = x.shape
    t = b * s
    bf = jnp.bfloat16
    top_k = min(TOPK_MAX, s // 4)
    assert d == D_MODEL and s % DIFF_T == 0 and s % DSA_TQ == 0 and top_k <= DSA_TQ
    assert PROJ_TM == DIFF_T == DSA_TK and DSA_TK % DSA_TQ == 0, "proj writes V^T in the attention kernels' key tiles"
    assert g_mix.shape[0] == 1, "single layer"
    lam_init = 0.8 - 0.6 * math.exp(-0.3 * 0)

    wi = w_in[0]
    c = 512
    seg = lambda k: wi[:, k * c:(k + 1) * c]
    o_ik = 7 * c
    w_ik = wi[:, o_ik:o_ik + IDX_DIM]
    w_iw = wi[:, o_ik + IDX_DIM:o_ik + IDX_DIM + IDX_HEADS]
    o_mq = o_ik + IDX_DIM + IDX_HEADS
    w_mq = wi[:, o_mq:o_mq + c]
    w_gl = wi[:, o_mq + c:]
    qs = DIFF_QK_DIM ** -0.5
    qs2 = qs * LOG2E
    src = _pair_lane_source()
    src_seg = np.concatenate([blk * LANES + src for blk in range(c // LANES)])
    rseg = lambda k: seg(k)[:, src_seg]
    w_a = jnp.concatenate([rseg(0) * qs2, rseg(1), seg(2), rseg(3) * qs2, rseg(4), seg(5), rseg(6) * qs, w_mq],
                          axis=1).astype(bf)
    w_s = jnp.concatenate([jnp.concatenate([w_ik, w_ik], axis=1)[:, src], w_iw,
                           jnp.zeros((d, LANES - IDX_HEADS), wi.dtype)], axis=1).astype(bf)
    w_g = w_gl.reshape(d, 3, d).transpose(1, 0, 2).astype(bf)
    b_g = b_gate[0].reshape(3, 1, d)
    w_br = jnp.stack([w_br_diff[0], w_br_dsa[0], w_br_mem[0]]).astype(bf)
    w_r = jnp.concatenate([w_route_group[0], w_route_expert[0],
                           jnp.zeros((d, LANES - N_GROUPS - N_EXPERTS), wi.dtype)], axis=1).astype(bf)
    b_r = jnp.concatenate([b_route_group[0], b_route_expert[0],
                           jnp.zeros((LANES - N_GROUPS - N_EXPERTS,), jnp.float32)]).reshape(1, LANES)

    rot = IDX_DIM // ROPE_FRACTION
    inv_freq = ROPE_THETA ** (-jnp.arange(0, rot, 2, dtype=jnp.float32) / rot)
    inv_col = inv_freq.reshape(rot // 2, 1)
    expand = jnp.asarray(_rope_expanders(), bf)

    x2 = x.reshape(t, d)
    pos_rows = positions.reshape(t // PROJ_TM, 1, PROJ_TM)
    p, ikk, dvt, svt, iwt = _proj(x2, pos_rows, g_mix, inv_col, expand, w_a, w_s)

    y_diff = _diff_attention(p, dvt, lambda_q1, lambda_k1, lambda_q2, lambda_k2,
                             g_diff_sub.reshape(DIFF_V_DIM, 1), b, s, lam_init)
    y_dsa = _dsa_attention(p, ikk, iwt, svt, b, s, top_k)
    mkv = _memkv(mem.reshape(b * MEM_LEN, d), g_mem, w_mem_kv[0].astype(bf), b)
    h, hn, lg = _merge(x2, y_diff, y_dsa, p, mkv, s, g_mix, w_g, b_g, w_br, w_out[0].astype(bf), g_ffn, w_r, b_r)
    out = _moe(hn, lg, h, w_exp_in[0].astype(bf), w_exp_out[0].astype(bf), g_final.reshape(1, d))
    return out.reshape(b, s, d)
```

```python
import functools
import math

import jax
import jax.numpy as jnp
import numpy as np
from jax import lax
from jax.experimental import pallas as pl
from jax.experimental.pallas import tpu as pltpu

D_MODEL = 1024
MEM_LEN = 256
XA_HEADS = 4
XA_HEAD_DIM = 128
DIFF_HEADS = 4
DIFF_QK_DIM = 64
DIFF_V_DIM = 128
DSA_HEADS = 8
DSA_HEAD_DIM = 64
IDX_HEADS = 8
IDX_DIM = 64
TOPK_MAX = 256
ROPE_THETA = 500000.0
ROPE_FRACTION = 4
N_GROUPS = 4
EXPERTS_PER_GROUP = 4
N_EXPERTS = 16
D_EXPERT = 512
EPS = 1e-6

LANES = 128
SUBLANES = 8
VMEM_LIMIT = 56 * 1024 * 1024
NEG_BIG = -1e30
ONES_ROWS = 16
LOG2E = math.log2(math.e)

PROJ_TM = 512
PROJ_TN = 512
DIFF_T = 512
DSA_TQ = 256
DSA_TK = 512
MERGE_TM = 512

_NT = (((1,), (1,)), ((), ()))


def _dot(a, b):
    return jnp.dot(a, b, preferred_element_type=jnp.float32)


def _dot_nt(a, b):
    return lax.dot_general(a, b, _NT, preferred_element_type=jnp.float32)


def _rms(xf, g):
    return xf * lax.rsqrt(jnp.mean(xf * xf, axis=-1, keepdims=True) + EPS) * g


def _pair_lane_source():
    half, rot = IDX_DIM, IDX_DIM // ROPE_FRACTION
    hr = rot // 2
    src = np.zeros(LANES, np.int32)
    src[0:hr] = np.arange(0, hr)
    src[hr:rot] = half + np.arange(0, hr)
    src[rot:half] = np.arange(rot, half)
    src[half:half + hr] = np.arange(hr, rot)
    src[half + hr:half + rot] = half + np.arange(hr, rot)
    src[half + rot:] = half + np.arange(rot, half)
    return src


def _first_head_lanes(shape):
    half, rot = IDX_DIM, IDX_DIM // ROPE_FRACTION
    lane = lax.broadcasted_iota(jnp.int32, shape, len(shape) - 1) % LANES
    return (lane < rot // 2) | ((lane >= rot) & (lane < half + rot // 2))


def _rope_expanders():
    rot = IDX_DIM // ROPE_FRACTION
    e = np.zeros((2, LANES, LANES), np.float32)
    for l in list(range(rot)) + list(range(IDX_DIM, IDX_DIM + rot)):
        e[0, l % (rot // 2), l] = 1.0
        e[1, l % (rot // 2), l] = -1.0 if l < rot else 1.0
    return e


_ROPE_SEGMENTS = (0, 1, 3, 4, 6)


def _proj_kernel(x_ref, posr_ref, g_ref, inv_ref, e_ref, wa_ref, ws_ref,
                 p_ref, ikk_ref, dvt_ref, svt_ref, iwt_ref):
    xf = x_ref[...]
    tm = xf.shape[0]
    n = _rms(xf, g_ref[...]).astype(jnp.bfloat16)
    ang = posr_ref[0].astype(jnp.float32) * inv_ref[...]
    pad = jnp.zeros((LANES - ang.shape[0], tm), jnp.float32)
    ct = jnp.concatenate([jnp.cos(ang), pad], axis=0).T
    st = jnp.concatenate([jnp.sin(ang), pad], axis=0).T
    crot = sum(_dot(part, e_ref[0]) for part in _split3(ct))
    lane = lax.broadcasted_iota(jnp.int32, crot.shape, 1) % IDX_DIM
    cos_t = jnp.where(lane < IDX_DIM // ROPE_FRACTION, crot, 1.0)
    sin_t = sum(_dot(part, e_ref[1]) for part in _split3(st))

    def rope(v):
        k = v.shape[1] // LANES
        blocks = [pltpu.roll(v[:, b * LANES:(b + 1) * LANES], LANES // 2, 1) for b in range(k)]
        partner = jnp.concatenate(blocks, axis=1) if k > 1 else blocks[0]
        c = jnp.concatenate([cos_t] * k, axis=1) if k > 1 else cos_t
        s = jnp.concatenate([sin_t] * k, axis=1) if k > 1 else sin_t
        return v * c + partner * s

    small = _dot(n, ws_ref[...])
    ikk_ref[...] = rope(small[:, :LANES]).astype(jnp.bfloat16)
    iwt_ref[...] = (small[:, LANES:] * (IDX_HEADS ** -0.5)).T[:IDX_HEADS, :]

    for j in range(wa_ref.shape[1] // PROJ_TN):
        cols = slice(j * PROJ_TN, (j + 1) * PROJ_TN)
        acc = _dot(n, wa_ref[:, cols])
        p_ref[:, cols] = (rope(acc) if j in _ROPE_SEGMENTS else acc).astype(jnp.bfloat16)
        if j == 2:
            dvt_ref[0] = acc.T.astype(jnp.bfloat16)
        if j == 5:
            svt_ref[0] = acc.T.astype(jnp.bfloat16)


def _proj(x2, pos_rows, g_mix, inv_col, expand, w_a, w_s):
    t = x2.shape[0]
    once = pl.Buffered(1)
    return pl.pallas_call(
        _proj_kernel,
        grid=(t // PROJ_TM,),
        in_specs=[
            pl.BlockSpec((PROJ_TM, D_MODEL), lambda i: (i, 0)),
            pl.BlockSpec((1, 1, PROJ_TM), lambda i: (i, 0, 0)),
            pl.BlockSpec((1, D_MODEL), lambda i: (0, 0)),
            pl.BlockSpec(inv_col.shape, lambda i: (0, 0)),
            pl.BlockSpec(expand.shape, lambda i: (0, 0, 0)),
            pl.BlockSpec(w_a.shape, lambda i: (0, 0), pipeline_mode=once),
            pl.BlockSpec(w_s.shape, lambda i: (0, 0), pipeline_mode=once),
        ],
        out_specs=[
            pl.BlockSpec((PROJ_TM, w_a.shape[1]), lambda i: (i, 0)),
            pl.BlockSpec((PROJ_TM, LANES), lambda i: (i, 0)),
            pl.BlockSpec((1, PROJ_TN, PROJ_TM), lambda i: (i, 0, 0)),
            pl.BlockSpec((1, PROJ_TN, PROJ_TM), lambda i: (i, 0, 0)),
            pl.BlockSpec((IDX_HEADS, PROJ_TM), lambda i: (0, i)),
        ],
        out_shape=[
            jax.ShapeDtypeStruct((t, w_a.shape[1]), jnp.bfloat16),
            jax.ShapeDtypeStruct((t, LANES), jnp.bfloat16),
            jax.ShapeDtypeStruct((t // PROJ_TM, PROJ_TN, PROJ_TM), jnp.bfloat16),
            jax.ShapeDtypeStruct((t // PROJ_TM, PROJ_TN, PROJ_TM), jnp.bfloat16),
            jax.ShapeDtypeStruct((IDX_HEADS, t), jnp.float32),
        ],
        compiler_params=pltpu.CompilerParams(
            dimension_semantics=("parallel",), vmem_limit_bytes=VMEM_LIMIT),
        name="proj",
    )(x2, pos_rows, g_mix, inv_col, expand, w_a, w_s)


def _diff_kernel(q_ref, k_ref, vt_ref, lq1_ref, lk1_ref, lq2_ref, lk2_ref, gs_ref, o_ref,
                 qm, m_s, acc, st_a, st_b, p_scr, *, lam_init):
    qi = pl.program_id(2)
    t = DIFF_T
    q = q_ref[...]
    first = _first_head_lanes(q.shape)
    qm[0] = jnp.where(first, q, jnp.zeros_like(q))
    qm[1] = jnp.where(first, jnp.zeros_like(q), q)
    m_s[...] = jnp.full(m_s.shape, NEG_BIG, jnp.float32)
    acc[...] = jnp.zeros(acc.shape, jnp.float32)

    def qk(j, st_ref):
        k = k_ref[pl.ds(pl.multiple_of(j * t, t), t), :]
        for i in range(2):
            st_ref[i] = _dot_nt(k, qm[i])

    def softmax_pv(j, st_ref, diagonal):
        vt = jnp.concatenate([vt_ref[j], jnp.ones((ONES_ROWS, t), jnp.bfloat16)], axis=0)
        if diagonal:
            krow = lax.broadcasted_iota(jnp.int32, (t, t), 0)
            qcol = lax.broadcasted_iota(jnp.int32, (t, t), 1)
            keep = krow <= qcol
        alphas = []
        for i in range(2):
            st = st_ref[i]
            if diagonal:
                st = jnp.where(keep, st, NEG_BIG)
            m_old = m_s[i]
            m_new = jnp.maximum(m_old, jnp.max(st, axis=0, keepdims=True))
            alphas.append(jnp.exp2(m_old - m_new))
            m_s[i] = m_new
            p_scr[i] = jnp.exp2(st - m_new).astype(jnp.bfloat16)
        for i in range(2):
            acc[i] = alphas[i] * acc[i] + _dot(vt, p_scr[i])

    qk(0, st_a)

    def pair(tt, carry):
        j = 2 * tt
        qk(j + 1, st_b)
        softmax_pv(j, st_a, False)
        qk(j + 2, st_a)
        softmax_pv(j + 1, st_b, False)
        return carry

    def quad(tt, carry):
        pair(2 * tt, carry)
        pair(2 * tt + 1, carry)
        return carry

    lax.fori_loop(0, qi // 4, quad, 0)
    lax.fori_loop(2 * (qi // 4), qi // 2, pair, 0)

    @pl.when(qi % 2 == 0)
    def _():
        softmax_pv(qi, st_a, True)

    @pl.when(qi % 2 == 1)
    def _():
        qk(qi, st_b)
        softmax_pv(qi - 1, st_a, False)
        softmax_pv(qi, st_b, True)

    lam = (jnp.exp(jnp.sum(lq1_ref[...] * lk1_ref[...], axis=1, keepdims=True))
           - jnp.exp(jnp.sum(lq2_ref[...] * lk2_ref[...], axis=1, keepdims=True))
           + lam_init)
    dv = DIFF_V_DIM
    ot = (acc[0, :dv, :] / acc[0, dv:dv + 1, :]
          - lam * (acc[1, :dv, :] / acc[1, dv:dv + 1, :]))
    yt = ot * lax.rsqrt(jnp.mean(ot * ot, axis=0, keepdims=True) + EPS) * gs_ref[...]
    o_ref[...] = (yt * (1.0 - lam_init)).T.astype(o_ref.dtype)


def _diff_attention(p, dvt, lq1, lk1, lq2, lk2, g_sub_col, batch, seq, lam_init):
    nb = seq // DIFF_T
    vec = pl.BlockSpec((1, DIFF_QK_DIM), lambda b, h, qi: (0, 0))
    return pl.pallas_call(
        functools.partial(_diff_kernel, lam_init=lam_init),
        grid=(batch, DIFF_HEADS, nb),
        in_specs=[
            pl.BlockSpec((DIFF_T, LANES), lambda b, h, qi: (b * nb + qi, h)),
            pl.BlockSpec((seq, LANES), lambda b, h, qi: (b, DIFF_HEADS + h)),
            pl.BlockSpec((nb, DIFF_V_DIM, DIFF_T), lambda b, h, qi: (b, h, 0)),
            vec, vec, vec, vec,
            pl.BlockSpec((DIFF_V_DIM, 1), lambda b, h, qi: (0, 0)),
        ],
        out_specs=pl.BlockSpec((DIFF_T, LANES), lambda b, h, qi: (b * nb + qi, h)),
        out_shape=jax.ShapeDtypeStruct((batch * seq, DIFF_HEADS * DIFF_V_DIM), jnp.bfloat16),
        scratch_shapes=[
            pltpu.VMEM((2, DIFF_T, LANES), jnp.bfloat16),
            pltpu.VMEM((2, 1, DIFF_T), jnp.float32),
            pltpu.VMEM((2, DIFF_V_DIM + ONES_ROWS, DIFF_T), jnp.float32),
            pltpu.VMEM((2, DIFF_T, DIFF_T), jnp.float32),
            pltpu.VMEM((2, DIFF_T, DIFF_T), jnp.float32),
            pltpu.VMEM((2, DIFF_T, DIFF_T), jnp.bfloat16),
        ],
        compiler_params=pltpu.CompilerParams(
            dimension_semantics=("parallel", "parallel", "arbitrary"),
            vmem_limit_bytes=VMEM_LIMIT),
        name="diffattn",
    )(p, p, dvt, lq1, lk1, lq2, lk2, g_sub_col)


def _key_to_float(key):
    bits = jnp.where(key >= 0, key, key ^ jnp.int32(0x7FFFFFFF))
    return lax.bitcast_convert_type(bits, jnp.float32)


def _count_rows(hit):
    tk, tq = hit.shape
    return jnp.sum(hit.reshape(tk // (4 * SUBLANES), 4 * SUBLANES, tq), axis=0)


def _dsa_kernel(sq_ref, iq_ref, iwt_ref, ikk_ref, sk_ref, svt_ref, o_ref,
                score, hi16, lo16, iqm, sqm, acc, m_s, thr, st_a, st_b, mx_a, mx_b, p_scr, *, seq, top_k):
    qi = pl.program_id(1)
    tq, tk = DSA_TQ, DSA_TK
    last = (qi * tq) // tk
    nkc = last + 1
    first = _first_head_lanes((tq, LANES))

    for h in range(DSA_HEADS):
        pr = h // 2
        iqp = iq_ref[:, pr * LANES:(pr + 1) * LANES]
        sqp = sq_ref[:, pr * LANES:(pr + 1) * LANES]
        mine = first if h % 2 == 0 else jnp.logical_not(first)
        iqm[h] = jnp.where(mine, iqp, jnp.zeros_like(iqp))
        sqm[h] = jnp.where(mine, sqp, jnp.zeros_like(sqp))

    krow = lax.broadcasted_iota(jnp.int32, (tk, tq), 0)
    qcol = lax.broadcasted_iota(jnp.int32, (tk, tq), 1)

    def idx_logits(c, lg_ref):
        kk = ikk_ref[pl.ds(pl.multiple_of(c * tk, tk), tk), :]
        for h in range(IDX_HEADS):
            lg_ref[h] = _dot_nt(kk, iqm[h])

    def idx_score(c, lg_ref):
        sc = jnp.zeros((tk, tq), jnp.float32)
        for h in range(IDX_HEADS):
            sc = sc + iwt_ref[h:h + 1, :] * jnp.maximum(lg_ref[h], 0.0)
        sc = jnp.where(krow <= qcol + (qi * tq - c * tk), sc, -jnp.inf)
        score[c] = sc
        bits = lax.bitcast_convert_type(sc, jnp.int32)
        okey = jnp.where(bits >= 0, bits, bits ^ jnp.int32(0x7FFFFFFF))
        hi16[c] = lax.shift_right_arithmetic(okey, 16).astype(jnp.int16)
        lo16[c] = ((okey & 0xFFFF) - 2 ** 15).astype(jnp.int16)

    idx_logits(0, st_a)

    def idx_pair(tt, carry):
        c = 2 * tt
        idx_logits(c + 1, st_b)
        idx_score(c, st_a)
        idx_logits(c + 2, st_a)
        idx_score(c + 1, st_b)
        return carry

    def idx_quad(tt, carry):
        idx_pair(2 * tt, carry)
        idx_pair(2 * tt + 1, carry)
        return carry

    lax.fori_loop(0, last // 4, idx_quad, 0)
    lax.fori_loop(2 * (last // 4), last // 2, idx_pair, 0)

    @pl.when(last % 2 == 0)
    def _():
        idx_score(last, st_a)

    @pl.when(last % 2 == 1)
    def _():
        idx_logits(last, st_b)
        idx_score(last - 1, st_a)
        idx_score(last, st_b)

    zero_cnt = jnp.zeros((4 * SUBLANES, tq), jnp.float32)
    i16_min = -2 ** 15

    def count16(buf, pred):
        def body(c, cnt):
            hit = jnp.where(pred(buf[c]), jnp.int16(1), jnp.int16(0))
            h3 = hit.reshape(tk // (4 * SUBLANES), 4 * SUBLANES, tq)
            part = h3[0]
            for r in range(1, h3.shape[0]):
                part = part + h3[r]
            return cnt + part

        def two(t2, cnt):
            return body(2 * t2 + 1, body(2 * t2, cnt))

        cnt = lax.fori_loop(0, nkc // 2, two, jnp.zeros((4 * SUBLANES, tq), jnp.int16))
        cnt = lax.fori_loop(2 * (nkc // 2), nkc, body, cnt)
        return jnp.sum(cnt.astype(jnp.int32), axis=0, keepdims=True)

    def bisect16(buf, want):
        def bit_step(i, cur):
            cand = cur + lax.shift_left(jnp.int32(1), jnp.int32(15) - i)
            c16 = cand.astype(jnp.int16)
            return jnp.where(count16(buf, lambda blk: blk >= c16) >= want, cand, cur)
        return lax.fori_loop(0, 16, bit_step, jnp.full((1, tq), i16_min, jnp.int32))

    t_hi = bisect16(hi16, top_k)
    t_hi16 = t_hi.astype(jnp.int16)
    rest = top_k - count16(hi16, lambda blk: blk > t_hi16)

    def bucket_only(c, carry):
        lo16[c] = jnp.where(hi16[c] == t_hi16, lo16[c], jnp.int16(i16_min))
        return carry

    lax.fori_loop(0, nkc, bucket_only, 0)
    t_lo = bisect16(lo16, rest)
    key = lax.shift_left(t_hi, 16) + (t_lo - i16_min)
    t_f = _key_to_float(key)

    def count_gt_ge(c, carry):
        gt, ge = carry
        blk = score[c]
        return (gt + _count_rows(jnp.where(blk > t_f, 1.0, 0.0)),
                ge + _count_rows(jnp.where(blk >= t_f, 1.0, 0.0)))

    gt, ge = lax.fori_loop(0, nkc, count_gt_ge, (zero_cnt, zero_cnt))
    n_gt = jnp.sum(gt, axis=0, keepdims=True)
    n_ge = jnp.sum(ge, axis=0, keepdims=True)
    q_pos = qi * tq + lax.broadcasted_iota(jnp.int32, (1, tq), 1)
    few = q_pos < top_k - 1
    thr[...] = jnp.where(few, float(jnp.finfo(jnp.float32).min), t_f)
    need = float(top_k) - n_gt
    split = jnp.logical_and(jnp.logical_not(few), n_ge > float(top_k))

    @pl.when(jnp.max(jnp.where(split, 1.0, 0.0)) > 0.0)
    def _():
        def count_eq_below(jc):
            def body(c, cnt):
                hit = jnp.where((score[c] == t_f) & (c * tk + krow < jc), 1.0, 0.0)
                return cnt + _count_rows(hit)
            return jnp.sum(lax.fori_loop(0, nkc, body, zero_cnt), axis=0, keepdims=True)

        nbits = (seq - 1).bit_length()

        def jbit(i, jv):
            cand = jv + lax.shift_left(jnp.int32(1), jnp.int32(nbits - 1) - i)
            return jnp.where(count_eq_below(cand) < need, cand, jv)

        jv = lax.fori_loop(0, nbits, jbit, jnp.zeros((1, tq), jnp.int32))

        def drop_ties(c, carry):
            blk = score[c]
            score[c] = jnp.where(split & (blk == t_f) & (c * tk + krow > jv), -jnp.inf, blk)
            return carry

        lax.fori_loop(0, nkc, drop_ties, 0)

    m_s[...] = jnp.full(m_s.shape, NEG_BIG, jnp.float32)
    acc[...] = jnp.zeros(acc.shape, jnp.float32)

    def qk(c, slot):
        st_ref, mx_ref = slot
        off = pl.multiple_of(c * tk, tk)
        bias = jnp.where(score[c] >= thr[...], 0.0, NEG_BIG)
        for h in range(DSA_HEADS):
            kp = sk_ref[pl.ds(off, tk), (h // 2) * LANES:(h // 2 + 1) * LANES]
            st = _dot_nt(kp, sqm[h]) + bias
            st_ref[h] = st
            mx_ref[h] = jnp.max(st, axis=0, keepdims=True)

    def softmax_pv(c, slot):
        st_ref, mx_ref = slot
        alphas = []
        for h in range(DSA_HEADS):
            m_old = m_s[h]
            m_new = jnp.maximum(m_old, mx_ref[h])
            alphas.append(jnp.exp2(m_old - m_new))
            m_s[h] = m_new
            p_scr[h] = jnp.exp2(st_ref[h] - m_new).astype(jnp.bfloat16)
        ones = jnp.ones((ONES_ROWS, tk), jnp.bfloat16)
        for h in range(DSA_HEADS):
            vt = jnp.concatenate([svt_ref[c, h * DSA_HEAD_DIM:(h + 1) * DSA_HEAD_DIM, :], ones], axis=0)
            acc[h] = alphas[h] * acc[h] + _dot(vt, p_scr[h])

    slot_a, slot_b = (st_a, mx_a), (st_b, mx_b)
    qk(0, slot_a)

    def pair(tt, carry):
        c = 2 * tt
        qk(c + 1, slot_b)
        softmax_pv(c, slot_a)
        qk(c + 2, slot_a)
        softmax_pv(c + 1, slot_b)
        return carry

    def quad(tt, carry):
        pair(2 * tt, carry)
        pair(2 * tt + 1, carry)
        return carry

    lax.fori_loop(0, last // 4, quad, 0)
    lax.fori_loop(2 * (last // 4), last // 2, pair, 0)

    @pl.when(last % 2 == 0)
    def _():
        softmax_pv(last, slot_a)

    @pl.when(last % 2 == 1)
    def _():
        qk(last, slot_b)
        softmax_pv(last - 1, slot_a)
        softmax_pv(last, slot_b)

    dh = DSA_HEAD_DIM
    outs = [acc[h, :dh, :] / acc[h, dh:dh + 1, :] for h in range(DSA_HEADS)]
    o_ref[...] = jnp.concatenate(outs, axis=0).T.astype(o_ref.dtype)


def _dsa_attention(p, ikk, iwt, svt, batch, seq, top_k):
    nq = seq // DSA_TQ
    w = DSA_HEADS * DSA_HEAD_DIM
    once = pl.Buffered(1)
    return pl.pallas_call(
        functools.partial(_dsa_kernel, seq=seq, top_k=top_k),
        grid=(batch, nq),
        in_specs=[
            pl.BlockSpec((DSA_TQ, w), lambda b, qi: (b * nq + qi, 3)),
            pl.BlockSpec((DSA_TQ, w), lambda b, qi: (b * nq + qi, 6)),
            pl.BlockSpec((IDX_HEADS, DSA_TQ), lambda b, qi: (0, b * nq + qi)),
            pl.BlockSpec((seq, LANES), lambda b, qi: (b, 0), pipeline_mode=once),
            pl.BlockSpec((seq, w), lambda b, qi: (b, 4), pipeline_mode=once),
            pl.BlockSpec((seq // DSA_TK, w, DSA_TK), lambda b, qi: (b, 0, 0), pipeline_mode=once),
        ],
        out_specs=pl.BlockSpec((DSA_TQ, w), lambda b, qi: (b * nq + qi, 0)),
        out_shape=jax.ShapeDtypeStruct((batch * seq, w), jnp.bfloat16),
        scratch_shapes=[
            pltpu.VMEM((seq // DSA_TK, DSA_TK, DSA_TQ), jnp.float32),
            pltpu.VMEM((seq // DSA_TK, DSA_TK, DSA_TQ), jnp.int16),
            pltpu.VMEM((seq // DSA_TK, DSA_TK, DSA_TQ), jnp.int16),
            pltpu.VMEM((IDX_HEADS, DSA_TQ, LANES), jnp.bfloat16),
            pltpu.VMEM((DSA_HEADS, DSA_TQ, LANES), jnp.bfloat16),
            pltpu.VMEM((DSA_HEADS, DSA_HEAD_DIM + ONES_ROWS, DSA_TQ), jnp.float32),
            pltpu.VMEM((DSA_HEADS, 1, DSA_TQ), jnp.float32),
            pltpu.VMEM((1, DSA_TQ), jnp.float32),
            pltpu.VMEM((DSA_HEADS, DSA_TK, DSA_TQ), jnp.float32),
            pltpu.VMEM((DSA_HEADS, DSA_TK, DSA_TQ), jnp.float32),
            pltpu.VMEM((DSA_HEADS, 1, DSA_TQ), jnp.float32),
            pltpu.VMEM((DSA_HEADS, 1, DSA_TQ), jnp.float32),
            pltpu.VMEM((DSA_HEADS, DSA_TK, DSA_TQ), jnp.bfloat16),
        ],
        compiler_params=pltpu.CompilerParams(
            dimension_semantics=("parallel", "arbitrary"), vmem_limit_bytes=VMEM_LIMIT),
        name="dsa",
    )(p, p, iwt, ikk, p, svt)


def _memkv_kernel(mem_ref, g_ref, w_ref, o_ref):
    n = _rms(mem_ref[...], g_ref[...]).astype(jnp.bfloat16)
    o_ref[...] = _dot(n, w_ref[...]).astype(o_ref.dtype)


def _memkv(mem2, g_mem, w_kv, batch):
    return pl.pallas_call(
        _memkv_kernel,
        grid=(batch,),
        in_specs=[
            pl.BlockSpec((MEM_LEN, D_MODEL), lambda b: (b, 0)),
            pl.BlockSpec((1, D_MODEL), lambda b: (0, 0)),
            pl.BlockSpec(w_kv.shape, lambda b: (0, 0)),
        ],
        out_specs=pl.BlockSpec((MEM_LEN, w_kv.shape[1]), lambda b: (b, 0)),
        out_shape=jax.ShapeDtypeStruct((batch * MEM_LEN, w_kv.shape[1]), jnp.bfloat16),
        compiler_params=pltpu.CompilerParams(
            dimension_semantics=("parallel",), vmem_limit_bytes=VMEM_LIMIT),
        name="memkv",
    )(mem2, g_mem, w_kv)


def _mem_attention(q, kv):
    scale = XA_HEAD_DIM ** -0.5
    outs = []
    for h in range(XA_HEADS):
        qh = q[:, h * LANES:(h + 1) * LANES]
        k = kv[:, h * LANES:(h + 1) * LANES]
        v = kv[:, (XA_HEADS + h) * LANES:(XA_HEADS + h + 1) * LANES]
        s = _dot_nt(qh, k) * scale
        m = jnp.max(s, axis=1, keepdims=True)
        p = jnp.exp(s - m)
        l = jnp.sum(p, axis=1, keepdims=True)
        outs.append(_dot((p / l).astype(jnp.bfloat16), v))
    return jnp.concatenate(outs, axis=1).astype(jnp.bfloat16)


def _merge_kernel(x_ref, yd_ref, ys_ref, mq_ref, mkv_ref, gmix_ref, wg_ref, bg_ref, wbr_ref, wout_ref,
                  gffn_ref, wr_ref, br_ref, h_ref, hn_ref, lg_ref):
    xf = x_ref[...]
    n = _rms(xf, gmix_ref[...]).astype(jnp.bfloat16)
    branches = (yd_ref[...], ys_ref[...], _mem_attention(mq_ref[...], mkv_ref[...]))
    merged = jnp.zeros(xf.shape, jnp.float32)
    for i, y in enumerate(branches):
        gate = jax.nn.sigmoid(_dot(n, wg_ref[i]) + bg_ref[i])
        merged = merged + gate * _dot(y, wbr_ref[i])
    h = xf + _dot(merged.astype(jnp.bfloat16), wout_ref[...])
    h_ref[...] = h
    hn = _rms(h, gffn_ref[...]).astype(jnp.bfloat16)
    hn_ref[...] = hn
    lg_ref[...] = _dot(hn, wr_ref[...]) + br_ref[...]


def _merge(x2, y_diff, y_dsa, p, mkv, seq, g_mix, w_g, b_g, w_br, w_out, g_ffn, w_r, b_r):
    t = x2.shape[0]
    tiles_per_seq = seq // MERGE_TM
    row = lambda w: pl.BlockSpec((MERGE_TM, w), lambda i: (i, 0))
    full = lambda a: pl.BlockSpec(a.shape, lambda i: (0,) * a.ndim, pipeline_mode=pl.Buffered(1))
    return pl.pallas_call(
        _merge_kernel,
        grid=(t // MERGE_TM,),
        in_specs=[row(D_MODEL), row(512), row(512),
                  pl.BlockSpec((MERGE_TM, 512), lambda i: (i, 7)),
                  pl.BlockSpec((MEM_LEN, mkv.shape[1]), lambda i: (i // tiles_per_seq, 0)),
                  full(g_mix), full(w_g), full(b_g), full(w_br), full(w_out), full(g_ffn), full(w_r), full(b_r)],
        out_specs=[row(D_MODEL), row(D_MODEL), row(LANES)],
        out_shape=[
            jax.ShapeDtypeStruct((t, D_MODEL), jnp.float32),
            jax.ShapeDtypeStruct((t, D_MODEL), jnp.bfloat16),
            jax.ShapeDtypeStruct((t, LANES), jnp.float32),
        ],
        compiler_params=pltpu.CompilerParams(
            dimension_semantics=("parallel",), vmem_limit_bytes=VMEM_LIMIT),
        name="merge",
    )(x2, y_diff, y_dsa, p, mkv, g_mix, w_g, b_g, w_br, w_out, g_ffn, w_r, b_r)


MOE_ST = 2048
MOE_SUB = 512
MOE_ALIGN = 16
MOE_BLK = 256
MOE_NSUB = MOE_ST // MOE_SUB
MOE_ZROWS = MOE_SUB + N_GROUPS * MOE_ALIGN
MOE_ZK = -(-MOE_ZROWS // LANES) * LANES
MOE_GROWS = MOE_ST + MOE_NSUB * N_GROUPS * MOE_ALIGN + MOE_BLK
_SEG_ROWS, _SEG_LOCAL, _SEG_GLOBAL = 0, 16, 32
_GRP_BASE, _GRP_ROWS = 48, 52


def _route(lg):
    lane = lax.broadcasted_iota(jnp.int32, lg.shape, 1).astype(jnp.float32)
    big = float(LANES)
    gl = jnp.where(lane < N_GROUPS, lg, -jnp.inf)
    gmax = jnp.max(gl, axis=1, keepdims=True)
    grp = jnp.min(jnp.where(gl == gmax, lane, big), axis=1, keepdims=True)
    gsum = jnp.sum(jnp.where(lane < N_GROUPS, jnp.exp(gl - gmax), 0.0), axis=1, keepdims=True)
    p_grp = 1.0 / gsum
    lo = N_GROUPS + grp * EXPERTS_PER_GROUP
    el = jnp.where((lane >= lo) & (lane < lo + EXPERTS_PER_GROUP), lg, -jnp.inf)
    e1 = jnp.max(el, axis=1, keepdims=True)
    i1 = jnp.min(jnp.where(el == e1, lane, big), axis=1, keepdims=True)
    el2 = jnp.where(lane == i1, -jnp.inf, el)
    e2 = jnp.max(el2, axis=1, keepdims=True)
    i2 = jnp.min(jnp.where(el2 == e2, lane, big), axis=1, keepdims=True)
    r = jnp.exp(e2 - e1)
    w1 = p_grp / (1.0 + r)
    w2 = p_grp * r / (1.0 + r)
    return jnp.where(lane == i1, w1, 0.0) + jnp.where(lane == i2, w2, 0.0), grp


def _split3(x):
    a = x.astype(jnp.bfloat16)
    r = x - a.astype(jnp.float32)
    b = r.astype(jnp.bfloat16)
    c = (r - b.astype(jnp.float32)).astype(jnp.bfloat16)
    return a, b, c


def _moe_kernel(hn_ref, lg_ref, h_ref, gfin_ref, win_hbm, wout_hbm, o_ref,
                g_buf, ys, cw_s, dest_s, z_buf, cwz_buf, zy_buf, win_buf, wout_buf, sem, seg):
    j = pl.program_id(1)
    sub, al, bf = MOE_SUB, MOE_ALIGN, jnp.bfloat16
    lane_row = lax.broadcasted_iota(jnp.int32, (1, LANES), 1)

    def copy_rows(src_ref, dst_ref, src, dst, nblk):
        def body(i, carry):
            s0 = pl.multiple_of(src + i * al, al)
            d0 = pl.multiple_of(dst + i * al, al)
            for s_ref, d_ref in zip(src_ref, dst_ref):
                d_ref[pl.ds(d0, al), :] = s_ref[pl.ds(s0, al), :]
            return carry
        lax.fori_loop(0, nblk, body, 0)

    @pl.when(j == 0)
    def _():
        cw, grp = _route(lg_ref[...])
        lane = lax.broadcasted_iota(jnp.int32, (sub, LANES), 1).astype(jnp.float32)
        hots = [lane == grp[u * sub:(u + 1) * sub] for u in range(MOE_NSUB)]
        for u in range(MOE_NSUB):
            n_vec = jnp.sum(jnp.where(hots[u], 1.0, 0.0), axis=0, keepdims=True)
            off = jnp.int32(0)
            for g in range(N_GROUPS):
                n = jnp.sum(jnp.where(lane_row == g, n_vec, 0.0)).astype(jnp.int32)
                rows = ((n + (al - 1)) // al) * al
                seg[_SEG_ROWS + u * N_GROUPS + g] = rows
                seg[_SEG_LOCAL + u * N_GROUPS + g] = off
                off = off + rows
        base = jnp.int32(0)
        for g in range(N_GROUPS):
            seg[_GRP_BASE + g] = base
            pos = base
            for u in range(MOE_NSUB):
                seg[_SEG_GLOBAL + u * N_GROUPS + g] = pos
                pos = pos + seg[_SEG_ROWS + u * N_GROUPS + g]
            seg[_GRP_ROWS + g] = pos - base
            base = pos

        ri = lax.broadcasted_iota(jnp.int32, (sub, sub), 0)
        ci = lax.broadcasted_iota(jnp.int32, (sub, sub), 1)
        earlier = jnp.where(ci < ri, 1.0, 0.0).astype(bf)
        zrow = lax.broadcasted_iota(jnp.int32, (MOE_ZROWS, sub), 0).astype(jnp.float32)
        zy_buf[...] = jnp.zeros(zy_buf.shape, jnp.float32)
        for u in range(MOE_NSUB):
            hot = hots[u]
            before = _dot(earlier, jnp.where(hot, 1.0, 0.0).astype(bf))
            start = jnp.zeros((1, LANES), jnp.float32)
            for g in range(N_GROUPS):
                start = jnp.where(lane_row == g, seg[_SEG_LOCAL + u * N_GROUPS + g].astype(jnp.float32), start)
            dest = jnp.sum(jnp.where(hot, before + start, 0.0), axis=1, keepdims=True)
            dest_b = jnp.broadcast_to(dest, (sub, LANES))
            dest_s[u] = dest_b
            perm = jnp.where(zrow == dest_b.T[0:1, :], 1.0, 0.0).astype(bf)
            z_buf[...] = _dot(perm, hn_ref[u * sub:(u + 1) * sub, :]).astype(bf)
            c3 = _split3(cw[u * sub:(u + 1) * sub])
            cwz_buf[...] = _dot(perm, c3[0]) + _dot(perm, c3[1]) + _dot(perm, c3[2])
            for g in range(N_GROUPS):
                k = u * N_GROUPS + g
                copy_rows((z_buf, cwz_buf), (g_buf, cw_s), seg[_SEG_LOCAL + k], seg[_SEG_GLOBAL + k],
                          seg[_SEG_ROWS + k] // al)

        def weights(e, slot):
            return (pltpu.make_async_copy(win_hbm.at[e], win_buf.at[slot], sem.at[0, slot]),
                    pltpu.make_async_copy(wout_hbm.at[e], wout_buf.at[slot], sem.at[1, slot]))

        for cp in weights(0, 0):
            cp.start()

        def expert(e, carry):
            slot = e % 2
            g = e // EXPERTS_PER_GROUP

            @pl.when(e + 1 < N_EXPERTS)
            def _():
                for cp in weights(e + 1, 1 - slot):
                    cp.start()

            for cp in weights(e, slot):
                cp.wait()
            start = seg[_GRP_BASE + g]
            rows = seg[_GRP_ROWS + g]
            big = 2 * MOE_BLK
            nbig = rows // big
            rem = rows - nbig * big
            first = e % EXPERTS_PER_GROUP == 0

            def block(row0, n):
                row0 = pl.multiple_of(row0, al)
                gu = _dot(g_buf[pl.ds(row0, n), :], win_buf[slot])
                gt = gu[:, :D_EXPERT]
                hid = (gt * jax.nn.sigmoid(gt)) * gu[:, D_EXPERT:]
                y = _dot(hid.astype(bf), wout_buf[slot])
                ln = lax.broadcasted_iota(jnp.int32, (n, LANES), 1)
                cw_e = jnp.sum(jnp.where(ln == N_GROUPS + e, cw_s[pl.ds(row0, n), :], 0.0), axis=1, keepdims=True)

                @pl.when(first)
                def _():
                    ys[pl.ds(row0, n), :] = cw_e * y

                @pl.when(jnp.logical_not(first))
                def _():
                    ys[pl.ds(row0, n), :] += cw_e * y

            def big_block(b, c):
                block(start + b * big, big)
                return c

            lax.fori_loop(0, nbig, big_block, 0)

            tail = start + nbig * big

            @pl.when(rem > MOE_BLK)
            def _():
                block(tail, big)

            @pl.when((rem > MOE_BLK // 2) & (rem <= MOE_BLK))
            def _():
                block(tail, MOE_BLK)

            @pl.when((rem > 0) & (rem <= MOE_BLK // 2))
            def _():
                block(tail, MOE_BLK // 2)

            return carry

        lax.fori_loop(0, N_EXPERTS, expert, 0)

    @pl.when(j > 0)
    def _():
        u = j - 1
        for g in range(N_GROUPS):
            k = u * N_GROUPS + g
            copy_rows((ys,), (zy_buf,), seg[_SEG_GLOBAL + k], seg[_SEG_LOCAL + k], seg[_SEG_ROWS + k] // al)
        zy = zy_buf[...]
        hi = zy.astype(bf)
        lo = (zy - hi.astype(jnp.float32)).astype(bf)
        dest = jnp.concatenate([dest_s[u]] * (MOE_ZK // LANES), axis=1)
        col = lax.broadcasted_iota(jnp.int32, (sub, MOE_ZK), 1).astype(jnp.float32)
        unperm = jnp.where(col == dest, 1.0, 0.0).astype(bf)
        moe = _dot(unperm, hi) + _dot(unperm, lo)
        o_ref[...] = _rms(h_ref[...] + moe, gfin_ref[...])


def _moe(hn, lg, h, w_e_in, w_e_out, g_final):
    t = hn.shape[0]
    row = lambda s, j: (s * MOE_NSUB + jnp.maximum(j - 1, 0), 0)
    return pl.pallas_call(
        _moe_kernel,
        grid=(t // MOE_ST, 1 + MOE_NSUB),
        in_specs=[
            pl.BlockSpec((MOE_ST, D_MODEL), lambda s, j: (s, 0)),
            pl.BlockSpec((MOE_ST, LANES), lambda s, j: (s, 0)),
            pl.BlockSpec((MOE_SUB, D_MODEL), row),
            pl.BlockSpec((1, D_MODEL), lambda s, j: (0, 0)),
            pl.BlockSpec(memory_space=pl.ANY),
            pl.BlockSpec(memory_space=pl.ANY),
        ],
        out_specs=pl.BlockSpec((MOE_SUB, D_MODEL), row),
        out_shape=jax.ShapeDtypeStruct((t, D_MODEL), jnp.float32),
        scratch_shapes=[
            pltpu.VMEM((MOE_GROWS, D_MODEL), jnp.bfloat16),
            pltpu.VMEM((MOE_GROWS, D_MODEL), jnp.float32),
            pltpu.VMEM((MOE_GROWS, LANES), jnp.float32),
            pltpu.VMEM((MOE_NSUB, MOE_SUB, LANES), jnp.float32),
            pltpu.VMEM((MOE_ZROWS, D_MODEL), jnp.bfloat16),
            pltpu.VMEM((MOE_ZROWS, LANES), jnp.float32),
            pltpu.VMEM((MOE_ZK, D_MODEL), jnp.float32),
            pltpu.VMEM((2, D_MODEL, 2 * D_EXPERT), jnp.bfloat16),
            pltpu.VMEM((2, D_EXPERT, D_MODEL), jnp.bfloat16),
            pltpu.SemaphoreType.DMA((2, 2)),
            pltpu.SMEM((64,), jnp.int32),
        ],
        compiler_params=pltpu.CompilerParams(
            dimension_semantics=("parallel", "arbitrary"), vmem_limit_bytes=VMEM_LIMIT),
        name="moe",
    )(hn, lg, h, g_final, w_e_in, w_e_out)


def kernel(x, positions, mem, g_mix, w_in, b_gate, lambda_q1, lambda_k1, lambda_q2, lambda_k2,
           g_diff_sub, g_mem, w_mem_kv, w_br_diff, w_br_dsa, w_br_mem, w_out, g_ffn,
           w_route_group, b_route_group, w_route_expert, b_route_expert, w_exp_in, w_exp_out,
           g_final):
    b, s, d = x.shape
    t = b * s
    bf = jnp.bfloat16
    top_k = min(TOPK_MAX, s // 4)
    assert d == D_MODEL and s % DIFF_T == 0 and s % DSA_TQ == 0 and top_k <= DSA_TQ
    assert PROJ_TM == DIFF_T == DSA_TK and DSA_TK % DSA_TQ == 0, "proj writes V^T in the attention kernels' key tiles"
    assert g_mix.shape[0] == 1, "single layer"
    lam_init = 0.8 - 0.6 * math.exp(-0.3 * 0)

    wi = w_in[0]
    c = 512
    seg = lambda k: wi[:, k * c:(k + 1) * c]
    o_ik = 7 * c
    w_ik = wi[:, o_ik:o_ik + IDX_DIM]
    w_iw = wi[:, o_ik + IDX_DIM:o_ik + IDX_DIM + IDX_HEADS]
    o_mq = o_ik + IDX_DIM + IDX_HEADS
    w_mq = wi[:, o_mq:o_mq + c]
    w_gl = wi[:, o_mq + c:]
    qs = DIFF_QK_DIM ** -0.5
    qs2 = qs * LOG2E
    src = _pair_lane_source()
    src_seg = np.concatenate([blk * LANES + src for blk in range(c // LANES)])
    rseg = lambda k: seg(k)[:, src_seg]
    w_a = jnp.concatenate([rseg(0) * qs2, rseg(1), seg(2), rseg(3) * qs2, rseg(4), seg(5), rseg(6) * qs, w_mq],
                          axis=1).astype(bf)
    w_s = jnp.concatenate([jnp.concatenate([w_ik, w_ik], axis=1)[:, src], w_iw,
                           jnp.zeros((d, LANES - IDX_HEADS), wi.dtype)], axis=1).astype(bf)
    w_g = w_gl.reshape(d, 3, d).transpose(1, 0, 2).astype(bf)
    b_g = b_gate[0].reshape(3, 1, d)
    w_br = jnp.stack([w_br_diff[0], w_br_dsa[0], w_br_mem[0]]).astype(bf)
    w_r = jnp.concatenate([w_route_group[0], w_route_expert[0],
                           jnp.zeros((d, LANES - N_GROUPS - N_EXPERTS), wi.dtype)], axis=1).astype(bf)
    b_r = jnp.concatenate([b_route_group[0], b_route_expert[0],
                           jnp.zeros((LANES - N_GROUPS - N_EXPERTS,), jnp.float32)]).reshape(1, LANES)

    rot = IDX_DIM // ROPE_FRACTION
    inv_freq = ROPE_THETA ** (-jnp.arange(0, rot, 2, dtype=jnp.float32) / rot)
    inv_col = inv_freq.reshape(rot // 2, 1)
    expand = jnp.asarray(_rope_expanders(), bf)

    x2 = x.reshape(t, d)
    pos_rows = positions.reshape(t // PROJ_TM, 1, PROJ_TM)
    p, ikk, dvt, svt, iwt = _proj(x2, pos_rows, g_mix, inv_col, expand, w_a, w_s)

    y_diff = _diff_attention(p, dvt, lambda_q1, lambda_k1, lambda_q2, lambda_k2,
                             g_diff_sub.reshape(DIFF_V_DIM, 1), b, s, lam_init)
    y_dsa = _dsa_attention(p, ikk, iwt, svt, b, s, top_k)
    mkv = _memkv(mem.reshape(b * MEM_LEN, d), g_mem, w_mem_kv[0].astype(bf), b)
    h, hn, lg = _merge(x2, y_diff, y_dsa, p, mkv, s, g_mix, w_g, b_g, w_br, w_out[0].astype(bf), g_ffn, w_r, b_r)
    out = _moe(hn, lg, h, w_exp_in[0].astype(bf), w_exp_out[0].astype(bf), g_final.reshape(1, d))
    return out.reshape(b, s, d)
```

```python
import functools
import math

import jax
import jax.numpy as jnp
import numpy as np
from jax import lax
from jax.experimental import pallas as pl
from jax.experimental.pallas import tpu as pltpu

D_MODEL = 1024
MEM_LEN = 256
XA_HEADS = 4
XA_HEAD_DIM = 128
DIFF_HEADS = 4
DIFF_QK_DIM = 64
DIFF_V_DIM = 128
DSA_HEADS = 8
DSA_HEAD_DIM = 64
IDX_HEADS = 8
IDX_DIM = 64
TOPK_MAX = 256
ROPE_THETA = 500000.0
ROPE_FRACTION = 4
N_GROUPS = 4
EXPERTS_PER_GROUP = 4
N_EXPERTS = 16
D_EXPERT = 512
EPS = 1e-6

LANES = 128
SUBLANES = 8
VMEM_LIMIT = 56 * 1024 * 1024
NEG_BIG = -1e30
ONES_ROWS = 16
LOG2E = math.log2(math.e)

PROJ_TM = 512
PROJ_TN = 512
DIFF_T = 512
DSA_TQ = 256
DSA_TK = 512
MERGE_TM = 512

_NT = (((1,), (1,)), ((), ()))


def _dot(a, b):
    return jnp.dot(a, b, preferred_element_type=jnp.float32)


def _dot_nt(a, b):
    return lax.dot_general(a, b, _NT, preferred_element_type=jnp.float32)


def _rms(xf, g):
    return xf * lax.rsqrt(jnp.mean(xf * xf, axis=-1, keepdims=True) + EPS) * g


def _pair_lane_source():
    half, rot = IDX_DIM, IDX_DIM // ROPE_FRACTION
    hr = rot // 2
    src = np.zeros(LANES, np.int32)
    src[0:hr] = np.arange(0, hr)
    src[hr:rot] = half + np.arange(0, hr)
    src[rot:half] = np.arange(rot, half)
    src[half:half + hr] = np.arange(hr, rot)
    src[half + hr:half + rot] = half + np.arange(hr, rot)
    src[half + rot:] = half + np.arange(rot, half)
    return src


def _first_head_lanes(shape):
    half, rot = IDX_DIM, IDX_DIM // ROPE_FRACTION
    lane = lax.broadcasted_iota(jnp.int32, shape, len(shape) - 1) % LANES
    return (lane < rot // 2) | ((lane >= rot) & (lane < half + rot // 2))


def _rope_expanders():
    rot = IDX_DIM // ROPE_FRACTION
    e = np.zeros((2, LANES, LANES), np.float32)
    for l in list(range(rot)) + list(range(IDX_DIM, IDX_DIM + rot)):
        e[0, l % (rot // 2), l] = 1.0
        e[1, l % (rot // 2), l] = -1.0 if l < rot else 1.0
    return e


_ROPE_SEGMENTS = (0, 1, 3, 4, 6)


def _proj_kernel(x_ref, posr_ref, g_ref, inv_ref, e_ref, wa_ref, ws_ref,
                 p_ref, ikk_ref, dvt_ref, svt_ref, iwt_ref):
    xf = x_ref[...]
    tm = xf.shape[0]
    n = _rms(xf, g_ref[...]).astype(jnp.bfloat16)
    ang = posr_ref[0].astype(jnp.float32) * inv_ref[...]
    pad = jnp.zeros((LANES - ang.shape[0], tm), jnp.float32)
    ct = jnp.concatenate([jnp.cos(ang), pad], axis=0).T
    st = jnp.concatenate([jnp.sin(ang), pad], axis=0).T
    crot = sum(_dot(part, e_ref[0]) for part in _split3(ct))
    lane = lax.broadcasted_iota(jnp.int32, crot.shape, 1) % IDX_DIM
    cos_t = jnp.where(lane < IDX_DIM // ROPE_FRACTION, crot, 1.0)
    sin_t = sum(_dot(part, e_ref[1]) for part in _split3(st))

    def rope(v):
        k = v.shape[1] // LANES
        blocks = [pltpu.roll(v[:, b * LANES:(b + 1) * LANES], LANES // 2, 1) for b in range(k)]
        partner = jnp.concatenate(blocks, axis=1) if k > 1 else blocks[0]
        c = jnp.concatenate([cos_t] * k, axis=1) if k > 1 else cos_t
        s = jnp.concatenate([sin_t] * k, axis=1) if k > 1 else sin_t
        return v * c + partner * s

    small = _dot(n, ws_ref[...])
    ikk_ref[...] = rope(small[:, :LANES]).astype(jnp.bfloat16)
    iwt_ref[...] = (small[:, LANES:] * (IDX_HEADS ** -0.5)).T[:IDX_HEADS, :]

    for j in range(wa_ref.shape[1] // PROJ_TN):
        cols = slice(j * PROJ_TN, (j + 1) * PROJ_TN)
        acc = _dot(n, wa_ref[:, cols])
        p_ref[:, cols] = (rope(acc) if j in _ROPE_SEGMENTS else acc).astype(jnp.bfloat16)
        if j == 2:
            dvt_ref[0] = acc.T.astype(jnp.bfloat16)
        if j == 5:
            svt_ref[0] = acc.T.astype(jnp.bfloat16)


def _proj(x2, pos_rows, g_mix, inv_col, expand, w_a, w_s):
    t = x2.shape[0]
    once = pl.Buffered(1)
    return pl.pallas_call(
        _proj_kernel,
        grid=(t // PROJ_TM,),
        in_specs=[
            pl.BlockSpec((PROJ_TM, D_MODEL), lambda i: (i, 0)),
            pl.BlockSpec((1, 1, PROJ_TM), lambda i: (i, 0, 0)),
            pl.BlockSpec((1, D_MODEL), lambda i: (0, 0)),
            pl.BlockSpec(inv_col.shape, lambda i: (0, 0)),
            pl.BlockSpec(expand.shape, lambda i: (0, 0, 0)),
            pl.BlockSpec(w_a.shape, lambda i: (0, 0), pipeline_mode=once),
            pl.BlockSpec(w_s.shape, lambda i: (0, 0), pipeline_mode=once),
        ],
        out_specs=[
            pl.BlockSpec((PROJ_TM, w_a.shape[1]), lambda i: (i, 0)),
            pl.BlockSpec((PROJ_TM, LANES), lambda i: (i, 0)),
            pl.BlockSpec((1, PROJ_TN, PROJ_TM), lambda i: (i, 0, 0)),
            pl.BlockSpec((1, PROJ_TN, PROJ_TM), lambda i: (i, 0, 0)),
            pl.BlockSpec((IDX_HEADS, PROJ_TM), lambda i: (0, i)),
        ],
        out_shape=[
            jax.ShapeDtypeStruct((t, w_a.shape[1]), jnp.bfloat16),
            jax.ShapeDtypeStruct((t, LANES), jnp.bfloat16),
            jax.ShapeDtypeStruct((t // PROJ_TM, PROJ_TN, PROJ_TM), jnp.bfloat16),
            jax.ShapeDtypeStruct((t // PROJ_TM, PROJ_TN, PROJ_TM), jnp.bfloat16),
            jax.ShapeDtypeStruct((IDX_HEADS, t), jnp.float32),
        ],
        compiler_params=pltpu.CompilerParams(
            dimension_semantics=("parallel",), vmem_limit_bytes=VMEM_LIMIT),
        name="proj",
    )(x2, pos_rows, g_mix, inv_col, expand, w_a, w_s)


def _diff_kernel(q_ref, k_ref, vt_ref, lq1_ref, lk1_ref, lq2_ref, lk2_ref, gs_ref, o_ref,
                 qm, m_s, acc, st_a, st_b, p_scr, *, lam_init):
    qi = pl.program_id(2)
    t = DIFF_T
    q = q_ref[...]
    first = _first_head_lanes(q.shape)
    qm[0] = jnp.where(first, q, jnp.zeros_like(q))
    qm[1] = jnp.where(first, jnp.zeros_like(q), q)
    m_s[...] = jnp.full(m_s.shape, NEG_BIG, jnp.float32)
    acc[...] = jnp.zeros(acc.shape, jnp.float32)

    def qk(j, st_ref):
        k = k_ref[pl.ds(pl.multiple_of(j * t, t), t), :]
        for i in range(2):
            st_ref[i] = _dot_nt(k, qm[i])

    def softmax_pv(j, st_ref, diagonal):
        vt = jnp.concatenate([vt_ref[j], jnp.ones((ONES_ROWS, t), jnp.bfloat16)], axis=0)
        if diagonal:
            krow = lax.broadcasted_iota(jnp.int32, (t, t), 0)
            qcol = lax.broadcasted_iota(jnp.int32, (t, t), 1)
            keep = krow <= qcol
        alphas = []
        for i in range(2):
            st = st_ref[i]
            if diagonal:
                st = jnp.where(keep, st, NEG_BIG)
            m_old = m_s[i]
            m_new = jnp.maximum(m_old, jnp.max(st, axis=0, keepdims=True))
            alphas.append(jnp.exp2(m_old - m_new))
            m_s[i] = m_new
            p_scr[i] = jnp.exp2(st - m_new).astype(jnp.bfloat16)
        for i in range(2):
            acc[i] = alphas[i] * acc[i] + _dot(vt, p_scr[i])

    qk(0, st_a)

    def pair(tt, carry):
        j = 2 * tt
        qk(j + 1, st_b)
        softmax_pv(j, st_a, False)
        qk(j + 2, st_a)
        softmax_pv(j + 1, st_b, False)
        return carry

    lax.fori_loop(0, qi // 2, pair, 0)

    @pl.when(qi % 2 == 0)
    def _():
        softmax_pv(qi, st_a, True)

    @pl.when(qi % 2 == 1)
    def _():
        qk(qi, st_b)
        softmax_pv(qi - 1, st_a, False)
        softmax_pv(qi, st_b, True)

    lam = (jnp.exp(jnp.sum(lq1_ref[...] * lk1_ref[...], axis=1, keepdims=True))
           - jnp.exp(jnp.sum(lq2_ref[...] * lk2_ref[...], axis=1, keepdims=True))
           + lam_init)
    dv = DIFF_V_DIM
    ot = (acc[0, :dv, :] / acc[0, dv:dv + 1, :]
          - lam * (acc[1, :dv, :] / acc[1, dv:dv + 1, :]))
    yt = ot * lax.rsqrt(jnp.mean(ot * ot, axis=0, keepdims=True) + EPS) * gs_ref[...]
    o_ref[...] = (yt * (1.0 - lam_init)).T.astype(o_ref.dtype)


def _diff_attention(p, dvt, lq1, lk1, lq2, lk2, g_sub_col, batch, seq, lam_init):
    nb = seq // DIFF_T
    vec = pl.BlockSpec((1, DIFF_QK_DIM), lambda b, h, qi: (0, 0))
    return pl.pallas_call(
        functools.partial(_diff_kernel, lam_init=lam_init),
        grid=(batch, DIFF_HEADS, nb),
        in_specs=[
            pl.BlockSpec((DIFF_T, LANES), lambda b, h, qi: (b * nb + qi, h)),
            pl.BlockSpec((seq, LANES), lambda b, h, qi: (b, DIFF_HEADS + h)),
            pl.BlockSpec((nb, DIFF_V_DIM, DIFF_T), lambda b, h, qi: (b, h, 0)),
            vec, vec, vec, vec,
            pl.BlockSpec((DIFF_V_DIM, 1), lambda b, h, qi: (0, 0)),
        ],
        out_specs=pl.BlockSpec((DIFF_T, LANES), lambda b, h, qi: (b * nb + qi, h)),
        out_shape=jax.ShapeDtypeStruct((batch * seq, DIFF_HEADS * DIFF_V_DIM), jnp.bfloat16),
        scratch_shapes=[
            pltpu.VMEM((2, DIFF_T, LANES), jnp.bfloat16),
            pltpu.VMEM((2, 1, DIFF_T), jnp.float32),
            pltpu.VMEM((2, DIFF_V_DIM + ONES_ROWS, DIFF_T), jnp.float32),
            pltpu.VMEM((2, DIFF_T, DIFF_T), jnp.float32),
            pltpu.VMEM((2, DIFF_T, DIFF_T), jnp.float32),
            pltpu.VMEM((2, DIFF_T, DIFF_T), jnp.bfloat16),
        ],
        compiler_params=pltpu.CompilerParams(
            dimension_semantics=("parallel", "parallel", "arbitrary"),
            vmem_limit_bytes=VMEM_LIMIT),
        name="diffattn",
    )(p, p, dvt, lq1, lk1, lq2, lk2, g_sub_col)


def _key_to_float(key):
    bits = jnp.where(key >= 0, key, key ^ jnp.int32(0x7FFFFFFF))
    return lax.bitcast_convert_type(bits, jnp.float32)


def _count_rows(hit):
    tk, tq = hit.shape
    return jnp.sum(hit.reshape(tk // (4 * SUBLANES), 4 * SUBLANES, tq), axis=0)


def _dsa_kernel(sq_ref, iq_ref, iwt_ref, ikk_ref, sk_ref, svt_ref, o_ref,
                score, hi16, lo16, iqm, sqm, acc, m_s, thr, sel, st_a, st_b, mx_a, mx_b, p_scr, *, seq, top_k):
    qi = pl.program_id(1)
    tq, tk = DSA_TQ, DSA_TK
    last = (qi * tq) // tk
    nkc = last + 1
    first = _first_head_lanes((tq, LANES))

    for h in range(DSA_HEADS):
        pr = h // 2
        iqp = iq_ref[:, pr * LANES:(pr + 1) * LANES]
        sqp = sq_ref[:, pr * LANES:(pr + 1) * LANES]
        mine = first if h % 2 == 0 else jnp.logical_not(first)
        iqm[h] = jnp.where(mine, iqp, jnp.zeros_like(iqp))
        sqm[h] = jnp.where(mine, sqp, jnp.zeros_like(sqp))

    krow = lax.broadcasted_iota(jnp.int32, (tk, tq), 0)
    qcol = lax.broadcasted_iota(jnp.int32, (tk, tq), 1)

    def idx_logits(c, lg_ref):
        kk = ikk_ref[pl.ds(pl.multiple_of(c * tk, tk), tk), :]
        for h in range(IDX_HEADS):
            lg_ref[h] = _dot_nt(kk, iqm[h])

    def idx_score(c, lg_ref):
        sc = jnp.zeros((tk, tq), jnp.float32)
        for h in range(IDX_HEADS):
            sc = sc + iwt_ref[h:h + 1, :] * jnp.maximum(lg_ref[h], 0.0)
        sc = jnp.where(krow <= qcol + (qi * tq - c * tk), sc, -jnp.inf)
        score[c] = sc
        bits = lax.bitcast_convert_type(sc, jnp.int32)
        okey = jnp.where(bits >= 0, bits, bits ^ jnp.int32(0x7FFFFFFF))
        hi16[c] = lax.shift_right_arithmetic(okey, 16).astype(jnp.int16)
        lo16[c] = ((okey & 0xFFFF) - 2 ** 15).astype(jnp.int16)

    idx_logits(0, st_a)

    def idx_pair(tt, carry):
        c = 2 * tt
        idx_logits(c + 1, st_b)
        idx_score(c, st_a)
        idx_logits(c + 2, st_a)
        idx_score(c + 1, st_b)
        return carry

    lax.fori_loop(0, last // 2, idx_pair, 0)

    @pl.when(last % 2 == 0)
    def _():
        idx_score(last, st_a)

    @pl.when(last % 2 == 1)
    def _():
        idx_logits(last, st_b)
        idx_score(last - 1, st_a)
        idx_score(last, st_b)

    zero_cnt = jnp.zeros((4 * SUBLANES, tq), jnp.float32)
    i16_min = -2 ** 15

    def count16(buf, pred):
        def body(c, cnt):
            hit = jnp.where(pred(buf[c]), jnp.int16(1), jnp.int16(0))
            h3 = hit.reshape(tk // (4 * SUBLANES), 4 * SUBLANES, tq)
            part = h3[0]
            for r in range(1, h3.shape[0]):
                part = part + h3[r]
            return cnt + part

        def two(t2, cnt):
            return body(2 * t2 + 1, body(2 * t2, cnt))

        cnt = lax.fori_loop(0, nkc // 2, two, jnp.zeros((4 * SUBLANES, tq), jnp.int16))
        cnt = lax.fori_loop(2 * (nkc // 2), nkc, body, cnt)
        return jnp.sum(cnt.astype(jnp.int32), axis=0, keepdims=True)

    def bisect16(buf, want):
        def bit_step(i, cur):
            cand = cur + lax.shift_left(jnp.int32(1), jnp.int32(15) - i)
            c16 = cand.astype(jnp.int16)
            return jnp.where(count16(buf, lambda blk: blk >= c16) >= want, cand, cur)
        return lax.fori_loop(0, 16, bit_step, jnp.full((1, tq), i16_min, jnp.int32))

    t_hi = bisect16(hi16, top_k)
    t_hi16 = t_hi.astype(jnp.int16)
    rest = top_k - count16(hi16, lambda blk: blk > t_hi16)

    def bucket_only(c, carry):
        lo16[c] = jnp.where(hi16[c] == t_hi16, lo16[c], jnp.int16(i16_min))
        return carry

    lax.fori_loop(0, nkc, bucket_only, 0)
    t_lo = bisect16(lo16, rest)
    key = lax.shift_left(t_hi, 16) + (t_lo - i16_min)
    t_f = _key_to_float(key)

    def counts(t):
        def body(c, carry):
            gt, ge = carry
            blk = score[c]
            return (gt + _count_rows(jnp.where(blk > t, 1.0, 0.0)),
                    ge + _count_rows(jnp.where(blk >= t, 1.0, 0.0)))
        gt, ge = lax.fori_loop(0, nkc, body, (zero_cnt, zero_cnt))
        return jnp.sum(gt, axis=0, keepdims=True), jnp.sum(ge, axis=0, keepdims=True)

    n_gt, n_ge = counts(t_f)
    q_pos = qi * tq + lax.broadcasted_iota(jnp.int32, (1, tq), 1)
    few = q_pos < top_k - 1
    sel[0], sel[1], sel[2] = t_f, n_gt, n_ge

    off = jnp.logical_not(few) & ((n_gt >= float(top_k)) | (n_ge < float(top_k)))

    @pl.when(jnp.max(jnp.where(off, 1.0, 0.0)) > 0.0)
    def _():
        def bit_step(i, cur):
            cand = cur + lax.shift_left(jnp.int32(1), jnp.int32(31) - i)
            return jnp.where(counts(_key_to_float(cand))[1] >= float(top_k), cand, cur)
        t2 = _key_to_float(lax.fori_loop(0, 32, bit_step, jnp.full((1, tq), -2 ** 31, jnp.int32)))
        g2, e2 = counts(t2)
        sel[0], sel[1], sel[2] = t2, g2, e2

    t_f, n_gt, n_ge = sel[0], sel[1], sel[2]
    thr[...] = jnp.where(few, float(jnp.finfo(jnp.float32).min), t_f)
    need = float(top_k) - n_gt
    split = jnp.logical_and(jnp.logical_not(few), n_ge > float(top_k))

    @pl.when(jnp.max(jnp.where(split, 1.0, 0.0)) > 0.0)
    def _():
        def count_eq_below(jc):
            def body(c, cnt):
                hit = jnp.where((score[c] == t_f) & (c * tk + krow < jc), 1.0, 0.0)
                return cnt + _count_rows(hit)
            return jnp.sum(lax.fori_loop(0, nkc, body, zero_cnt), axis=0, keepdims=True)

        nbits = (seq - 1).bit_length()

        def jbit(i, jv):
            cand = jv + lax.shift_left(jnp.int32(1), jnp.int32(nbits - 1) - i)
            return jnp.where(count_eq_below(cand) < need, cand, jv)

        jv = lax.fori_loop(0, nbits, jbit, jnp.zeros((1, tq), jnp.int32))

        def drop_ties(c, carry):
            blk = score[c]
            score[c] = jnp.where(split & (blk == t_f) & (c * tk + krow > jv), -jnp.inf, blk)
            return carry

        lax.fori_loop(0, nkc, drop_ties, 0)

    m_s[...] = jnp.full(m_s.shape, NEG_BIG, jnp.float32)
    acc[...] = jnp.zeros(acc.shape, jnp.float32)

    def qk(c, slot):
        st_ref, mx_ref = slot
        off = pl.multiple_of(c * tk, tk)
        bias = jnp.where(score[c] >= thr[...], 0.0, NEG_BIG)
        for h in range(DSA_HEADS):
            kp = sk_ref[pl.ds(off, tk), (h // 2) * LANES:(h // 2 + 1) * LANES]
            st = _dot_nt(kp, sqm[h]) + bias
            st_ref[h] = st
            mx_ref[h] = jnp.max(st, axis=0, keepdims=True)

    def softmax_pv(c, slot):
        st_ref, mx_ref = slot
        alphas = []
        for h in range(DSA_HEADS):
            m_old = m_s[h]
            m_new = jnp.maximum(m_old, mx_ref[h])
            alphas.append(jnp.exp2(m_old - m_new))
            m_s[h] = m_new
            p_scr[h] = jnp.exp2(st_ref[h] - m_new).astype(jnp.bfloat16)
        ones = jnp.ones((ONES_ROWS, tk), jnp.bfloat16)
        for h in range(DSA_HEADS):
            vt = jnp.concatenate([svt_ref[c, h * DSA_HEAD_DIM:(h + 1) * DSA_HEAD_DIM, :], ones], axis=0)
            acc[h] = alphas[h] * acc[h] + _dot(vt, p_scr[h])

    slot_a, slot_b = (st_a, mx_a), (st_b, mx_b)
    qk(0, slot_a)

    def pair(tt, carry):
        c = 2 * tt
        qk(c + 1, slot_b)
        softmax_pv(c, slot_a)
        qk(c + 2, slot_a)
        softmax_pv(c + 1, slot_b)
        return carry

    lax.fori_loop(0, last // 2, pair, 0)

    @pl.when(last % 2 == 0)
    def _():
        softmax_pv(last, slot_a)

    @pl.when(last % 2 == 1)
    def _():
        qk(last, slot_b)
        softmax_pv(last - 1, slot_a)
        softmax_pv(last, slot_b)

    dh = DSA_HEAD_DIM
    outs = [acc[h, :dh, :] / acc[h, dh:dh + 1, :] for h in range(DSA_HEADS)]
    o_ref[...] = jnp.concatenate(outs, axis=0).T.astype(o_ref.dtype)


def _dsa_attention(p, ikk, iwt, svt, batch, seq, top_k):
    nq = seq // DSA_TQ
    w = DSA_HEADS * DSA_HEAD_DIM
    once = pl.Buffered(1)
    return pl.pallas_call(
        functools.partial(_dsa_kernel, seq=seq, top_k=top_k),
        grid=(batch, nq),
        in_specs=[
            pl.BlockSpec((DSA_TQ, w), lambda b, qi: (b * nq + qi, 3)),
            pl.BlockSpec((DSA_TQ, w), lambda b, qi: (b * nq + qi, 6)),
            pl.BlockSpec((IDX_HEADS, DSA_TQ), lambda b, qi: (0, b * nq + qi)),
            pl.BlockSpec((seq, LANES), lambda b, qi: (b, 0), pipeline_mode=once),
            pl.BlockSpec((seq, w), lambda b, qi: (b, 4), pipeline_mode=once),
            pl.BlockSpec((seq // DSA_TK, w, DSA_TK), lambda b, qi: (b, 0, 0), pipeline_mode=once),
        ],
        out_specs=pl.BlockSpec((DSA_TQ, w), lambda b, qi: (b * nq + qi, 0)),
        out_shape=jax.ShapeDtypeStruct((batch * seq, w), jnp.bfloat16),
        scratch_shapes=[
            pltpu.VMEM((seq // DSA_TK, DSA_TK, DSA_TQ), jnp.float32),
            pltpu.VMEM((seq // DSA_TK, DSA_TK, DSA_TQ), jnp.int16),
            pltpu.VMEM((seq // DSA_TK, DSA_TK, DSA_TQ), jnp.int16),
            pltpu.VMEM((IDX_HEADS, DSA_TQ, LANES), jnp.bfloat16),
            pltpu.VMEM((DSA_HEADS, DSA_TQ, LANES), jnp.bfloat16),
            pltpu.VMEM((DSA_HEADS, DSA_HEAD_DIM + ONES_ROWS, DSA_TQ), jnp.float32),
            pltpu.VMEM((DSA_HEADS, 1, DSA_TQ), jnp.float32),
            pltpu.VMEM((1, DSA_TQ), jnp.float32),
            pltpu.VMEM((3, 1, DSA_TQ), jnp.float32),
            pltpu.VMEM((DSA_HEADS, DSA_TK, DSA_TQ), jnp.float32),
            pltpu.VMEM((DSA_HEADS, DSA_TK, DSA_TQ), jnp.float32),
            pltpu.VMEM((DSA_HEADS, 1, DSA_TQ), jnp.float32),
            pltpu.VMEM((DSA_HEADS, 1, DSA_TQ), jnp.float32),
            pltpu.VMEM((DSA_HEADS, DSA_TK, DSA_TQ), jnp.bfloat16),
        ],
        compiler_params=pltpu.CompilerParams(
            dimension_semantics=("parallel", "arbitrary"), vmem_limit_bytes=VMEM_LIMIT),
        name="dsa",
    )(p, p, iwt, ikk, p, svt)


def _memkv_kernel(mem_ref, g_ref, w_ref, o_ref):
    n = _rms(mem_ref[...], g_ref[...]).astype(jnp.bfloat16)
    o_ref[...] = _dot(n, w_ref[...]).astype(o_ref.dtype)


def _memkv(mem2, g_mem, w_kv, batch):
    return pl.pallas_call(
        _memkv_kernel,
        grid=(batch,),
        in_specs=[
            pl.BlockSpec((MEM_LEN, D_MODEL), lambda b: (b, 0)),
            pl.BlockSpec((1, D_MODEL), lambda b: (0, 0)),
            pl.BlockSpec(w_kv.shape, lambda b: (0, 0)),
        ],
        out_specs=pl.BlockSpec((MEM_LEN, w_kv.shape[1]), lambda b: (b, 0)),
        out_shape=jax.ShapeDtypeStruct((batch * MEM_LEN, w_kv.shape[1]), jnp.bfloat16),
        compiler_params=pltpu.CompilerParams(
            dimension_semantics=("parallel",), vmem_limit_bytes=VMEM_LIMIT),
        name="memkv",
    )(mem2, g_mem, w_kv)


def _mem_attention(q, kv):
    scale = XA_HEAD_DIM ** -0.5
    outs = []
    for h in range(XA_HEADS):
        qh = q[:, h * LANES:(h + 1) * LANES]
        k = kv[:, h * LANES:(h + 1) * LANES]
        v = kv[:, (XA_HEADS + h) * LANES:(XA_HEADS + h + 1) * LANES]
        s = _dot_nt(qh, k) * scale
        m = jnp.max(s, axis=1, keepdims=True)
        p = jnp.exp(s - m)
        l = jnp.sum(p, axis=1, keepdims=True)
        outs.append(_dot((p / l).astype(jnp.bfloat16), v))
    return jnp.concatenate(outs, axis=1).astype(jnp.bfloat16)


def _merge_kernel(x_ref, yd_ref, ys_ref, mq_ref, mkv_ref, gmix_ref, wg_ref, bg_ref, wbr_ref, wout_ref,
                  gffn_ref, wr_ref, br_ref, h_ref, hn_ref, lg_ref):
    xf = x_ref[...]
    n = _rms(xf, gmix_ref[...]).astype(jnp.bfloat16)
    branches = (yd_ref[...], ys_ref[...], _mem_attention(mq_ref[...], mkv_ref[...]))
    merged = jnp.zeros(xf.shape, jnp.float32)
    for i, y in enumerate(branches):
        gate = jax.nn.sigmoid(_dot(n, wg_ref[i]) + bg_ref[i])
        merged = merged + gate * _dot(y, wbr_ref[i])
    h = xf + _dot(merged.astype(jnp.bfloat16), wout_ref[...])
    h_ref[...] = h
    hn = _rms(h, gffn_ref[...]).astype(jnp.bfloat16)
    hn_ref[...] = hn
    lg_ref[...] = _dot(hn, wr_ref[...]) + br_ref[...]


def _merge(x2, y_diff, y_dsa, p, mkv, seq, g_mix, w_g, b_g, w_br, w_out, g_ffn, w_r, b_r):
    t = x2.shape[0]
    tiles_per_seq = seq // MERGE_TM
    row = lambda w: pl.BlockSpec((MERGE_TM, w), lambda i: (i, 0))
    full = lambda a: pl.BlockSpec(a.shape, lambda i: (0,) * a.ndim, pipeline_mode=pl.Buffered(1))
    return pl.pallas_call(
        _merge_kernel,
        grid=(t // MERGE_TM,),
        in_specs=[row(D_MODEL), row(512), row(512),
                  pl.BlockSpec((MERGE_TM, 512), lambda i: (i, 7)),
                  pl.BlockSpec((MEM_LEN, mkv.shape[1]), lambda i: (i // tiles_per_seq, 0)),
                  full(g_mix), full(w_g), full(b_g), full(w_br), full(w_out), full(g_ffn), full(w_r), full(b_r)],
        out_specs=[row(D_MODEL), row(D_MODEL), row(LANES)],
        out_shape=[
            jax.ShapeDtypeStruct((t, D_MODEL), jnp.float32),
            jax.ShapeDtypeStruct((t, D_MODEL), jnp.bfloat16),
            jax.ShapeDtypeStruct((t, LANES), jnp.float32),
        ],
        compiler_params=pltpu.CompilerParams(
            dimension_semantics=("parallel",), vmem_limit_bytes=VMEM_LIMIT),
        name="merge",
    )(x2, y_diff, y_dsa, p, mkv, g_mix, w_g, b_g, w_br, w_out, g_ffn, w_r, b_r)


MOE_ST = 2048
MOE_SUB = 512
MOE_ALIGN = 16
MOE_BLK = 256
MOE_NSUB = MOE_ST // MOE_SUB
MOE_ZROWS = MOE_SUB + N_GROUPS * MOE_ALIGN
MOE_ZK = -(-MOE_ZROWS // LANES) * LANES
MOE_GROWS = MOE_ST + MOE_NSUB * N_GROUPS * MOE_ALIGN + MOE_BLK
_SEG_ROWS, _SEG_LOCAL, _SEG_GLOBAL = 0, 16, 32
_GRP_BASE, _GRP_ROWS = 48, 52


def _route(lg):
    lane = lax.broadcasted_iota(jnp.int32, lg.shape, 1).astype(jnp.float32)
    big = float(LANES)
    gl = jnp.where(lane < N_GROUPS, lg, -jnp.inf)
    gmax = jnp.max(gl, axis=1, keepdims=True)
    grp = jnp.min(jnp.where(gl == gmax, lane, big), axis=1, keepdims=True)
    gsum = jnp.sum(jnp.where(lane < N_GROUPS, jnp.exp(gl - gmax), 0.0), axis=1, keepdims=True)
    p_grp = 1.0 / gsum
    lo = N_GROUPS + grp * EXPERTS_PER_GROUP
    el = jnp.where((lane >= lo) & (lane < lo + EXPERTS_PER_GROUP), lg, -jnp.inf)
    e1 = jnp.max(el, axis=1, keepdims=True)
    i1 = jnp.min(jnp.where(el == e1, lane, big), axis=1, keepdims=True)
    el2 = jnp.where(lane == i1, -jnp.inf, el)
    e2 = jnp.max(el2, axis=1, keepdims=True)
    i2 = jnp.min(jnp.where(el2 == e2, lane, big), axis=1, keepdims=True)
    r = jnp.exp(e2 - e1)
    w1 = p_grp / (1.0 + r)
    w2 = p_grp * r / (1.0 + r)
    return jnp.where(lane == i1, w1, 0.0) + jnp.where(lane == i2, w2, 0.0), grp


def _split3(x):
    a = x.astype(jnp.bfloat16)
    r = x - a.astype(jnp.float32)
    b = r.astype(jnp.bfloat16)
    c = (r - b.astype(jnp.float32)).astype(jnp.bfloat16)
    return a, b, c


def _moe_kernel(hn_ref, lg_ref, h_ref, gfin_ref, win_hbm, wout_hbm, o_ref,
                g_buf, ys, cw_s, dest_s, z_buf, cwz_buf, zy_buf, win_buf, wout_buf, sem, seg):
    j = pl.program_id(1)
    sub, al, bf = MOE_SUB, MOE_ALIGN, jnp.bfloat16
    lane_row = lax.broadcasted_iota(jnp.int32, (1, LANES), 1)

    def copy_rows(src_ref, dst_ref, src, dst, nblk):
        def body(i, carry):
            s0 = pl.multiple_of(src + i * al, al)
            d0 = pl.multiple_of(dst + i * al, al)
            for s_ref, d_ref in zip(src_ref, dst_ref):
                d_ref[pl.ds(d0, al), :] = s_ref[pl.ds(s0, al), :]
            return carry
        lax.fori_loop(0, nblk, body, 0)

    @pl.when(j == 0)
    def _():
        cw, grp = _route(lg_ref[...])
        lane = lax.broadcasted_iota(jnp.int32, (sub, LANES), 1).astype(jnp.float32)
        hots = [lane == grp[u * sub:(u + 1) * sub] for u in range(MOE_NSUB)]
        for u in range(MOE_NSUB):
            n_vec = jnp.sum(jnp.where(hots[u], 1.0, 0.0), axis=0, keepdims=True)
            off = jnp.int32(0)
            for g in range(N_GROUPS):
                n = jnp.sum(jnp.where(lane_row == g, n_vec, 0.0)).astype(jnp.int32)
                rows = ((n + (al - 1)) // al) * al
                seg[_SEG_ROWS + u * N_GROUPS + g] = rows
                seg[_SEG_LOCAL + u * N_GROUPS + g] = off
                off = off + rows
        base = jnp.int32(0)
        for g in range(N_GROUPS):
            seg[_GRP_BASE + g] = base
            pos = base
            for u in range(MOE_NSUB):
                seg[_SEG_GLOBAL + u * N_GROUPS + g] = pos
                pos = pos + seg[_SEG_ROWS + u * N_GROUPS + g]
            seg[_GRP_ROWS + g] = pos - base
            base = pos

        ri = lax.broadcasted_iota(jnp.int32, (sub, sub), 0)
        ci = lax.broadcasted_iota(jnp.int32, (sub, sub), 1)
        earlier = jnp.where(ci < ri, 1.0, 0.0).astype(bf)
        zrow = lax.broadcasted_iota(jnp.int32, (MOE_ZROWS, sub), 0).astype(jnp.float32)
        zy_buf[...] = jnp.zeros(zy_buf.shape, jnp.float32)
        for u in range(MOE_NSUB):
            hot = hots[u]
            before = _dot(earlier, jnp.where(hot, 1.0, 0.0).astype(bf))
            start = jnp.zeros((1, LANES), jnp.float32)
            for g in range(N_GROUPS):
                start = jnp.where(lane_row == g, seg[_SEG_LOCAL + u * N_GROUPS + g].astype(jnp.float32), start)
            dest = jnp.sum(jnp.where(hot, before + start, 0.0), axis=1, keepdims=True)
            dest_b = jnp.broadcast_to(dest, (sub, LANES))
            dest_s[u] = dest_b
            perm = jnp.where(zrow == dest_b.T[0:1, :], 1.0, 0.0).astype(bf)
            z_buf[...] = _dot(perm, hn_ref[u * sub:(u + 1) * sub, :]).astype(bf)
            c3 = _split3(cw[u * sub:(u + 1) * sub])
            cwz_buf[...] = _dot(perm, c3[0]) + _dot(perm, c3[1]) + _dot(perm, c3[2])
            for g in range(N_GROUPS):
                k = u * N_GROUPS + g
                copy_rows((z_buf, cwz_buf), (g_buf, cw_s), seg[_SEG_LOCAL + k], seg[_SEG_GLOBAL + k],
                          seg[_SEG_ROWS + k] // al)

        def weights(e, slot):
            return (pltpu.make_async_copy(win_hbm.at[e], win_buf.at[slot], sem.at[0, slot]),
                    pltpu.make_async_copy(wout_hbm.at[e], wout_buf.at[slot], sem.at[1, slot]))

        for cp in weights(0, 0):
            cp.start()

        def expert(e, carry):
            slot = e % 2
            g = e // EXPERTS_PER_GROUP

            @pl.when(e + 1 < N_EXPERTS)
            def _():
                for cp in weights(e + 1, 1 - slot):
                    cp.start()

            for cp in weights(e, slot):
                cp.wait()
            start = seg[_GRP_BASE + g]
            rows = seg[_GRP_ROWS + g]
            big = 2 * MOE_BLK
            nbig = rows // big
            rem = rows - nbig * big
            first = e % EXPERTS_PER_GROUP == 0

            def block(row0, n):
                row0 = pl.multiple_of(row0, al)
                gu = _dot(g_buf[pl.ds(row0, n), :], win_buf[slot])
                gt = gu[:, :D_EXPERT]
                hid = (gt * jax.nn.sigmoid(gt)) * gu[:, D_EXPERT:]
                y = _dot(hid.astype(bf), wout_buf[slot])
                ln = lax.broadcasted_iota(jnp.int32, (n, LANES), 1)
                cw_e = jnp.sum(jnp.where(ln == N_GROUPS + e, cw_s[pl.ds(row0, n), :], 0.0), axis=1, keepdims=True)

                @pl.when(first)
                def _():
                    ys[pl.ds(row0, n), :] = cw_e * y

                @pl.when(jnp.logical_not(first))
                def _():
                    ys[pl.ds(row0, n), :] += cw_e * y

            def big_block(b, c):
                block(start + b * big, big)
                return c

            lax.fori_loop(0, nbig, big_block, 0)

            tail = start + nbig * big

            @pl.when(rem > MOE_BLK)
            def _():
                block(tail, big)

            @pl.when((rem > MOE_BLK // 2) & (rem <= MOE_BLK))
            def _():
                block(tail, MOE_BLK)

            @pl.when((rem > 0) & (rem <= MOE_BLK // 2))
            def _():
                block(tail, MOE_BLK // 2)

            return carry

        lax.fori_loop(0, N_EXPERTS, expert, 0)

    @pl.when(j > 0)
    def _():
        u = j - 1
        for g in range(N_GROUPS):
            k = u * N_GROUPS + g
            copy_rows((ys,), (zy_buf,), seg[_SEG_GLOBAL + k], seg[_SEG_LOCAL + k], seg[_SEG_ROWS + k] // al)
        zy = zy_buf[...]
        hi = zy.astype(bf)
        lo = (zy - hi.astype(jnp.float32)).astype(bf)
        dest = jnp.concatenate([dest_s[u]] * (MOE_ZK // LANES), axis=1)
        col = lax.broadcasted_iota(jnp.int32, (sub, MOE_ZK), 1).astype(jnp.float32)
        unperm = jnp.where(col == dest, 1.0, 0.0).astype(bf)
        moe = _dot(unperm, hi) + _dot(unperm, lo)
        o_ref[...] = _rms(h_ref[...] + moe, gfin_ref[...])


def _moe(hn, lg, h, w_e_in, w_e_out, g_final):
    t = hn.shape[0]
    row = lambda s, j: (s * MOE_NSUB + jnp.maximum(j - 1, 0), 0)
    return pl.pallas_call(
        _moe_kernel,
        grid=(t // MOE_ST, 1 + MOE_NSUB),
        in_specs=[
            pl.BlockSpec((MOE_ST, D_MODEL), lambda s, j: (s, 0)),
            pl.BlockSpec((MOE_ST, LANES), lambda s, j: (s, 0)),
            pl.BlockSpec((MOE_SUB, D_MODEL), row),
            pl.BlockSpec((1, D_MODEL), lambda s, j: (0, 0)),
            pl.BlockSpec(memory_space=pl.ANY),
            pl.BlockSpec(memory_space=pl.ANY),
        ],
        out_specs=pl.BlockSpec((MOE_SUB, D_MODEL), row),
        out_shape=jax.ShapeDtypeStruct((t, D_MODEL), jnp.float32),
        scratch_shapes=[
            pltpu.VMEM((MOE_GROWS, D_MODEL), jnp.bfloat16),
            pltpu.VMEM((MOE_GROWS, D_MODEL), jnp.float32),
            pltpu.VMEM((MOE_GROWS, LANES), jnp.float32),
            pltpu.VMEM((MOE_NSUB, MOE_SUB, LANES), jnp.float32),
            pltpu.VMEM((MOE_ZROWS, D_MODEL), jnp.bfloat16),
            pltpu.VMEM((MOE_ZROWS, LANES), jnp.float32),
            pltpu.VMEM((MOE_ZK, D_MODEL), jnp.float32),
            pltpu.VMEM((2, D_MODEL, 2 * D_EXPERT), jnp.bfloat16),
            pltpu.VMEM((2, D_EXPERT, D_MODEL), jnp.bfloat16),
            pltpu.SemaphoreType.DMA((2, 2)),
            pltpu.SMEM((64,), jnp.int32),
        ],
        compiler_params=pltpu.CompilerParams(
            dimension_semantics=("parallel", "arbitrary"), vmem_limit_bytes=VMEM_LIMIT),
        name="moe",
    )(hn, lg, h, g_final, w_e_in, w_e_out)


def kernel(x, positions, mem, g_mix, w_in, b_gate, lambda_q1, lambda_k1, lambda_q2, lambda_k2,
           g_diff_sub, g_mem, w_mem_kv, w_br_diff, w_br_dsa, w_br_mem, w_out, g_ffn,
           w_route_group, b_route_group, w_route_expert, b_route_expert, w_exp_in, w_exp_out,
           g_final):
    b, s, d = x.shape
    t = b * s
    bf = jnp.bfloat16
    top_k = min(TOPK_MAX, s // 4)
    assert d == D_MODEL and s % DIFF_T == 0 and s % DSA_TQ == 0 and top_k <= DSA_TQ
    assert PROJ_TM == DIFF_T == DSA_TK and DSA_TK % DSA_TQ == 0, "proj writes V^T in the attention kernels' key tiles"
    assert g_mix.shape[0] == 1, "single layer"
    lam_init = 0.8 - 0.6 * math.exp(-0.3 * 0)

    wi = w_in[0]
    c = 512
    seg = lambda k: wi[:, k * c:(k + 1) * c]
    o_ik = 7 * c
    w_ik = wi[:, o_ik:o_ik + IDX_DIM]
    w_iw = wi[:, o_ik + IDX_DIM:o_ik + IDX_DIM + IDX_HEADS]
    o_mq = o_ik + IDX_DIM + IDX_HEADS
    w_mq = wi[:, o_mq:o_mq + c]
    w_gl = wi[:, o_mq + c:]
    qs = DIFF_QK_DIM ** -0.5
    qs2 = qs * LOG2E
    src = _pair_lane_source()
    src_seg = np.concatenate([blk * LANES + src for blk in range(c // LANES)])
    rseg = lambda k: seg(k)[:, src_seg]
    w_a = jnp.concatenate([rseg(0) * qs2, rseg(1), seg(2), rseg(3) * qs2, rseg(4), seg(5), rseg(6) * qs, w_mq],
                          axis=1).astype(bf)
    w_s = jnp.concatenate([jnp.concatenate([w_ik, w_ik], axis=1)[:, src], w_iw,
                           jnp.zeros((d, LANES - IDX_HEADS), wi.dtype)], axis=1).astype(bf)
    w_g = w_gl.reshape(d, 3, d).transpose(1, 0, 2).astype(bf)
    b_g = b_gate[0].reshape(3, 1, d)
    w_br = jnp.stack([w_br_diff[0], w_br_dsa[0], w_br_mem[0]]).astype(bf)
    w_r = jnp.concatenate([w_route_group[0], w_route_expert[0],
                           jnp.zeros((d, LANES - N_GROUPS - N_EXPERTS), wi.dtype)], axis=1).astype(bf)
    b_r = jnp.concatenate([b_route_group[0], b_route_expert[0],
                           jnp.zeros((LANES - N_GROUPS - N_EXPERTS,), jnp.float32)]).reshape(1, LANES)

    rot = IDX_DIM // ROPE_FRACTION
    inv_freq = ROPE_THETA ** (-jnp.arange(0, rot, 2, dtype=jnp.float32) / rot)
    inv_col = inv_freq.reshape(rot // 2, 1)
    expand = jnp.asarray(_rope_expanders(), bf)

    x2 = x.reshape(t, d)
    pos_rows = positions.reshape(t // PROJ_TM, 1, PROJ_TM)
    p, ikk, dvt, svt, iwt = _proj(x2, pos_rows, g_mix, inv_col, expand, w_a, w_s)

    y_diff = _diff_attention(p, dvt, lambda_q1, lambda_k1, lambda_q2, lambda_k2,
                             g_diff_sub.reshape(DIFF_V_DIM, 1), b, s, lam_init)
    y_dsa = _dsa_attention(p, ikk, iwt, svt, b, s, top_k)
    mkv = _memkv(mem.reshape(b * MEM_LEN, d), g_mem, w_mem_kv[0].astype(bf), b)
    h, hn, lg = _merge(x2, y_diff, y_dsa, p, mkv, s, g_mix, w_g, b_g, w_br, w_out[0].astype(bf), g_ffn, w_r, b_r)
    out = _moe(hn, lg, h, w_exp_in[0].astype(bf), w_exp_out[0].astype(bf), g_final.reshape(1, d))
    return out.reshape(b, s, d)
```

```python
import functools
import math

import jax
import jax.numpy as jnp
import numpy as np
from jax import lax
from jax.experimental import pallas as pl
from jax.experimental.pallas import tpu as pltpu

D_MODEL = 1024
MEM_LEN = 256
XA_HEADS = 4
XA_HEAD_DIM = 128
DIFF_HEADS = 4
DIFF_QK_DIM = 64
DIFF_V_DIM = 128
DSA_HEADS = 8
DSA_HEAD_DIM = 64
IDX_HEADS = 8
IDX_DIM = 64
TOPK_MAX = 256
ROPE_THETA = 500000.0
ROPE_FRACTION = 4
N_GROUPS = 4
EXPERTS_PER_GROUP = 4
N_EXPERTS = 16
D_EXPERT = 512
EPS = 1e-6

LANES = 128
SUBLANES = 8
VMEM_LIMIT = 56 * 1024 * 1024
NEG_BIG = -1e30
ONES_ROWS = 16
LOG2E = math.log2(math.e)

PROJ_TM = 512
PROJ_TN = 512
DIFF_T = 512
DSA_TQ = 256
DSA_TK = 512
MERGE_TM = 512

_NT = (((1,), (1,)), ((), ()))


def _dot(a, b):
    return jnp.dot(a, b, preferred_element_type=jnp.float32)


def _dot_nt(a, b):
    return lax.dot_general(a, b, _NT, preferred_element_type=jnp.float32)


def _rms(xf, g):
    return xf * lax.rsqrt(jnp.mean(xf * xf, axis=-1, keepdims=True) + EPS) * g


def _pair_lane_source():
    half, rot = IDX_DIM, IDX_DIM // ROPE_FRACTION
    hr = rot // 2
    src = np.zeros(LANES, np.int32)
    src[0:hr] = np.arange(0, hr)
    src[hr:rot] = half + np.arange(0, hr)
    src[rot:half] = np.arange(rot, half)
    src[half:half + hr] = np.arange(hr, rot)
    src[half + hr:half + rot] = half + np.arange(hr, rot)
    src[half + rot:] = half + np.arange(rot, half)
    return src


def _first_head_lanes(shape):
    half, rot = IDX_DIM, IDX_DIM // ROPE_FRACTION
    lane = lax.broadcasted_iota(jnp.int32, shape, len(shape) - 1) % LANES
    return (lane < rot // 2) | ((lane >= rot) & (lane < half + rot // 2))


def _rope_expanders():
    rot = IDX_DIM // ROPE_FRACTION
    e = np.zeros((2, LANES, LANES), np.float32)
    for l in list(range(rot)) + list(range(IDX_DIM, IDX_DIM + rot)):
        e[0, l % (rot // 2), l] = 1.0
        e[1, l % (rot // 2), l] = -1.0 if l < rot else 1.0
    return e


_ROPE_SEGMENTS = (0, 1, 3, 4, 6)


def _proj_kernel(x_ref, posr_ref, g_ref, inv_ref, e_ref, wa_ref, ws_ref,
                 p_ref, ikk_ref, dvt_ref, svt_ref, iwt_ref):
    xf = x_ref[...]
    tm = xf.shape[0]
    n = _rms(xf, g_ref[...]).astype(jnp.bfloat16)
    ang = posr_ref[0].astype(jnp.float32) * inv_ref[...]
    pad = jnp.zeros((LANES - ang.shape[0], tm), jnp.float32)
    ct = jnp.concatenate([jnp.cos(ang), pad], axis=0).T
    st = jnp.concatenate([jnp.sin(ang), pad], axis=0).T
    crot = sum(_dot(part, e_ref[0]) for part in _split3(ct))
    lane = lax.broadcasted_iota(jnp.int32, crot.shape, 1) % IDX_DIM
    cos_t = jnp.where(lane < IDX_DIM // ROPE_FRACTION, crot, 1.0)
    sin_t = sum(_dot(part, e_ref[1]) for part in _split3(st))

    def rope(v):
        k = v.shape[1] // LANES
        blocks = [pltpu.roll(v[:, b * LANES:(b + 1) * LANES], LANES // 2, 1) for b in range(k)]
        partner = jnp.concatenate(blocks, axis=1) if k > 1 else blocks[0]
        c = jnp.concatenate([cos_t] * k, axis=1) if k > 1 else cos_t
        s = jnp.concatenate([sin_t] * k, axis=1) if k > 1 else sin_t
        return v * c + partner * s

    small = _dot(n, ws_ref[...])
    ikk_ref[...] = rope(small[:, :LANES]).astype(jnp.bfloat16)
    iwt_ref[...] = (small[:, LANES:] * (IDX_HEADS ** -0.5)).T[:IDX_HEADS, :]

    for j in range(wa_ref.shape[1] // PROJ_TN):
        cols = slice(j * PROJ_TN, (j + 1) * PROJ_TN)
        acc = _dot(n, wa_ref[:, cols])
        p_ref[:, cols] = (rope(acc) if j in _ROPE_SEGMENTS else acc).astype(jnp.bfloat16)
        if j == 2:
            dvt_ref[0] = acc.T.astype(jnp.bfloat16)
        if j == 5:
            svt_ref[0] = acc.T.astype(jnp.bfloat16)


def _proj(x2, pos_rows, g_mix, inv_col, expand, w_a, w_s):
    t = x2.shape[0]
    once = pl.Buffered(1)
    return pl.pallas_call(
        _proj_kernel,
        grid=(t // PROJ_TM,),
        in_specs=[
            pl.BlockSpec((PROJ_TM, D_MODEL), lambda i: (i, 0)),
            pl.BlockSpec((1, 1, PROJ_TM), lambda i: (i, 0, 0)),
            pl.BlockSpec((1, D_MODEL), lambda i: (0, 0)),
            pl.BlockSpec(inv_col.shape, lambda i: (0, 0)),
            pl.BlockSpec(expand.shape, lambda i: (0, 0, 0)),
            pl.BlockSpec(w_a.shape, lambda i: (0, 0), pipeline_mode=once),
            pl.BlockSpec(w_s.shape, lambda i: (0, 0), pipeline_mode=once),
        ],
        out_specs=[
            pl.BlockSpec((PROJ_TM, w_a.shape[1]), lambda i: (i, 0)),
            pl.BlockSpec((PROJ_TM, LANES), lambda i: (i, 0)),
            pl.BlockSpec((1, PROJ_TN, PROJ_TM), lambda i: (i, 0, 0)),
            pl.BlockSpec((1, PROJ_TN, PROJ_TM), lambda i: (i, 0, 0)),
            pl.BlockSpec((IDX_HEADS, PROJ_TM), lambda i: (0, i)),
        ],
        out_shape=[
            jax.ShapeDtypeStruct((t, w_a.shape[1]), jnp.bfloat16),
            jax.ShapeDtypeStruct((t, LANES), jnp.bfloat16),
            jax.ShapeDtypeStruct((t // PROJ_TM, PROJ_TN, PROJ_TM), jnp.bfloat16),
            jax.ShapeDtypeStruct((t // PROJ_TM, PROJ_TN, PROJ_TM), jnp.bfloat16),
            jax.ShapeDtypeStruct((IDX_HEADS, t), jnp.float32),
        ],
        compiler_params=pltpu.CompilerParams(
            dimension_semantics=("parallel",), vmem_limit_bytes=VMEM_LIMIT),
        name="proj",
    )(x2, pos_rows, g_mix, inv_col, expand, w_a, w_s)


DIFF_HPS = 2


def _diff_kernel(q_ref, k_ref, vt_ref, lq1_ref, lk1_ref, lq2_ref, lk2_ref, gs_ref, o_ref,
                 qm, m_s, acc, st_a, st_b, p_scr, *, lam_init):
    qi = pl.program_id(2)
    t = DIFF_T
    nch = 2 * DIFF_HPS
    for hd in range(DIFF_HPS):
        q = q_ref[:, hd * LANES:(hd + 1) * LANES]
        first = _first_head_lanes(q.shape)
        qm[2 * hd] = jnp.where(first, q, jnp.zeros_like(q))
        qm[2 * hd + 1] = jnp.where(first, jnp.zeros_like(q), q)
    m_s[...] = jnp.full(m_s.shape, NEG_BIG, jnp.float32)
    acc[...] = jnp.zeros(acc.shape, jnp.float32)

    def qk(j, st_ref):
        k = k_ref[pl.ds(pl.multiple_of(j * t, t), t), :]
        for ch in range(nch):
            hd = ch // 2
            st_ref[ch] = _dot_nt(k[:, hd * LANES:(hd + 1) * LANES], qm[ch])

    def softmax_pv(j, st_ref, diagonal):
        vt_all = vt_ref[j]
        ones = jnp.ones((ONES_ROWS, t), jnp.bfloat16)
        if diagonal:
            krow = lax.broadcasted_iota(jnp.int32, (t, t), 0)
            qcol = lax.broadcasted_iota(jnp.int32, (t, t), 1)
            keep = krow <= qcol
        alphas = []
        for ch in range(nch):
            st = st_ref[ch]
            if diagonal:
                st = jnp.where(keep, st, NEG_BIG)
            m_old = m_s[ch]
            m_new = jnp.maximum(m_old, jnp.max(st, axis=0, keepdims=True))
            alphas.append(jnp.exp2(m_old - m_new))
            m_s[ch] = m_new
            p_scr[ch] = jnp.exp2(st - m_new).astype(jnp.bfloat16)
        for ch in range(nch):
            hd = ch // 2
            vt = jnp.concatenate([vt_all[hd * DIFF_V_DIM:(hd + 1) * DIFF_V_DIM, :], ones], axis=0)
            acc[ch] = alphas[ch] * acc[ch] + _dot(vt, p_scr[ch])

    qk(0, st_a)

    def pair(tt, carry):
        j = 2 * tt
        qk(j + 1, st_b)
        softmax_pv(j, st_a, False)
        qk(j + 2, st_a)
        softmax_pv(j + 1, st_b, False)
        return carry

    lax.fori_loop(0, qi // 2, pair, 0)

    @pl.when(qi % 2 == 0)
    def _():
        softmax_pv(qi, st_a, True)

    @pl.when(qi % 2 == 1)
    def _():
        qk(qi, st_b)
        softmax_pv(qi - 1, st_a, False)
        softmax_pv(qi, st_b, True)

    lam = (jnp.exp(jnp.sum(lq1_ref[...] * lk1_ref[...], axis=1, keepdims=True))
           - jnp.exp(jnp.sum(lq2_ref[...] * lk2_ref[...], axis=1, keepdims=True))
           + lam_init)
    dv = DIFF_V_DIM
    outs = []
    for hd in range(DIFF_HPS):
        a1, a2 = acc[2 * hd], acc[2 * hd + 1]
        ot = a1[:dv, :] / a1[dv:dv + 1, :] - lam * (a2[:dv, :] / a2[dv:dv + 1, :])
        yt = ot * lax.rsqrt(jnp.mean(ot * ot, axis=0, keepdims=True) + EPS) * gs_ref[...]
        outs.append((yt * (1.0 - lam_init)).T)
    o_ref[...] = jnp.concatenate(outs, axis=1).astype(o_ref.dtype)


def _diff_attention(p, dvt, lq1, lk1, lq2, lk2, g_sub_col, batch, seq, lam_init):
    nb = seq // DIFF_T
    w = DIFF_HPS * LANES
    nch = 2 * DIFF_HPS
    vec = pl.BlockSpec((1, DIFF_QK_DIM), lambda b, h, qi: (0, 0))
    return pl.pallas_call(
        functools.partial(_diff_kernel, lam_init=lam_init),
        grid=(batch, DIFF_HEADS // DIFF_HPS, nb),
        in_specs=[
            pl.BlockSpec((DIFF_T, w), lambda b, h, qi: (b * nb + qi, h)),
            pl.BlockSpec((seq, w), lambda b, h, qi: (b, DIFF_HEADS // DIFF_HPS + h)),
            pl.BlockSpec((nb, DIFF_HPS * DIFF_V_DIM, DIFF_T), lambda b, h, qi: (b, h, 0)),
            vec, vec, vec, vec,
            pl.BlockSpec((DIFF_V_DIM, 1), lambda b, h, qi: (0, 0)),
        ],
        out_specs=pl.BlockSpec((DIFF_T, w), lambda b, h, qi: (b * nb + qi, h)),
        out_shape=jax.ShapeDtypeStruct((batch * seq, DIFF_HEADS * DIFF_V_DIM), jnp.bfloat16),
        scratch_shapes=[
            pltpu.VMEM((nch, DIFF_T, LANES), jnp.bfloat16),
            pltpu.VMEM((nch, 1, DIFF_T), jnp.float32),
            pltpu.VMEM((nch, DIFF_V_DIM + ONES_ROWS, DIFF_T), jnp.float32),
            pltpu.VMEM((nch, DIFF_T, DIFF_T), jnp.float32),
            pltpu.VMEM((nch, DIFF_T, DIFF_T), jnp.float32),
            pltpu.VMEM((nch, DIFF_T, DIFF_T), jnp.bfloat16),
        ],
        compiler_params=pltpu.CompilerParams(
            dimension_semantics=("parallel", "parallel", "arbitrary"),
            vmem_limit_bytes=VMEM_LIMIT),
        name="diffattn",
    )(p, p, dvt, lq1, lk1, lq2, lk2, g_sub_col)


def _key_to_float(key):
    bits = jnp.where(key >= 0, key, key ^ jnp.int32(0x7FFFFFFF))
    return lax.bitcast_convert_type(bits, jnp.float32)


def _count_rows(hit):
    tk, tq = hit.shape
    return jnp.sum(hit.reshape(tk // (4 * SUBLANES), 4 * SUBLANES, tq), axis=0)


def _dsa_kernel(sq_ref, iq_ref, iwt_ref, ikk_ref, sk_ref, svt_ref, o_ref,
                score, hi16, lo16, iqm, sqm, acc, m_s, thr, sel, st_a, st_b, mx_a, mx_b, p_scr, *, seq, top_k):
    qi = pl.program_id(1)
    tq, tk = DSA_TQ, DSA_TK
    last = (qi * tq) // tk
    nkc = last + 1
    first = _first_head_lanes((tq, LANES))

    for h in range(DSA_HEADS):
        pr = h // 2
        iqp = iq_ref[:, pr * LANES:(pr + 1) * LANES]
        sqp = sq_ref[:, pr * LANES:(pr + 1) * LANES]
        mine = first if h % 2 == 0 else jnp.logical_not(first)
        iqm[h] = jnp.where(mine, iqp, jnp.zeros_like(iqp))
        sqm[h] = jnp.where(mine, sqp, jnp.zeros_like(sqp))

    krow = lax.broadcasted_iota(jnp.int32, (tk, tq), 0)
    qcol = lax.broadcasted_iota(jnp.int32, (tk, tq), 1)

    def idx_logits(c, lg_ref):
        kk = ikk_ref[pl.ds(pl.multiple_of(c * tk, tk), tk), :]
        for h in range(IDX_HEADS):
            lg_ref[h] = _dot_nt(kk, iqm[h])

    def idx_score(c, lg_ref):
        sc = jnp.zeros((tk, tq), jnp.float32)
        for h in range(IDX_HEADS):
            sc = sc + iwt_ref[h:h + 1, :] * jnp.maximum(lg_ref[h], 0.0)
        sc = jnp.where(krow <= qcol + (qi * tq - c * tk), sc, -jnp.inf)
        score[c] = sc
        bits = lax.bitcast_convert_type(sc, jnp.int32)
        okey = jnp.where(bits >= 0, bits, bits ^ jnp.int32(0x7FFFFFFF))
        hi16[c] = lax.shift_right_arithmetic(okey, 16).astype(jnp.int16)
        lo16[c] = ((okey & 0xFFFF) - 2 ** 15).astype(jnp.int16)

    idx_logits(0, st_a)

    def idx_pair(tt, carry):
        c = 2 * tt
        idx_logits(c + 1, st_b)
        idx_score(c, st_a)
        idx_logits(c + 2, st_a)
        idx_score(c + 1, st_b)
        return carry

    lax.fori_loop(0, last // 2, idx_pair, 0)

    @pl.when(last % 2 == 0)
    def _():
        idx_score(last, st_a)

    @pl.when(last % 2 == 1)
    def _():
        idx_logits(last, st_b)
        idx_score(last - 1, st_a)
        idx_score(last, st_b)

    zero_cnt = jnp.zeros((4 * SUBLANES, tq), jnp.float32)
    i16_min = -2 ** 15

    def count16(buf, pred):
        def body(c, cnt):
            hit = jnp.where(pred(buf[c]), jnp.int16(1), jnp.int16(0))
            h3 = hit.reshape(tk // (4 * SUBLANES), 4 * SUBLANES, tq)
            part = h3[0]
            for r in range(1, h3.shape[0]):
                part = part + h3[r]
            return cnt + part

        def two(t2, cnt):
            return body(2 * t2 + 1, body(2 * t2, cnt))

        cnt = lax.fori_loop(0, nkc // 2, two, jnp.zeros((4 * SUBLANES, tq), jnp.int16))
        cnt = lax.fori_loop(2 * (nkc // 2), nkc, body, cnt)
        return jnp.sum(cnt.astype(jnp.int32), axis=0, keepdims=True)

    def bisect16(buf, want):
        def bit_step(i, cur):
            cand = cur + lax.shift_left(jnp.int32(1), jnp.int32(15) - i)
            c16 = cand.astype(jnp.int16)
            return jnp.where(count16(buf, lambda blk: blk >= c16) >= want, cand, cur)
        return lax.fori_loop(0, 16, bit_step, jnp.full((1, tq), i16_min, jnp.int32))

    t_hi = bisect16(hi16, top_k)
    t_hi16 = t_hi.astype(jnp.int16)
    rest = top_k - count16(hi16, lambda blk: blk > t_hi16)

    def bucket_only(c, carry):
        lo16[c] = jnp.where(hi16[c] == t_hi16, lo16[c], jnp.int16(i16_min))
        return carry

    lax.fori_loop(0, nkc, bucket_only, 0)
    t_lo = bisect16(lo16, rest)
    key = lax.shift_left(t_hi, 16) + (t_lo - i16_min)
    t_f = _key_to_float(key)

    def counts(t):
        def body(c, carry):
            gt, ge = carry
            blk = score[c]
            return (gt + _count_rows(jnp.where(blk > t, 1.0, 0.0)),
                    ge + _count_rows(jnp.where(blk >= t, 1.0, 0.0)))
        gt, ge = lax.fori_loop(0, nkc, body, (zero_cnt, zero_cnt))
        return jnp.sum(gt, axis=0, keepdims=True), jnp.sum(ge, axis=0, keepdims=True)

    n_gt, n_ge = counts(t_f)
    q_pos = qi * tq + lax.broadcasted_iota(jnp.int32, (1, tq), 1)
    few = q_pos < top_k - 1
    sel[0], sel[1], sel[2] = t_f, n_gt, n_ge

    off = jnp.logical_not(few) & ((n_gt >= float(top_k)) | (n_ge < float(top_k)))

    @pl.when(jnp.max(jnp.where(off, 1.0, 0.0)) > 0.0)
    def _():
        def bit_step(i, cur):
            cand = cur + lax.shift_left(jnp.int32(1), jnp.int32(31) - i)
            return jnp.where(counts(_key_to_float(cand))[1] >= float(top_k), cand, cur)
        t2 = _key_to_float(lax.fori_loop(0, 32, bit_step, jnp.full((1, tq), -2 ** 31, jnp.int32)))
        g2, e2 = counts(t2)
        sel[0], sel[1], sel[2] = t2, g2, e2

    t_f, n_gt, n_ge = sel[0], sel[1], sel[2]
    thr[...] = jnp.where(few, float(jnp.finfo(jnp.float32).min), t_f)
    need = float(top_k) - n_gt
    split = jnp.logical_and(jnp.logical_not(few), n_ge > float(top_k))

    @pl.when(jnp.max(jnp.where(split, 1.0, 0.0)) > 0.0)
    def _():
        def count_eq_below(jc):
            def body(c, cnt):
                hit = jnp.where((score[c] == t_f) & (c * tk + krow < jc), 1.0, 0.0)
                return cnt + _count_rows(hit)
            return jnp.sum(lax.fori_loop(0, nkc, body, zero_cnt), axis=0, keepdims=True)

        nbits = (seq - 1).bit_length()

        def jbit(i, jv):
            cand = jv + lax.shift_left(jnp.int32(1), jnp.int32(nbits - 1) - i)
            return jnp.where(count_eq_below(cand) < need, cand, jv)

        jv = lax.fori_loop(0, nbits, jbit, jnp.zeros((1, tq), jnp.int32))

        def drop_ties(c, carry):
            blk = score[c]
            score[c] = jnp.where(split & (blk == t_f) & (c * tk + krow > jv), -jnp.inf, blk)
            return carry

        lax.fori_loop(0, nkc, drop_ties, 0)

    m_s[...] = jnp.full(m_s.shape, NEG_BIG, jnp.float32)
    acc[...] = jnp.zeros(acc.shape, jnp.float32)

    def qk(c, slot):
        st_ref, mx_ref = slot
        off = pl.multiple_of(c * tk, tk)
        bias = jnp.where(score[c] >= thr[...], 0.0, NEG_BIG)
        for h in range(DSA_HEADS):
            kp = sk_ref[pl.ds(off, tk), (h // 2) * LANES:(h // 2 + 1) * LANES]
            st = _dot_nt(kp, sqm[h]) + bias
            st_ref[h] = st
            mx_ref[h] = jnp.max(st, axis=0, keepdims=True)

    def softmax_pv(c, slot):
        st_ref, mx_ref = slot
        alphas = []
        for h in range(DSA_HEADS):
            m_old = m_s[h]
            m_new = jnp.maximum(m_old, mx_ref[h])
            alphas.append(jnp.exp2(m_old - m_new))
            m_s[h] = m_new
            p_scr[h] = jnp.exp2(st_ref[h] - m_new).astype(jnp.bfloat16)
        ones = jnp.ones((ONES_ROWS, tk), jnp.bfloat16)
        for h in range(DSA_HEADS):
            vt = jnp.concatenate([svt_ref[c, h * DSA_HEAD_DIM:(h + 1) * DSA_HEAD_DIM, :], ones], axis=0)
            acc[h] = alphas[h] * acc[h] + _dot(vt, p_scr[h])

    slot_a, slot_b = (st_a, mx_a), (st_b, mx_b)
    qk(0, slot_a)

    def pair(tt, carry):
        c = 2 * tt
        qk(c + 1, slot_b)
        softmax_pv(c, slot_a)
        qk(c + 2, slot_a)
        softmax_pv(c + 1, slot_b)
        return carry

    lax.fori_loop(0, last // 2, pair, 0)

    @pl.when(last % 2 == 0)
    def _():
        softmax_pv(last, slot_a)

    @pl.when(last % 2 == 1)
    def _():
        qk(last, slot_b)
        softmax_pv(last - 1, slot_a)
        softmax_pv(last, slot_b)

    dh = DSA_HEAD_DIM
    outs = [acc[h, :dh, :] / acc[h, dh:dh + 1, :] for h in range(DSA_HEADS)]
    o_ref[...] = jnp.concatenate(outs, axis=0).T.astype(o_ref.dtype)


def _dsa_attention(p, ikk, iwt, svt, batch, seq, top_k):
    nq = seq // DSA_TQ
    w = DSA_HEADS * DSA_HEAD_DIM
    once = pl.Buffered(1)
    return pl.pallas_call(
        functools.partial(_dsa_kernel, seq=seq, top_k=top_k),
        grid=(batch, nq),
        in_specs=[
            pl.BlockSpec((DSA_TQ, w), lambda b, qi: (b * nq + qi, 3)),
            pl.BlockSpec((DSA_TQ, w), lambda b, qi: (b * nq + qi, 6)),
            pl.BlockSpec((IDX_HEADS, DSA_TQ), lambda b, qi: (0, b * nq + qi)),
            pl.BlockSpec((seq, LANES), lambda b, qi: (b, 0), pipeline_mode=once),
            pl.BlockSpec((seq, w), lambda b, qi: (b, 4), pipeline_mode=once),
            pl.BlockSpec((seq // DSA_TK, w, DSA_TK), lambda b, qi: (b, 0, 0), pipeline_mode=once),
        ],
        out_specs=pl.BlockSpec((DSA_TQ, w), lambda b, qi: (b * nq + qi, 0)),
        out_shape=jax.ShapeDtypeStruct((batch * seq, w), jnp.bfloat16),
        scratch_shapes=[
            pltpu.VMEM((seq // DSA_TK, DSA_TK, DSA_TQ), jnp.float32),
            pltpu.VMEM((seq // DSA_TK, DSA_TK, DSA_TQ), jnp.int16),
            pltpu.VMEM((seq // DSA_TK, DSA_TK, DSA_TQ), jnp.int16),
            pltpu.VMEM((IDX_HEADS, DSA_TQ, LANES), jnp.bfloat16),
            pltpu.VMEM((DSA_HEADS, DSA_TQ, LANES), jnp.bfloat16),
            pltpu.VMEM((DSA_HEADS, DSA_HEAD_DIM + ONES_ROWS, DSA_TQ), jnp.float32),
            pltpu.VMEM((DSA_HEADS, 1, DSA_TQ), jnp.float32),
            pltpu.VMEM((1, DSA_TQ), jnp.float32),
            pltpu.VMEM((3, 1, DSA_TQ), jnp.float32),
            pltpu.VMEM((DSA_HEADS, DSA_TK, DSA_TQ), jnp.float32),
            pltpu.VMEM((DSA_HEADS, DSA_TK, DSA_TQ), jnp.float32),
            pltpu.VMEM((DSA_HEADS, 1, DSA_TQ), jnp.float32),
            pltpu.VMEM((DSA_HEADS, 1, DSA_TQ), jnp.float32),
            pltpu.VMEM((DSA_HEADS, DSA_TK, DSA_TQ), jnp.bfloat16),
        ],
        compiler_params=pltpu.CompilerParams(
            dimension_semantics=("parallel", "arbitrary"), vmem_limit_bytes=VMEM_LIMIT),
        name="dsa",
    )(p, p, iwt, ikk, p, svt)


def _memkv_kernel(mem_ref, g_ref, w_ref, o_ref):
    n = _rms(mem_ref[...], g_ref[...]).astype(jnp.bfloat16)
    o_ref[...] = _dot(n, w_ref[...]).astype(o_ref.dtype)


def _memkv(mem2, g_mem, w_kv, batch):
    return pl.pallas_call(
        _memkv_kernel,
        grid=(batch,),
        in_specs=[
            pl.BlockSpec((MEM_LEN, D_MODEL), lambda b: (b, 0)),
            pl.BlockSpec((1, D_MODEL), lambda b: (0, 0)),
            pl.BlockSpec(w_kv.shape, lambda b: (0, 0)),
        ],
        out_specs=pl.BlockSpec((MEM_LEN, w_kv.shape[1]), lambda b: (b, 0)),
        out_shape=jax.ShapeDtypeStruct((batch * MEM_LEN, w_kv.shape[1]), jnp.bfloat16),
        compiler_params=pltpu.CompilerParams(
            dimension_semantics=("parallel",), vmem_limit_bytes=VMEM_LIMIT),
        name="memkv",
    )(mem2, g_mem, w_kv)


def _mem_attention(q, kv):
    scale = XA_HEAD_DIM ** -0.5
    outs = []
    for h in range(XA_HEADS):
        qh = q[:, h * LANES:(h + 1) * LANES]
        k = kv[:, h * LANES:(h + 1) * LANES]
        v = kv[:, (XA_HEADS + h) * LANES:(XA_HEADS + h + 1) * LANES]
        s = _dot_nt(qh, k) * scale
        m = jnp.max(s, axis=1, keepdims=True)
        p = jnp.exp(s - m)
        l = jnp.sum(p, axis=1, keepdims=True)
        outs.append(_dot((p / l).astype(jnp.bfloat16), v))
    return jnp.concatenate(outs, axis=1).astype(jnp.bfloat16)


def _merge_kernel(x_ref, yd_ref, ys_ref, mq_ref, mkv_ref, gmix_ref, wg_ref, bg_ref, wbr_ref, wout_ref,
                  gffn_ref, wr_ref, br_ref, h_ref, hn_ref, lg_ref):
    xf = x_ref[...]
    n = _rms(xf, gmix_ref[...]).astype(jnp.bfloat16)
    branches = (yd_ref[...], ys_ref[...], _mem_attention(mq_ref[...], mkv_ref[...]))
    merged = jnp.zeros(xf.shape, jnp.float32)
    for i, y in enumerate(branches):
        gate = jax.nn.sigmoid(_dot(n, wg_ref[i]) + bg_ref[i])
        merged = merged + gate * _dot(y, wbr_ref[i])
    h = xf + _dot(merged.astype(jnp.bfloat16), wout_ref[...])
    h_ref[...] = h
    hn = _rms(h, gffn_ref[...]).astype(jnp.bfloat16)
    hn_ref[...] = hn
    lg_ref[...] = _dot(hn, wr_ref[...]) + br_ref[...]


def _merge(x2, y_diff, y_dsa, p, mkv, seq, g_mix, w_g, b_g, w_br, w_out, g_ffn, w_r, b_r):
    t = x2.shape[0]
    tiles_per_seq = seq // MERGE_TM
    row = lambda w: pl.BlockSpec((MERGE_TM, w), lambda i: (i, 0))
    full = lambda a: pl.BlockSpec(a.shape, lambda i: (0,) * a.ndim, pipeline_mode=pl.Buffered(1))
    return pl.pallas_call(
        _merge_kernel,
        grid=(t // MERGE_TM,),
        in_specs=[row(D_MODEL), row(512), row(512),
                  pl.BlockSpec((MERGE_TM, 512), lambda i: (i, 7)),
                  pl.BlockSpec((MEM_LEN, mkv.shape[1]), lambda i: (i // tiles_per_seq, 0)),
                  full(g_mix), full(w_g), full(b_g), full(w_br), full(w_out), full(g_ffn), full(w_r), full(b_r)],
        out_specs=[row(D_MODEL), row(D_MODEL), row(LANES)],
        out_shape=[
            jax.ShapeDtypeStruct((t, D_MODEL), jnp.float32),
            jax.ShapeDtypeStruct((t, D_MODEL), jnp.bfloat16),
            jax.ShapeDtypeStruct((t, LANES), jnp.float32),
        ],
        compiler_params=pltpu.CompilerParams(
            dimension_semantics=("parallel",), vmem_limit_bytes=VMEM_LIMIT),
        name="merge",
    )(x2, y_diff, y_dsa, p, mkv, g_mix, w_g, b_g, w_br, w_out, g_ffn, w_r, b_r)


MOE_ST = 2048
MOE_SUB = 512
MOE_ALIGN = 16
MOE_BLK = 256
MOE_NSUB = MOE_ST // MOE_SUB
MOE_ZROWS = MOE_SUB + N_GROUPS * MOE_ALIGN
MOE_ZK = -(-MOE_ZROWS // LANES) * LANES
MOE_GROWS = MOE_ST + MOE_NSUB * N_GROUPS * MOE_ALIGN + MOE_BLK
_SEG_ROWS, _SEG_LOCAL, _SEG_GLOBAL = 0, 16, 32
_GRP_BASE, _GRP_ROWS = 48, 52


def _route(lg):
    lane = lax.broadcasted_iota(jnp.int32, lg.shape, 1).astype(jnp.float32)
    big = float(LANES)
    gl = jnp.where(lane < N_GROUPS, lg, -jnp.inf)
    gmax = jnp.max(gl, axis=1, keepdims=True)
    grp = jnp.min(jnp.where(gl == gmax, lane, big), axis=1, keepdims=True)
    gsum = jnp.sum(jnp.where(lane < N_GROUPS, jnp.exp(gl - gmax), 0.0), axis=1, keepdims=True)
    p_grp = 1.0 / gsum
    lo = N_GROUPS + grp * EXPERTS_PER_GROUP
    el = jnp.where((lane >= lo) & (lane < lo + EXPERTS_PER_GROUP), lg, -jnp.inf)
    e1 = jnp.max(el, axis=1, keepdims=True)
    i1 = jnp.min(jnp.where(el == e1, lane, big), axis=1, keepdims=True)
    el2 = jnp.where(lane == i1, -jnp.inf, el)
    e2 = jnp.max(el2, axis=1, keepdims=True)
    i2 = jnp.min(jnp.where(el2 == e2, lane, big), axis=1, keepdims=True)
    r = jnp.exp(e2 - e1)
    w1 = p_grp / (1.0 + r)
    w2 = p_grp * r / (1.0 + r)
    return jnp.where(lane == i1, w1, 0.0) + jnp.where(lane == i2, w2, 0.0), grp


def _split3(x):
    a = x.astype(jnp.bfloat16)
    r = x - a.astype(jnp.float32)
    b = r.astype(jnp.bfloat16)
    c = (r - b.astype(jnp.float32)).astype(jnp.bfloat16)
    return a, b, c


def _moe_kernel(hn_ref, lg_ref, h_ref, gfin_ref, win_hbm, wout_hbm, o_ref,
                g_buf, ys, cw_s, dest_s, z_buf, cwz_buf, zy_buf, win_buf, wout_buf, sem, seg):
    j = pl.program_id(1)
    sub, al, bf = MOE_SUB, MOE_ALIGN, jnp.bfloat16
    lane_row = lax.broadcasted_iota(jnp.int32, (1, LANES), 1)

    def copy_rows(src_ref, dst_ref, src, dst, nblk):
        def body(i, carry):
            s0 = pl.multiple_of(src + i * al, al)
            d0 = pl.multiple_of(dst + i * al, al)
            for s_ref, d_ref in zip(src_ref, dst_ref):
                d_ref[pl.ds(d0, al), :] = s_ref[pl.ds(s0, al), :]
            return carry
        lax.fori_loop(0, nblk, body, 0)

    @pl.when(j == 0)
    def _():
        cw, grp = _route(lg_ref[...])
        lane = lax.broadcasted_iota(jnp.int32, (sub, LANES), 1).astype(jnp.float32)
        hots = [lane == grp[u * sub:(u + 1) * sub] for u in range(MOE_NSUB)]
        for u in range(MOE_NSUB):
            n_vec = jnp.sum(jnp.where(hots[u], 1.0, 0.0), axis=0, keepdims=True)
            off = jnp.int32(0)
            for g in range(N_GROUPS):
                n = jnp.sum(jnp.where(lane_row == g, n_vec, 0.0)).astype(jnp.int32)
                rows = ((n + (al - 1)) // al) * al
                seg[_SEG_ROWS + u * N_GROUPS + g] = rows
                seg[_SEG_LOCAL + u * N_GROUPS + g] = off
                off = off + rows
        base = jnp.int32(0)
        for g in range(N_GROUPS):
            seg[_GRP_BASE + g] = base
            pos = base
            for u in range(MOE_NSUB):
                seg[_SEG_GLOBAL + u * N_GROUPS + g] = pos
                pos = pos + seg[_SEG_ROWS + u * N_GROUPS + g]
            seg[_GRP_ROWS + g] = pos - base
            base = pos

        ri = lax.broadcasted_iota(jnp.int32, (sub, sub), 0)
        ci = lax.broadcasted_iota(jnp.int32, (sub, sub), 1)
        earlier = jnp.where(ci < ri, 1.0, 0.0).astype(bf)
        zrow = lax.broadcasted_iota(jnp.int32, (MOE_ZROWS, sub), 0).astype(jnp.float32)
        zy_buf[...] = jnp.zeros(zy_buf.shape, jnp.float32)
        for u in range(MOE_NSUB):
            hot = hots[u]
            before = _dot(earlier, jnp.where(hot, 1.0, 0.0).astype(bf))
            start = jnp.zeros((1, LANES), jnp.float32)
            for g in range(N_GROUPS):
                start = jnp.where(lane_row == g, seg[_SEG_LOCAL + u * N_GROUPS + g].astype(jnp.float32), start)
            dest = jnp.sum(jnp.where(hot, before + start, 0.0), axis=1, keepdims=True)
            dest_b = jnp.broadcast_to(dest, (sub, LANES))
            dest_s[u] = dest_b
            perm = jnp.where(zrow == dest_b.T[0:1, :], 1.0, 0.0).astype(bf)
            z_buf[...] = _dot(perm, hn_ref[u * sub:(u + 1) * sub, :]).astype(bf)
            c3 = _split3(cw[u * sub:(u + 1) * sub])
            cwz_buf[...] = _dot(perm, c3[0]) + _dot(perm, c3[1]) + _dot(perm, c3[2])
            for g in range(N_GROUPS):
                k = u * N_GROUPS + g
                copy_rows((z_buf, cwz_buf), (g_buf, cw_s), seg[_SEG_LOCAL + k], seg[_SEG_GLOBAL + k],
                          seg[_SEG_ROWS + k] // al)

        def weights(e, slot):
            return (pltpu.make_async_copy(win_hbm.at[e], win_buf.at[slot], sem.at[0, slot]),
                    pltpu.make_async_copy(wout_hbm.at[e], wout_buf.at[slot], sem.at[1, slot]))

        for cp in weights(0, 0):
            cp.start()

        def expert(e, carry):
            slot = e % 2
            g = e // EXPERTS_PER_GROUP

            @pl.when(e + 1 < N_EXPERTS)
            def _():
                for cp in weights(e + 1, 1 - slot):
                    cp.start()

            for cp in weights(e, slot):
                cp.wait()
            start = seg[_GRP_BASE + g]
            rows = seg[_GRP_ROWS + g]
            big = 2 * MOE_BLK
            nbig = rows // big
            rem = rows - nbig * big
            first = e % EXPERTS_PER_GROUP == 0

            def block(row0, n):
                row0 = pl.multiple_of(row0, al)
                gu = _dot(g_buf[pl.ds(row0, n), :], win_buf[slot])
                gt = gu[:, :D_EXPERT]
                hid = (gt * jax.nn.sigmoid(gt)) * gu[:, D_EXPERT:]
                y = _dot(hid.astype(bf), wout_buf[slot])
                ln = lax.broadcasted_iota(jnp.int32, (n, LANES), 1)
                cw_e = jnp.sum(jnp.where(ln == N_GROUPS + e, cw_s[pl.ds(row0, n), :], 0.0), axis=1, keepdims=True)

                @pl.when(first)
                def _():
                    ys[pl.ds(row0, n), :] = cw_e * y

                @pl.when(jnp.logical_not(first))
                def _():
                    ys[pl.ds(row0, n), :] += cw_e * y

            def big_block(b, c):
                block(start + b * big, big)
                return c

            lax.fori_loop(0, nbig, big_block, 0)

            tail = start + nbig * big

            @pl.when(rem > MOE_BLK)
            def _():
                block(tail, big)

            @pl.when((rem > MOE_BLK // 2) & (rem <= MOE_BLK))
            def _():
                block(tail, MOE_BLK)

            @pl.when((rem > 0) & (rem <= MOE_BLK // 2))
            def _():
                block(tail, MOE_BLK // 2)

            return carry

        lax.fori_loop(0, N_EXPERTS, expert, 0)

    @pl.when(j > 0)
    def _():
        u = j - 1
        for g in range(N_GROUPS):
            k = u * N_GROUPS + g
            copy_rows((ys,), (zy_buf,), seg[_SEG_GLOBAL + k], seg[_SEG_LOCAL + k], seg[_SEG_ROWS + k] // al)
        zy = zy_buf[...]
        hi = zy.astype(bf)
        lo = (zy - hi.astype(jnp.float32)).astype(bf)
        dest = jnp.concatenate([dest_s[u]] * (MOE_ZK // LANES), axis=1)
        col = lax.broadcasted_iota(jnp.int32, (sub, MOE_ZK), 1).astype(jnp.float32)
        unperm = jnp.where(col == dest, 1.0, 0.0).astype(bf)
        moe = _dot(unperm, hi) + _dot(unperm, lo)
        o_ref[...] = _rms(h_ref[...] + moe, gfin_ref[...])


def _moe(hn, lg, h, w_e_in, w_e_out, g_final):
    t = hn.shape[0]
    row = lambda s, j: (s * MOE_NSUB + jnp.maximum(j - 1, 0), 0)
    return pl.pallas_call(
        _moe_kernel,
        grid=(t // MOE_ST, 1 + MOE_NSUB),
        in_specs=[
            pl.BlockSpec((MOE_ST, D_MODEL), lambda s, j: (s, 0)),
            pl.BlockSpec((MOE_ST, LANES), lambda s, j: (s, 0)),
            pl.BlockSpec((MOE_SUB, D_MODEL), row),
            pl.BlockSpec((1, D_MODEL), lambda s, j: (0, 0)),
            pl.BlockSpec(memory_space=pl.ANY),
            pl.BlockSpec(memory_space=pl.ANY),
        ],
        out_specs=pl.BlockSpec((MOE_SUB, D_MODEL), row),
        out_shape=jax.ShapeDtypeStruct((t, D_MODEL), jnp.float32),
        scratch_shapes=[
            pltpu.VMEM((MOE_GROWS, D_MODEL), jnp.bfloat16),
            pltpu.VMEM((MOE_GROWS, D_MODEL), jnp.float32),
            pltpu.VMEM((MOE_GROWS, LANES), jnp.float32),
            pltpu.VMEM((MOE_NSUB, MOE_SUB, LANES), jnp.float32),
            pltpu.VMEM((MOE_ZROWS, D_MODEL), jnp.bfloat16),
            pltpu.VMEM((MOE_ZROWS, LANES), jnp.float32),
            pltpu.VMEM((MOE_ZK, D_MODEL), jnp.float32),
            pltpu.VMEM((2, D_MODEL, 2 * D_EXPERT), jnp.bfloat16),
            pltpu.VMEM((2, D_EXPERT, D_MODEL), jnp.bfloat16),
            pltpu.SemaphoreType.DMA((2, 2)),
            pltpu.SMEM((64,), jnp.int32),
        ],
        compiler_params=pltpu.CompilerParams(
            dimension_semantics=("parallel", "arbitrary"), vmem_limit_bytes=VMEM_LIMIT),
        name="moe",
    )(hn, lg, h, g_final, w_e_in, w_e_out)


def kernel(x, positions, mem, g_mix, w_in, b_gate, lambda_q1, lambda_k1, lambda_q2, lambda_k2,
           g_diff_sub, g_mem, w_mem_kv, w_br_diff, w_br_dsa, w_br_mem, w_out, g_ffn,
           w_route_group, b_route_group, w_route_expert, b_route_expert, w_exp_in, w_exp_out,
           g_final):
    b, s, d = x.shape
    t = b * s
    bf = jnp.bfloat16
    top_k = min(TOPK_MAX, s // 4)
    assert d == D_MODEL and s % DIFF_T == 0 and s % DSA_TQ == 0 and top_k <= DSA_TQ
    assert PROJ_TM == DIFF_T == DSA_TK and DSA_TK % DSA_TQ == 0, "proj writes V^T in the attention kernels' key tiles"
    assert g_mix.shape[0] == 1, "single layer"
    lam_init = 0.8 - 0.6 * math.exp(-0.3 * 0)

    wi = w_in[0]
    c = 512
    seg = lambda k: wi[:, k * c:(k + 1) * c]
    o_ik = 7 * c
    w_ik = wi[:, o_ik:o_ik + IDX_DIM]
    w_iw = wi[:, o_ik + IDX_DIM:o_ik + IDX_DIM + IDX_HEADS]
    o_mq = o_ik + IDX_DIM + IDX_HEADS
    w_mq = wi[:, o_mq:o_mq + c]
    w_gl = wi[:, o_mq + c:]
    qs = DIFF_QK_DIM ** -0.5
    qs2 = qs * LOG2E
    src = _pair_lane_source()
    src_seg = np.concatenate([blk * LANES + src for blk in range(c // LANES)])
    rseg = lambda k: seg(k)[:, src_seg]
    w_a = jnp.concatenate([rseg(0) * qs2, rseg(1), seg(2), rseg(3) * qs2, rseg(4), seg(5), rseg(6) * qs, w_mq],
                          axis=1).astype(bf)
    w_s = jnp.concatenate([jnp.concatenate([w_ik, w_ik], axis=1)[:, src], w_iw,
                           jnp.zeros((d, LANES - IDX_HEADS), wi.dtype)], axis=1).astype(bf)
    w_g = w_gl.reshape(d, 3, d).transpose(1, 0, 2).astype(bf)
    b_g = b_gate[0].reshape(3, 1, d)
    w_br = jnp.stack([w_br_diff[0], w_br_dsa[0], w_br_mem[0]]).astype(bf)
    w_r = jnp.concatenate([w_route_group[0], w_route_expert[0],
                           jnp.zeros((d, LANES - N_GROUPS - N_EXPERTS), wi.dtype)], axis=1).astype(bf)
    b_r = jnp.concatenate([b_route_group[0], b_route_expert[0],
                           jnp.zeros((LANES - N_GROUPS - N_EXPERTS,), jnp.float32)]).reshape(1, LANES)

    rot = IDX_DIM // ROPE_FRACTION
    inv_freq = ROPE_THETA ** (-jnp.arange(0, rot, 2, dtype=jnp.float32) / rot)
    inv_col = inv_freq.reshape(rot // 2, 1)
    expand = jnp.asarray(_rope_expanders(), bf)

    x2 = x.reshape(t, d)
    pos_rows = positions.reshape(t // PROJ_TM, 1, PROJ_TM)
    p, ikk, dvt, svt, iwt = _proj(x2, pos_rows, g_mix, inv_col, expand, w_a, w_s)

    y_diff = _diff_attention(p, dvt, lambda_q1, lambda_k1, lambda_q2, lambda_k2,
                             g_diff_sub.reshape(DIFF_V_DIM, 1), b, s, lam_init)
    y_dsa = _dsa_attention(p, ikk, iwt, svt, b, s, top_k)
    mkv = _memkv(mem.reshape(b * MEM_LEN, d), g_mem, w_mem_kv[0].astype(bf), b)
    h, hn, lg = _merge(x2, y_diff, y_dsa, p, mkv, s, g_mix, w_g, b_g, w_br, w_out[0].astype(bf), g_ffn, w_r, b_r)
    out = _moe(hn, lg, h, w_exp_in[0].astype(bf), w_exp_out[0].astype(bf), g_final.reshape(1, d))
    return out.reshape(b, s, d)
```
